```python
import jax, jax.numpy as jnp
from jax import lax
import numpy as np

D_MODEL = 1024
BATCH = 8
SEQ = 2048
DEPTH = 1

CHUNK = 64
MIX_WIDTH = D_MODEL
RET_WIDTH = MIX_WIDTH // 2
RET_HEADS = 4
RET_DV = RET_WIDTH // RET_HEADS
RET_DK = RET_DV // 2
RET_QK = RET_HEADS * RET_DK
RWKV_WIDTH = MIX_WIDTH - RET_WIDTH
RWKV_HEAD = 64
RWKV_HEADS = RWKV_WIDTH // RWKV_HEAD
LORA_W = 64
LORA_A = 64
RET_COLS = 2 * RET_QK + 2 * RET_WIDTH
RWKV_COLS = 4 * RWKV_WIDTH + LORA_W + LORA_A
IN_COLS = RET_COLS + RWKV_COLS
ROPE_BASE = 10000.0
RMS_EPS = 1e-6
RET_GN_EPS = 1e-5
RWKV_GN_EPS = 64e-5

kernel_name = "retention_rwkv7_parallel_hybrid"


def rmsnorm(x, g):
    xf = x.astype(jnp.float32)
    y = xf * lax.rsqrt(jnp.mean(xf * xf, axis=-1, keepdims=True) + RMS_EPS)
    return (y * g.astype(jnp.float32)).astype(x.dtype)


def head_norm(x, eps):
    mu = jnp.mean(x, axis=-1, keepdims=True)
    var = jnp.mean(jnp.square(x - mu), axis=-1, keepdims=True)
    return (x - mu) * lax.rsqrt(var + eps)


def rope(x, pos):
    half = x.shape[-1] // 2
    expo = -jnp.arange(half, dtype=jnp.float32) / jnp.float32(half)
    freqs = jnp.exp(expo * jnp.float32(np.log(ROPE_BASE)))
    ang = pos.astype(jnp.float32)[:, None] * freqs[None, :]
    cos = jnp.cos(ang)[None, :, None, :]
    sin = jnp.sin(ang)[None, :, None, :]
    x1 = x[..., :half]
    x2 = x[..., half:]
    return jnp.concatenate([x1 * cos - x2 * sin, x1 * sin + x2 * cos], axis=-1)


def chunk_retention(q, k, v):
    q = q.astype(jnp.float32)
    k = k.astype(jnp.float32)
    v = v.astype(jnp.float32)
    B, T, H, dk = q.shape
    dv = v.shape[-1]
    nc = T // CHUNK
    hidx = jnp.arange(H, dtype=jnp.float32)
    lg = jnp.log(1.0 - jnp.exp2(-5.0 - hidx))
    idx = jnp.arange(CHUNK, dtype=jnp.float32)
    intra_decay = jnp.exp(lg[:, None, None] * jnp.abs(idx[:, None] - idx[None, :]))
    q_dec = jnp.transpose(jnp.exp(lg[:, None] * (idx[None, :] + 1.0)))
    k_dec = jnp.transpose(jnp.exp(lg[:, None] * (CHUNK - 1.0 - idx[None, :])))
    chunk_dec = jnp.exp(lg * CHUNK).reshape(1, H, 1, 1)

    qc = q.reshape(B, nc, CHUNK, H, dk)
    kc = k.reshape(B, nc, CHUNK, H, dk)
    vc = v.reshape(B, nc, CHUNK, H, dv)

    scores = jnp.einsum('bnqhd,bnkhd->bnhqk', qc, kc) * intra_decay[None, None]
    intra = jnp.einsum('bnhqk,bnkhe->bnqhe', scores, vc)

    kv = jnp.einsum('bnkhd,bnkhe->nbhde', kc * k_dec[None, None, :, :, None], vc)

    def step(S, kv_c):
        S_new = (S * chunk_dec + kv_c).astype(S.dtype)
        return S_new, S

    S0 = jnp.zeros(kv.shape[1:], dtype=kv.dtype)
    _, s_in = lax.scan(step, S0, kv)
    inter = jnp.einsum('bnqhd,nbhde->bnqhe', qc * q_dec[None, None, :, :, None], s_in)
    return (intra + inter).reshape(B, T, H, dv)


def wkv7_scan(r, w, k, v, kk, a):
    B, T, H, N = r.shape
    xs = tuple(jnp.moveaxis(t.astype(jnp.float32), 1, 0) for t in (r, w, k, v, kk, a))

    def step(S, inp):
        r_t, w_t, k_t, v_t, kk_t, a_t = inp
        sa = jnp.einsum('bhij,bhj->bhi', S, -kk_t)
        S_new = (S * w_t[:, :, None, :]
                 + sa[..., None] * (kk_t * a_t)[:, :, None, :]
                 + v_t[..., None] * k_t[:, :, None, :]).astype(S.dtype)
        return S_new, jnp.einsum('bhij,bhj->bhi', S_new, r_t)

    S0 = jnp.zeros((B, H, N, N), dtype=jnp.float32)
    _, o = lax.scan(step, S0, xs)
    return jnp.moveaxis(o, 0, 1)


def setup_inputs(seed: int = 0) -> dict:
    key = jax.random.key(seed)
    ks = jax.random.split(key, 20)
    f32 = jnp.float32
    L = DEPTH
    x = jax.random.normal(ks[0], (BATCH, SEQ, D_MODEL), f32)
    norm_g = 1.0 + 0.01 * jax.random.normal(ks[1], (L, D_MODEL), f32)
    w_in = jax.random.normal(ks[2], (L, D_MODEL, IN_COLS), f32) * D_MODEL ** -0.5
    ret_gn_g = 1.0 + 0.01 * jax.random.normal(ks[3], (L, RET_WIDTH), f32)
    rwkv_mu = jax.random.uniform(ks[4], (L, RWKV_COLS), f32)
    w_lora_up = jax.random.normal(ks[5], (L, LORA_W, RWKV_WIDTH), f32) * 0.1
    w0 = jax.random.uniform(ks[6], (L, RWKV_WIDTH), f32, minval=-3.0, maxval=1.0)
    a_lora_up = jax.random.normal(ks[7], (L, LORA_A, RWKV_WIDTH), f32) * 0.1
    a0 = 0.1 * jax.random.normal(ks[8], (L, RWKV_WIDTH), f32)
    k_k = 0.85 + 0.05 * jax.random.normal(ks[9], (L, RWKV_WIDTH), f32)
    k_a = 1.0 + 0.05 * jax.random.normal(ks[10], (L, RWKV_WIDTH), f32)
    r_k = 0.1 * jax.random.normal(ks[11], (L, RWKV_HEADS, RWKV_HEAD), f32)
    rwkv_gn_g = 1.0 + 0.01 * jax.random.normal(ks[12], (L, RWKV_WIDTH), f32)
    rwkv_gn_b = 0.01 * jax.random.normal(ks[13], (L, RWKV_WIDTH), f32)
    w_out = jax.random.normal(ks[14], (L, MIX_WIDTH, D_MODEL), f32) * MIX_WIDTH ** -0.5
    final_norm_g = 1.0 + 0.01 * jax.random.normal(ks[15], (D_MODEL,), f32)
    return {"x": x, "norm_g": norm_g, "w_in": w_in, "ret_gn_g": ret_gn_g,
            "rwkv_mu": rwkv_mu, "w_lora_up": w_lora_up, "w0": w0,
            "a_lora_up": a_lora_up, "a0": a0, "k_k": k_k, "k_a": k_a, "r_k": r_k,
            "rwkv_gn_g": rwkv_gn_g, "rwkv_gn_b": rwkv_gn_b, "w_out": w_out,
            "final_norm_g": final_norm_g}


def reference(x, norm_g, w_in, ret_gn_g, rwkv_mu, w_lora_up, w0, a_lora_up, a0,
              k_k, k_a, r_k, rwkv_gn_g, rwkv_gn_b, w_out, final_norm_g):
    B, T, _ = x.shape
    f32 = jnp.float32
    pos = jnp.arange(T, dtype=jnp.int32)
    W = RWKV_WIDTH
    h = x
    for l in range(DEPTH):
        u = rmsnorm(h, norm_g[l])
        p = jnp.einsum('btd,dc->btc', u, w_in[l]).astype(f32)
        p_ret = p[..., :RET_COLS]
        p_rwkv = p[..., RET_COLS:]

        q = p_ret[..., :RET_QK]
        k = p_ret[..., RET_QK:2 * RET_QK]
        v = p_ret[..., 2 * RET_QK:2 * RET_QK + RET_WIDTH]
        g_ret = p_ret[..., 2 * RET_QK + RET_WIDTH:]
        q = rope(q.reshape(B, T, RET_HEADS, RET_DK), pos)
        k = rope(k.reshape(B, T, RET_HEADS, RET_DK), pos) * (RET_DK ** -0.5)
        v = v.reshape(B, T, RET_HEADS, RET_DV)
        ret = head_norm(chunk_retention(q, k, v), RET_GN_EPS).reshape(B, T, RET_WIDTH)
        y_ret = jax.nn.silu(g_ret) * (ret * ret_gn_g[l].astype(f32))

        prev = jnp.pad(p_rwkv, ((0, 0), (1, 0), (0, 0)))[:, :-1]
        ps = p_rwkv + rwkv_mu[l].astype(f32) * (prev - p_rwkv)
        r = ps[..., :W]
        kr = ps[..., W:2 * W]
        vr = ps[..., 2 * W:3 * W]
        g_rw = ps[..., 3 * W:4 * W]
        xw = ps[..., 4 * W:4 * W + LORA_W]
        xa = ps[..., 4 * W + LORA_W:]
        w_log = -jax.nn.softplus(-(w0[l].astype(f32) + jnp.tanh(xw) @ w_lora_up[l].astype(f32))) - 0.5
        decay = jnp.exp(-jnp.exp(w_log))
        a = jax.nn.sigmoid(a0[l].astype(f32) + xa @ a_lora_up[l].astype(f32))
        kk = (kr * k_k[l].astype(f32)).reshape(B, T, RWKV_HEADS, RWKV_HEAD)
        kk = kk / jnp.maximum(jnp.sqrt(jnp.sum(kk * kk, axis=-1, keepdims=True)), 1e-12)
        kr = kr * (1.0 + (a - 1.0) * k_a[l].astype(f32))
        r4 = r.reshape(B, T, RWKV_HEADS, RWKV_HEAD)
        k4 = kr.reshape(B, T, RWKV_HEADS, RWKV_HEAD)
        v4 = vr.reshape(B, T, RWKV_HEADS, RWKV_HEAD)
        w4 = decay.reshape(B, T, RWKV_HEADS, RWKV_HEAD)
        a4 = a.reshape(B, T, RWKV_HEADS, RWKV_HEAD)
        o = wkv7_scan(r4, w4, k4, v4, kk, a4)
        gn_g = rwkv_gn_g[l].astype(f32).reshape(RWKV_HEADS, RWKV_HEAD)
        gn_b = rwkv_gn_b[l].astype(f32).reshape(RWKV_HEADS, RWKV_HEAD)
        o = head_norm(o, RWKV_GN_EPS) * gn_g + gn_b
        bonus = jnp.sum(r4 * k4 * r_k[l].astype(f32), axis=-1, keepdims=True) * v4
        y_rwkv = jax.nn.silu(g_rw) * (o + bonus).reshape(B, T, RWKV_WIDTH)

        y = jnp.concatenate([y_ret, y_rwkv], axis=-1).astype(x.dtype)
        h = h + jnp.einsum('btc,cd->btd', y, w_out[l]).astype(h.dtype)
    return rmsnorm(h, final_norm_g)
```

```python
import numpy as np
import jax
import jax.numpy as jnp
from jax import lax
from jax.experimental import pallas as pl
from jax.experimental.pallas import tpu as pltpu

D_MODEL = 1024
CHUNK = 64
RET_HEADS = 4
RET_DK = 64
RET_DV = 128
RET_QK = RET_HEADS * RET_DK
RET_WIDTH = RET_HEADS * RET_DV
RWKV_WIDTH = 512
RWKV_HEAD = 64
LORA = 64
RET_COLS = 2 * RET_QK + 2 * RET_WIDTH
RWKV_COLS = 4 * RWKV_WIDTH + 2 * LORA
IN_COLS = RET_COLS + RWKV_COLS
ROPE_BASE = 10000.0
RMS_EPS = 1e-6
RET_GN_EPS = 1e-5
RWKV_GN_EPS = 64e-5
GROUP = 256
PROJ_TILE = 512
VMEM_LIMIT = 56 * 1024 * 1024

f32 = jnp.float32
bf16 = jnp.bfloat16


def _mm(a, b):
    return jnp.dot(a.astype(bf16), b.astype(bf16), preferred_element_type=f32)


def _mm_nt(a, b):
    return lax.dot_general(a.astype(bf16), b.astype(bf16), (((1,), (1,)), ((), ())),
                           preferred_element_type=f32)


def _mm_tn(a, b):
    return lax.dot_general(a.astype(bf16), b.astype(bf16), (((0,), (0,)), ((), ())),
                           preferred_element_type=f32)


def _split3(x):
    hi = x.astype(bf16)
    r1 = x - hi.astype(f32)
    mid = r1.astype(bf16)
    lo = (r1 - mid.astype(f32)).astype(bf16)
    return hi, mid, lo


def _block_diag(x, col_width):
    t = jnp.concatenate([x] * 4, axis=0)
    return jnp.where(_same_block(t.shape, CHUNK, col_width), t, jnp.zeros_like(t))


def _same_block(shape, row_width, col_width):
    ri = lax.broadcasted_iota(jnp.int32, shape, 0) >> (row_width.bit_length() - 1)
    ci = lax.broadcasted_iota(jnp.int32, shape, 1) >> (col_width.bit_length() - 1)
    return ri == ci


def _sigmoid(x):
    return 1.0 / (1.0 + jnp.exp(-x))


def _in_proj_kernel(x_ref, g_ref, w_ref, p_ref):
    x = x_ref[...]
    u = x * lax.rsqrt(jnp.mean(x * x, axis=-1, keepdims=True) + RMS_EPS) * g_ref[...]
    p_ref[...] = jnp.dot(u.astype(bf16), w_ref[...], preferred_element_type=f32)


def _out_proj_kernel(x_ref, y_ref, w_ref, g_ref, o_ref):
    h = x_ref[...] + jnp.dot(y_ref[...], w_ref[...], preferred_element_type=f32)
    o_ref[...] = h * lax.rsqrt(jnp.mean(h * h, axis=-1, keepdims=True) + RMS_EPS) * g_ref[...]


def _retention_chunk(q, k, v, s_bd, decay4, qdec, kdec, cdec):
    scores = _mm_nt(q, _block_diag(k, RET_DK)) * decay4
    v_bd = _block_diag(v, RET_DV)
    out = _mm(scores, v_bd) + _mm(q * qdec, s_bd)
    kv = _mm_tn(k * kdec, v)
    s_new = s_bd * cdec + jnp.where(_same_block(kv.shape, RET_DK, RET_DV), kv, 0.0)
    return out, s_new


def _rwkv_chunk(r, k, v, kkn, a, lw, m_bd, tri_incl):
    C = CHUNK
    hi, mid, lo = _split3(lw)
    cum = (jnp.dot(tri_incl, hi, preferred_element_type=f32)
           + jnp.dot(tri_incl, mid, preferred_element_type=f32)
           + jnp.dot(tri_incl, lo, preferred_element_type=f32))
    cum_end = cum[C - 1:C, :]
    g_inc = jnp.exp(cum)
    g_exc = jnp.exp(cum - lw)
    g_inv = jnp.exp(-cum)
    g_end = jnp.exp(cum_end - cum)
    beta = kkn * a
    lhs = jnp.concatenate([-kkn * g_exc, r * g_inc], axis=0)
    rhs_bd = jnp.concatenate([_block_diag(beta * g_inv, RWKV_HEAD),
                              _block_diag(k * g_inv, RWKV_HEAD)], axis=0)
    amat = _mm_nt(lhs, rhs_bd)
    ti = lax.broadcasted_iota(jnp.int32, (C, GROUP), 0)
    si = lax.broadcasted_iota(jnp.int32, (C, GROUP), 1) & (C - 1)
    strict = si < ti
    incl = si <= ti
    a_ab = jnp.where(strict, amat[:C, :GROUP], 0.0)
    a_ak = jnp.where(strict, amat[:C, GROUP:], 0.0)
    a_rb = jnp.where(incl, amat[C:, :GROUP], 0.0)
    a_rk = jnp.where(incl, amat[C:, GROUP:], 0.0)
    s_pow = a_ab
    t_inv = jnp.where(si == ti, 1.0, 0.0) + a_ab
    s_pow = _mm(s_pow, _block_diag(s_pow, RWKV_HEAD))
    for _ in range(4):
        st = _mm(jnp.concatenate([s_pow, t_inv], axis=0), _block_diag(s_pow, RWKV_HEAD))
        s_pow = st[:C]
        t_inv = t_inv + st[C:]
    t_inv = t_inv + _mm(t_inv, _block_diag(s_pow, RWKV_HEAD))
    mv = _mm(lhs, m_bd)
    av = _mm(jnp.concatenate([a_ak, a_rk], axis=0), _block_diag(v, RWKV_HEAD))
    y = mv[:C] + av[:C]
    u = _mm(t_inv, _block_diag(y, RWKV_HEAD))
    out = mv[C:] + av[C:] + _mm(a_rb, _block_diag(u, RWKV_HEAD))
    upd = _mm_tn(jnp.concatenate([beta * g_end, k * g_end], axis=0),
                 jnp.concatenate([u, v], axis=0))
    ones = jnp.ones((C, GROUP), bf16)
    col = (lax.dot_general(hi, ones, (((0,), (0,)), ((), ())), preferred_element_type=f32)
           + lax.dot_general(mid, ones, (((0,), (0,)), ((), ())), preferred_element_type=f32)
           + lax.dot_general(lo, ones, (((0,), (0,)), ((), ())), preferred_element_type=f32))
    m_new = jnp.exp(col) * m_bd + jnp.where(_same_block(upd.shape, RWKV_HEAD, RWKV_HEAD), upd, 0.0)
    return out, m_new


def _mixer_kernel(p_ref, cos_ref, sin_ref, dec_ref, cdec_ref, mu_ref, lora_ref, vec_ref, rgn_ref,
                  seg_ref, y_ref, carry_ref, m_ref, s_ref):
    C = CHUNK

    @pl.when(pl.program_id(1) == 0)
    def _():
        carry_ref[...] = jnp.zeros_like(carry_ref)
        m_ref[...] = jnp.zeros_like(m_ref)
        s_ref[...] = jnp.zeros_like(s_ref)

    cos = cos_ref[...]
    sin = sin_ref[...]
    half = (lax.broadcasted_iota(jnp.int32, (C, RET_QK), 1) & (RET_DK - 1)) < RET_DK // 2

    def rope(x):
        swapped = jnp.where(half, pltpu.roll(x, RET_QK - RET_DK // 2, 1), pltpu.roll(x, RET_DK // 2, 1))
        return x * cos + swapped * sin

    q = rope(p_ref[0, :, 0:RET_QK])
    k = rope(p_ref[0, :, RET_QK:2 * RET_QK])
    v = p_ref[0, :, 2 * RET_QK:2 * RET_QK + RET_WIDTH]
    g_ret = p_ref[0, :, 2 * RET_QK + RET_WIDTH:RET_COLS]
    ret, s_new = _retention_chunk(q, k, v, s_ref[...], dec_ref[0], dec_ref[1], dec_ref[2], cdec_ref[...])
    s_ref[...] = s_new
    rgn = rgn_ref[...]
    for h in range(RET_HEADS):
        sl = slice(h * RET_DV, (h + 1) * RET_DV)
        xh = ret[:, sl]
        d = xh - jnp.mean(xh, axis=-1, keepdims=True)
        var = jnp.mean(d * d, axis=-1, keepdims=True)
        gh = g_ret[:, sl]
        y_ref[0, :, sl] = (gh * _sigmoid(gh) * (d * lax.rsqrt(var + RET_GN_EPS) * rgn[:, sl])).astype(y_ref.dtype)

    W = RWKV_WIDTH
    pr = p_ref[0, :, RET_COLS:IN_COLS]
    row = lax.broadcasted_iota(jnp.int32, pr.shape, 0)
    prev = jnp.where(row == 0, carry_ref[...], pltpu.roll(pr, 1, 0))
    carry_ref[...] = pr[C - 1:C, :]
    ps = pr + mu_ref[...] * (prev - pr)
    r = ps[:, 0:W]
    kr = ps[:, W:2 * W]
    vr = ps[:, 2 * W:3 * W]
    g_rw = ps[:, 3 * W:4 * W]
    xwa = ps[:, 4 * W:4 * W + 2 * LORA]
    lane = lax.broadcasted_iota(jnp.int32, xwa.shape, 1)
    lora = _mm(jnp.where(lane < LORA, jnp.tanh(xwa), xwa), lora_ref[...])
    w0, a0, k_k, k_a, r_k, gn_g, gn_b = (vec_ref[i:i + 1, :] for i in range(7))
    lw = -np.float32(np.exp(-0.5)) * _sigmoid(w0 + lora[:, :W])
    a = _sigmoid(a0 + lora[:, W:])
    seg = seg_ref[...]

    def segsum(x):
        hi = x.astype(bf16)
        lo = (x - hi.astype(f32)).astype(bf16)
        return jnp.dot(hi, seg, preferred_element_type=f32) + jnp.dot(lo, seg, preferred_element_type=f32)

    kk = kr * k_k
    kkn = kk * lax.rsqrt(jnp.maximum(segsum(kk * kk), 1e-24))
    kmod = kr * (1.0 + (a - 1.0) * k_a)
    ti = lax.broadcasted_iota(jnp.int32, (C, C), 0)
    si = lax.broadcasted_iota(jnp.int32, (C, C), 1)
    tri_incl = jnp.where(si <= ti, 1.0, 0.0).astype(bf16)
    outs = []
    for grp in range(W // GROUP):
        sl = slice(grp * GROUP, (grp + 1) * GROUP)
        o, m_new = _rwkv_chunk(r[:, sl], kmod[:, sl], vr[:, sl], kkn[:, sl], a[:, sl], lw[:, sl],
                               m_ref[grp], tri_incl)
        m_ref[grp] = m_new
        outs.append(o)
    o = jnp.concatenate(outs, axis=1)
    d = o - segsum(o) * (1.0 / RWKV_HEAD)
    var = segsum(d * d) * (1.0 / RWKV_HEAD)
    o = d * lax.rsqrt(var + RWKV_GN_EPS) * gn_g + gn_b
    bonus = segsum(r * kmod * r_k) * vr
    y_ref[0, :, RET_WIDTH:] = (g_rw * _sigmoid(g_rw) * (o + bonus)).astype(y_ref.dtype)


def _rope_tables(seq):
    half = RET_DK // 2
    expo = -jnp.arange(half, dtype=f32) / f32(half)
    freqs = jnp.exp(expo * f32(np.log(ROPE_BASE)))
    ang = jnp.arange(seq, dtype=jnp.int32).astype(f32)[:, None] * freqs[None, :]
    cos = jnp.cos(ang)
    sin = jnp.sin(ang)
    cos_full = jnp.tile(jnp.concatenate([cos, cos], axis=1), (1, RET_HEADS))
    sin_signed = jnp.tile(jnp.concatenate([-sin, sin], axis=1), (1, RET_HEADS))
    return cos_full, sin_signed


def _retention_constants():
    C = CHUNK
    lg = np.log(1.0 - np.exp2(-5.0 - np.arange(RET_HEADS, dtype=np.float64)))
    lane_lg = np.repeat(lg, RET_DK)[None, :]
    n = np.arange(C, dtype=np.float64)[:, None]
    m = np.tile(np.arange(C, dtype=np.float64), RET_HEADS)[None, :]
    scale = RET_DK ** -0.5
    decay4 = scale * np.exp(lane_lg * np.abs(n - m))
    qdec = np.exp(lane_lg * (n + 1.0)) * np.ones((1, RET_QK))
    kdec = scale * np.exp(lane_lg * (C - 1.0 - n)) * np.ones((1, RET_QK))
    dec = np.stack([decay4, qdec, kdec]).astype(np.float32)
    ri = np.arange(RET_QK)[:, None] // RET_DK
    ci = np.arange(RET_WIDTH)[None, :] // RET_DV
    cdec = np.where(ri == ci, np.exp(np.repeat(lg, RET_DK) * C)[:, None], 0.0).astype(np.float32)
    return jnp.asarray(dec), jnp.asarray(cdec)


def _const_spec(shape):
    return pl.BlockSpec(shape, lambda *_: (0,) * len(shape))


@jax.jit
def kernel(x, norm_g, w_in, ret_gn_g, rwkv_mu, w_lora_up, w0, a_lora_up, a0, k_k, k_a, r_k,
           rwkv_gn_g, rwkv_gn_b, w_out, final_norm_g):
    B, T, D = x.shape
    assert D == D_MODEL and T % CHUNK == 0 and (B * T) % PROJ_TILE == 0
    assert norm_g.shape[0] == 1, "single-layer block"
    n_tok = B * T
    xf = x.reshape(n_tok, D)
    params = pltpu.CompilerParams(dimension_semantics=("arbitrary",), vmem_limit_bytes=VMEM_LIMIT)

    p = pl.pallas_call(
        _in_proj_kernel,
        grid=(n_tok // PROJ_TILE,),
        in_specs=[pl.BlockSpec((PROJ_TILE, D), lambda i: (i, 0)),
                  _const_spec((1, D)),
                  _const_spec((D, IN_COLS))],
        out_specs=pl.BlockSpec((PROJ_TILE, IN_COLS), lambda i: (i, 0)),
        out_shape=jax.ShapeDtypeStruct((n_tok, IN_COLS), f32),
        compiler_params=params,
        name="in_proj",
    )(xf, norm_g[0][None, :], w_in[0].astype(bf16))

    cos, sin = _rope_tables(T)
    dec, cdec = _retention_constants()
    zeros = jnp.zeros((LORA, RWKV_WIDTH), f32)
    lora_w = jnp.concatenate(
        [jnp.concatenate([w_lora_up[0], zeros], axis=1),
         jnp.concatenate([zeros, a_lora_up[0]], axis=1)], axis=0).astype(bf16)
    vecs = jnp.stack([w0[0], a0[0], k_k[0], k_a[0], r_k[0].reshape(-1), rwkv_gn_g[0], rwkv_gn_b[0],
                      jnp.zeros((RWKV_WIDTH,), f32)])
    seg_ids = np.arange(RWKV_WIDTH) // RWKV_HEAD
    seg = jnp.asarray(seg_ids[:, None] == seg_ids[None, :], bf16)

    y = pl.pallas_call(
        _mixer_kernel,
        grid=(B, T // CHUNK),
        in_specs=[pl.BlockSpec((1, CHUNK, IN_COLS), lambda b, c: (b, c, 0)),
                  pl.BlockSpec((CHUNK, RET_QK), lambda b, c: (c, 0)),
                  pl.BlockSpec((CHUNK, RET_QK), lambda b, c: (c, 0)),
                  _const_spec((3, CHUNK, RET_QK)),
                  _const_spec((RET_QK, RET_WIDTH)),
                  _const_spec((1, RWKV_COLS)),
                  _const_spec((2 * LORA, 2 * RWKV_WIDTH)),
                  _const_spec((8, RWKV_WIDTH)),
                  _const_spec((1, RET_WIDTH)),
                  _const_spec((RWKV_WIDTH, RWKV_WIDTH))],
        out_specs=pl.BlockSpec((1, CHUNK, D), lambda b, c: (b, c, 0)),
        out_shape=jax.ShapeDtypeStruct((B, T, D), bf16),
        scratch_shapes=[pltpu.VMEM((1, RWKV_COLS), f32),
                        pltpu.VMEM((RWKV_WIDTH // GROUP, GROUP, GROUP), f32),
                        pltpu.VMEM((RET_QK, RET_WIDTH), f32)],
        compiler_params=pltpu.CompilerParams(dimension_semantics=("arbitrary", "arbitrary"),
                                             vmem_limit_bytes=VMEM_LIMIT),
        name="mixers",
    )(p.reshape(B, T, IN_COLS), cos, sin, dec, cdec, rwkv_mu[0][None, :], lora_w, vecs,
      ret_gn_g[0][None, :], seg)

    out = pl.pallas_call(
        _out_proj_kernel,
        grid=(n_tok // PROJ_TILE,),
        in_specs=[pl.BlockSpec((PROJ_TILE, D), lambda i: (i, 0)),
                  pl.BlockSpec((PROJ_TILE, D), lambda i: (i, 0)),
                  _const_spec((D, D)),
                  _const_spec((1, D))],
        out_specs=pl.BlockSpec((PROJ_TILE, D), lambda i: (i, 0)),
        out_shape=jax.ShapeDtypeStruct((n_tok, D), f32),
        compiler_params=params,
        name="out_proj",
    )(xf, y.reshape(n_tok, D), w_out[0].astype(bf16), final_norm_g[None, :])
    return out.reshape(B, T, D)
```

```python
import numpy as np
import jax
import jax.numpy as jnp
from jax import lax
from jax.experimental import pallas as pl
from jax.experimental.pallas import tpu as pltpu

D_MODEL = 1024
CHUNK = 64
RET_HEADS = 4
RET_DK = 64
RET_DV = 128
RET_QK = RET_HEADS * RET_DK
RET_WIDTH = RET_HEADS * RET_DV
RWKV_WIDTH = 512
RWKV_HEAD = 64
LORA = 64
RET_COLS = 2 * RET_QK + 2 * RET_WIDTH
RWKV_COLS = 4 * RWKV_WIDTH + 2 * LORA
IN_COLS = RET_COLS + RWKV_COLS
ROPE_BASE = 10000.0
RMS_EPS = 1e-6
RET_GN_EPS = 1e-5
RWKV_GN_EPS = 64e-5
GROUP = 256
PROJ_TILE = 512
BATCH_BLOCK = 4
VMEM_LIMIT = 56 * 1024 * 1024

f32 = jnp.float32
bf16 = jnp.bfloat16


def _mm(a, b):
    return jnp.dot(a.astype(bf16), b.astype(bf16), preferred_element_type=f32)


def _mm_nt(a, b):
    return lax.dot_general(a.astype(bf16), b.astype(bf16), (((1,), (1,)), ((), ())),
                           preferred_element_type=f32)


def _mm_tn(a, b):
    return lax.dot_general(a.astype(bf16), b.astype(bf16), (((0,), (0,)), ((), ())),
                           preferred_element_type=f32)


def _split3(x):
    hi = x.astype(bf16)
    r1 = x - hi.astype(f32)
    mid = r1.astype(bf16)
    lo = (r1 - mid.astype(f32)).astype(bf16)
    return hi, mid, lo


def _same_block(shape, row_width, col_width):
    ri = lax.broadcasted_iota(jnp.int32, shape, 0) >> (row_width.bit_length() - 1)
    ci = lax.broadcasted_iota(jnp.int32, shape, 1) >> (col_width.bit_length() - 1)
    return ri == ci


def _block_diag(x, col_width):
    t = jnp.concatenate([x] * 4, axis=0)
    return jnp.where(_same_block(t.shape, CHUNK, col_width), t, jnp.zeros_like(t))


def _sigmoid(x):
    return 1.0 / (1.0 + jnp.exp(-x))


def _in_proj_kernel(x_ref, g_ref, w_ref, p_ref):
    x = x_ref[...]
    u = x * lax.rsqrt(jnp.mean(x * x, axis=-1, keepdims=True) + RMS_EPS) * g_ref[...]
    p_ref[...] = jnp.dot(u.astype(bf16), w_ref[...], preferred_element_type=f32)


def _out_proj_kernel(x_ref, y_ref, w_ref, g_ref, o_ref):
    h = x_ref[...] + jnp.dot(y_ref[...], w_ref[...], preferred_element_type=f32)
    o_ref[...] = h * lax.rsqrt(jnp.mean(h * h, axis=-1, keepdims=True) + RMS_EPS) * g_ref[...]


def _retention_chunks(qs, ks, vs, s_bds, decay4, qdec, kdec, cdec):
    n = range(len(qs))
    scores = [_mm_nt(qs[i], _block_diag(ks[i], RET_DK)) * decay4 for i in n]
    intra = [_mm(scores[i], _block_diag(vs[i], RET_DV)) for i in n]
    inter = [_mm(qs[i] * qdec, s_bds[i]) for i in n]
    kv = [_mm_tn(ks[i] * kdec, vs[i]) for i in n]
    keep = _same_block(kv[0].shape, RET_DK, RET_DV)
    s_new = [s_bds[i] * cdec + jnp.where(keep, kv[i], 0.0) for i in n]
    return [intra[i] + inter[i] for i in n], s_new


def _rwkv_chunks(rs, ks, vs, kkns, aas, lws, m_bds):
    C = CHUNK
    n = range(len(rs))
    ti = lax.broadcasted_iota(jnp.int32, (C, C), 0)
    si = lax.broadcasted_iota(jnp.int32, (C, C), 1)
    tri_incl = jnp.where(si <= ti, 1.0, 0.0).astype(bf16)
    parts = [_split3(lws[i]) for i in n]
    cum = [sum(jnp.dot(tri_incl, part, preferred_element_type=f32) for part in parts[i]) for i in n]
    g_inc = [jnp.exp(cum[i]) for i in n]
    g_exc = [jnp.exp(cum[i] - lws[i]) for i in n]
    g_inv = [jnp.exp(-cum[i]) for i in n]
    g_end = [jnp.exp(cum[i][C - 1:C, :] - cum[i]) for i in n]
    beta = [kkns[i] * aas[i] for i in n]
    lhs = [jnp.concatenate([-kkns[i] * g_exc[i], rs[i] * g_inc[i]], axis=0) for i in n]
    amat = [_mm_nt(lhs[i], jnp.concatenate([_block_diag(beta[i] * g_inv[i], RWKV_HEAD),
                                            _block_diag(ks[i] * g_inv[i], RWKV_HEAD)], axis=0)) for i in n]
    ti = lax.broadcasted_iota(jnp.int32, (C, GROUP), 0)
    si = lax.broadcasted_iota(jnp.int32, (C, GROUP), 1) & (C - 1)
    strict = si < ti
    incl = si <= ti
    a_ab = [jnp.where(strict, amat[i][:C, :GROUP], 0.0) for i in n]
    a_ak = [jnp.where(strict, amat[i][:C, GROUP:], 0.0) for i in n]
    a_rb = [jnp.where(incl, amat[i][C:, :GROUP], 0.0) for i in n]
    a_rk = [jnp.where(incl, amat[i][C:, GROUP:], 0.0) for i in n]
    mv = [_mm(lhs[i], m_bds[i]) for i in n]
    av = [_mm(jnp.concatenate([a_ak[i], a_rk[i]], axis=0), _block_diag(vs[i], RWKV_HEAD)) for i in n]
    eye = jnp.where(si == ti, 1.0, 0.0)
    t_inv = [eye + a_ab[i] for i in n]
    s_pow = [_mm(a_ab[i], _block_diag(a_ab[i], RWKV_HEAD)) for i in n]
    for _ in range(4):
        st = [_mm(jnp.concatenate([s_pow[i], t_inv[i]], axis=0), _block_diag(s_pow[i], RWKV_HEAD)) for i in n]
        s_pow = [st[i][:C] for i in n]
        t_inv = [t_inv[i] + st[i][C:] for i in n]
    t_inv = [t_inv[i] + _mm(t_inv[i], _block_diag(s_pow[i], RWKV_HEAD)) for i in n]
    u = [_mm(t_inv[i], _block_diag(mv[i][:C] + av[i][:C], RWKV_HEAD)) for i in n]
    out = [mv[i][C:] + av[i][C:] + _mm(a_rb[i], _block_diag(u[i], RWKV_HEAD)) for i in n]
    upd = [_mm_tn(jnp.concatenate([beta[i] * g_end[i], ks[i] * g_end[i]], axis=0),
                  jnp.concatenate([u[i], vs[i]], axis=0)) for i in n]
    ones = jnp.ones((C, GROUP), bf16)
    col = [sum(lax.dot_general(part, ones, (((0,), (0,)), ((), ())), preferred_element_type=f32)
               for part in parts[i]) for i in n]
    keep = _same_block(upd[0].shape, RWKV_HEAD, RWKV_HEAD)
    m_new = [jnp.exp(col[i]) * m_bds[i] + jnp.where(keep, upd[i], 0.0) for i in n]
    return out, m_new


def _mixer_kernel(p_ref, cos_ref, sin_ref, dec_ref, cdec_ref, mu_ref, lora_ref, vec_ref, rgn_ref,
                  seg_ref, y_ref, carry_ref, m_ref, s_ref):
    C = CHUNK
    NB = BATCH_BLOCK
    R = NB * C
    W = RWKV_WIDTH
    NG = W // GROUP

    @pl.when(pl.program_id(1) == 0)
    def _():
        carry_ref[...] = jnp.zeros_like(carry_ref)
        m_ref[...] = jnp.zeros_like(m_ref)
        s_ref[...] = jnp.zeros_like(s_ref)

    def rows(x, bi):
        return x[bi * C:(bi + 1) * C]

    cos = jnp.concatenate([cos_ref[...]] * NB, axis=0)
    sin = jnp.concatenate([sin_ref[...]] * NB, axis=0)
    half = (lax.broadcasted_iota(jnp.int32, (R, RET_QK), 1) & (RET_DK - 1)) < RET_DK // 2

    def rope(x):
        swapped = jnp.where(half, pltpu.roll(x, RET_QK - RET_DK // 2, 1), pltpu.roll(x, RET_DK // 2, 1))
        return x * cos + swapped * sin

    q = rope(p_ref[:, :, 0:RET_QK].reshape(R, RET_QK))
    k = rope(p_ref[:, :, RET_QK:2 * RET_QK].reshape(R, RET_QK))
    v = p_ref[:, :, 2 * RET_QK:2 * RET_QK + RET_WIDTH].reshape(R, RET_WIDTH)
    rets, s_new = _retention_chunks([rows(q, bi) for bi in range(NB)], [rows(k, bi) for bi in range(NB)],
                                    [rows(v, bi) for bi in range(NB)], [s_ref[bi] for bi in range(NB)],
                                    dec_ref[0], dec_ref[1], dec_ref[2], cdec_ref[...])
    for bi in range(NB):
        s_ref[bi] = s_new[bi]
    ret = jnp.concatenate(rets, axis=0)
    rgn = rgn_ref[...]
    for h in range(RET_HEADS):
        sl = slice(h * RET_DV, (h + 1) * RET_DV)
        xh = ret[:, sl]
        d = xh - jnp.mean(xh, axis=-1, keepdims=True)
        var = jnp.mean(d * d, axis=-1, keepdims=True)
        gh = p_ref[:, :, 2 * RET_QK + RET_WIDTH + h * RET_DV:2 * RET_QK + RET_WIDTH + (h + 1) * RET_DV]
        gh = gh.reshape(R, RET_DV)
        yh = gh * _sigmoid(gh) * (d * lax.rsqrt(var + RET_GN_EPS) * rgn[:, sl])
        y_ref[:, :, sl] = yh.reshape(NB, C, RET_DV).astype(y_ref.dtype)

    pr = p_ref[:, :, RET_COLS:IN_COLS].reshape(R, RWKV_COLS)
    row = lax.broadcasted_iota(jnp.int32, (R, RWKV_COLS), 0)
    prev = pltpu.roll(pr, 1, 0)
    for bi in range(NB):
        prev = jnp.where(row == bi * C, carry_ref[bi], prev)
        carry_ref[bi] = pr[(bi + 1) * C - 1:(bi + 1) * C, :]
    ps = pr + mu_ref[...] * (prev - pr)
    r = ps[:, 0:W]
    kr = ps[:, W:2 * W]
    vr = ps[:, 2 * W:3 * W]
    g_rw = ps[:, 3 * W:4 * W]
    xwa = ps[:, 4 * W:4 * W + 2 * LORA]
    lane = lax.broadcasted_iota(jnp.int32, xwa.shape, 1)
    lora = _mm(jnp.where(lane < LORA, jnp.tanh(xwa), xwa), lora_ref[...])
    w0, a0, k_k, k_a, r_k, gn_g, gn_b = (vec_ref[i:i + 1, :] for i in range(7))
    lw = -np.float32(np.exp(-0.5)) * _sigmoid(w0 + lora[:, :W])
    a = _sigmoid(a0 + lora[:, W:])
    seg = seg_ref[...]

    def segsum(x):
        xs = jnp.concatenate([x[:, g * GROUP:(g + 1) * GROUP] for g in range(NG)], axis=0)
        hi = xs.astype(bf16)
        lo = (xs - hi.astype(f32)).astype(bf16)
        tot = jnp.dot(jnp.concatenate([hi, lo], axis=0), seg, preferred_element_type=f32)
        tot = tot[:NG * R] + tot[NG * R:]
        return jnp.concatenate([tot[g * R:(g + 1) * R] for g in range(NG)], axis=1)

    kk = kr * k_k
    kkn = kk * lax.rsqrt(jnp.maximum(segsum(kk * kk), 1e-24))
    kmod = kr * (1.0 + (a - 1.0) * k_a)

    chains = [(bi, g) for bi in range(NB) for g in range(NG)]

    def pick(x):
        return [x[bi * C:(bi + 1) * C, g * GROUP:(g + 1) * GROUP] for bi, g in chains]

    outs, m_new = _rwkv_chunks(pick(r), pick(kmod), pick(vr), pick(kkn), pick(a), pick(lw),
                               [m_ref[bi, g] for bi, g in chains])
    for i, (bi, g) in enumerate(chains):
        m_ref[bi, g] = m_new[i]
    o = jnp.concatenate([jnp.concatenate(outs[bi * NG:(bi + 1) * NG], axis=1) for bi in range(NB)], axis=0)
    d = o - segsum(o) * (1.0 / RWKV_HEAD)
    var = segsum(d * d) * (1.0 / RWKV_HEAD)
    o = d * lax.rsqrt(var + RWKV_GN_EPS) * gn_g + gn_b
    bonus = segsum(r * kmod * r_k) * vr
    y_rw = g_rw * _sigmoid(g_rw) * (o + bonus)
    y_ref[:, :, RET_WIDTH:] = y_rw.reshape(NB, C, W).astype(y_ref.dtype)


def _rope_tables(seq):
    half = RET_DK // 2
    expo = -jnp.arange(half, dtype=f32) / f32(half)
    freqs = jnp.exp(expo * f32(np.log(ROPE_BASE)))
    ang = jnp.arange(seq, dtype=jnp.int32).astype(f32)[:, None] * freqs[None, :]
    cos = jnp.cos(ang)
    sin = jnp.sin(ang)
    cos_full = jnp.tile(jnp.concatenate([cos, cos], axis=1), (1, RET_HEADS))
    sin_signed = jnp.tile(jnp.concatenate([-sin, sin], axis=1), (1, RET_HEADS))
    return cos_full, sin_signed


def _retention_constants():
    C = CHUNK
    lg = np.log(1.0 - np.exp2(-5.0 - np.arange(RET_HEADS, dtype=np.float64)))
    lane_lg = np.repeat(lg, RET_DK)[None, :]
    n = np.arange(C, dtype=np.float64)[:, None]
    m = np.tile(np.arange(C, dtype=np.float64), RET_HEADS)[None, :]
    scale = RET_DK ** -0.5
    decay4 = scale * np.exp(lane_lg * np.abs(n - m))
    qdec = np.exp(lane_lg * (n + 1.0)) * np.ones((1, RET_QK))
    kdec = scale * np.exp(lane_lg * (C - 1.0 - n)) * np.ones((1, RET_QK))
    dec = np.stack([decay4, qdec, kdec]).astype(np.float32)
    ri = np.arange(RET_QK)[:, None] // RET_DK
    ci = np.arange(RET_WIDTH)[None, :] // RET_DV
    cdec = np.where(ri == ci, np.exp(np.repeat(lg, RET_DK) * C)[:, None], 0.0).astype(np.float32)
    return jnp.asarray(dec), jnp.asarray(cdec)


def _const_spec(shape):
    return pl.BlockSpec(shape, lambda *_: (0,) * len(shape))


@jax.jit
def kernel(x, norm_g, w_in, ret_gn_g, rwkv_mu, w_lora_up, w0, a_lora_up, a0, k_k, k_a, r_k,
           rwkv_gn_g, rwkv_gn_b, w_out, final_norm_g):
    B, T, D = x.shape
    assert D == D_MODEL and T % CHUNK == 0 and (B * T) % PROJ_TILE == 0 and B % BATCH_BLOCK == 0
    assert norm_g.shape[0] == 1, "single-layer block"
    n_tok = B * T
    xf = x.reshape(n_tok, D)
    params = pltpu.CompilerParams(dimension_semantics=("arbitrary",), vmem_limit_bytes=VMEM_LIMIT)

    p = pl.pallas_call(
        _in_proj_kernel,
        grid=(n_tok // PROJ_TILE,),
        in_specs=[pl.BlockSpec((PROJ_TILE, D), lambda i: (i, 0)),
                  _const_spec((1, D)),
                  _const_spec((D, IN_COLS))],
        out_specs=pl.BlockSpec((PROJ_TILE, IN_COLS), lambda i: (i, 0)),
        out_shape=jax.ShapeDtypeStruct((n_tok, IN_COLS), f32),
        compiler_params=params,
        name="in_proj",
    )(xf, norm_g[0][None, :], w_in[0].astype(bf16))

    cos, sin = _rope_tables(T)
    dec, cdec = _retention_constants()
    zeros = jnp.zeros((LORA, RWKV_WIDTH), f32)
    lora_w = jnp.concatenate(
        [jnp.concatenate([w_lora_up[0], zeros], axis=1),
         jnp.concatenate([zeros, a_lora_up[0]], axis=1)], axis=0).astype(bf16)
    vecs = jnp.stack([w0[0], a0[0], k_k[0], k_a[0], r_k[0].reshape(-1), rwkv_gn_g[0], rwkv_gn_b[0],
                      jnp.zeros((RWKV_WIDTH,), f32)])
    seg_ids = np.arange(GROUP) // RWKV_HEAD
    seg = jnp.asarray(seg_ids[:, None] == seg_ids[None, :], bf16)

    y = pl.pallas_call(
        _mixer_kernel,
        grid=(B // BATCH_BLOCK, T // CHUNK),
        in_specs=[pl.BlockSpec((BATCH_BLOCK, CHUNK, IN_COLS), lambda b, c: (b, c, 0)),
                  pl.BlockSpec((CHUNK, RET_QK), lambda b, c: (c, 0)),
                  pl.BlockSpec((CHUNK, RET_QK), lambda b, c: (c, 0)),
                  _const_spec((3, CHUNK, RET_QK)),
                  _const_spec((RET_QK, RET_WIDTH)),
                  _const_spec((1, RWKV_COLS)),
                  _const_spec((2 * LORA, 2 * RWKV_WIDTH)),
                  _const_spec((8, RWKV_WIDTH)),
                  _const_spec((1, RET_WIDTH)),
                  _const_spec((GROUP, GROUP))],
        out_specs=pl.BlockSpec((BATCH_BLOCK, CHUNK, D), lambda b, c: (b, c, 0)),
        out_shape=jax.ShapeDtypeStruct((B, T, D), bf16),
        scratch_shapes=[pltpu.VMEM((BATCH_BLOCK, 1, RWKV_COLS), f32),
                        pltpu.VMEM((BATCH_BLOCK, RWKV_WIDTH // GROUP, GROUP, GROUP), f32),
                        pltpu.VMEM((BATCH_BLOCK, RET_QK, RET_WIDTH), f32)],
        compiler_params=pltpu.CompilerParams(dimension_semantics=("arbitrary", "arbitrary"),
                                             vmem_limit_bytes=VMEM_LIMIT),
        name="mixers",
    )(p.reshape(B, T, IN_COLS), cos, sin, dec, cdec, rwkv_mu[0][None, :], lora_w, vecs,
      ret_gn_g[0][None, :], seg)

    out = pl.pallas_call(
        _out_proj_kernel,
        grid=(n_tok // PROJ_TILE,),
        in_specs=[pl.BlockSpec((PROJ_TILE, D), lambda i: (i, 0)),
                  pl.BlockSpec((PROJ_TILE, D), lambda i: (i, 0)),
                  _const_spec((D, D)),
                  _const_spec((1, D))],
        out_specs=pl.BlockSpec((PROJ_TILE, D), lambda i: (i, 0)),
        out_shape=jax.ShapeDtypeStruct((n_tok, D), f32),
        compiler_params=params,
        name="out_proj",
    )(xf, y.reshape(n_tok, D), w_out[0].astype(bf16), final_norm_g[None, :])
    return out.reshape(B, T, D)
```

```python
import numpy as np
import jax
import jax.numpy as jnp
from jax import lax
from jax.experimental import pallas as pl
from jax.experimental.pallas import tpu as pltpu

D_MODEL = 1024
CHUNK = 64
RET_HEADS = 4
RET_DK = 64
RET_DV = 128
RET_QK = RET_HEADS * RET_DK
RET_WIDTH = RET_HEADS * RET_DV
RWKV_WIDTH = 512
RWKV_HEAD = 64
LORA = 64
RET_COLS = 2 * RET_QK + 2 * RET_WIDTH
RWKV_COLS = 4 * RWKV_WIDTH + 2 * LORA
IN_COLS = RET_COLS + RWKV_COLS
ROPE_BASE = 10000.0
RMS_EPS = 1e-6
RET_GN_EPS = 1e-5
RWKV_GN_EPS = 64e-5
GROUP = 256
PROJ_TILE = 512
BATCH_BLOCK = 4
VMEM_LIMIT = 56 * 1024 * 1024

f32 = jnp.float32
bf16 = jnp.bfloat16


def _mm(a, b):
    return jnp.dot(a.astype(bf16), b.astype(bf16), preferred_element_type=f32)


def _mm_nt(a, b):
    return lax.dot_general(a.astype(bf16), b.astype(bf16), (((1,), (1,)), ((), ())),
                           preferred_element_type=f32)


def _mm_tn(a, b):
    return lax.dot_general(a.astype(bf16), b.astype(bf16), (((0,), (0,)), ((), ())),
                           preferred_element_type=f32)


def _split3(x):
    hi = x.astype(bf16)
    r1 = x - hi.astype(f32)
    mid = r1.astype(bf16)
    lo = (r1 - mid.astype(f32)).astype(bf16)
    return hi, mid, lo


def _same_block(shape, row_width, col_width):
    ri = lax.broadcasted_iota(jnp.int32, shape, 0) >> (row_width.bit_length() - 1)
    ci = lax.broadcasted_iota(jnp.int32, shape, 1) >> (col_width.bit_length() - 1)
    return ri == ci


def _block_diag(x, keep):
    t = jnp.concatenate([x] * 4, axis=0)
    return jnp.where(keep, t, jnp.zeros_like(t))


def _sigmoid(x):
    return 1.0 / (1.0 + jnp.exp(-x))


def _in_proj_kernel(x_ref, g_ref, w_ref, p_ref):
    x = x_ref[...]
    u = x * lax.rsqrt(jnp.mean(x * x, axis=-1, keepdims=True) + RMS_EPS) * g_ref[...]
    p_ref[...] = jnp.dot(u.astype(bf16), w_ref[...], preferred_element_type=f32)


def _out_proj_kernel(x_ref, y_ref, w_ref, g_ref, o_ref):
    h = x_ref[...] + jnp.dot(y_ref[...], w_ref[...], preferred_element_type=f32)
    o_ref[...] = h * lax.rsqrt(jnp.mean(h * h, axis=-1, keepdims=True) + RMS_EPS) * g_ref[...]


def _retention_chunks(qs, ks, vs, s_bds, decay4, qdec, kdec, cdec, keep_k, keep_v):
    n = range(len(qs))
    scores = [_mm_nt(qs[i], _block_diag(ks[i], keep_k)) * decay4 for i in n]
    intra = [_mm(scores[i], _block_diag(vs[i], keep_v)) for i in n]
    inter = [_mm(qs[i] * qdec, s_bds[i]) for i in n]
    kv = [_mm_tn(ks[i] * kdec, vs[i]) for i in n]
    s_new = [s_bds[i] * cdec + jnp.where(keep_v, kv[i], 0.0) for i in n]
    return [intra[i] + inter[i] for i in n], s_new


def _rwkv_chunks(rs, ks, vs, kkns, aas, lws, m_bds, keep):
    C = CHUNK
    n = range(len(rs))
    ti = lax.broadcasted_iota(jnp.int32, (C, C), 0)
    si = lax.broadcasted_iota(jnp.int32, (C, C), 1)
    tri_incl = jnp.where(si <= ti, 1.0, 0.0).astype(bf16)
    parts = [_split3(lws[i]) for i in n]
    cum = [sum(jnp.dot(tri_incl, part, preferred_element_type=f32) for part in parts[i]) for i in n]
    g_inc = [jnp.exp(cum[i]) for i in n]
    g_exc = [jnp.exp(cum[i] - lws[i]) for i in n]
    g_inv = [jnp.exp(-cum[i]) for i in n]
    g_end = [jnp.exp(cum[i][C - 1:C, :] - cum[i]) for i in n]
    beta = [kkns[i] * aas[i] for i in n]
    lhs = [jnp.concatenate([-kkns[i] * g_exc[i], rs[i] * g_inc[i]], axis=0) for i in n]
    amat = [_mm_nt(lhs[i], jnp.concatenate([_block_diag(beta[i] * g_inv[i], keep),
                                            _block_diag(ks[i] * g_inv[i], keep)], axis=0)) for i in n]
    ti = lax.broadcasted_iota(jnp.int32, (C, GROUP), 0)
    si = lax.broadcasted_iota(jnp.int32, (C, GROUP), 1) & (C - 1)
    strict = si < ti
    incl = si <= ti
    a_ab = [jnp.where(strict, amat[i][:C, :GROUP], 0.0) for i in n]
    a_ak = [jnp.where(strict, amat[i][:C, GROUP:], 0.0) for i in n]
    a_rb = [jnp.where(incl, amat[i][C:, :GROUP], 0.0) for i in n]
    a_rk = [jnp.where(incl, amat[i][C:, GROUP:], 0.0) for i in n]
    mv = [_mm_nt(lhs[i], m_bds[i]) for i in n]
    av = [_mm(jnp.concatenate([a_ak[i], a_rk[i]], axis=0), _block_diag(vs[i], keep)) for i in n]
    eye = jnp.where(si == ti, 1.0, 0.0)
    t_inv = [eye + a_ab[i] for i in n]
    s_pow = [_mm(a_ab[i], _block_diag(a_ab[i], keep)) for i in n]
    for _ in range(4):
        st = [_mm(jnp.concatenate([s_pow[i], t_inv[i]], axis=0), _block_diag(s_pow[i], keep)) for i in n]
        s_pow = [st[i][:C] for i in n]
        t_inv = [t_inv[i] + st[i][C:] for i in n]
    t_inv = [t_inv[i] + _mm(t_inv[i], _block_diag(s_pow[i], keep)) for i in n]
    u = [_mm(t_inv[i], _block_diag(mv[i][:C] + av[i][:C], keep)) for i in n]
    out = [mv[i][C:] + av[i][C:] + _mm(a_rb[i], _block_diag(u[i], keep)) for i in n]
    upd = [_mm_tn(jnp.concatenate([u[i], vs[i]], axis=0),
                  jnp.concatenate([beta[i] * g_end[i], ks[i] * g_end[i]], axis=0)) for i in n]
    m_new = [m_bds[i] * g_inc[i][C - 1:C, :] + jnp.where(keep, upd[i], 0.0) for i in n]
    return out, m_new


def _mixer_kernel(p_ref, cos_ref, sin_ref, dec_ref, cdec_ref, mu_ref, lora_ref, vec_ref, rgn_ref,
                  seg_ref, y_ref, carry_ref, m_ref, s_ref):
    C = CHUNK
    NB = BATCH_BLOCK
    R = NB * C
    W = RWKV_WIDTH
    NG = W // GROUP

    @pl.when(pl.program_id(1) == 0)
    def _():
        carry_ref[...] = jnp.zeros_like(carry_ref)
        m_ref[...] = jnp.zeros_like(m_ref)
        s_ref[...] = jnp.zeros_like(s_ref)

    def rows(x, bi):
        return x[bi * C:(bi + 1) * C]

    cos = jnp.concatenate([cos_ref[...]] * NB, axis=0)
    sin = jnp.concatenate([sin_ref[...]] * NB, axis=0)
    half = (lax.broadcasted_iota(jnp.int32, (R, RET_QK), 1) & (RET_DK - 1)) < RET_DK // 2

    def rope(x):
        swapped = jnp.where(half, pltpu.roll(x, RET_QK - RET_DK // 2, 1), pltpu.roll(x, RET_DK // 2, 1))
        return x * cos + swapped * sin

    q = rope(p_ref[:, :, 0:RET_QK].reshape(R, RET_QK))
    k = rope(p_ref[:, :, RET_QK:2 * RET_QK].reshape(R, RET_QK))
    v = p_ref[:, :, 2 * RET_QK:2 * RET_QK + RET_WIDTH].reshape(R, RET_WIDTH)
    keep_sq = _same_block((4 * C, GROUP), C, RWKV_HEAD)
    keep_v = _same_block((4 * C, RET_WIDTH), C, RET_DV)
    rets, s_new = _retention_chunks([rows(q, bi) for bi in range(NB)], [rows(k, bi) for bi in range(NB)],
                                    [rows(v, bi) for bi in range(NB)], [s_ref[bi] for bi in range(NB)],
                                    dec_ref[0], dec_ref[1], dec_ref[2], cdec_ref[...], keep_sq, keep_v)
    for bi in range(NB):
        s_ref[bi] = s_new[bi]
    ret = jnp.concatenate(rets, axis=0)
    rgn = rgn_ref[...]
    for h in range(RET_HEADS):
        sl = slice(h * RET_DV, (h + 1) * RET_DV)
        xh = ret[:, sl]
        d = xh - jnp.mean(xh, axis=-1, keepdims=True)
        var = jnp.mean(d * d, axis=-1, keepdims=True)
        gh = p_ref[:, :, 2 * RET_QK + RET_WIDTH + h * RET_DV:2 * RET_QK + RET_WIDTH + (h + 1) * RET_DV]
        gh = gh.reshape(R, RET_DV)
        yh = gh * _sigmoid(gh) * (d * lax.rsqrt(var + RET_GN_EPS) * rgn[:, sl])
        y_ref[:, :, sl] = yh.reshape(NB, C, RET_DV).astype(y_ref.dtype)

    pr = p_ref[:, :, RET_COLS:IN_COLS].reshape(R, RWKV_COLS)
    row = lax.broadcasted_iota(jnp.int32, (R, RWKV_COLS), 0)
    prev = pltpu.roll(pr, 1, 0)
    for bi in range(NB):
        prev = jnp.where(row == bi * C, carry_ref[bi], prev)
        carry_ref[bi] = pr[(bi + 1) * C - 1:(bi + 1) * C, :]
    ps = pr + mu_ref[...] * (prev - pr)
    r = ps[:, 0:W]
    kr = ps[:, W:2 * W]
    vr = ps[:, 2 * W:3 * W]
    g_rw = ps[:, 3 * W:4 * W]
    xwa = ps[:, 4 * W:4 * W + 2 * LORA]
    lane = lax.broadcasted_iota(jnp.int32, xwa.shape, 1)
    lora = _mm(jnp.where(lane < LORA, jnp.tanh(xwa), xwa), lora_ref[...])
    w0, a0, k_k, k_a, r_k, gn_g, gn_b = (vec_ref[i:i + 1, :] for i in range(7))
    lw = -np.float32(np.exp(-0.5)) * _sigmoid(w0 + lora[:, :W])
    a = _sigmoid(a0 + lora[:, W:])
    seg = seg_ref[...]

    def segsum(x):
        xs = jnp.concatenate([x[:, g * GROUP:(g + 1) * GROUP] for g in range(NG)], axis=0)
        tot = jnp.dot(xs.astype(bf16), seg, preferred_element_type=f32)
        return jnp.concatenate([tot[g * R:(g + 1) * R] for g in range(NG)], axis=1)

    kk = kr * k_k
    kkn = kk * lax.rsqrt(jnp.maximum(segsum(kk * kk), 1e-24))
    kmod = kr * (1.0 + (a - 1.0) * k_a)

    chains = [(bi, g) for bi in range(NB) for g in range(NG)]

    def pick(x):
        return [x[bi * C:(bi + 1) * C, g * GROUP:(g + 1) * GROUP] for bi, g in chains]

    outs, m_new = _rwkv_chunks(pick(r), pick(kmod), pick(vr), pick(kkn), pick(a), pick(lw),
                               [m_ref[bi, g] for bi, g in chains], keep_sq)
    for i, (bi, g) in enumerate(chains):
        m_ref[bi, g] = m_new[i]
    o = jnp.concatenate([jnp.concatenate(outs[bi * NG:(bi + 1) * NG], axis=1) for bi in range(NB)], axis=0)
    d = o - segsum(o) * (1.0 / RWKV_HEAD)
    var = segsum(d * d) * (1.0 / RWKV_HEAD)
    o = d * lax.rsqrt(var + RWKV_GN_EPS) * gn_g + gn_b
    bonus = segsum(r * kmod * r_k) * vr
    y_rw = g_rw * _sigmoid(g_rw) * (o + bonus)
    y_ref[:, :, RET_WIDTH:] = y_rw.reshape(NB, C, W).astype(y_ref.dtype)


def _rope_tables(seq):
    half = RET_DK // 2
    expo = -jnp.arange(half, dtype=f32) / f32(half)
    freqs = jnp.exp(expo * f32(np.log(ROPE_BASE)))
    ang = jnp.arange(seq, dtype=jnp.int32).astype(f32)[:, None] * freqs[None, :]
    cos = jnp.cos(ang)
    sin = jnp.sin(ang)
    cos_full = jnp.tile(jnp.concatenate([cos, cos], axis=1), (1, RET_HEADS))
    sin_signed = jnp.tile(jnp.concatenate([-sin, sin], axis=1), (1, RET_HEADS))
    return cos_full, sin_signed


def _retention_constants():
    C = CHUNK
    lg = np.log(1.0 - np.exp2(-5.0 - np.arange(RET_HEADS, dtype=np.float64)))
    lane_lg = np.repeat(lg, RET_DK)[None, :]
    n = np.arange(C, dtype=np.float64)[:, None]
    m = np.tile(np.arange(C, dtype=np.float64), RET_HEADS)[None, :]
    scale = RET_DK ** -0.5
    decay4 = scale * np.exp(lane_lg * np.abs(n - m))
    qdec = np.exp(lane_lg * (n + 1.0)) * np.ones((1, RET_QK))
    kdec = scale * np.exp(lane_lg * (C - 1.0 - n)) * np.ones((1, RET_QK))
    dec = np.stack([decay4, qdec, kdec]).astype(np.float32)
    ri = np.arange(RET_QK)[:, None] // RET_DK
    ci = np.arange(RET_WIDTH)[None, :] // RET_DV
    cdec = np.where(ri == ci, np.exp(np.repeat(lg, RET_DK) * C)[:, None], 0.0).astype(np.float32)
    return jnp.asarray(dec), jnp.asarray(cdec)


def _const_spec(shape):
    return pl.BlockSpec(shape, lambda *_: (0,) * len(shape))


@jax.jit
def kernel(x, norm_g, w_in, ret_gn_g, rwkv_mu, w_lora_up, w0, a_lora_up, a0, k_k, k_a, r_k,
           rwkv_gn_g, rwkv_gn_b, w_out, final_norm_g):
    B, T, D = x.shape
    assert D == D_MODEL and T % CHUNK == 0 and (B * T) % PROJ_TILE == 0 and B % BATCH_BLOCK == 0
    assert norm_g.shape[0] == 1, "single-layer block"
    n_tok = B * T
    xf = x.reshape(n_tok, D)
    params = pltpu.CompilerParams(dimension_semantics=("arbitrary",), vmem_limit_bytes=VMEM_LIMIT)

    p = pl.pallas_call(
        _in_proj_kernel,
        grid=(n_tok // PROJ_TILE,),
        in_specs=[pl.BlockSpec((PROJ_TILE, D), lambda i: (i, 0)),
                  _const_spec((1, D)),
                  _const_spec((D, IN_COLS))],
        out_specs=pl.BlockSpec((PROJ_TILE, IN_COLS), lambda i: (i, 0)),
        out_shape=jax.ShapeDtypeStruct((n_tok, IN_COLS), f32),
        compiler_params=params,
        name="in_proj",
    )(xf, norm_g[0][None, :], w_in[0].astype(bf16))

    cos, sin = _rope_tables(T)
    dec, cdec = _retention_constants()
    zeros = jnp.zeros((LORA, RWKV_WIDTH), f32)
    lora_w = jnp.concatenate(
        [jnp.concatenate([w_lora_up[0], zeros], axis=1),
         jnp.concatenate([zeros, a_lora_up[0]], axis=1)], axis=0).astype(bf16)
    vecs = jnp.stack([w0[0], a0[0], k_k[0], k_a[0], r_k[0].reshape(-1), rwkv_gn_g[0], rwkv_gn_b[0],
                      jnp.zeros((RWKV_WIDTH,), f32)])
    seg_ids = np.arange(GROUP) // RWKV_HEAD
    seg = jnp.asarray(seg_ids[:, None] == seg_ids[None, :], bf16)

    y = pl.pallas_call(
        _mixer_kernel,
        grid=(B // BATCH_BLOCK, T // CHUNK),
        in_specs=[pl.BlockSpec((BATCH_BLOCK, CHUNK, IN_COLS), lambda b, c: (b, c, 0)),
                  pl.BlockSpec((CHUNK, RET_QK), lambda b, c: (c, 0)),
                  pl.BlockSpec((CHUNK, RET_QK), lambda b, c: (c, 0)),
                  _const_spec((3, CHUNK, RET_QK)),
                  _const_spec((RET_QK, RET_WIDTH)),
                  _const_spec((1, RWKV_COLS)),
                  _const_spec((2 * LORA, 2 * RWKV_WIDTH)),
                  _const_spec((8, RWKV_WIDTH)),
                  _const_spec((1, RET_WIDTH)),
                  _const_spec((GROUP, GROUP))],
        out_specs=pl.BlockSpec((BATCH_BLOCK, CHUNK, D), lambda b, c: (b, c, 0)),
        out_shape=jax.ShapeDtypeStruct((B, T, D), bf16),
        scratch_shapes=[pltpu.VMEM((BATCH_BLOCK, 1, RWKV_COLS), f32),
                        pltpu.VMEM((BATCH_BLOCK, RWKV_WIDTH // GROUP, GROUP, GROUP), f32),
                        pltpu.VMEM((BATCH_BLOCK, RET_QK, RET_WIDTH), f32)],
        compiler_params=pltpu.CompilerParams(dimension_semantics=("arbitrary", "arbitrary"),
                                             vmem_limit_bytes=VMEM_LIMIT),
        name="mixers",
    )(p.reshape(B, T, IN_COLS), cos, sin, dec, cdec, rwkv_mu[0][None, :], lora_w, vecs,
      ret_gn_g[0][None, :], seg)

    out = pl.pallas_call(
        _out_proj_kernel,
        grid=(n_tok // PROJ_TILE,),
        in_specs=[pl.BlockSpec((PROJ_TILE, D), lambda i: (i, 0)),
                  pl.BlockSpec((PROJ_TILE, D), lambda i: (i, 0)),
                  _const_spec((D, D)),
                  _const_spec((1, D))],
        out_specs=pl.BlockSpec((PROJ_TILE, D), lambda i: (i, 0)),
        out_shape=jax.ShapeDtypeStruct((n_tok, D), f32),
        compiler_params=params,
        name="out_proj",
    )(xf, y.reshape(n_tok, D), w_out[0].astype(bf16), final_norm_g[None, :])
    return out.reshape(B, T, D)
```

```python
import numpy as np
import jax
import jax.numpy as jnp
from jax import lax
from jax.experimental import pallas as pl
from jax.experimental.pallas import tpu as pltpu

D_MODEL = 1024
CHUNK = 64
RET_HEADS = 4
RET_DK = 64
RET_DV = 128
RET_QK = RET_HEADS * RET_DK
RET_WIDTH = RET_HEADS * RET_DV
RWKV_WIDTH = 512
RWKV_HEAD = 64
LORA = 64
RET_COLS = 2 * RET_QK + 2 * RET_WIDTH
RWKV_COLS = 4 * RWKV_WIDTH + 2 * LORA
IN_COLS = RET_COLS + RWKV_COLS
ROPE_BASE = 10000.0
RMS_EPS = 1e-6
RET_GN_EPS = 1e-5
RWKV_GN_EPS = 64e-5
GROUP = 256
PROJ_TILE = 512
BATCH_BLOCK = 8
MIX_TASKS = 1
VMEM_LIMIT = 56 * 1024 * 1024

f32 = jnp.float32
bf16 = jnp.bfloat16


def _mm(a, b):
    return jnp.dot(a.astype(bf16), b.astype(bf16), preferred_element_type=f32)


def _mm_nt(a, b):
    return lax.dot_general(a.astype(bf16), b.astype(bf16), (((1,), (1,)), ((), ())),
                           preferred_element_type=f32)


def _mm_tn(a, b):
    return lax.dot_general(a.astype(bf16), b.astype(bf16), (((0,), (0,)), ((), ())),
                           preferred_element_type=f32)


def _split3(x):
    hi = x.astype(bf16)
    r1 = x - hi.astype(f32)
    mid = r1.astype(bf16)
    lo = (r1 - mid.astype(f32)).astype(bf16)
    return hi, mid, lo


def _same_block(shape, row_width, col_width):
    ri = lax.broadcasted_iota(jnp.int32, shape, 0) >> (row_width.bit_length() - 1)
    ci = lax.broadcasted_iota(jnp.int32, shape, 1) >> (col_width.bit_length() - 1)
    return ri == ci


def _block_diag(x):
    x = x.astype(bf16)
    rows, lanes = x.shape
    zero = jnp.zeros((rows, 128), bf16)
    cols = []
    if lanes == 4 * 128:
        for j in range(4):
            cols.append(jnp.concatenate([x[:, j * 128:(j + 1) * 128] if i == j else zero for i in range(4)],
                                        axis=0))
    else:
        low = lax.broadcasted_iota(jnp.int32, (rows, 128), 1) < 64
        for j in range(2):
            xj = x[:, j * 128:(j + 1) * 128]
            pair = [jnp.where(low, xj, zero), jnp.where(low, zero, xj)]
            cols.append(jnp.concatenate([zero] * (2 * j) + pair + [zero] * (2 - 2 * j), axis=0))
    return jnp.concatenate(cols, axis=1)


def _sigmoid(x):
    return 1.0 / (1.0 + jnp.exp(-x))


def _in_proj_kernel(x_ref, g_ref, w_ref, p_ref):
    x = x_ref[...]
    u = x * lax.rsqrt(jnp.mean(x * x, axis=-1, keepdims=True) + RMS_EPS) * g_ref[...]
    p_ref[...] = jnp.dot(u.astype(bf16), w_ref[...], preferred_element_type=f32)


def _out_proj_kernel(x_ref, y_ref, w_ref, g_ref, o_ref):
    h = x_ref[...] + jnp.dot(y_ref[...], w_ref[...], preferred_element_type=f32)
    o_ref[...] = h * lax.rsqrt(jnp.mean(h * h, axis=-1, keepdims=True) + RMS_EPS) * g_ref[...]


def _retention_chunks(qs, ks, vs, s_bds, decay4, qdec, kdec, cdec, keep_v):
    n = range(len(qs))
    scores = [_mm_nt(qs[i], _block_diag(ks[i])) * decay4 for i in n]
    intra = [_mm(scores[i], _block_diag(vs[i])) for i in n]
    inter = [_mm(qs[i] * qdec, s_bds[i]) for i in n]
    kv = [_mm_tn(ks[i] * kdec, vs[i]) for i in n]
    s_new = [s_bds[i] * cdec + jnp.where(keep_v, kv[i], 0.0) for i in n]
    return [intra[i] + inter[i] for i in n], s_new


def _rwkv_chunks(rs, ks, vs, kkns, aas, lws, m_bds, keep):
    C = CHUNK
    n = range(len(rs))
    ti = lax.broadcasted_iota(jnp.int32, (C, C), 0)
    si = lax.broadcasted_iota(jnp.int32, (C, C), 1)
    tri_incl = jnp.where(si <= ti, 1.0, 0.0).astype(bf16)
    parts = [_split3(lws[i]) for i in n]
    cum = [sum(jnp.dot(tri_incl, part, preferred_element_type=f32) for part in parts[i]) for i in n]
    g_inc = [jnp.exp(cum[i]) for i in n]
    g_exc = [jnp.exp(cum[i] - lws[i]) for i in n]
    g_inv = [jnp.exp(-cum[i]) for i in n]
    g_end = [jnp.exp(cum[i][C - 1:C, :] - cum[i]) for i in n]
    beta = [kkns[i] * aas[i] for i in n]
    lhs = [jnp.concatenate([-kkns[i] * g_exc[i], rs[i] * g_inc[i]], axis=0) for i in n]
    amat = [_mm_nt(lhs[i], jnp.concatenate([_block_diag(beta[i] * g_inv[i]),
                                            _block_diag(ks[i] * g_inv[i])], axis=0)) for i in n]
    ti = lax.broadcasted_iota(jnp.int32, (C, GROUP), 0)
    si = lax.broadcasted_iota(jnp.int32, (C, GROUP), 1) & (C - 1)
    strict = si < ti
    incl = si <= ti
    a_ab = [jnp.where(strict, amat[i][:C, :GROUP], 0.0) for i in n]
    a_ak = [jnp.where(strict, amat[i][:C, GROUP:], 0.0) for i in n]
    a_rb = [jnp.where(incl, amat[i][C:, :GROUP], 0.0) for i in n]
    a_rk = [jnp.where(incl, amat[i][C:, GROUP:], 0.0) for i in n]
    mv = [_mm_nt(lhs[i], m_bds[i]) for i in n]
    av = [_mm(jnp.concatenate([a_ak[i], a_rk[i]], axis=0), _block_diag(vs[i])) for i in n]
    yield
    eye = jnp.where(si == ti, 1.0, 0.0)
    t_inv = [eye + a_ab[i] for i in n]
    s_pow = [_mm(a_ab[i], _block_diag(a_ab[i])) for i in n]
    yield
    for _ in range(4):
        st = [_mm(jnp.concatenate([s_pow[i], t_inv[i]], axis=0), _block_diag(s_pow[i])) for i in n]
        s_pow = [st[i][:C] for i in n]
        t_inv = [t_inv[i] + st[i][C:] for i in n]
        yield
    t_inv = [t_inv[i] + _mm(t_inv[i], _block_diag(s_pow[i])) for i in n]
    yield
    u = [_mm(t_inv[i], _block_diag(mv[i][:C] + av[i][:C])) for i in n]
    out = [mv[i][C:] + av[i][C:] + _mm(a_rb[i], _block_diag(u[i])) for i in n]
    yield
    upd = [_mm_tn(jnp.concatenate([u[i], vs[i]], axis=0),
                  jnp.concatenate([beta[i] * g_end[i], ks[i] * g_end[i]], axis=0)) for i in n]
    m_new = [m_bds[i] * g_inc[i][C - 1:C, :] + jnp.where(keep, upd[i], 0.0) for i in n]
    return out, m_new


def _mix_task(b0, nb, p_ref, cos_ref, sin_ref, dec_ref, cdec_ref, mu_ref, lora_ref, vec_ref, rgn_ref,
              seg_ref, y_ref, carry_ref, m_ref, s_ref):
    C = CHUNK
    R = nb * C
    W = RWKV_WIDTH
    NG = W // GROUP
    bsl = slice(b0, b0 + nb)

    def rows(x, bi):
        return x[bi * C:(bi + 1) * C]

    pr = p_ref[bsl, :, RET_COLS:IN_COLS].reshape(R, RWKV_COLS)
    row = lax.broadcasted_iota(jnp.int32, (R, RWKV_COLS), 0)
    prev = pltpu.roll(pr, 1, 0)
    for bi in range(nb):
        prev = jnp.where(row == bi * C, carry_ref[b0 + bi], prev)
        carry_ref[b0 + bi] = pr[(bi + 1) * C - 1:(bi + 1) * C, :]
    ps = pr + mu_ref[...] * (prev - pr)
    r = ps[:, 0:W]
    kr = ps[:, W:2 * W]
    vr = ps[:, 2 * W:3 * W]
    g_rw = ps[:, 3 * W:4 * W]
    xwa = ps[:, 4 * W:4 * W + 2 * LORA]
    lane = lax.broadcasted_iota(jnp.int32, xwa.shape, 1)
    lora = _mm(jnp.where(lane < LORA, jnp.tanh(xwa), xwa), lora_ref[...])
    w0, a0, k_k, k_a, r_k, gn_g, gn_b = (vec_ref[i:i + 1, :] for i in range(7))
    lw = -np.float32(np.exp(-0.5)) * _sigmoid(w0 + lora[:, :W])
    a = _sigmoid(a0 + lora[:, W:])
    seg = seg_ref[...]
    yield

    def segsum(x):
        xs = jnp.concatenate([x[:, g * GROUP:(g + 1) * GROUP] for g in range(NG)], axis=0)
        tot = jnp.dot(xs.astype(bf16), seg, preferred_element_type=f32)
        return jnp.concatenate([tot[g * R:(g + 1) * R] for g in range(NG)], axis=1)

    kk = kr * k_k
    kkn = kk * lax.rsqrt(jnp.maximum(segsum(kk * kk), 1e-24))
    kmod = kr * (1.0 + (a - 1.0) * k_a)
    yield

    cos = jnp.concatenate([cos_ref[...]] * nb, axis=0)
    sin = jnp.concatenate([sin_ref[...]] * nb, axis=0)
    half = (lax.broadcasted_iota(jnp.int32, (R, RET_QK), 1) & (RET_DK - 1)) < RET_DK // 2

    def rope(x):
        swapped = jnp.where(half, pltpu.roll(x, RET_QK - RET_DK // 2, 1), pltpu.roll(x, RET_DK // 2, 1))
        return x * cos + swapped * sin

    q = rope(p_ref[bsl, :, 0:RET_QK].reshape(R, RET_QK))
    k = rope(p_ref[bsl, :, RET_QK:2 * RET_QK].reshape(R, RET_QK))
    v = p_ref[bsl, :, 2 * RET_QK:2 * RET_QK + RET_WIDTH].reshape(R, RET_WIDTH)
    keep_sq = _same_block((4 * C, GROUP), C, RWKV_HEAD)
    keep_v = _same_block((4 * C, RET_WIDTH), C, RET_DV)
    rets, s_new = _retention_chunks([rows(q, bi) for bi in range(nb)], [rows(k, bi) for bi in range(nb)],
                                    [rows(v, bi) for bi in range(nb)], [s_ref[b0 + bi] for bi in range(nb)],
                                    dec_ref[0], dec_ref[1], dec_ref[2], cdec_ref[...], keep_v)
    for bi in range(nb):
        s_ref[b0 + bi] = s_new[bi]
    ret = jnp.concatenate(rets, axis=0)
    rgn = rgn_ref[...]
    for h in range(RET_HEADS):
        sl = slice(h * RET_DV, (h + 1) * RET_DV)
        xh = ret[:, sl]
        d = xh - jnp.mean(xh, axis=-1, keepdims=True)
        var = jnp.mean(d * d, axis=-1, keepdims=True)
        gh = p_ref[bsl, :, 2 * RET_QK + RET_WIDTH + h * RET_DV:2 * RET_QK + RET_WIDTH + (h + 1) * RET_DV]
        gh = gh.reshape(R, RET_DV)
        yh = gh * _sigmoid(gh) * (d * lax.rsqrt(var + RET_GN_EPS) * rgn[:, sl])
        y_ref[bsl, :, sl] = yh.reshape(nb, C, RET_DV).astype(y_ref.dtype)
    yield

    chains = [(bi, g) for bi in range(nb) for g in range(NG)]

    def pick(x):
        return [x[bi * C:(bi + 1) * C, g * GROUP:(g + 1) * GROUP] for bi, g in chains]

    outs, m_new = yield from _rwkv_chunks(pick(r), pick(kmod), pick(vr), pick(kkn), pick(a), pick(lw),
                                          [m_ref[b0 + bi, g] for bi, g in chains], keep_sq)
    for i, (bi, g) in enumerate(chains):
        m_ref[b0 + bi, g] = m_new[i]
    yield
    o = jnp.concatenate([jnp.concatenate(outs[bi * NG:(bi + 1) * NG], axis=1) for bi in range(nb)], axis=0)
    d = o - segsum(o) * (1.0 / RWKV_HEAD)
    var = segsum(d * d) * (1.0 / RWKV_HEAD)
    o = d * lax.rsqrt(var + RWKV_GN_EPS) * gn_g + gn_b
    yield
    bonus = segsum(r * kmod * r_k) * vr
    y_rw = g_rw * _sigmoid(g_rw) * (o + bonus)
    y_ref[bsl, :, RET_WIDTH:] = y_rw.reshape(nb, C, W).astype(y_ref.dtype)


def _run_staggered(tasks, lead):
    live = {}
    rnd = 0
    while live or rnd <= lead * (len(tasks) - 1):
        if rnd % lead == 0 and rnd // lead < len(tasks):
            live[rnd // lead] = tasks[rnd // lead]
        for t in sorted(live):
            try:
                next(live[t])
            except StopIteration:
                del live[t]
        rnd += 1


def _mixer_kernel(p_ref, cos_ref, sin_ref, dec_ref, cdec_ref, mu_ref, lora_ref, vec_ref, rgn_ref,
                  seg_ref, y_ref, carry_ref, m_ref, s_ref):
    @pl.when(pl.program_id(1) == 0)
    def _():
        carry_ref[...] = jnp.zeros_like(carry_ref)
        m_ref[...] = jnp.zeros_like(m_ref)
        s_ref[...] = jnp.zeros_like(s_ref)

    nb = BATCH_BLOCK // MIX_TASKS
    tasks = [_mix_task(t * nb, nb, p_ref, cos_ref, sin_ref, dec_ref, cdec_ref, mu_ref, lora_ref, vec_ref,
                       rgn_ref, seg_ref, y_ref, carry_ref, m_ref, s_ref) for t in range(MIX_TASKS)]
    _run_staggered(tasks, lead=4)


def _rope_tables(seq):
    half = RET_DK // 2
    expo = -jnp.arange(half, dtype=f32) / f32(half)
    freqs = jnp.exp(expo * f32(np.log(ROPE_BASE)))
    ang = jnp.arange(seq, dtype=jnp.int32).astype(f32)[:, None] * freqs[None, :]
    cos = jnp.cos(ang)
    sin = jnp.sin(ang)
    cos_full = jnp.tile(jnp.concatenate([cos, cos], axis=1), (1, RET_HEADS))
    sin_signed = jnp.tile(jnp.concatenate([-sin, sin], axis=1), (1, RET_HEADS))
    return cos_full, sin_signed


def _retention_constants():
    C = CHUNK
    lg = np.log(1.0 - np.exp2(-5.0 - np.arange(RET_HEADS, dtype=np.float64)))
    lane_lg = np.repeat(lg, RET_DK)[None, :]
    n = np.arange(C, dtype=np.float64)[:, None]
    m = np.tile(np.arange(C, dtype=np.float64), RET_HEADS)[None, :]
    scale = RET_DK ** -0.5
    decay4 = scale * np.exp(lane_lg * np.abs(n - m))
    qdec = np.exp(lane_lg * (n + 1.0)) * np.ones((1, RET_QK))
    kdec = scale * np.exp(lane_lg * (C - 1.0 - n)) * np.ones((1, RET_QK))
    dec = np.stack([decay4, qdec, kdec]).astype(np.float32)
    ri = np.arange(RET_QK)[:, None] // RET_DK
    ci = np.arange(RET_WIDTH)[None, :] // RET_DV
    cdec = np.where(ri == ci, np.exp(np.repeat(lg, RET_DK) * C)[:, None], 0.0).astype(np.float32)
    return jnp.asarray(dec), jnp.asarray(cdec)


def _const_spec(shape):
    return pl.BlockSpec(shape, lambda *_: (0,) * len(shape))


@jax.jit
def kernel(x, norm_g, w_in, ret_gn_g, rwkv_mu, w_lora_up, w0, a_lora_up, a0, k_k, k_a, r_k,
           rwkv_gn_g, rwkv_gn_b, w_out, final_norm_g):
    B, T, D = x.shape
    assert D == D_MODEL and T % CHUNK == 0 and (B * T) % PROJ_TILE == 0 and B % BATCH_BLOCK == 0
    assert norm_g.shape[0] == 1, "single-layer block"
    n_tok = B * T
    xf = x.reshape(n_tok, D)
    params = pltpu.CompilerParams(dimension_semantics=("arbitrary",), vmem_limit_bytes=VMEM_LIMIT)

    p = pl.pallas_call(
        _in_proj_kernel,
        grid=(n_tok // PROJ_TILE,),
        in_specs=[pl.BlockSpec((PROJ_TILE, D), lambda i: (i, 0)),
                  _const_spec((1, D)),
                  _const_spec((D, IN_COLS))],
        out_specs=pl.BlockSpec((PROJ_TILE, IN_COLS), lambda i: (i, 0)),
        out_shape=jax.ShapeDtypeStruct((n_tok, IN_COLS), f32),
        compiler_params=params,
        name="in_proj",
    )(xf, norm_g[0][None, :], w_in[0].astype(bf16))

    cos, sin = _rope_tables(T)
    dec, cdec = _retention_constants()
    zeros = jnp.zeros((LORA, RWKV_WIDTH), f32)
    lora_w = jnp.concatenate(
        [jnp.concatenate([w_lora_up[0], zeros], axis=1),
         jnp.concatenate([zeros, a_lora_up[0]], axis=1)], axis=0).astype(bf16)
    vecs = jnp.stack([w0[0], a0[0], k_k[0], k_a[0], r_k[0].reshape(-1), rwkv_gn_g[0], rwkv_gn_b[0],
                      jnp.zeros((RWKV_WIDTH,), f32)])
    seg_ids = np.arange(GROUP) // RWKV_HEAD
    seg = jnp.asarray(seg_ids[:, None] == seg_ids[None, :], bf16)

    y = pl.pallas_call(
        _mixer_kernel,
        grid=(B // BATCH_BLOCK, T // CHUNK),
        in_specs=[pl.BlockSpec((BATCH_BLOCK, CHUNK, IN_COLS), lambda b, c: (b, c, 0)),
                  pl.BlockSpec((CHUNK, RET_QK), lambda b, c: (c, 0)),
                  pl.BlockSpec((CHUNK, RET_QK), lambda b, c: (c, 0)),
                  _const_spec((3, CHUNK, RET_QK)),
                  _const_spec((RET_QK, RET_WIDTH)),
                  _const_spec((1, RWKV_COLS)),
                  _const_spec((2 * LORA, 2 * RWKV_WIDTH)),
                  _const_spec((8, RWKV_WIDTH)),
                  _const_spec((1, RET_WIDTH)),
                  _const_spec((GROUP, GROUP))],
        out_specs=pl.BlockSpec((BATCH_BLOCK, CHUNK, D), lambda b, c: (b, c, 0)),
        out_shape=jax.ShapeDtypeStruct((B, T, D), bf16),
        scratch_shapes=[pltpu.VMEM((BATCH_BLOCK, 1, RWKV_COLS), f32),
                        pltpu.VMEM((BATCH_BLOCK, RWKV_WIDTH // GROUP, GROUP, GROUP), f32),
                        pltpu.VMEM((BATCH_BLOCK, RET_QK, RET_WIDTH), f32)],
        compiler_params=pltpu.CompilerParams(dimension_semantics=("arbitrary", "arbitrary"),
                                             vmem_limit_bytes=VMEM_LIMIT),
        name="mixers",
    )(p.reshape(B, T, IN_COLS), cos, sin, dec, cdec, rwkv_mu[0][None, :], lora_w, vecs,
      ret_gn_g[0][None, :], seg)

    out = pl.pallas_call(
        _out_proj_kernel,
        grid=(n_tok // PROJ_TILE,),
        in_specs=[pl.BlockSpec((PROJ_TILE, D), lambda i: (i, 0)),
                  pl.BlockSpec((PROJ_TILE, D), lambda i: (i, 0)),
                  _const_spec((D, D)),
                  _const_spec((1, D))],
        out_specs=pl.BlockSpec((PROJ_TILE, D), lambda i: (i, 0)),
        out_shape=jax.ShapeDtypeStruct((n_tok, D), f32),
        compiler_params=params,
        name="out_proj",
    )(xf, y.reshape(n_tok, D), w_out[0].astype(bf16), final_norm_g[None, :])
    return out.reshape(B, T, D)
```

```python
import numpy as np
import jax
import jax.numpy as jnp
from jax import lax
from jax.experimental import pallas as pl
from jax.experimental.pallas import tpu as pltpu

D_MODEL = 1024
CHUNK = 64
RET_HEADS = 4
RET_DK = 64
RET_DV = 128
RET_QK = RET_HEADS * RET_DK
RET_WIDTH = RET_HEADS * RET_DV
RWKV_WIDTH = 512
RWKV_HEAD = 64
LORA = 64
RET_COLS = 2 * RET_QK + 2 * RET_WIDTH
RWKV_COLS = 4 * RWKV_WIDTH + 2 * LORA
IN_COLS = RET_COLS + RWKV_COLS
ROPE_BASE = 10000.0
RMS_EPS = 1e-6
RET_GN_EPS = 1e-5
RWKV_GN_EPS = 64e-5
GROUP = 256
PROJ_TILE = 512
BATCH_BLOCK = 8
MIX_TASKS = 1
VMEM_LIMIT = 56 * 1024 * 1024

f32 = jnp.float32
bf16 = jnp.bfloat16


def _mm(a, b):
    return jnp.dot(a.astype(bf16), b.astype(bf16), preferred_element_type=f32)


def _mm_nt(a, b):
    return lax.dot_general(a.astype(bf16), b.astype(bf16), (((1,), (1,)), ((), ())),
                           preferred_element_type=f32)


def _mm_tn(a, b):
    return lax.dot_general(a.astype(bf16), b.astype(bf16), (((0,), (0,)), ((), ())),
                           preferred_element_type=f32)


def _split3(x):
    hi = x.astype(bf16)
    r1 = x - hi.astype(f32)
    mid = r1.astype(bf16)
    lo = (r1 - mid.astype(f32)).astype(bf16)
    return hi, mid, lo


def _same_block(shape, row_width, col_width):
    ri = lax.broadcasted_iota(jnp.int32, shape, 0) >> (row_width.bit_length() - 1)
    ci = lax.broadcasted_iota(jnp.int32, shape, 1) >> (col_width.bit_length() - 1)
    return ri == ci


def _block_diag(x):
    x = x.astype(bf16)
    rows, lanes = x.shape
    zero = jnp.zeros((rows, 128), bf16)
    cols = []
    if lanes == 4 * 128:
        for j in range(4):
            cols.append(jnp.concatenate([x[:, j * 128:(j + 1) * 128] if i == j else zero for i in range(4)],
                                        axis=0))
    else:
        low = lax.broadcasted_iota(jnp.int32, (rows, 128), 1) < 64
        for j in range(2):
            xj = x[:, j * 128:(j + 1) * 128]
            pair = [jnp.where(low, xj, zero), jnp.where(low, zero, xj)]
            cols.append(jnp.concatenate([zero] * (2 * j) + pair + [zero] * (2 - 2 * j), axis=0))
    return jnp.concatenate(cols, axis=1)


def _sigmoid(x):
    return 1.0 / (1.0 + jnp.exp(-x))


def _in_proj_kernel(x_ref, g_ref, w_ref, p_ref):
    x = x_ref[...]
    u = x * lax.rsqrt(jnp.mean(x * x, axis=-1, keepdims=True) + RMS_EPS) * g_ref[...]
    p_ref[...] = jnp.dot(u.astype(bf16), w_ref[...], preferred_element_type=f32)


def _retention_chunks(qs, ks, vs, s_bds, decay4, qdec, kdec, cdec, keep_v):
    n = range(len(qs))
    scores = [_mm_nt(qs[i], _block_diag(ks[i])) * decay4 for i in n]
    intra = [_mm(scores[i], _block_diag(vs[i])) for i in n]
    inter = [_mm(qs[i] * qdec, s_bds[i]) for i in n]
    kv = [_mm_tn(ks[i] * kdec, vs[i]) for i in n]
    s_new = [s_bds[i] * cdec + jnp.where(keep_v, kv[i], 0.0) for i in n]
    return [intra[i] + inter[i] for i in n], s_new


def _rwkv_chunks(rs, ks, vs, kkns, aas, lws, m_bds, keep):
    C = CHUNK
    n = range(len(rs))
    ti = lax.broadcasted_iota(jnp.int32, (C, C), 0)
    si = lax.broadcasted_iota(jnp.int32, (C, C), 1)
    tri_incl = jnp.where(si <= ti, 1.0, 0.0).astype(bf16)
    parts = [_split3(lws[i]) for i in n]
    cum = [sum(jnp.dot(tri_incl, part, preferred_element_type=f32) for part in parts[i]) for i in n]
    g_inc = [jnp.exp(cum[i]) for i in n]
    g_exc = [jnp.exp(cum[i] - lws[i]) for i in n]
    g_inv = [jnp.exp(-cum[i]) for i in n]
    g_end = [jnp.exp(cum[i][C - 1:C, :] - cum[i]) for i in n]
    beta = [kkns[i] * aas[i] for i in n]
    lhs = [jnp.concatenate([-kkns[i] * g_exc[i], rs[i] * g_inc[i]], axis=0) for i in n]
    amat = [_mm_nt(lhs[i], jnp.concatenate([_block_diag(beta[i] * g_inv[i]),
                                            _block_diag(ks[i] * g_inv[i])], axis=0)) for i in n]
    ti = lax.broadcasted_iota(jnp.int32, (C, GROUP), 0)
    si = lax.broadcasted_iota(jnp.int32, (C, GROUP), 1) & (C - 1)
    strict = si < ti
    incl = si <= ti
    a_ab = [jnp.where(strict, amat[i][:C, :GROUP], 0.0) for i in n]
    a_ak = [jnp.where(strict, amat[i][:C, GROUP:], 0.0) for i in n]
    a_rb = [jnp.where(incl, amat[i][C:, :GROUP], 0.0) for i in n]
    a_rk = [jnp.where(incl, amat[i][C:, GROUP:], 0.0) for i in n]
    mv = [_mm_nt(lhs[i], m_bds[i]) for i in n]
    av = [_mm(jnp.concatenate([a_ak[i], a_rk[i]], axis=0), _block_diag(vs[i])) for i in n]
    yield
    eye = jnp.where(si == ti, 1.0, 0.0)
    t_inv = [eye + a_ab[i] for i in n]
    s_pow = [_mm(a_ab[i], _block_diag(a_ab[i])) for i in n]
    yield
    for _ in range(4):
        st = [_mm(jnp.concatenate([s_pow[i], t_inv[i]], axis=0), _block_diag(s_pow[i])) for i in n]
        s_pow = [st[i][:C] for i in n]
        t_inv = [t_inv[i] + st[i][C:] for i in n]
        yield
    t_inv = [t_inv[i] + _mm(t_inv[i], _block_diag(s_pow[i])) for i in n]
    yield
    u = [_mm(t_inv[i], _block_diag(mv[i][:C] + av[i][:C])) for i in n]
    out = [mv[i][C:] + av[i][C:] + _mm(a_rb[i], _block_diag(u[i])) for i in n]
    yield
    upd = [_mm_tn(jnp.concatenate([u[i], vs[i]], axis=0),
                  jnp.concatenate([beta[i] * g_end[i], ks[i] * g_end[i]], axis=0)) for i in n]
    m_new = [m_bds[i] * g_inc[i][C - 1:C, :] + jnp.where(keep, upd[i], 0.0) for i in n]
    return out, m_new


def _mix_task(b0, nb, p_ref, cos_ref, sin_ref, dec_ref, cdec_ref, mu_ref, lora_ref, vec_ref, rgn_ref,
              seg_ref, y_ref, carry_ref, m_ref, s_ref):
    C = CHUNK
    R = nb * C
    W = RWKV_WIDTH
    NG = W // GROUP
    bsl = slice(b0, b0 + nb)

    def rows(x, bi):
        return x[bi * C:(bi + 1) * C]

    pr = p_ref[bsl, :, RET_COLS:IN_COLS].reshape(R, RWKV_COLS)
    row = lax.broadcasted_iota(jnp.int32, (R, RWKV_COLS), 0)
    prev = pltpu.roll(pr, 1, 0)
    for bi in range(nb):
        prev = jnp.where(row == bi * C, carry_ref[b0 + bi], prev)
        carry_ref[b0 + bi] = pr[(bi + 1) * C - 1:(bi + 1) * C, :]
    ps = pr + mu_ref[...] * (prev - pr)
    r = ps[:, 0:W]
    kr = ps[:, W:2 * W]
    vr = ps[:, 2 * W:3 * W]
    g_rw = ps[:, 3 * W:4 * W]
    xwa = ps[:, 4 * W:4 * W + 2 * LORA]
    lane = lax.broadcasted_iota(jnp.int32, xwa.shape, 1)
    lora = _mm(jnp.where(lane < LORA, jnp.tanh(xwa), xwa), lora_ref[...])
    w0, a0, k_k, k_a, r_k, gn_g, gn_b = (vec_ref[i:i + 1, :] for i in range(7))
    lw = -np.float32(np.exp(-0.5)) * _sigmoid(w0 + lora[:, :W])
    a = _sigmoid(a0 + lora[:, W:])
    seg = seg_ref[...]
    yield

    def segsum(x):
        xs = jnp.concatenate([x[:, g * GROUP:(g + 1) * GROUP] for g in range(NG)], axis=0)
        tot = jnp.dot(xs.astype(bf16), seg, preferred_element_type=f32)
        return jnp.concatenate([tot[g * R:(g + 1) * R] for g in range(NG)], axis=1)

    kk = kr * k_k
    kkn = kk * lax.rsqrt(jnp.maximum(segsum(kk * kk), 1e-24))
    kmod = kr * (1.0 + (a - 1.0) * k_a)
    yield

    cos = jnp.concatenate([cos_ref[...]] * nb, axis=0)
    sin = jnp.concatenate([sin_ref[...]] * nb, axis=0)
    half = (lax.broadcasted_iota(jnp.int32, (R, RET_QK), 1) & (RET_DK - 1)) < RET_DK // 2

    def rope(x):
        swapped = jnp.where(half, pltpu.roll(x, RET_QK - RET_DK // 2, 1), pltpu.roll(x, RET_DK // 2, 1))
        return x * cos + swapped * sin

    q = rope(p_ref[bsl, :, 0:RET_QK].reshape(R, RET_QK))
    k = rope(p_ref[bsl, :, RET_QK:2 * RET_QK].reshape(R, RET_QK))
    v = p_ref[bsl, :, 2 * RET_QK:2 * RET_QK + RET_WIDTH].reshape(R, RET_WIDTH)
    keep_sq = _same_block((4 * C, GROUP), C, RWKV_HEAD)
    keep_v = _same_block((4 * C, RET_WIDTH), C, RET_DV)
    rets, s_new = _retention_chunks([rows(q, bi) for bi in range(nb)], [rows(k, bi) for bi in range(nb)],
                                    [rows(v, bi) for bi in range(nb)], [s_ref[b0 + bi] for bi in range(nb)],
                                    dec_ref[0], dec_ref[1], dec_ref[2], cdec_ref[...], keep_v)
    for bi in range(nb):
        s_ref[b0 + bi] = s_new[bi]
    ret = jnp.concatenate(rets, axis=0)
    rgn = rgn_ref[...]
    for h in range(RET_HEADS):
        sl = slice(h * RET_DV, (h + 1) * RET_DV)
        xh = ret[:, sl]
        d = xh - jnp.mean(xh, axis=-1, keepdims=True)
        var = jnp.mean(d * d, axis=-1, keepdims=True)
        gh = p_ref[bsl, :, 2 * RET_QK + RET_WIDTH + h * RET_DV:2 * RET_QK + RET_WIDTH + (h + 1) * RET_DV]
        gh = gh.reshape(R, RET_DV)
        yh = gh * _sigmoid(gh) * (d * lax.rsqrt(var + RET_GN_EPS) * rgn[:, sl])
        y_ref[bsl, :, sl] = yh.reshape(nb, C, RET_DV).astype(y_ref.dtype)
    yield

    chains = [(bi, g) for bi in range(nb) for g in range(NG)]

    def pick(x):
        return [x[bi * C:(bi + 1) * C, g * GROUP:(g + 1) * GROUP] for bi, g in chains]

    outs, m_new = yield from _rwkv_chunks(pick(r), pick(kmod), pick(vr), pick(kkn), pick(a), pick(lw),
                                          [m_ref[b0 + bi, g] for bi, g in chains], keep_sq)
    for i, (bi, g) in enumerate(chains):
        m_ref[b0 + bi, g] = m_new[i]
    yield
    o = jnp.concatenate([jnp.concatenate(outs[bi * NG:(bi + 1) * NG], axis=1) for bi in range(nb)], axis=0)
    d = o - segsum(o) * (1.0 / RWKV_HEAD)
    var = segsum(d * d) * (1.0 / RWKV_HEAD)
    o = d * lax.rsqrt(var + RWKV_GN_EPS) * gn_g + gn_b
    yield
    bonus = segsum(r * kmod * r_k) * vr
    y_rw = g_rw * _sigmoid(g_rw) * (o + bonus)
    y_ref[bsl, :, RET_WIDTH:] = y_rw.reshape(nb, C, W).astype(y_ref.dtype)


def _run_staggered(tasks, lead):
    live = {}
    rnd = 0
    while live or rnd <= lead * (len(tasks) - 1):
        if rnd % lead == 0 and rnd // lead < len(tasks):
            live[rnd // lead] = tasks[rnd // lead]
        for t in sorted(live):
            try:
                next(live[t])
            except StopIteration:
                del live[t]
        rnd += 1


def _out_task(x_ref, y_ref, w_ref, g_ref, o_ref):
    nb, C, D = x_ref.shape
    y = y_ref[...].reshape(nb * C, D)
    piece = GROUP
    hs = []
    for j in range(D // piece):
        sl = slice(j * piece, (j + 1) * piece)
        hs.append(x_ref[:, :, sl].reshape(nb * C, piece)
                  + jnp.dot(y, w_ref[:, sl], preferred_element_type=f32))
        yield
    h = jnp.concatenate(hs, axis=1)
    out = h * lax.rsqrt(jnp.mean(h * h, axis=-1, keepdims=True) + RMS_EPS) * g_ref[...]
    o_ref[...] = out.reshape(nb, C, D)


def _mixer_kernel(p_ref, cos_ref, sin_ref, dec_ref, cdec_ref, mu_ref, lora_ref, vec_ref, rgn_ref,
                  seg_ref, x_ref, wout_ref, fg_ref, o_ref, y_ref, carry_ref, m_ref, s_ref):
    @pl.when(pl.program_id(1) == 0)
    def _():
        carry_ref[...] = jnp.zeros_like(carry_ref)
        m_ref[...] = jnp.zeros_like(m_ref)
        s_ref[...] = jnp.zeros_like(s_ref)
        y_ref[...] = jnp.zeros_like(y_ref)

    nb = BATCH_BLOCK // MIX_TASKS
    tasks = [_out_task(x_ref, y_ref, wout_ref, fg_ref, o_ref)]
    tasks += [_mix_task(t * nb, nb, p_ref, cos_ref, sin_ref, dec_ref, cdec_ref, mu_ref, lora_ref, vec_ref,
                        rgn_ref, seg_ref, y_ref, carry_ref, m_ref, s_ref) for t in range(MIX_TASKS)]
    _run_staggered(tasks, lead=1)


def _rope_tables(seq):
    half = RET_DK // 2
    expo = -jnp.arange(half, dtype=f32) / f32(half)
    freqs = jnp.exp(expo * f32(np.log(ROPE_BASE)))
    ang = jnp.arange(seq, dtype=jnp.int32).astype(f32)[:, None] * freqs[None, :]
    cos = jnp.cos(ang)
    sin = jnp.sin(ang)
    cos_full = jnp.tile(jnp.concatenate([cos, cos], axis=1), (1, RET_HEADS))
    sin_signed = jnp.tile(jnp.concatenate([-sin, sin], axis=1), (1, RET_HEADS))
    return cos_full, sin_signed


def _retention_constants():
    C = CHUNK
    lg = np.log(1.0 - np.exp2(-5.0 - np.arange(RET_HEADS, dtype=np.float64)))
    lane_lg = np.repeat(lg, RET_DK)[None, :]
    n = np.arange(C, dtype=np.float64)[:, None]
    m = np.tile(np.arange(C, dtype=np.float64), RET_HEADS)[None, :]
    scale = RET_DK ** -0.5
    decay4 = scale * np.exp(lane_lg * np.abs(n - m))
    qdec = np.exp(lane_lg * (n + 1.0)) * np.ones((1, RET_QK))
    kdec = scale * np.exp(lane_lg * (C - 1.0 - n)) * np.ones((1, RET_QK))
    dec = np.stack([decay4, qdec, kdec]).astype(np.float32)
    ri = np.arange(RET_QK)[:, None] // RET_DK
    ci = np.arange(RET_WIDTH)[None, :] // RET_DV
    cdec = np.where(ri == ci, np.exp(np.repeat(lg, RET_DK) * C)[:, None], 0.0).astype(np.float32)
    return jnp.asarray(dec), jnp.asarray(cdec)


def _const_spec(shape):
    return pl.BlockSpec(shape, lambda *_: (0,) * len(shape))


@jax.jit
def kernel(x, norm_g, w_in, ret_gn_g, rwkv_mu, w_lora_up, w0, a_lora_up, a0, k_k, k_a, r_k,
           rwkv_gn_g, rwkv_gn_b, w_out, final_norm_g):
    B, T, D = x.shape
    assert D == D_MODEL and T % CHUNK == 0 and (B * T) % PROJ_TILE == 0 and B % BATCH_BLOCK == 0
    assert norm_g.shape[0] == 1, "single-layer block"
    n_tok = B * T
    xf = x.reshape(n_tok, D)
    params = pltpu.CompilerParams(dimension_semantics=("arbitrary",), vmem_limit_bytes=VMEM_LIMIT)

    p = pl.pallas_call(
        _in_proj_kernel,
        grid=(n_tok // PROJ_TILE,),
        in_specs=[pl.BlockSpec((PROJ_TILE, D), lambda i: (i, 0)),
                  _const_spec((1, D)),
                  _const_spec((D, IN_COLS))],
        out_specs=pl.BlockSpec((PROJ_TILE, IN_COLS), lambda i: (i, 0)),
        out_shape=jax.ShapeDtypeStruct((n_tok, IN_COLS), f32),
        compiler_params=params,
        name="in_proj",
    )(xf, norm_g[0][None, :], w_in[0].astype(bf16))

    cos, sin = _rope_tables(T)
    dec, cdec = _retention_constants()
    zeros = jnp.zeros((LORA, RWKV_WIDTH), f32)
    lora_w = jnp.concatenate(
        [jnp.concatenate([w_lora_up[0], zeros], axis=1),
         jnp.concatenate([zeros, a_lora_up[0]], axis=1)], axis=0).astype(bf16)
    vecs = jnp.stack([w0[0], a0[0], k_k[0], k_a[0], r_k[0].reshape(-1), rwkv_gn_g[0], rwkv_gn_b[0],
                      jnp.zeros((RWKV_WIDTH,), f32)])
    seg_ids = np.arange(GROUP) // RWKV_HEAD
    seg = jnp.asarray(seg_ids[:, None] == seg_ids[None, :], bf16)

    n_chunks = T // CHUNK
    last = n_chunks - 1
    return pl.pallas_call(
        _mixer_kernel,
        grid=(B // BATCH_BLOCK, n_chunks + 1),
        in_specs=[pl.BlockSpec((BATCH_BLOCK, CHUNK, IN_COLS), lambda b, c: (b, jnp.minimum(c, last), 0)),
                  pl.BlockSpec((CHUNK, RET_QK), lambda b, c: (jnp.minimum(c, last), 0)),
                  pl.BlockSpec((CHUNK, RET_QK), lambda b, c: (jnp.minimum(c, last), 0)),
                  _const_spec((3, CHUNK, RET_QK)),
                  _const_spec((RET_QK, RET_WIDTH)),
                  _const_spec((1, RWKV_COLS)),
                  _const_spec((2 * LORA, 2 * RWKV_WIDTH)),
                  _const_spec((8, RWKV_WIDTH)),
                  _const_spec((1, RET_WIDTH)),
                  _const_spec((GROUP, GROUP)),
                  pl.BlockSpec((BATCH_BLOCK, CHUNK, D), lambda b, c: (b, jnp.maximum(c - 1, 0), 0)),
                  _const_spec((D, D)),
                  _const_spec((1, D))],
        out_specs=pl.BlockSpec((BATCH_BLOCK, CHUNK, D), lambda b, c: (b, jnp.maximum(c - 1, 0), 0)),
        out_shape=jax.ShapeDtypeStruct((B, T, D), f32),
        scratch_shapes=[pltpu.VMEM((BATCH_BLOCK, CHUNK, D), bf16),
                        pltpu.VMEM((BATCH_BLOCK, 1, RWKV_COLS), f32),
                        pltpu.VMEM((BATCH_BLOCK, RWKV_WIDTH // GROUP, GROUP, GROUP), f32),
                        pltpu.VMEM((BATCH_BLOCK, RET_QK, RET_WIDTH), f32)],
        compiler_params=pltpu.CompilerParams(dimension_semantics=("arbitrary", "arbitrary"),
                                             vmem_limit_bytes=VMEM_LIMIT),
        name="mixers",
    )(p.reshape(B, T, IN_COLS), cos, sin, dec, cdec, rwkv_mu[0][None, :], lora_w, vecs,
      ret_gn_g[0][None, :], seg, x, w_out[0].astype(bf16), final_norm_g[None, :])
```

```python
import numpy as np
import jax
import jax.numpy as jnp
from jax import lax
from jax.experimental import pallas as pl
from jax.experimental.pallas import tpu as pltpu

D_MODEL = 1024
CHUNK = 64
RET_HEADS = 4
RET_DK = 64
RET_DV = 128
RET_QK = RET_HEADS * RET_DK
RET_WIDTH = RET_HEADS * RET_DV
RWKV_WIDTH = 512
RWKV_HEAD = 64
LORA = 64
RET_COLS = 2 * RET_QK + 2 * RET_WIDTH
RWKV_COLS = 4 * RWKV_WIDTH + 2 * LORA
IN_COLS = RET_COLS + RWKV_COLS
ROPE_BASE = 10000.0
RMS_EPS = 1e-6
RET_GN_EPS = 1e-5
RWKV_GN_EPS = 64e-5
GROUP = 256
PROJ_TILE = 512
BATCH_BLOCK = 8
MIX_TASKS = 2
MIX_TASK_LAG = 5000
VMEM_LIMIT = 56 * 1024 * 1024

f32 = jnp.float32
bf16 = jnp.bfloat16


def _mm(a, b):
    return jnp.dot(a.astype(bf16), b.astype(bf16), preferred_element_type=f32)


def _mm_nt(a, b):
    return lax.dot_general(a.astype(bf16), b.astype(bf16), (((1,), (1,)), ((), ())),
                           preferred_element_type=f32)


def _mm_tn(a, b):
    return lax.dot_general(a.astype(bf16), b.astype(bf16), (((0,), (0,)), ((), ())),
                           preferred_element_type=f32)


def _split2(x):
    hi = x.astype(bf16)
    lo = (x - hi.astype(f32)).astype(bf16)
    return hi, lo


def _same_block(shape, row_width, col_width):
    ri = lax.broadcasted_iota(jnp.int32, shape, 0) >> (row_width.bit_length() - 1)
    ci = lax.broadcasted_iota(jnp.int32, shape, 1) >> (col_width.bit_length() - 1)
    return ri == ci


def _block_diag(x):
    x = x.astype(bf16)
    rows, lanes = x.shape
    zero = jnp.zeros((rows, 128), bf16)
    cols = []
    if lanes == 4 * 128:
        for j in range(4):
            cols.append(jnp.concatenate([x[:, j * 128:(j + 1) * 128] if i == j else zero for i in range(4)],
                                        axis=0))
    else:
        low = lax.broadcasted_iota(jnp.int32, (rows, 128), 1) < 64
        for j in range(2):
            xj = x[:, j * 128:(j + 1) * 128]
            pair = [jnp.where(low, xj, zero), jnp.where(low, zero, xj)]
            cols.append(jnp.concatenate([zero] * (2 * j) + pair + [zero] * (2 - 2 * j), axis=0))
    return jnp.concatenate(cols, axis=1)


def _sigmoid(x):
    return 1.0 / (1.0 + jnp.exp(-x))


def _in_proj_kernel(x_ref, g_ref, w_ref, p_ref):
    x = x_ref[...]
    u = x * lax.rsqrt(jnp.mean(x * x, axis=-1, keepdims=True) + RMS_EPS) * g_ref[...]
    p_ref[...] = jnp.dot(u.astype(bf16), w_ref[...], preferred_element_type=f32)


def _weave(tasks):
    clock = [start for _, start in tasks]
    live = list(range(len(tasks)))
    while live:
        i = min(live, key=lambda j: (clock[j], j))
        try:
            clock[i] += next(tasks[i][0])
        except StopIteration:
            live.remove(i)


def _retention_chunk(q, k, v, s_bd, decay4, qdec, kdec, cdec, keep_v):
    scores = _mm_nt(q, _block_diag(k)) * decay4
    out = _mm(scores, _block_diag(v)) + _mm(q * qdec, s_bd)
    kv = _mm_tn(k * kdec, v)
    return out, s_bd * cdec + jnp.where(keep_v, kv, 0.0)


def _rwkv_chunks(rs, ks, vs, kkns, aas, lws, m_bds, keep):
    C = CHUNK
    n = len(rs)
    ti = lax.broadcasted_iota(jnp.int32, (C, C), 0)
    si = lax.broadcasted_iota(jnp.int32, (C, C), 1)
    tri_incl = jnp.where(si <= ti, 1.0, 0.0).astype(bf16)
    tri_incl = jnp.concatenate([tri_incl, tri_incl], axis=1)
    ti = lax.broadcasted_iota(jnp.int32, (C, GROUP), 0)
    si = lax.broadcasted_iota(jnp.int32, (C, GROUP), 1) & (C - 1)
    strict = si < ti
    incl = si <= ti
    eye = jnp.where(si == ti, 1.0, 0.0)
    cum, lhs, amat, decay_end, key_end, a_ab, a_rb, mv, av = ([None] * n for _ in range(9))
    for i in range(n):
        cum[i] = jnp.dot(tri_incl, jnp.concatenate(_split2(lws[i]), axis=0), preferred_element_type=f32)
        yield 50
    for i in range(n):
        g_inc = jnp.exp(cum[i])
        g_inv = jnp.exp(-cum[i])
        g_end = jnp.exp(cum[i][C - 1:C, :] - cum[i])
        beta = kkns[i] * aas[i]
        lhs[i] = jnp.concatenate([-kkns[i] * jnp.exp(cum[i] - lws[i]), rs[i] * g_inc], axis=0)
        amat[i] = _mm_nt(lhs[i], jnp.concatenate([_block_diag(beta * g_inv), _block_diag(ks[i] * g_inv)],
                                                 axis=0))
        decay_end[i] = g_inc[C - 1:C, :]
        key_end[i] = jnp.concatenate([beta * g_end, ks[i] * g_end], axis=0)
        yield 120
    for i in range(n):
        a_ab[i] = jnp.where(strict, amat[i][:C, :GROUP], 0.0)
        a_rb[i] = jnp.where(incl, amat[i][C:, :GROUP], 0.0)
        a_k = jnp.concatenate([jnp.where(strict, amat[i][:C, GROUP:], 0.0),
                               jnp.where(incl, amat[i][C:, GROUP:], 0.0)], axis=0)
        mv[i] = _mm_nt(lhs[i], m_bds[i])
        av[i] = _mm(a_k, _block_diag(vs[i]))
        yield 80
    t_inv, s_pow = [None] * n, [None] * n
    for i in range(n):
        t_inv[i] = eye + a_ab[i]
        s_pow[i] = _mm(a_ab[i], _block_diag(a_ab[i]))
        yield 50
    for _ in range(4):
        for i in range(n):
            st = _mm(jnp.concatenate([s_pow[i], t_inv[i]], axis=0), _block_diag(s_pow[i]))
            s_pow[i] = st[:C]
            t_inv[i] = t_inv[i] + st[C:]
            yield 65
    for i in range(n):
        t_inv[i] = t_inv[i] + _mm(t_inv[i], _block_diag(s_pow[i]))
        yield 50
    u, out, m_new = [None] * n, [None] * n, [None] * n
    for i in range(n):
        u[i] = _mm(t_inv[i], _block_diag(mv[i][:C] + av[i][:C]))
        yield 50
    for i in range(n):
        out[i] = mv[i][C:] + av[i][C:] + _mm(a_rb[i], _block_diag(u[i]))
        yield 50
    for i in range(n):
        upd = _mm_tn(jnp.concatenate([u[i], vs[i]], axis=0), key_end[i])
        m_new[i] = m_bds[i] * decay_end[i] + jnp.where(keep, upd, 0.0)
        yield 100
    return out, m_new


def _mix_task(b0, nb, p_ref, cos_ref, sin_ref, dec_ref, cdec_ref, mu_ref, lora_ref, vec_ref, rgn_ref,
              seg_ref, y_ref, carry_ref, m_ref, s_ref):
    C = CHUNK
    R = nb * C
    W = RWKV_WIDTH
    NG = W // GROUP
    bsl = slice(b0, b0 + nb)
    w0, a0, k_k, k_a, r_k, gn_g, gn_b = (vec_ref[i:i + 1, :] for i in range(7))
    seg = seg_ref[...]
    cost = 60 * nb

    def segsum(x):
        xs = jnp.concatenate([x[:, g * GROUP:(g + 1) * GROUP] for g in range(NG)], axis=0)
        tot = jnp.dot(xs.astype(bf16), seg, preferred_element_type=f32)
        return jnp.concatenate([tot[g * R:(g + 1) * R] for g in range(NG)], axis=1)

    def shifted(lo, hi):
        pr = p_ref[bsl, :, RET_COLS + lo:RET_COLS + hi].reshape(R, hi - lo)
        row = lax.broadcasted_iota(jnp.int32, pr.shape, 0)
        prev = pltpu.roll(pr, 1, 0)
        for bi in range(nb):
            prev = jnp.where(row == bi * C, carry_ref[b0 + bi, :, lo:hi], prev)
            carry_ref[b0 + bi, :, lo:hi] = pr[(bi + 1) * C - 1:(bi + 1) * C, :]
        return pr + mu_ref[:, lo:hi] * (prev - pr)

    xwa = shifted(4 * W, 4 * W + 2 * LORA)
    lane = lax.broadcasted_iota(jnp.int32, xwa.shape, 1)
    lora = _mm(jnp.where(lane < LORA, jnp.tanh(xwa), xwa), lora_ref[...])
    yield cost // 2
    kr = shifted(W, 2 * W)
    kk = kr * k_k
    kk_sq = segsum(kk * kk)
    yield cost
    r = shifted(0, W)
    yield cost
    vr = shifted(2 * W, 3 * W)
    yield cost
    g_rw = shifted(3 * W, 4 * W)
    gate = g_rw * _sigmoid(g_rw)
    yield cost
    lw = -np.float32(np.exp(-0.5)) * _sigmoid(w0 + lora[:, :W])
    yield cost
    a = _sigmoid(a0 + lora[:, W:])
    yield cost
    kkn = kk * lax.rsqrt(jnp.maximum(kk_sq, 1e-24))
    kmod = kr * (1.0 + (a - 1.0) * k_a)
    bonus_sum = segsum(r * kmod * r_k)
    yield cost

    cos = jnp.concatenate([cos_ref[...]] * nb, axis=0)
    sin = jnp.concatenate([sin_ref[...]] * nb, axis=0)
    half = (lax.broadcasted_iota(jnp.int32, (R, RET_QK), 1) & (RET_DK - 1)) < RET_DK // 2

    def rope(x):
        swapped = jnp.where(half, pltpu.roll(x, RET_QK - RET_DK // 2, 1), pltpu.roll(x, RET_DK // 2, 1))
        return x * cos + swapped * sin

    keep_sq = _same_block((4 * C, GROUP), C, RWKV_HEAD)
    keep_v = _same_block((4 * C, RET_WIDTH), C, RET_DV)
    q = rope(p_ref[bsl, :, 0:RET_QK].reshape(R, RET_QK))
    yield cost
    k = rope(p_ref[bsl, :, RET_QK:2 * RET_QK].reshape(R, RET_QK))
    v = p_ref[bsl, :, 2 * RET_QK:2 * RET_QK + RET_WIDTH].reshape(R, RET_WIDTH)
    yield cost
    rets = []
    for bi in range(nb):
        rsl = slice(bi * C, (bi + 1) * C)
        ret, s_ref[b0 + bi] = _retention_chunk(q[rsl], k[rsl], v[rsl], s_ref[b0 + bi], dec_ref[0], dec_ref[1],
                                               dec_ref[2], cdec_ref[...], keep_v)
        rets.append(ret)
        yield 300
    ret = jnp.concatenate(rets, axis=0)
    rgn = rgn_ref[...]
    for h in range(RET_HEADS):
        sl = slice(h * RET_DV, (h + 1) * RET_DV)
        xh = ret[:, sl]
        d = xh - jnp.mean(xh, axis=-1, keepdims=True)
        var = jnp.mean(d * d, axis=-1, keepdims=True)
        gh = p_ref[bsl, :, 2 * RET_QK + RET_WIDTH + h * RET_DV:2 * RET_QK + RET_WIDTH + (h + 1) * RET_DV]
        gh = gh.reshape(R, RET_DV)
        yh = gh * _sigmoid(gh) * (d * lax.rsqrt(var + RET_GN_EPS) * rgn[:, sl])
        y_ref[bsl, :, sl] = yh.reshape(nb, C, RET_DV).astype(y_ref.dtype)
        yield cost

    chains = [(bi, g) for bi in range(nb) for g in range(NG)]

    def pick(x):
        return [x[bi * C:(bi + 1) * C, g * GROUP:(g + 1) * GROUP] for bi, g in chains]

    outs, m_new = yield from _rwkv_chunks(pick(r), pick(kmod), pick(vr), pick(kkn), pick(a), pick(lw),
                                          [m_ref[b0 + bi, g] for bi, g in chains], keep_sq)
    for i, (bi, g) in enumerate(chains):
        m_ref[b0 + bi, g] = m_new[i]
    o = jnp.concatenate([jnp.concatenate(outs[bi * NG:(bi + 1) * NG], axis=1) for bi in range(nb)], axis=0)
    d = o - segsum(o) * (1.0 / RWKV_HEAD)
    yield cost
    var = segsum(d * d) * (1.0 / RWKV_HEAD)
    yield cost
    o = d * lax.rsqrt(var + RWKV_GN_EPS) * gn_g + gn_b
    y_ref[bsl, :, RET_WIDTH:] = (gate * (o + bonus_sum * vr)).reshape(nb, C, W).astype(y_ref.dtype)
    yield cost


def _out_task(x_ref, y, w_ref, g_ref, o_ref):
    nb, C, D = x_ref.shape
    hb = nb // 2
    for half in range(2):
        yh = y[half * hb * C:(half + 1) * hb * C]
        hs = []
        for j in range(D // GROUP):
            sl = slice(j * GROUP, (j + 1) * GROUP)
            hs.append(x_ref[half * hb:(half + 1) * hb, :, sl].reshape(hb * C, GROUP)
                      + jnp.dot(yh, w_ref[:, sl], preferred_element_type=f32))
            yield 350
        h = jnp.concatenate(hs, axis=1)
        out = h * lax.rsqrt(jnp.mean(h * h, axis=-1, keepdims=True) + RMS_EPS) * g_ref[...]
        o_ref[half * hb:(half + 1) * hb] = out.reshape(hb, C, D)
        yield 200


def _mixer_kernel(p_ref, cos_ref, sin_ref, dec_ref, cdec_ref, mu_ref, lora_ref, vec_ref, rgn_ref,
                  seg_ref, x_ref, wout_ref, fg_ref, o_ref, y_ref, carry_ref, m_ref, s_ref):
    c = pl.program_id(1)
    n_chunks = pl.num_programs(1) - 1

    @pl.when(c == 0)
    def _():
        carry_ref[...] = jnp.zeros_like(carry_ref)
        m_ref[...] = jnp.zeros_like(m_ref)
        s_ref[...] = jnp.zeros_like(s_ref)

    @pl.when(c > 0)
    def _():
        y_prev = y_ref[...].reshape(BATCH_BLOCK * CHUNK, D_MODEL)
        for _ in _out_task(x_ref, y_prev, wout_ref, fg_ref, o_ref):
            pass

    @pl.when(c < n_chunks)
    def _():
        nb = BATCH_BLOCK // MIX_TASKS
        _weave([(_mix_task(t * nb, nb, p_ref, cos_ref, sin_ref, dec_ref, cdec_ref, mu_ref, lora_ref, vec_ref,
                           rgn_ref, seg_ref, y_ref, carry_ref, m_ref, s_ref), t * MIX_TASK_LAG)
                for t in range(MIX_TASKS)])


def _rope_tables(seq):
    half = RET_DK // 2
    expo = -jnp.arange(half, dtype=f32) / f32(half)
    freqs = jnp.exp(expo * f32(np.log(ROPE_BASE)))
    ang = jnp.arange(seq, dtype=jnp.int32).astype(f32)[:, None] * freqs[None, :]
    cos = jnp.cos(ang)
    sin = jnp.sin(ang)
    cos_full = jnp.tile(jnp.concatenate([cos, cos], axis=1), (1, RET_HEADS))
    sin_signed = jnp.tile(jnp.concatenate([-sin, sin], axis=1), (1, RET_HEADS))
    return cos_full, sin_signed


def _retention_constants():
    C = CHUNK
    lg = np.log(1.0 - np.exp2(-5.0 - np.arange(RET_HEADS, dtype=np.float64)))
    lane_lg = np.repeat(lg, RET_DK)[None, :]
    n = np.arange(C, dtype=np.float64)[:, None]
    m = np.tile(np.arange(C, dtype=np.float64), RET_HEADS)[None, :]
    scale = RET_DK ** -0.5
    decay4 = scale * np.exp(lane_lg * np.abs(n - m))
    qdec = np.exp(lane_lg * (n + 1.0)) * np.ones((1, RET_QK))
    kdec = scale * np.exp(lane_lg * (C - 1.0 - n)) * np.ones((1, RET_QK))
    dec = np.stack([decay4, qdec, kdec]).astype(np.float32)
    ri = np.arange(RET_QK)[:, None] // RET_DK
    ci = np.arange(RET_WIDTH)[None, :] // RET_DV
    cdec = np.where(ri == ci, np.exp(np.repeat(lg, RET_DK) * C)[:, None], 0.0).astype(np.float32)
    return jnp.asarray(dec), jnp.asarray(cdec)


def _const_spec(shape):
    return pl.BlockSpec(shape, lambda *_: (0,) * len(shape))


@jax.jit
def kernel(x, norm_g, w_in, ret_gn_g, rwkv_mu, w_lora_up, w0, a_lora_up, a0, k_k, k_a, r_k,
           rwkv_gn_g, rwkv_gn_b, w_out, final_norm_g):
    B, T, D = x.shape
    assert D == D_MODEL and T % CHUNK == 0 and (B * T) % PROJ_TILE == 0
    assert B % BATCH_BLOCK == 0 and BATCH_BLOCK % (2 * MIX_TASKS) == 0
    assert norm_g.shape[0] == 1, "single-layer block"
    n_tok = B * T
    xf = x.reshape(n_tok, D)
    params = pltpu.CompilerParams(dimension_semantics=("arbitrary",), vmem_limit_bytes=VMEM_LIMIT)

    p = pl.pallas_call(
        _in_proj_kernel,
        grid=(n_tok // PROJ_TILE,),
        in_specs=[pl.BlockSpec((PROJ_TILE, D), lambda i: (i, 0)),
                  _const_spec((1, D)),
                  _const_spec((D, IN_COLS))],
        out_specs=pl.BlockSpec((PROJ_TILE, IN_COLS), lambda i: (i, 0)),
        out_shape=jax.ShapeDtypeStruct((n_tok, IN_COLS), f32),
        compiler_params=params,
        name="in_proj",
    )(xf, norm_g[0][None, :], w_in[0].astype(bf16))

    cos, sin = _rope_tables(T)
    dec, cdec = _retention_constants()
    zeros = jnp.zeros((LORA, RWKV_WIDTH), f32)
    lora_w = jnp.concatenate(
        [jnp.concatenate([w_lora_up[0], zeros], axis=1),
         jnp.concatenate([zeros, a_lora_up[0]], axis=1)], axis=0).astype(bf16)
    vecs = jnp.stack([w0[0], a0[0], k_k[0], k_a[0], r_k[0].reshape(-1), rwkv_gn_g[0], rwkv_gn_b[0],
                      jnp.zeros((RWKV_WIDTH,), f32)])
    seg_ids = np.arange(GROUP) // RWKV_HEAD
    seg = jnp.asarray(seg_ids[:, None] == seg_ids[None, :], bf16)

    n_chunks = T // CHUNK
    last = n_chunks - 1
    return pl.pallas_call(
        _mixer_kernel,
        grid=(B // BATCH_BLOCK, n_chunks + 1),
        in_specs=[pl.BlockSpec((BATCH_BLOCK, CHUNK, IN_COLS), lambda b, c: (b, jnp.minimum(c, last), 0)),
                  pl.BlockSpec((CHUNK, RET_QK), lambda b, c: (jnp.minimum(c, last), 0)),
                  pl.BlockSpec((CHUNK, RET_QK), lambda b, c: (jnp.minimum(c, last), 0)),
                  _const_spec((3, CHUNK, RET_QK)),
                  _const_spec((RET_QK, RET_WIDTH)),
                  _const_spec((1, RWKV_COLS)),
                  _const_spec((2 * LORA, 2 * RWKV_WIDTH)),
                  _const_spec((8, RWKV_WIDTH)),
                  _const_spec((1, RET_WIDTH)),
                  _const_spec((GROUP, GROUP)),
                  pl.BlockSpec((BATCH_BLOCK, CHUNK, D), lambda b, c: (b, jnp.maximum(c - 1, 0), 0)),
                  _const_spec((D, D)),
                  _const_spec((1, D))],
        out_specs=pl.BlockSpec((BATCH_BLOCK, CHUNK, D), lambda b, c: (b, jnp.maximum(c - 1, 0), 0)),
        out_shape=jax.ShapeDtypeStruct((B, T, D), f32),
        scratch_shapes=[pltpu.VMEM((BATCH_BLOCK, CHUNK, D), bf16),
                        pltpu.VMEM((BATCH_BLOCK, 1, RWKV_COLS), f32),
                        pltpu.VMEM((BATCH_BLOCK, RWKV_WIDTH // GROUP, GROUP, GROUP), f32),
                        pltpu.VMEM((BATCH_BLOCK, RET_QK, RET_WIDTH), f32)],
        compiler_params=pltpu.CompilerParams(dimension_semantics=("arbitrary", "arbitrary"),
                                             vmem_limit_bytes=VMEM_LIMIT),
        name="mixers",
    )(p.reshape(B, T, IN_COLS), cos, sin, dec, cdec, rwkv_mu[0][None, :], lora_w, vecs,
      ret_gn_g[0][None, :], seg, x, w_out[0].astype(bf16), final_norm_g[None, :])
```

```python
import numpy as np
import jax
import jax.numpy as jnp
from jax import lax
from jax.experimental import pallas as pl
from jax.experimental.pallas import tpu as pltpu

D_MODEL = 1024
CHUNK = 64
RET_HEADS = 4
RET_DK = 64
RET_DV = 128
RET_QK = RET_HEADS * RET_DK
RET_WIDTH = RET_HEADS * RET_DV
RWKV_WIDTH = 512
RWKV_HEAD = 64
LORA = 64
RET_COLS = 2 * RET_QK + 2 * RET_WIDTH
RWKV_COLS = 4 * RWKV_WIDTH + 2 * LORA
IN_COLS = RET_COLS + RWKV_COLS
ROPE_BASE = 10000.0
RMS_EPS = 1e-6
RET_GN_EPS = 1e-5
RWKV_GN_EPS = 64e-5
GROUP = 256
PROJ_TILE = 512
BATCH_BLOCK = 8
VMEM_LIMIT = 56 * 1024 * 1024

f32 = jnp.float32
bf16 = jnp.bfloat16


def _mm(a, b):
    return jnp.dot(a.astype(bf16), b.astype(bf16), preferred_element_type=f32)


def _mm_nt(a, b):
    return lax.dot_general(a.astype(bf16), b.astype(bf16), (((1,), (1,)), ((), ())),
                           preferred_element_type=f32)


def _mm_tn(a, b):
    return lax.dot_general(a.astype(bf16), b.astype(bf16), (((0,), (0,)), ((), ())),
                           preferred_element_type=f32)


def _split3(x):
    hi = x.astype(bf16)
    r1 = x - hi.astype(f32)
    mid = r1.astype(bf16)
    lo = (r1 - mid.astype(f32)).astype(bf16)
    return hi, mid, lo


def _same_block(shape, row_width, col_width):
    ri = lax.broadcasted_iota(jnp.int32, shape, 0) >> (row_width.bit_length() - 1)
    ci = lax.broadcasted_iota(jnp.int32, shape, 1) >> (col_width.bit_length() - 1)
    return ri == ci


def _block_diag(x):
    x = x.astype(bf16)
    rows, lanes = x.shape
    zero = jnp.zeros((rows, 128), bf16)
    cols = []
    if lanes == 4 * 128:
        for j in range(4):
            cols.append(jnp.concatenate([x[:, j * 128:(j + 1) * 128] if i == j else zero for i in range(4)],
                                        axis=0))
    else:
        low = lax.broadcasted_iota(jnp.int32, (rows, 128), 1) < 64
        for j in range(2):
            xj = x[:, j * 128:(j + 1) * 128]
            pair = [jnp.where(low, xj, zero), jnp.where(low, zero, xj)]
            cols.append(jnp.concatenate([zero] * (2 * j) + pair + [zero] * (2 - 2 * j), axis=0))
    return jnp.concatenate(cols, axis=1)


def _sigmoid(x):
    return 1.0 / (1.0 + jnp.exp(-x))


def _in_proj_kernel(x_ref, g_ref, w_ref, p_ref):
    x = x_ref[...]
    u = x * lax.rsqrt(jnp.mean(x * x, axis=-1, keepdims=True) + RMS_EPS) * g_ref[...]
    p_ref[...] = jnp.dot(u.astype(bf16), w_ref[...], preferred_element_type=f32)


def _retention_chunks(qs, ks, vs, s_bds, decay4, qdec, kdec, cdec, keep_v):
    n = range(len(qs))
    scores = [_mm_nt(qs[i], _block_diag(ks[i])) * decay4 for i in n]
    intra = [_mm(scores[i], _block_diag(vs[i])) for i in n]
    inter = [_mm(qs[i] * qdec, s_bds[i]) for i in n]
    kv = [_mm_tn(ks[i] * kdec, vs[i]) for i in n]
    s_new = [s_bds[i] * cdec + jnp.where(keep_v, kv[i], 0.0) for i in n]
    return [intra[i] + inter[i] for i in n], s_new


def _rwkv_chunks(rs, ks, vs, kkns, aas, lws, m_bds, keep):
    C = CHUNK
    n = range(len(rs))
    ti = lax.broadcasted_iota(jnp.int32, (C, C), 0)
    si = lax.broadcasted_iota(jnp.int32, (C, C), 1)
    tri_incl = jnp.where(si <= ti, 1.0, 0.0).astype(bf16)
    parts = [_split3(lws[i]) for i in n]
    cum = [sum(jnp.dot(tri_incl, part, preferred_element_type=f32) for part in parts[i]) for i in n]
    g_inc = [jnp.exp(cum[i]) for i in n]
    g_exc = [jnp.exp(cum[i] - lws[i]) for i in n]
    g_inv = [jnp.exp(-cum[i]) for i in n]
    g_end = [jnp.exp(cum[i][C - 1:C, :] - cum[i]) for i in n]
    beta = [kkns[i] * aas[i] for i in n]
    lhs = [jnp.concatenate([-kkns[i] * g_exc[i], rs[i] * g_inc[i]], axis=0) for i in n]
    amat = [_mm_nt(lhs[i], jnp.concatenate([_block_diag(beta[i] * g_inv[i]),
                                            _block_diag(ks[i] * g_inv[i])], axis=0)) for i in n]
    ti = lax.broadcasted_iota(jnp.int32, (C, GROUP), 0)
    si = lax.broadcasted_iota(jnp.int32, (C, GROUP), 1) & (C - 1)
    strict = si < ti
    incl = si <= ti
    a_ab = [jnp.where(strict, amat[i][:C, :GROUP], 0.0) for i in n]
    a_ak = [jnp.where(strict, amat[i][:C, GROUP:], 0.0) for i in n]
    a_rb = [jnp.where(incl, amat[i][C:, :GROUP], 0.0) for i in n]
    a_rk = [jnp.where(incl, amat[i][C:, GROUP:], 0.0) for i in n]
    mv = [_mm_nt(lhs[i], m_bds[i]) for i in n]
    av = [_mm(jnp.concatenate([a_ak[i], a_rk[i]], axis=0), _block_diag(vs[i])) for i in n]
    eye = jnp.where(si == ti, 1.0, 0.0)
    t_inv = [eye + a_ab[i] for i in n]
    s_pow = [_mm(a_ab[i], _block_diag(a_ab[i])) for i in n]
    for _ in range(4):
        st = [_mm(jnp.concatenate([s_pow[i], t_inv[i]], axis=0), _block_diag(s_pow[i])) for i in n]
        s_pow = [st[i][:C] for i in n]
        t_inv = [t_inv[i] + st[i][C:] for i in n]
    t_inv = [t_inv[i] + _mm(t_inv[i], _block_diag(s_pow[i])) for i in n]
    u = [_mm(t_inv[i], _block_diag(mv[i][:C] + av[i][:C])) for i in n]
    out = [mv[i][C:] + av[i][C:] + _mm(a_rb[i], _block_diag(u[i])) for i in n]
    upd = [_mm_tn(jnp.concatenate([u[i], vs[i]], axis=0),
                  jnp.concatenate([beta[i] * g_end[i], ks[i] * g_end[i]], axis=0)) for i in n]
    m_new = [m_bds[i] * g_inc[i][C - 1:C, :] + jnp.where(keep, upd[i], 0.0) for i in n]
    return out, m_new


def _mix_chunk(p_ref, cos_ref, sin_ref, dec_ref, cdec_ref, mu_ref, lora_ref, vec_ref, rgn_ref,
               seg_ref, y_ref, carry_ref, m_ref, s_ref):
    C = CHUNK
    nb = BATCH_BLOCK
    R = nb * C
    W = RWKV_WIDTH
    NG = W // GROUP

    def rows(x, bi):
        return x[bi * C:(bi + 1) * C]

    pr = p_ref[:, :, RET_COLS:IN_COLS].reshape(R, RWKV_COLS)
    row = lax.broadcasted_iota(jnp.int32, (R, RWKV_COLS), 0)
    prev = pltpu.roll(pr, 1, 0)
    for bi in range(nb):
        prev = jnp.where(row == bi * C, carry_ref[bi], prev)
        carry_ref[bi] = pr[(bi + 1) * C - 1:(bi + 1) * C, :]
    ps = pr + mu_ref[...] * (prev - pr)
    r = ps[:, 0:W]
    kr = ps[:, W:2 * W]
    vr = ps[:, 2 * W:3 * W]
    g_rw = ps[:, 3 * W:4 * W]
    xwa = ps[:, 4 * W:4 * W + 2 * LORA]
    lane = lax.broadcasted_iota(jnp.int32, xwa.shape, 1)
    lora = _mm(jnp.where(lane < LORA, jnp.tanh(xwa), xwa), lora_ref[...])
    w0, a0, k_k, k_a, r_k, gn_g, gn_b = (vec_ref[i:i + 1, :] for i in range(7))
    lw = -np.float32(np.exp(-0.5)) * _sigmoid(w0 + lora[:, :W])
    a = _sigmoid(a0 + lora[:, W:])
    seg = seg_ref[...]

    def segsum(x):
        xs = jnp.concatenate([x[:, g * GROUP:(g + 1) * GROUP] for g in range(NG)], axis=0)
        tot = jnp.dot(xs.astype(bf16), seg, preferred_element_type=f32)
        return jnp.concatenate([tot[g * R:(g + 1) * R] for g in range(NG)], axis=1)

    kk = kr * k_k
    kkn = kk * lax.rsqrt(jnp.maximum(segsum(kk * kk), 1e-24))
    kmod = kr * (1.0 + (a - 1.0) * k_a)

    cos = jnp.concatenate([cos_ref[...]] * nb, axis=0)
    sin = jnp.concatenate([sin_ref[...]] * nb, axis=0)
    half = (lax.broadcasted_iota(jnp.int32, (R, RET_QK), 1) & (RET_DK - 1)) < RET_DK // 2

    def rope(x):
        swapped = jnp.where(half, pltpu.roll(x, RET_QK - RET_DK // 2, 1), pltpu.roll(x, RET_DK // 2, 1))
        return x * cos + swapped * sin

    q = rope(p_ref[:, :, 0:RET_QK].reshape(R, RET_QK))
    k = rope(p_ref[:, :, RET_QK:2 * RET_QK].reshape(R, RET_QK))
    v = p_ref[:, :, 2 * RET_QK:2 * RET_QK + RET_WIDTH].reshape(R, RET_WIDTH)
    keep_sq = _same_block((4 * C, GROUP), C, RWKV_HEAD)
    keep_v = _same_block((4 * C, RET_WIDTH), C, RET_DV)
    rets, s_new = _retention_chunks([rows(q, bi) for bi in range(nb)], [rows(k, bi) for bi in range(nb)],
                                    [rows(v, bi) for bi in range(nb)], [s_ref[bi] for bi in range(nb)],
                                    dec_ref[0], dec_ref[1], dec_ref[2], cdec_ref[...], keep_v)
    for bi in range(nb):
        s_ref[bi] = s_new[bi]
    ret = jnp.concatenate(rets, axis=0)
    rgn = rgn_ref[...]
    for h in range(RET_HEADS):
        sl = slice(h * RET_DV, (h + 1) * RET_DV)
        xh = ret[:, sl]
        d = xh - jnp.mean(xh, axis=-1, keepdims=True)
        var = jnp.mean(d * d, axis=-1, keepdims=True)
        gh = p_ref[:, :, 2 * RET_QK + RET_WIDTH + h * RET_DV:2 * RET_QK + RET_WIDTH + (h + 1) * RET_DV]
        gh = gh.reshape(R, RET_DV)
        yh = gh * _sigmoid(gh) * (d * lax.rsqrt(var + RET_GN_EPS) * rgn[:, sl])
        y_ref[:, :, sl] = yh.reshape(nb, C, RET_DV).astype(y_ref.dtype)

    chains = [(bi, g) for bi in range(nb) for g in range(NG)]

    def pick(x):
        return [x[bi * C:(bi + 1) * C, g * GROUP:(g + 1) * GROUP] for bi, g in chains]

    outs, m_new = _rwkv_chunks(pick(r), pick(kmod), pick(vr), pick(kkn), pick(a), pick(lw),
                               [m_ref[bi, g] for bi, g in chains], keep_sq)
    for i, (bi, g) in enumerate(chains):
        m_ref[bi, g] = m_new[i]
    o = jnp.concatenate([jnp.concatenate(outs[bi * NG:(bi + 1) * NG], axis=1) for bi in range(nb)], axis=0)
    d = o - segsum(o) * (1.0 / RWKV_HEAD)
    var = segsum(d * d) * (1.0 / RWKV_HEAD)
    o = d * lax.rsqrt(var + RWKV_GN_EPS) * gn_g + gn_b
    bonus = segsum(r * kmod * r_k) * vr
    y_rw = g_rw * _sigmoid(g_rw) * (o + bonus)
    y_ref[:, :, RET_WIDTH:] = y_rw.reshape(nb, C, W).astype(y_ref.dtype)


def _project_chunk(x_ref, y_ref, w_ref, g_ref, o_ref):
    nb, C, D = x_ref.shape
    y = y_ref[...].reshape(nb * C, D)
    h = x_ref[...].reshape(nb * C, D) + jnp.dot(y, w_ref[...], preferred_element_type=f32)
    out = h * lax.rsqrt(jnp.mean(h * h, axis=-1, keepdims=True) + RMS_EPS) * g_ref[...]
    o_ref[...] = out.reshape(nb, C, D)


def _mixer_kernel(p_ref, cos_ref, sin_ref, dec_ref, cdec_ref, mu_ref, lora_ref, vec_ref, rgn_ref,
                  seg_ref, x_ref, wout_ref, fg_ref, o_ref, y_ref, carry_ref, m_ref, s_ref):
    c = pl.program_id(1)
    n_chunks = pl.num_programs(1) - 1

    @pl.when(c == 0)
    def _():
        carry_ref[...] = jnp.zeros_like(carry_ref)
        m_ref[...] = jnp.zeros_like(m_ref)
        s_ref[...] = jnp.zeros_like(s_ref)

    @pl.when(c > 0)
    def _():
        _project_chunk(x_ref, y_ref, wout_ref, fg_ref, o_ref)

    @pl.when(c < n_chunks)
    def _():
        _mix_chunk(p_ref, cos_ref, sin_ref, dec_ref, cdec_ref, mu_ref, lora_ref, vec_ref, rgn_ref,
                   seg_ref, y_ref, carry_ref, m_ref, s_ref)


def _rope_tables(seq):
    half = RET_DK // 2
    expo = -jnp.arange(half, dtype=f32) / f32(half)
    freqs = jnp.exp(expo * f32(np.log(ROPE_BASE)))
    ang = jnp.arange(seq, dtype=jnp.int32).astype(f32)[:, None] * freqs[None, :]
    cos = jnp.cos(ang)
    sin = jnp.sin(ang)
    cos_full = jnp.tile(jnp.concatenate([cos, cos], axis=1), (1, RET_HEADS))
    sin_signed = jnp.tile(jnp.concatenate([-sin, sin], axis=1), (1, RET_HEADS))
    return cos_full, sin_signed


def _retention_constants():
    C = CHUNK
    lg = np.log(1.0 - np.exp2(-5.0 - np.arange(RET_HEADS, dtype=np.float64)))
    lane_lg = np.repeat(lg, RET_DK)[None, :]
    n = np.arange(C, dtype=np.float64)[:, None]
    m = np.tile(np.arange(C, dtype=np.float64), RET_HEADS)[None, :]
    scale = RET_DK ** -0.5
    decay4 = scale * np.exp(lane_lg * np.abs(n - m))
    qdec = np.exp(lane_lg * (n + 1.0)) * np.ones((1, RET_QK))
    kdec = scale * np.exp(lane_lg * (C - 1.0 - n)) * np.ones((1, RET_QK))
    dec = np.stack([decay4, qdec, kdec]).astype(np.float32)
    ri = np.arange(RET_QK)[:, None] // RET_DK
    ci = np.arange(RET_WIDTH)[None, :] // RET_DV
    cdec = np.where(ri == ci, np.exp(np.repeat(lg, RET_DK) * C)[:, None], 0.0).astype(np.float32)
    return jnp.asarray(dec), jnp.asarray(cdec)


def _const_spec(shape):
    return pl.BlockSpec(shape, lambda *_: (0,) * len(shape))


@jax.jit
def kernel(x, norm_g, w_in, ret_gn_g, rwkv_mu, w_lora_up, w0, a_lora_up, a0, k_k, k_a, r_k,
           rwkv_gn_g, rwkv_gn_b, w_out, final_norm_g):
    B, T, D = x.shape
    assert D == D_MODEL and T % CHUNK == 0 and (B * T) % PROJ_TILE == 0 and B % BATCH_BLOCK == 0
    assert norm_g.shape[0] == 1, "single-layer block"
    n_tok = B * T
    xf = x.reshape(n_tok, D)
    params = pltpu.CompilerParams(dimension_semantics=("arbitrary",), vmem_limit_bytes=VMEM_LIMIT)

    p = pl.pallas_call(
        _in_proj_kernel,
        grid=(n_tok // PROJ_TILE,),
        in_specs=[pl.BlockSpec((PROJ_TILE, D), lambda i: (i, 0)),
                  _const_spec((1, D)),
                  _const_spec((D, IN_COLS))],
        out_specs=pl.BlockSpec((PROJ_TILE, IN_COLS), lambda i: (i, 0)),
        out_shape=jax.ShapeDtypeStruct((n_tok, IN_COLS), f32),
        compiler_params=params,
        name="in_proj",
    )(xf, norm_g[0][None, :], w_in[0].astype(bf16))

    cos, sin = _rope_tables(T)
    dec, cdec = _retention_constants()
    zeros = jnp.zeros((LORA, RWKV_WIDTH), f32)
    lora_w = jnp.concatenate(
        [jnp.concatenate([w_lora_up[0], zeros], axis=1),
         jnp.concatenate([zeros, a_lora_up[0]], axis=1)], axis=0).astype(bf16)
    vecs = jnp.stack([w0[0], a0[0], k_k[0], k_a[0], r_k[0].reshape(-1), rwkv_gn_g[0], rwkv_gn_b[0],
                      jnp.zeros((RWKV_WIDTH,), f32)])
    seg_ids = np.arange(GROUP) // RWKV_HEAD
    seg = jnp.asarray(seg_ids[:, None] == seg_ids[None, :], bf16)

    n_chunks = T // CHUNK
    last = n_chunks - 1
    return pl.pallas_call(
        _mixer_kernel,
        grid=(B // BATCH_BLOCK, n_chunks + 1),
        in_specs=[pl.BlockSpec((BATCH_BLOCK, CHUNK, IN_COLS), lambda b, c: (b, jnp.minimum(c, last), 0)),
                  pl.BlockSpec((CHUNK, RET_QK), lambda b, c: (jnp.minimum(c, last), 0)),
                  pl.BlockSpec((CHUNK, RET_QK), lambda b, c: (jnp.minimum(c, last), 0)),
                  _const_spec((3, CHUNK, RET_QK)),
                  _const_spec((RET_QK, RET_WIDTH)),
                  _const_spec((1, RWKV_COLS)),
                  _const_spec((2 * LORA, 2 * RWKV_WIDTH)),
                  _const_spec((8, RWKV_WIDTH)),
                  _const_spec((1, RET_WIDTH)),
                  _const_spec((GROUP, GROUP)),
                  pl.BlockSpec((BATCH_BLOCK, CHUNK, D), lambda b, c: (b, jnp.maximum(c - 1, 0), 0)),
                  _const_spec((D, D)),
                  _const_spec((1, D))],
        out_specs=pl.BlockSpec((BATCH_BLOCK, CHUNK, D), lambda b, c: (b, jnp.maximum(c - 1, 0), 0)),
        out_shape=jax.ShapeDtypeStruct((B, T, D), f32),
        scratch_shapes=[pltpu.VMEM((BATCH_BLOCK, CHUNK, D), bf16),
                        pltpu.VMEM((BATCH_BLOCK, 1, RWKV_COLS), f32),
                        pltpu.VMEM((BATCH_BLOCK, RWKV_WIDTH // GROUP, GROUP, GROUP), f32),
                        pltpu.VMEM((BATCH_BLOCK, RET_QK, RET_WIDTH), f32)],
        compiler_params=pltpu.CompilerParams(dimension_semantics=("arbitrary", "arbitrary"),
                                             vmem_limit_bytes=VMEM_LIMIT),
        name="mixers",
    )(p.reshape(B, T, IN_COLS), cos, sin, dec, cdec, rwkv_mu[0][None, :], lora_w, vecs,
      ret_gn_g[0][None, :], seg, x, w_out[0].astype(bf16), final_norm_g[None, :])
```

```python
import numpy as np
import jax
import jax.numpy as jnp
from jax import lax
from jax.experimental import pallas as pl
from jax.experimental.pallas import tpu as pltpu

D_MODEL = 1024
CHUNK = 64
RET_HEADS = 4
RET_DK = 64
RET_DV = 128
RET_QK = RET_HEADS * RET_DK
RET_WIDTH = RET_HEADS * RET_DV
RWKV_WIDTH = 512
RWKV_HEAD = 64
LORA = 64
RET_COLS = 2 * RET_QK + 2 * RET_WIDTH
RWKV_COLS = 4 * RWKV_WIDTH + 2 * LORA
IN_COLS = RET_COLS + RWKV_COLS
ROPE_BASE = 10000.0
RMS_EPS = 1e-6
RET_GN_EPS = 1e-5
RWKV_GN_EPS = 64e-5
GROUP = 256
PROJ_TILE = 512
BATCH_BLOCK = 8
VMEM_LIMIT = 56 * 1024 * 1024

f32 = jnp.float32
bf16 = jnp.bfloat16


def _mm(a, b):
    return jnp.dot(a.astype(bf16), b.astype(bf16), preferred_element_type=f32)


def _mm_nt(a, b):
    return lax.dot_general(a.astype(bf16), b.astype(bf16), (((1,), (1,)), ((), ())),
                           preferred_element_type=f32)


def _mm_tn(a, b):
    return lax.dot_general(a.astype(bf16), b.astype(bf16), (((0,), (0,)), ((), ())),
                           preferred_element_type=f32)


def _split2(x):
    hi = x.astype(bf16)
    lo = (x - hi.astype(f32)).astype(bf16)
    return hi, lo


def _same_block(shape, row_width, col_width):
    ri = lax.broadcasted_iota(jnp.int32, shape, 0) >> (row_width.bit_length() - 1)
    ci = lax.broadcasted_iota(jnp.int32, shape, 1) >> (col_width.bit_length() - 1)
    return ri == ci


def _block_diag(x):
    x = x.astype(bf16)
    rows, lanes = x.shape
    zero = jnp.zeros((rows, 128), bf16)
    cols = []
    if lanes == 4 * 128:
        for j in range(4):
            cols.append(jnp.concatenate([x[:, j * 128:(j + 1) * 128] if i == j else zero for i in range(4)],
                                        axis=0))
    else:
        low = lax.broadcasted_iota(jnp.int32, (rows, 128), 1) < 64
        for j in range(2):
            xj = x[:, j * 128:(j + 1) * 128]
            pair = [jnp.where(low, xj, zero), jnp.where(low, zero, xj)]
            cols.append(jnp.concatenate([zero] * (2 * j) + pair + [zero] * (2 - 2 * j), axis=0))
    return jnp.concatenate(cols, axis=1)


def _sigmoid(x):
    return 1.0 / (1.0 + jnp.exp(-x))


def _in_proj_kernel(x_ref, g_ref, w_ref, p_ref):
    x = x_ref[...]
    u = x * lax.rsqrt(jnp.mean(x * x, axis=-1, keepdims=True) + RMS_EPS) * g_ref[...]
    p_ref[...] = jnp.dot(u.astype(bf16), w_ref[...], preferred_element_type=f32)


def _retention_chunks(qs, ks, vs, s_bds, decay4, qdec, kdec, cdec, keep_v):
    n = range(len(qs))
    scores = [_mm_nt(qs[i], _block_diag(ks[i])) * decay4 for i in n]
    intra = [_mm(scores[i], _block_diag(vs[i])) for i in n]
    inter = [_mm(qs[i] * qdec, s_bds[i]) for i in n]
    kv = [_mm_tn(ks[i] * kdec, vs[i]) for i in n]
    s_new = [s_bds[i] * cdec + jnp.where(keep_v, kv[i], 0.0) for i in n]
    return [intra[i] + inter[i] for i in n], s_new


def _rwkv_chunks(rs, ks, vs, kkns, aas, lws, m_bds, keep):
    C = CHUNK
    n = range(len(rs))
    ti = lax.broadcasted_iota(jnp.int32, (C, C), 0)
    si = lax.broadcasted_iota(jnp.int32, (C, C), 1)
    tri_incl = jnp.where(si <= ti, 1.0, 0.0).astype(bf16)
    tri_incl = jnp.concatenate([tri_incl, tri_incl], axis=1)
    cum = [jnp.dot(tri_incl, jnp.concatenate(_split2(lws[i]), axis=0), preferred_element_type=f32) for i in n]
    g_inc = [jnp.exp(cum[i]) for i in n]
    g_exc = [jnp.exp(cum[i] - lws[i]) for i in n]
    g_inv = [jnp.exp(-cum[i]) for i in n]
    g_end = [jnp.exp(cum[i][C - 1:C, :] - cum[i]) for i in n]
    beta = [kkns[i] * aas[i] for i in n]
    lhs = [jnp.concatenate([-kkns[i] * g_exc[i], rs[i] * g_inc[i]], axis=0) for i in n]
    amat = [_mm_nt(lhs[i], jnp.concatenate([_block_diag(beta[i] * g_inv[i]),
                                            _block_diag(ks[i] * g_inv[i])], axis=0)) for i in n]
    ti = lax.broadcasted_iota(jnp.int32, (C, GROUP), 0)
    si = lax.broadcasted_iota(jnp.int32, (C, GROUP), 1) & (C - 1)
    strict = si < ti
    incl = si <= ti
    a_ab = [jnp.where(strict, amat[i][:C, :GROUP], 0.0) for i in n]
    a_ak = [jnp.where(strict, amat[i][:C, GROUP:], 0.0) for i in n]
    a_rb = [jnp.where(incl, amat[i][C:, :GROUP], 0.0) for i in n]
    a_rk = [jnp.where(incl, amat[i][C:, GROUP:], 0.0) for i in n]
    mv = [_mm_nt(lhs[i], m_bds[i]) for i in n]
    av = [_mm(jnp.concatenate([a_ak[i], a_rk[i]], axis=0), _block_diag(vs[i])) for i in n]
    eye = jnp.where(si == ti, 1.0, 0.0)
    t_inv = [eye + a_ab[i] for i in n]
    s_pow = [_mm(a_ab[i], _block_diag(a_ab[i])) for i in n]
    for _ in range(4):
        st = [_mm(jnp.concatenate([s_pow[i], t_inv[i]], axis=0), _block_diag(s_pow[i])) for i in n]
        s_pow = [st[i][:C] for i in n]
        t_inv = [t_inv[i] + st[i][C:] for i in n]
    t_inv = [t_inv[i] + _mm(t_inv[i], _block_diag(s_pow[i])) for i in n]
    u = [_mm(t_inv[i], _block_diag(mv[i][:C] + av[i][:C])) for i in n]
    out = [mv[i][C:] + av[i][C:] + _mm(a_rb[i], _block_diag(u[i])) for i in n]
    upd = [_mm_tn(jnp.concatenate([u[i], vs[i]], axis=0),
                  jnp.concatenate([beta[i] * g_end[i], ks[i] * g_end[i]], axis=0)) for i in n]
    m_new = [m_bds[i] * g_inc[i][C - 1:C, :] + jnp.where(keep, upd[i], 0.0) for i in n]
    return out, m_new


def _mix_chunk(p_ref, cos_ref, sin_ref, dec_ref, cdec_ref, mu_ref, lora_ref, vec_ref, rgn_ref,
               seg_ref, y_ref, carry_ref, m_ref, s_ref):
    C = CHUNK
    nb = BATCH_BLOCK
    R = nb * C
    W = RWKV_WIDTH
    NG = W // GROUP

    def rows(x, bi):
        return x[bi * C:(bi + 1) * C]

    pr = p_ref[:, :, RET_COLS:IN_COLS].reshape(R, RWKV_COLS)
    row = lax.broadcasted_iota(jnp.int32, (R, RWKV_COLS), 0)
    prev = pltpu.roll(pr, 1, 0)
    for bi in range(nb):
        prev = jnp.where(row == bi * C, carry_ref[bi], prev)
        carry_ref[bi] = pr[(bi + 1) * C - 1:(bi + 1) * C, :]
    ps = pr + mu_ref[...] * (prev - pr)
    r = ps[:, 0:W]
    kr = ps[:, W:2 * W]
    vr = ps[:, 2 * W:3 * W]
    g_rw = ps[:, 3 * W:4 * W]
    xwa = ps[:, 4 * W:4 * W + 2 * LORA]
    lane = lax.broadcasted_iota(jnp.int32, xwa.shape, 1)
    lora = _mm(jnp.where(lane < LORA, jnp.tanh(xwa), xwa), lora_ref[...])
    w0, a0, k_k, k_a, r_k, gn_g, gn_b = (vec_ref[i:i + 1, :] for i in range(7))
    lw = -np.float32(np.exp(-0.5)) * _sigmoid(w0 + lora[:, :W])
    a = _sigmoid(a0 + lora[:, W:])
    seg = seg_ref[...]

    def segsum(x):
        xs = jnp.concatenate([x[:, g * GROUP:(g + 1) * GROUP] for g in range(NG)], axis=0)
        tot = jnp.dot(xs.astype(bf16), seg, preferred_element_type=f32)
        return jnp.concatenate([tot[g * R:(g + 1) * R] for g in range(NG)], axis=1)

    kk = kr * k_k
    kkn = kk * lax.rsqrt(jnp.maximum(segsum(kk * kk), 1e-24))
    kmod = kr * (1.0 + (a - 1.0) * k_a)

    cos = jnp.concatenate([cos_ref[...]] * nb, axis=0)
    sin = jnp.concatenate([sin_ref[...]] * nb, axis=0)
    half = (lax.broadcasted_iota(jnp.int32, (R, RET_QK), 1) & (RET_DK - 1)) < RET_DK // 2

    def rope(x):
        swapped = jnp.where(half, pltpu.roll(x, RET_QK - RET_DK // 2, 1), pltpu.roll(x, RET_DK // 2, 1))
        return x * cos + swapped * sin

    q = rope(p_ref[:, :, 0:RET_QK].reshape(R, RET_QK))
    k = rope(p_ref[:, :, RET_QK:2 * RET_QK].reshape(R, RET_QK))
    v = p_ref[:, :, 2 * RET_QK:2 * RET_QK + RET_WIDTH].reshape(R, RET_WIDTH)
    keep_sq = _same_block((4 * C, GROUP), C, RWKV_HEAD)
    keep_v = _same_block((4 * C, RET_WIDTH), C, RET_DV)
    rets, s_new = _retention_chunks([rows(q, bi) for bi in range(nb)], [rows(k, bi) for bi in range(nb)],
                                    [rows(v, bi) for bi in range(nb)], [s_ref[bi] for bi in range(nb)],
                                    dec_ref[0], dec_ref[1], dec_ref[2], cdec_ref[...], keep_v)
    for bi in range(nb):
        s_ref[bi] = s_new[bi]
    ret = jnp.concatenate(rets, axis=0)
    rgn = rgn_ref[...]
    for h in range(RET_HEADS):
        sl = slice(h * RET_DV, (h + 1) * RET_DV)
        xh = ret[:, sl]
        d = xh - jnp.mean(xh, axis=-1, keepdims=True)
        var = jnp.mean(d * d, axis=-1, keepdims=True)
        gh = p_ref[:, :, 2 * RET_QK + RET_WIDTH + h * RET_DV:2 * RET_QK + RET_WIDTH + (h + 1) * RET_DV]
        gh = gh.reshape(R, RET_DV)
        yh = gh * _sigmoid(gh) * (d * lax.rsqrt(var + RET_GN_EPS) * rgn[:, sl])
        y_ref[:, :, sl] = yh.reshape(nb, C, RET_DV).astype(y_ref.dtype)

    chains = [(bi, g) for bi in range(nb) for g in range(NG)]

    def pick(x):
        return [x[bi * C:(bi + 1) * C, g * GROUP:(g + 1) * GROUP] for bi, g in chains]

    outs, m_new = _rwkv_chunks(pick(r), pick(kmod), pick(vr), pick(kkn), pick(a), pick(lw),
                               [m_ref[bi, g] for bi, g in chains], keep_sq)
    for i, (bi, g) in enumerate(chains):
        m_ref[bi, g] = m_new[i]
    o = jnp.concatenate([jnp.concatenate(outs[bi * NG:(bi + 1) * NG], axis=1) for bi in range(nb)], axis=0)
    d = o - segsum(o) * (1.0 / RWKV_HEAD)
    var = segsum(d * d) * (1.0 / RWKV_HEAD)
    o = d * lax.rsqrt(var + RWKV_GN_EPS) * gn_g + gn_b
    bonus = segsum(r * kmod * r_k) * vr
    y_rw = g_rw * _sigmoid(g_rw) * (o + bonus)
    y_ref[:, :, RET_WIDTH:] = y_rw.reshape(nb, C, W).astype(y_ref.dtype)


def _project_chunk(x_ref, y_ref, w_ref, g_ref, o_ref):
    nb, C, D = x_ref.shape
    y = y_ref[...].reshape(nb * C, D)
    h = x_ref[...].reshape(nb * C, D) + jnp.dot(y, w_ref[...], preferred_element_type=f32)
    out = h * lax.rsqrt(jnp.mean(h * h, axis=-1, keepdims=True) + RMS_EPS) * g_ref[...]
    o_ref[...] = out.reshape(nb, C, D)


def _mixer_kernel(p_ref, cos_ref, sin_ref, dec_ref, cdec_ref, mu_ref, lora_ref, vec_ref, rgn_ref,
                  seg_ref, x_ref, wout_ref, fg_ref, o_ref, y_ref, carry_ref, m_ref, s_ref):
    @pl.when(pl.program_id(1) == 0)
    def _():
        carry_ref[...] = jnp.zeros_like(carry_ref)
        m_ref[...] = jnp.zeros_like(m_ref)
        s_ref[...] = jnp.zeros_like(s_ref)
        y_ref[...] = jnp.zeros_like(y_ref)

    _project_chunk(x_ref, y_ref, wout_ref, fg_ref, o_ref)
    _mix_chunk(p_ref, cos_ref, sin_ref, dec_ref, cdec_ref, mu_ref, lora_ref, vec_ref, rgn_ref,
               seg_ref, y_ref, carry_ref, m_ref, s_ref)


def _rope_tables(seq):
    half = RET_DK // 2
    expo = -jnp.arange(half, dtype=f32) / f32(half)
    freqs = jnp.exp(expo * f32(np.log(ROPE_BASE)))
    ang = jnp.arange(seq, dtype=jnp.int32).astype(f32)[:, None] * freqs[None, :]
    cos = jnp.cos(ang)
    sin = jnp.sin(ang)
    cos_full = jnp.tile(jnp.concatenate([cos, cos], axis=1), (1, RET_HEADS))
    sin_signed = jnp.tile(jnp.concatenate([-sin, sin], axis=1), (1, RET_HEADS))
    return cos_full, sin_signed


def _retention_constants():
    C = CHUNK
    lg = np.log(1.0 - np.exp2(-5.0 - np.arange(RET_HEADS, dtype=np.float64)))
    lane_lg = np.repeat(lg, RET_DK)[None, :]
    n = np.arange(C, dtype=np.float64)[:, None]
    m = np.tile(np.arange(C, dtype=np.float64), RET_HEADS)[None, :]
    scale = RET_DK ** -0.5
    decay4 = scale * np.exp(lane_lg * np.abs(n - m))
    qdec = np.exp(lane_lg * (n + 1.0)) * np.ones((1, RET_QK))
    kdec = scale * np.exp(lane_lg * (C - 1.0 - n)) * np.ones((1, RET_QK))
    dec = np.stack([decay4, qdec, kdec]).astype(np.float32)
    ri = np.arange(RET_QK)[:, None] // RET_DK
    ci = np.arange(RET_WIDTH)[None, :] // RET_DV
    cdec = np.where(ri == ci, np.exp(np.repeat(lg, RET_DK) * C)[:, None], 0.0).astype(np.float32)
    return jnp.asarray(dec), jnp.asarray(cdec)


def _const_spec(shape):
    return pl.BlockSpec(shape, lambda *_: (0,) * len(shape))


@jax.jit
def kernel(x, norm_g, w_in, ret_gn_g, rwkv_mu, w_lora_up, w0, a_lora_up, a0, k_k, k_a, r_k,
           rwkv_gn_g, rwkv_gn_b, w_out, final_norm_g):
    B, T, D = x.shape
    assert D == D_MODEL and T % CHUNK == 0 and (B * T) % PROJ_TILE == 0 and B % BATCH_BLOCK == 0
    assert norm_g.shape[0] == 1, "single-layer block"
    n_tok = B * T
    xf = x.reshape(n_tok, D)
    params = pltpu.CompilerParams(dimension_semantics=("arbitrary",), vmem_limit_bytes=VMEM_LIMIT)

    p = pl.pallas_call(
        _in_proj_kernel,
        grid=(n_tok // PROJ_TILE,),
        in_specs=[pl.BlockSpec((PROJ_TILE, D), lambda i: (i, 0)),
                  _const_spec((1, D)),
                  _const_spec((D, IN_COLS))],
        out_specs=pl.BlockSpec((PROJ_TILE, IN_COLS), lambda i: (i, 0)),
        out_shape=jax.ShapeDtypeStruct((n_tok, IN_COLS), f32),
        compiler_params=params,
        name="in_proj",
    )(xf, norm_g[0][None, :], w_in[0].astype(bf16))

    cos, sin = _rope_tables(T)
    dec, cdec = _retention_constants()
    zeros = jnp.zeros((LORA, RWKV_WIDTH), f32)
    lora_w = jnp.concatenate(
        [jnp.concatenate([w_lora_up[0], zeros], axis=1),
         jnp.concatenate([zeros, a_lora_up[0]], axis=1)], axis=0).astype(bf16)
    vecs = jnp.stack([w0[0], a0[0], k_k[0], k_a[0], r_k[0].reshape(-1), rwkv_gn_g[0], rwkv_gn_b[0],
                      jnp.zeros((RWKV_WIDTH,), f32)])
    seg_ids = np.arange(GROUP) // RWKV_HEAD
    seg = jnp.asarray(seg_ids[:, None] == seg_ids[None, :], bf16)

    n_chunks = T // CHUNK
    last = n_chunks - 1
    return pl.pallas_call(
        _mixer_kernel,
        grid=(B // BATCH_BLOCK, n_chunks + 1),
        in_specs=[pl.BlockSpec((BATCH_BLOCK, CHUNK, IN_COLS), lambda b, c: (b, jnp.minimum(c, last), 0)),
                  pl.BlockSpec((CHUNK, RET_QK), lambda b, c: (jnp.minimum(c, last), 0)),
                  pl.BlockSpec((CHUNK, RET_QK), lambda b, c: (jnp.minimum(c, last), 0)),
                  _const_spec((3, CHUNK, RET_QK)),
                  _const_spec((RET_QK, RET_WIDTH)),
                  _const_spec((1, RWKV_COLS)),
                  _const_spec((2 * LORA, 2 * RWKV_WIDTH)),
                  _const_spec((8, RWKV_WIDTH)),
                  _const_spec((1, RET_WIDTH)),
                  _const_spec((GROUP, GROUP)),
                  pl.BlockSpec((BATCH_BLOCK, CHUNK, D), lambda b, c: (b, jnp.maximum(c - 1, 0), 0)),
                  _const_spec((D, D)),
                  _const_spec((1, D))],
        out_specs=pl.BlockSpec((BATCH_BLOCK, CHUNK, D), lambda b, c: (b, jnp.maximum(c - 1, 0), 0)),
        out_shape=jax.ShapeDtypeStruct((B, T, D), f32),
        scratch_shapes=[pltpu.VMEM((BATCH_BLOCK, CHUNK, D), bf16),
                        pltpu.VMEM((BATCH_BLOCK, 1, RWKV_COLS), f32),
                        pltpu.VMEM((BATCH_BLOCK, RWKV_WIDTH // GROUP, GROUP, GROUP), f32),
                        pltpu.VMEM((BATCH_BLOCK, RET_QK, RET_WIDTH), f32)],
        compiler_params=pltpu.CompilerParams(dimension_semantics=("arbitrary", "arbitrary"),
                                             vmem_limit_bytes=VMEM_LIMIT),
        name="mixers",
    )(p.reshape(B, T, IN_COLS), cos, sin, dec, cdec, rwkv_mu[0][None, :], lora_w, vecs,
      ret_gn_g[0][None, :], seg, x, w_out[0].astype(bf16), final_norm_g[None, :])
```

```python
import numpy as np
import jax
import jax.numpy as jnp
from jax import lax
from jax.experimental import pallas as pl
from jax.experimental.pallas import tpu as pltpu

D_MODEL = 1024
CHUNK = 64
RET_HEADS = 4
RET_DK = 64
RET_DV = 128
RET_QK = RET_HEADS * RET_DK
RET_WIDTH = RET_HEADS * RET_DV
RWKV_WIDTH = 512
RWKV_HEAD = 64
LORA = 64
RET_COLS = 2 * RET_QK + 2 * RET_WIDTH
RWKV_COLS = 4 * RWKV_WIDTH + 2 * LORA
IN_COLS = RET_COLS + RWKV_COLS
ROPE_BASE = 10000.0
RMS_EPS = 1e-6
RET_GN_EPS = 1e-5
RWKV_GN_EPS = 64e-5
GROUP = 256
PROJ_TILE = 512
BATCH_BLOCK = 8
VMEM_LIMIT = 56 * 1024 * 1024

f32 = jnp.float32
bf16 = jnp.bfloat16


def _mm(a, b):
    return jnp.dot(a.astype(bf16), b.astype(bf16), preferred_element_type=f32)


def _mm_nt(a, b):
    return lax.dot_general(a.astype(bf16), b.astype(bf16), (((1,), (1,)), ((), ())),
                           preferred_element_type=f32)


def _mm_tn(a, b):
    return lax.dot_general(a.astype(bf16), b.astype(bf16), (((0,), (0,)), ((), ())),
                           preferred_element_type=f32)


def _split2(x):
    hi = x.astype(bf16)
    lo = (x - hi.astype(f32)).astype(bf16)
    return hi, lo


def _same_block(shape, row_width, col_width):
    ri = lax.broadcasted_iota(jnp.int32, shape, 0) >> (row_width.bit_length() - 1)
    ci = lax.broadcasted_iota(jnp.int32, shape, 1) >> (col_width.bit_length() - 1)
    return ri == ci


def _block_diag(x):
    x = x.astype(bf16)
    rows, lanes = x.shape
    zero = jnp.zeros((rows, 128), bf16)
    cols = []
    if lanes == 4 * 128:
        for j in range(4):
            cols.append(jnp.concatenate([x[:, j * 128:(j + 1) * 128] if i == j else zero for i in range(4)],
                                        axis=0))
    else:
        low = lax.broadcasted_iota(jnp.int32, (rows, 128), 1) < 64
        for j in range(2):
            xj = x[:, j * 128:(j + 1) * 128]
            pair = [jnp.where(low, xj, zero), jnp.where(low, zero, xj)]
            cols.append(jnp.concatenate([zero] * (2 * j) + pair + [zero] * (2 - 2 * j), axis=0))
    return jnp.concatenate(cols, axis=1)


def _sigmoid(x):
    return 1.0 / (1.0 + jnp.exp(-x))


def _in_proj_kernel(x_ref, g_ref, w_ref, p_ref, wb_ref):
    @pl.when(pl.program_id(0) == 0)
    def _():
        wb_ref[...] = w_ref[...].astype(bf16)

    x = x_ref[...]
    u = x * lax.rsqrt(jnp.mean(x * x, axis=-1, keepdims=True) + RMS_EPS) * g_ref[...]
    p_ref[...] = jnp.dot(u.astype(bf16), wb_ref[...], preferred_element_type=f32)


def _retention_chunks(qs, ks, vs, s_bds, decay4, qdec, kdec, cdec, keep_v):
    n = range(len(qs))
    scores = [_mm_nt(qs[i], _block_diag(ks[i])) * decay4 for i in n]
    intra = [_mm(scores[i], _block_diag(vs[i])) for i in n]
    inter = [_mm(qs[i] * qdec, s_bds[i]) for i in n]
    kv = [_mm_tn(ks[i] * kdec, vs[i]) for i in n]
    s_new = [s_bds[i] * cdec + jnp.where(keep_v, kv[i], 0.0) for i in n]
    return [intra[i] + inter[i] for i in n], s_new


def _rwkv_chunks(rs, ks, vs, kkns, aas, lws, m_bds, keep):
    C = CHUNK
    n = range(len(rs))
    ti = lax.broadcasted_iota(jnp.int32, (C, C), 0)
    si = lax.broadcasted_iota(jnp.int32, (C, C), 1)
    tri_incl = jnp.where(si <= ti, 1.0, 0.0).astype(bf16)
    tri_incl = jnp.concatenate([tri_incl, tri_incl], axis=1)
    cum = [jnp.dot(tri_incl, jnp.concatenate(_split2(lws[i]), axis=0), preferred_element_type=f32) for i in n]
    g_inc = [jnp.exp(cum[i]) for i in n]
    g_exc = [jnp.exp(cum[i] - lws[i]) for i in n]
    g_inv = [jnp.exp(-cum[i]) for i in n]
    g_end = [jnp.exp(cum[i][C - 1:C, :] - cum[i]) for i in n]
    beta = [kkns[i] * aas[i] for i in n]
    lhs = [jnp.concatenate([-kkns[i] * g_exc[i], rs[i] * g_inc[i]], axis=0) for i in n]
    amat = [_mm_nt(lhs[i], jnp.concatenate([_block_diag(beta[i] * g_inv[i]),
                                            _block_diag(ks[i] * g_inv[i])], axis=0)) for i in n]
    ti = lax.broadcasted_iota(jnp.int32, (C, GROUP), 0)
    si = lax.broadcasted_iota(jnp.int32, (C, GROUP), 1) & (C - 1)
    strict = si < ti
    incl = si <= ti
    a_ab = [jnp.where(strict, amat[i][:C, :GROUP], 0.0) for i in n]
    a_ak = [jnp.where(strict, amat[i][:C, GROUP:], 0.0) for i in n]
    a_rb = [jnp.where(incl, amat[i][C:, :GROUP], 0.0) for i in n]
    a_rk = [jnp.where(incl, amat[i][C:, GROUP:], 0.0) for i in n]
    mv = [_mm_nt(lhs[i], m_bds[i]) for i in n]
    av = [_mm(jnp.concatenate([a_ak[i], a_rk[i]], axis=0), _block_diag(vs[i])) for i in n]
    eye = jnp.where(si == ti, 1.0, 0.0)
    t_inv = [eye + a_ab[i] for i in n]
    s_pow = [_mm(a_ab[i], _block_diag(a_ab[i])) for i in n]
    for _ in range(4):
        st = [_mm(jnp.concatenate([s_pow[i], t_inv[i]], axis=0), _block_diag(s_pow[i])) for i in n]
        s_pow = [st[i][:C] for i in n]
        t_inv = [t_inv[i] + st[i][C:] for i in n]
    t_inv = [t_inv[i] + _mm(t_inv[i], _block_diag(s_pow[i])) for i in n]
    u = [_mm(t_inv[i], _block_diag(mv[i][:C] + av[i][:C])) for i in n]
    out = [mv[i][C:] + av[i][C:] + _mm(a_rb[i], _block_diag(u[i])) for i in n]
    upd = [_mm_tn(jnp.concatenate([u[i], vs[i]], axis=0),
                  jnp.concatenate([beta[i] * g_end[i], ks[i] * g_end[i]], axis=0)) for i in n]
    m_new = [m_bds[i] * g_inc[i][C - 1:C, :] + jnp.where(keep, upd[i], 0.0) for i in n]
    return out, m_new


def _mix_chunk(p_ref, cos_ref, sin_ref, dec_ref, cdec_ref, mu_ref, lora_ref, vec_ref, rgn_ref,
               seg_ref, y_ref, carry_ref, m_ref, s_ref):
    C = CHUNK
    nb = BATCH_BLOCK
    R = nb * C
    W = RWKV_WIDTH
    NG = W // GROUP

    def rows(x, bi):
        return x[bi * C:(bi + 1) * C]

    pr = p_ref[:, :, RET_COLS:IN_COLS].reshape(R, RWKV_COLS)
    row = lax.broadcasted_iota(jnp.int32, (R, RWKV_COLS), 0)
    prev = pltpu.roll(pr, 1, 0)
    for bi in range(nb):
        prev = jnp.where(row == bi * C, carry_ref[bi], prev)
        carry_ref[bi] = pr[(bi + 1) * C - 1:(bi + 1) * C, :]
    ps = pr + mu_ref[...] * (prev - pr)
    r = ps[:, 0:W]
    kr = ps[:, W:2 * W]
    vr = ps[:, 2 * W:3 * W]
    g_rw = ps[:, 3 * W:4 * W]
    xwa = ps[:, 4 * W:4 * W + 2 * LORA]
    lane = lax.broadcasted_iota(jnp.int32, xwa.shape, 1)
    lora = _mm(jnp.where(lane < LORA, jnp.tanh(xwa), xwa), lora_ref[...])
    w0, a0, k_k, k_a, r_k, gn_g, gn_b = (vec_ref[i:i + 1, :] for i in range(7))
    lw = -np.float32(np.exp(-0.5)) * _sigmoid(w0 + lora[:, :W])
    a = _sigmoid(a0 + lora[:, W:])
    seg = seg_ref[...]

    def segsum(x):
        xs = jnp.concatenate([x[:, g * GROUP:(g + 1) * GROUP] for g in range(NG)], axis=0)
        tot = jnp.dot(xs.astype(bf16), seg, preferred_element_type=f32)
        return jnp.concatenate([tot[g * R:(g + 1) * R] for g in range(NG)], axis=1)

    kk = kr * k_k
    kkn = kk * lax.rsqrt(jnp.maximum(segsum(kk * kk), 1e-24))
    kmod = kr * (1.0 + (a - 1.0) * k_a)

    cos = jnp.concatenate([cos_ref[...]] * nb, axis=0)
    sin = jnp.concatenate([sin_ref[...]] * nb, axis=0)
    half = (lax.broadcasted_iota(jnp.int32, (R, RET_QK), 1) & (RET_DK - 1)) < RET_DK // 2

    def rope(x):
        swapped = jnp.where(half, pltpu.roll(x, RET_QK - RET_DK // 2, 1), pltpu.roll(x, RET_DK // 2, 1))
        return x * cos + swapped * sin

    q = rope(p_ref[:, :, 0:RET_QK].reshape(R, RET_QK))
    k = rope(p_ref[:, :, RET_QK:2 * RET_QK].reshape(R, RET_QK))
    v = p_ref[:, :, 2 * RET_QK:2 * RET_QK + RET_WIDTH].reshape(R, RET_WIDTH)
    keep_sq = _same_block((4 * C, GROUP), C, RWKV_HEAD)
    keep_v = _same_block((4 * C, RET_WIDTH), C, RET_DV)
    rets, s_new = _retention_chunks([rows(q, bi) for bi in range(nb)], [rows(k, bi) for bi in range(nb)],
                                    [rows(v, bi) for bi in range(nb)], [s_ref[bi] for bi in range(nb)],
                                    dec_ref[0], dec_ref[1], dec_ref[2], cdec_ref[...], keep_v)
    for bi in range(nb):
        s_ref[bi] = s_new[bi]
    ret = jnp.concatenate(rets, axis=0)
    rgn = rgn_ref[...]
    for h in range(RET_HEADS):
        sl = slice(h * RET_DV, (h + 1) * RET_DV)
        xh = ret[:, sl]
        d = xh - jnp.mean(xh, axis=-1, keepdims=True)
        var = jnp.mean(d * d, axis=-1, keepdims=True)
        gh = p_ref[:, :, 2 * RET_QK + RET_WIDTH + h * RET_DV:2 * RET_QK + RET_WIDTH + (h + 1) * RET_DV]
        gh = gh.reshape(R, RET_DV)
        yh = gh * _sigmoid(gh) * (d * lax.rsqrt(var + RET_GN_EPS) * rgn[:, sl])
        y_ref[:, :, sl] = yh.reshape(nb, C, RET_DV).astype(y_ref.dtype)

    chains = [(bi, g) for bi in range(nb) for g in range(NG)]

    def pick(x):
        return [x[bi * C:(bi + 1) * C, g * GROUP:(g + 1) * GROUP] for bi, g in chains]

    outs, m_new = _rwkv_chunks(pick(r), pick(kmod), pick(vr), pick(kkn), pick(a), pick(lw),
                               [m_ref[bi, g] for bi, g in chains], keep_sq)
    for i, (bi, g) in enumerate(chains):
        m_ref[bi, g] = m_new[i]
    o = jnp.concatenate([jnp.concatenate(outs[bi * NG:(bi + 1) * NG], axis=1) for bi in range(nb)], axis=0)
    d = o - segsum(o) * (1.0 / RWKV_HEAD)
    var = segsum(d * d) * (1.0 / RWKV_HEAD)
    o = d * lax.rsqrt(var + RWKV_GN_EPS) * gn_g + gn_b
    bonus = segsum(r * kmod * r_k) * vr
    y_rw = g_rw * _sigmoid(g_rw) * (o + bonus)
    y_ref[:, :, RET_WIDTH:] = y_rw.reshape(nb, C, W).astype(y_ref.dtype)


def _project_chunk(x_ref, y_ref, w_ref, g_ref, o_ref):
    nb, C, D = x_ref.shape
    y = y_ref[...].reshape(nb * C, D)
    h = x_ref[...].reshape(nb * C, D) + jnp.dot(y, w_ref[...], preferred_element_type=f32)
    out = h * lax.rsqrt(jnp.mean(h * h, axis=-1, keepdims=True) + RMS_EPS) * g_ref[...]
    o_ref[...] = out.reshape(nb, C, D)


def _mixer_kernel(p_ref, cos_ref, sin_ref, dec_ref, cdec_ref, mu_ref, lora_ref, vec_ref, rgn_ref,
                  seg_ref, x_ref, wout_ref, fg_ref, o_ref, y_ref, carry_ref, m_ref, s_ref, woutb_ref):
    @pl.when(pl.program_id(1) == 0)
    def _():
        carry_ref[...] = jnp.zeros_like(carry_ref)
        m_ref[...] = jnp.zeros_like(m_ref)
        s_ref[...] = jnp.zeros_like(s_ref)
        y_ref[...] = jnp.zeros_like(y_ref)
        woutb_ref[...] = wout_ref[...].astype(bf16)

    _project_chunk(x_ref, y_ref, woutb_ref, fg_ref, o_ref)
    _mix_chunk(p_ref, cos_ref, sin_ref, dec_ref, cdec_ref, mu_ref, lora_ref, vec_ref, rgn_ref,
               seg_ref, y_ref, carry_ref, m_ref, s_ref)


def _rope_tables(seq):
    half = RET_DK // 2
    expo = -jnp.arange(half, dtype=f32) / f32(half)
    freqs = jnp.exp(expo * f32(np.log(ROPE_BASE)))
    ang = jnp.arange(seq, dtype=jnp.int32).astype(f32)[:, None] * freqs[None, :]
    cos = jnp.cos(ang)
    sin = jnp.sin(ang)
    cos_full = jnp.tile(jnp.concatenate([cos, cos], axis=1), (1, RET_HEADS))
    sin_signed = jnp.tile(jnp.concatenate([-sin, sin], axis=1), (1, RET_HEADS))
    return cos_full, sin_signed


def _retention_constants():
    C = CHUNK
    lg = np.log(1.0 - np.exp2(-5.0 - np.arange(RET_HEADS, dtype=np.float64)))
    lane_lg = np.repeat(lg, RET_DK)[None, :]
    n = np.arange(C, dtype=np.float64)[:, None]
    m = np.tile(np.arange(C, dtype=np.float64), RET_HEADS)[None, :]
    scale = RET_DK ** -0.5
    decay4 = scale * np.exp(lane_lg * np.abs(n - m))
    qdec = np.exp(lane_lg * (n + 1.0)) * np.ones((1, RET_QK))
    kdec = scale * np.exp(lane_lg * (C - 1.0 - n)) * np.ones((1, RET_QK))
    dec = np.stack([decay4, qdec, kdec]).astype(np.float32)
    ri = np.arange(RET_QK)[:, None] // RET_DK
    ci = np.arange(RET_WIDTH)[None, :] // RET_DV
    cdec = np.where(ri == ci, np.exp(np.repeat(lg, RET_DK) * C)[:, None], 0.0).astype(np.float32)
    return jnp.asarray(dec), jnp.asarray(cdec)


def _const_spec(shape, single_buffer=False):
    mode = pl.Buffered(1) if single_buffer else None
    return pl.BlockSpec(shape, lambda *_: (0,) * len(shape), pipeline_mode=mode)


@jax.jit
def kernel(x, norm_g, w_in, ret_gn_g, rwkv_mu, w_lora_up, w0, a_lora_up, a0, k_k, k_a, r_k,
           rwkv_gn_g, rwkv_gn_b, w_out, final_norm_g):
    B, T, D = x.shape
    assert D == D_MODEL and T % CHUNK == 0 and (B * T) % PROJ_TILE == 0 and B % BATCH_BLOCK == 0
    assert norm_g.shape[0] == 1, "single-layer block"
    n_tok = B * T
    xf = x.reshape(n_tok, D)
    params = pltpu.CompilerParams(dimension_semantics=("arbitrary",), vmem_limit_bytes=VMEM_LIMIT)

    p = pl.pallas_call(
        _in_proj_kernel,
        grid=(n_tok // PROJ_TILE,),
        in_specs=[pl.BlockSpec((PROJ_TILE, D), lambda i: (i, 0)),
                  _const_spec((1, D)),
                  _const_spec((D, IN_COLS), single_buffer=True)],
        out_specs=pl.BlockSpec((PROJ_TILE, IN_COLS), lambda i: (i, 0)),
        out_shape=jax.ShapeDtypeStruct((n_tok, IN_COLS), f32),
        scratch_shapes=[pltpu.VMEM((D, IN_COLS), bf16)],
        compiler_params=params,
        name="in_proj",
    )(xf, norm_g[0][None, :], w_in[0])

    cos, sin = _rope_tables(T)
    dec, cdec = _retention_constants()
    zeros = jnp.zeros((LORA, RWKV_WIDTH), f32)
    lora_w = jnp.concatenate(
        [jnp.concatenate([w_lora_up[0], zeros], axis=1),
         jnp.concatenate([zeros, a_lora_up[0]], axis=1)], axis=0).astype(bf16)
    vecs = jnp.stack([w0[0], a0[0], k_k[0], k_a[0], r_k[0].reshape(-1), rwkv_gn_g[0], rwkv_gn_b[0],
                      jnp.zeros((RWKV_WIDTH,), f32)])
    seg_ids = np.arange(GROUP) // RWKV_HEAD
    seg = jnp.asarray(seg_ids[:, None] == seg_ids[None, :], bf16)

    n_chunks = T // CHUNK
    last = n_chunks - 1
    return pl.pallas_call(
        _mixer_kernel,
        grid=(B // BATCH_BLOCK, n_chunks + 1),
        in_specs=[pl.BlockSpec((BATCH_BLOCK, CHUNK, IN_COLS), lambda b, c: (b, jnp.minimum(c, last), 0)),
                  pl.BlockSpec((CHUNK, RET_QK), lambda b, c: (jnp.minimum(c, last), 0)),
                  pl.BlockSpec((CHUNK, RET_QK), lambda b, c: (jnp.minimum(c, last), 0)),
                  _const_spec((3, CHUNK, RET_QK)),
                  _const_spec((RET_QK, RET_WIDTH)),
                  _const_spec((1, RWKV_COLS)),
                  _const_spec((2 * LORA, 2 * RWKV_WIDTH)),
                  _const_spec((8, RWKV_WIDTH)),
                  _const_spec((1, RET_WIDTH)),
                  _const_spec((GROUP, GROUP)),
                  pl.BlockSpec((BATCH_BLOCK, CHUNK, D), lambda b, c: (b, jnp.maximum(c - 1, 0), 0)),
                  _const_spec((D, D), single_buffer=True),
                  _const_spec((1, D))],
        out_specs=pl.BlockSpec((BATCH_BLOCK, CHUNK, D), lambda b, c: (b, jnp.maximum(c - 1, 0), 0)),
        out_shape=jax.ShapeDtypeStruct((B, T, D), f32),
        scratch_shapes=[pltpu.VMEM((BATCH_BLOCK, CHUNK, D), bf16),
                        pltpu.VMEM((BATCH_BLOCK, 1, RWKV_COLS), f32),
                        pltpu.VMEM((BATCH_BLOCK, RWKV_WIDTH // GROUP, GROUP, GROUP), f32),
                        pltpu.VMEM((BATCH_BLOCK, RET_QK, RET_WIDTH), f32),
                        pltpu.VMEM((D, D), bf16)],
        compiler_params=pltpu.CompilerParams(dimension_semantics=("arbitrary", "arbitrary"),
                                             vmem_limit_bytes=VMEM_LIMIT),
        name="mixers",
    )(p.reshape(B, T, IN_COLS), cos, sin, dec, cdec, rwkv_mu[0][None, :], lora_w, vecs,
      ret_gn_g[0][None, :], seg, x, w_out[0], final_norm_g[None, :])
```

```python
import functools

import numpy as np
import jax
import jax.numpy as jnp
from jax import lax
from jax.experimental import pallas as pl
from jax.experimental.pallas import tpu as pltpu

D_MODEL = 1024
CHUNK = 64
RET_HEADS = 4
RET_DK = 64
RET_DV = 128
RET_QK = RET_HEADS * RET_DK
RET_WIDTH = RET_HEADS * RET_DV
RWKV_WIDTH = 512
RWKV_HEAD = 64
LORA = 64
RET_COLS = 2 * RET_QK + 2 * RET_WIDTH
RWKV_COLS = 4 * RWKV_WIDTH + 2 * LORA
IN_COLS = RET_COLS + RWKV_COLS
ROPE_BASE = 10000.0
RMS_EPS = 1e-6
RET_GN_EPS = 1e-5
RWKV_GN_EPS = 64e-5
GROUP = 256
PROJ_TILE = 512
BATCH_BLOCK = 8
VMEM_LIMIT = 56 * 1024 * 1024

f32 = jnp.float32
bf16 = jnp.bfloat16


def _mm(a, b):
    return jnp.dot(a.astype(bf16), b.astype(bf16), preferred_element_type=f32)


def _mm_nt(a, b):
    return lax.dot_general(a.astype(bf16), b.astype(bf16), (((1,), (1,)), ((), ())),
                           preferred_element_type=f32)


def _mm_tn(a, b):
    return lax.dot_general(a.astype(bf16), b.astype(bf16), (((0,), (0,)), ((), ())),
                           preferred_element_type=f32)


def _split2(x):
    hi = x.astype(bf16)
    lo = (x - hi.astype(f32)).astype(bf16)
    return hi, lo


def _same_block(shape, row_width, col_width):
    ri = lax.broadcasted_iota(jnp.int32, shape, 0) >> (row_width.bit_length() - 1)
    ci = lax.broadcasted_iota(jnp.int32, shape, 1) >> (col_width.bit_length() - 1)
    return ri == ci


def _block_diag(x):
    x = x.astype(bf16)
    rows, lanes = x.shape
    zero = jnp.zeros((rows, 128), bf16)
    cols = []
    if lanes == 4 * 128:
        for j in range(4):
            cols.append(jnp.concatenate([x[:, j * 128:(j + 1) * 128] if i == j else zero for i in range(4)],
                                        axis=0))
    else:
        low = lax.broadcasted_iota(jnp.int32, (rows, 128), 1) < 64
        for j in range(2):
            xj = x[:, j * 128:(j + 1) * 128]
            pair = [jnp.where(low, xj, zero), jnp.where(low, zero, xj)]
            cols.append(jnp.concatenate([zero] * (2 * j) + pair + [zero] * (2 - 2 * j), axis=0))
    return jnp.concatenate(cols, axis=1)


def _sigmoid(x):
    return 1.0 / (1.0 + jnp.exp(-x))


def _in_proj_kernel(tiles_per_seq, x_ref, g_ref, w_ref, cos_ref, sin_ref, mu_ref, p_ref, wb_ref, carry_ref):
    i = pl.program_id(0)

    @pl.when(i == 0)
    def _():
        wb_ref[...] = w_ref[...].astype(bf16)

    x = x_ref[...]
    u = (x * lax.rsqrt(jnp.mean(x * x, axis=-1, keepdims=True) + RMS_EPS) * g_ref[...]).astype(bf16)
    rows = x.shape[0]
    seq_start = (i % tiles_per_seq) == 0
    W = RWKV_WIDTH

    def rope(pg):
        cos = jnp.concatenate([cos_ref[...]] * 2, axis=1)
        sin = jnp.concatenate([sin_ref[...]] * 2, axis=1)
        half = (lax.broadcasted_iota(jnp.int32, pg.shape, 1) & (RET_DK - 1)) < RET_DK // 2
        lanes = pg.shape[1]
        swapped = jnp.where(half, pltpu.roll(pg, lanes - RET_DK // 2, 1), pltpu.roll(pg, RET_DK // 2, 1))
        return pg * cos + swapped * sin

    def silu(pg):
        return pg * _sigmoid(pg)

    def shifted(lo, hi):
        def fn(pg):
            first = jnp.where(seq_start, 0.0, carry_ref[:, lo:hi])
            row0 = lax.broadcasted_iota(jnp.int32, pg.shape, 0) == 0
            prev = jnp.where(row0, first, pltpu.roll(pg, 1, 0))
            carry_ref[:, lo:hi] = pg[rows - 1:rows, :]
            return pg + mu_ref[:, lo:hi] * (prev - pg)
        return fn

    groups = [(0, 2 * RET_QK, rope),
              (2 * RET_QK, 2 * RET_QK + RET_WIDTH, lambda pg: pg),
              (2 * RET_QK + RET_WIDTH, RET_COLS, silu),
              (RET_COLS, RET_COLS + W, shifted(0, W)),
              (RET_COLS + W, RET_COLS + 2 * W, shifted(W, 2 * W)),
              (RET_COLS + 2 * W, RET_COLS + 3 * W, shifted(2 * W, 3 * W)),
              (RET_COLS + 3 * W, RET_COLS + 4 * W, lambda pg: silu(shifted(3 * W, 4 * W)(pg))),
              (RET_COLS + 4 * W, IN_COLS, shifted(4 * W, RWKV_COLS))]
    pending = None
    for lo, hi, fn in groups:
        acc = jnp.dot(u, wb_ref[:, lo:hi], preferred_element_type=f32)
        if pending is not None:
            p_ref[:, pending[1]:pending[2]] = pending[3](pending[0])
        pending = (acc, lo, hi, fn)
    p_ref[:, pending[1]:pending[2]] = pending[3](pending[0])


def _retention_chunks(qs, ks, vs, s_bds, decay4, qdec, kdec, cdec, keep_v):
    n = range(len(qs))
    scores = [_mm_nt(qs[i], _block_diag(ks[i])) * decay4 for i in n]
    intra = [_mm(scores[i], _block_diag(vs[i])) for i in n]
    inter = [_mm(qs[i] * qdec, s_bds[i]) for i in n]
    kv = [_mm_tn(ks[i] * kdec, vs[i]) for i in n]
    s_new = [s_bds[i] * cdec + jnp.where(keep_v, kv[i], 0.0) for i in n]
    return [intra[i] + inter[i] for i in n], s_new


def _rwkv_chunks(rs, ks, vs, kkns, aas, lws, m_bds, keep):
    C = CHUNK
    n = range(len(rs))
    ti = lax.broadcasted_iota(jnp.int32, (C, C), 0)
    si = lax.broadcasted_iota(jnp.int32, (C, C), 1)
    tri_incl = jnp.where(si <= ti, 1.0, 0.0).astype(bf16)
    tri_incl = jnp.concatenate([tri_incl, tri_incl], axis=1)
    cum = [jnp.dot(tri_incl, jnp.concatenate(_split2(lws[i]), axis=0), preferred_element_type=f32) for i in n]
    g_inc = [jnp.exp(cum[i]) for i in n]
    g_exc = [jnp.exp(cum[i] - lws[i]) for i in n]
    g_inv = [jnp.exp(-cum[i]) for i in n]
    g_end = [jnp.exp(cum[i][C - 1:C, :] - cum[i]) for i in n]
    beta = [kkns[i] * aas[i] for i in n]
    lhs = [jnp.concatenate([-kkns[i] * g_exc[i], rs[i] * g_inc[i]], axis=0) for i in n]
    amat = [_mm_nt(lhs[i], jnp.concatenate([_block_diag(beta[i] * g_inv[i]),
                                            _block_diag(ks[i] * g_inv[i])], axis=0)) for i in n]
    ti = lax.broadcasted_iota(jnp.int32, (C, GROUP), 0)
    si = lax.broadcasted_iota(jnp.int32, (C, GROUP), 1) & (C - 1)
    strict = si < ti
    incl = si <= ti
    a_ab = [jnp.where(strict, amat[i][:C, :GROUP], 0.0) for i in n]
    a_ak = [jnp.where(strict, amat[i][:C, GROUP:], 0.0) for i in n]
    a_rb = [jnp.where(incl, amat[i][C:, :GROUP], 0.0) for i in n]
    a_rk = [jnp.where(incl, amat[i][C:, GROUP:], 0.0) for i in n]
    mv = [_mm_nt(lhs[i], m_bds[i]) for i in n]
    av = [_mm(jnp.concatenate([a_ak[i], a_rk[i]], axis=0), _block_diag(vs[i])) for i in n]
    eye = jnp.where(si == ti, 1.0, 0.0)
    t_inv = [eye + a_ab[i] for i in n]
    s_pow = [_mm(a_ab[i], _block_diag(a_ab[i])) for i in n]
    for _ in range(4):
        st = [_mm(jnp.concatenate([s_pow[i], t_inv[i]], axis=0), _block_diag(s_pow[i])) for i in n]
        s_pow = [st[i][:C] for i in n]
        t_inv = [t_inv[i] + st[i][C:] for i in n]
    t_inv = [t_inv[i] + _mm(t_inv[i], _block_diag(s_pow[i])) for i in n]
    u = [_mm(t_inv[i], _block_diag(mv[i][:C] + av[i][:C])) for i in n]
    out = [mv[i][C:] + av[i][C:] + _mm(a_rb[i], _block_diag(u[i])) for i in n]
    upd = [_mm_tn(jnp.concatenate([u[i], vs[i]], axis=0),
                  jnp.concatenate([beta[i] * g_end[i], ks[i] * g_end[i]], axis=0)) for i in n]
    m_new = [m_bds[i] * g_inc[i][C - 1:C, :] + jnp.where(keep, upd[i], 0.0) for i in n]
    return out, m_new


def _mix_chunk(p_ref, dec_ref, cdec_ref, lora_ref, vec_ref, rgn_ref, seg_ref, y_ref, m_ref, s_ref):
    C = CHUNK
    nb = BATCH_BLOCK
    R = nb * C
    W = RWKV_WIDTH
    NG = W // GROUP

    def rows(x, bi):
        return x[bi * C:(bi + 1) * C]

    def rwkv_cols(lo, hi):
        return p_ref[:, :, RET_COLS + lo:RET_COLS + hi].reshape(R, hi - lo)

    r = rwkv_cols(0, W)
    kr = rwkv_cols(W, 2 * W)
    vr = rwkv_cols(2 * W, 3 * W)
    gate_rw = rwkv_cols(3 * W, 4 * W)
    xwa = rwkv_cols(4 * W, RWKV_COLS)
    lane = lax.broadcasted_iota(jnp.int32, xwa.shape, 1)
    lora = _mm(jnp.where(lane < LORA, jnp.tanh(xwa), xwa), lora_ref[...])
    w0, a0, k_k, k_a, r_k, gn_g, gn_b = (vec_ref[i:i + 1, :] for i in range(7))
    lw = -np.float32(np.exp(-0.5)) * _sigmoid(w0 + lora[:, :W])
    a = _sigmoid(a0 + lora[:, W:])
    seg = seg_ref[...]

    def segsum(x):
        xs = jnp.concatenate([x[:, g * GROUP:(g + 1) * GROUP] for g in range(NG)], axis=0)
        tot = jnp.dot(xs.astype(bf16), seg, preferred_element_type=f32)
        return jnp.concatenate([tot[g * R:(g + 1) * R] for g in range(NG)], axis=1)

    kk = kr * k_k
    kkn = kk * lax.rsqrt(jnp.maximum(segsum(kk * kk), 1e-24))
    kmod = kr * (1.0 + (a - 1.0) * k_a)

    q = p_ref[:, :, 0:RET_QK].reshape(R, RET_QK)
    k = p_ref[:, :, RET_QK:2 * RET_QK].reshape(R, RET_QK)
    v = p_ref[:, :, 2 * RET_QK:2 * RET_QK + RET_WIDTH].reshape(R, RET_WIDTH)
    keep_sq = _same_block((4 * C, GROUP), C, RWKV_HEAD)
    keep_v = _same_block((4 * C, RET_WIDTH), C, RET_DV)
    rets, s_new = _retention_chunks([rows(q, bi) for bi in range(nb)], [rows(k, bi) for bi in range(nb)],
                                    [rows(v, bi) for bi in range(nb)], [s_ref[bi] for bi in range(nb)],
                                    dec_ref[0], dec_ref[1], dec_ref[2], cdec_ref[...], keep_v)
    for bi in range(nb):
        s_ref[bi] = s_new[bi]
    ret = jnp.concatenate(rets, axis=0)
    rgn = rgn_ref[...]
    for h in range(RET_HEADS):
        sl = slice(h * RET_DV, (h + 1) * RET_DV)
        xh = ret[:, sl]
        d = xh - jnp.mean(xh, axis=-1, keepdims=True)
        var = jnp.mean(d * d, axis=-1, keepdims=True)
        gh = p_ref[:, :, 2 * RET_QK + RET_WIDTH + h * RET_DV:2 * RET_QK + RET_WIDTH + (h + 1) * RET_DV]
        gh = gh.reshape(R, RET_DV)
        yh = gh * (d * lax.rsqrt(var + RET_GN_EPS) * rgn[:, sl])
        y_ref[:, :, sl] = yh.reshape(nb, C, RET_DV).astype(y_ref.dtype)

    chains = [(bi, g) for bi in range(nb) for g in range(NG)]

    def pick(x):
        return [x[bi * C:(bi + 1) * C, g * GROUP:(g + 1) * GROUP] for bi, g in chains]

    outs, m_new = _rwkv_chunks(pick(r), pick(kmod), pick(vr), pick(kkn), pick(a), pick(lw),
                               [m_ref[bi, g] for bi, g in chains], keep_sq)
    for i, (bi, g) in enumerate(chains):
        m_ref[bi, g] = m_new[i]
    o = jnp.concatenate([jnp.concatenate(outs[bi * NG:(bi + 1) * NG], axis=1) for bi in range(nb)], axis=0)
    d = o - segsum(o) * (1.0 / RWKV_HEAD)
    var = segsum(d * d) * (1.0 / RWKV_HEAD)
    o = d * lax.rsqrt(var + RWKV_GN_EPS) * gn_g + gn_b
    bonus = segsum(r * kmod * r_k) * vr
    y_rw = gate_rw * (o + bonus)
    y_ref[:, :, RET_WIDTH:] = y_rw.reshape(nb, C, W).astype(y_ref.dtype)


def _project_chunk(x_ref, y_ref, w_ref, g_ref, o_ref):
    nb, C, D = x_ref.shape
    y = y_ref[...].reshape(nb * C, D)
    h = x_ref[...].reshape(nb * C, D) + jnp.dot(y, w_ref[...], preferred_element_type=f32)
    out = h * lax.rsqrt(jnp.mean(h * h, axis=-1, keepdims=True) + RMS_EPS) * g_ref[...]
    o_ref[...] = out.reshape(nb, C, D)


def _mixer_kernel(p_ref, dec_ref, cdec_ref, lora_ref, vec_ref, rgn_ref,
                  seg_ref, x_ref, wout_ref, fg_ref, o_ref, y_ref, m_ref, s_ref, woutb_ref):
    @pl.when(pl.program_id(1) == 0)
    def _():
        m_ref[...] = jnp.zeros_like(m_ref)
        s_ref[...] = jnp.zeros_like(s_ref)
        y_ref[...] = jnp.zeros_like(y_ref)
        woutb_ref[...] = wout_ref[...].astype(bf16)

    _project_chunk(x_ref, y_ref, woutb_ref, fg_ref, o_ref)
    _mix_chunk(p_ref, dec_ref, cdec_ref, lora_ref, vec_ref, rgn_ref, seg_ref, y_ref, m_ref, s_ref)


def _rope_tables(seq):
    half = RET_DK // 2
    expo = -jnp.arange(half, dtype=f32) / f32(half)
    freqs = jnp.exp(expo * f32(np.log(ROPE_BASE)))
    ang = jnp.arange(seq, dtype=jnp.int32).astype(f32)[:, None] * freqs[None, :]
    cos = jnp.cos(ang)
    sin = jnp.sin(ang)
    cos_full = jnp.tile(jnp.concatenate([cos, cos], axis=1), (1, RET_HEADS))
    sin_signed = jnp.tile(jnp.concatenate([-sin, sin], axis=1), (1, RET_HEADS))
    return cos_full, sin_signed


def _retention_constants():
    C = CHUNK
    lg = np.log(1.0 - np.exp2(-5.0 - np.arange(RET_HEADS, dtype=np.float64)))
    lane_lg = np.repeat(lg, RET_DK)[None, :]
    n = np.arange(C, dtype=np.float64)[:, None]
    m = np.tile(np.arange(C, dtype=np.float64), RET_HEADS)[None, :]
    scale = RET_DK ** -0.5
    decay4 = scale * np.exp(lane_lg * np.abs(n - m))
    qdec = np.exp(lane_lg * (n + 1.0)) * np.ones((1, RET_QK))
    kdec = scale * np.exp(lane_lg * (C - 1.0 - n)) * np.ones((1, RET_QK))
    dec = np.stack([decay4, qdec, kdec]).astype(np.float32)
    ri = np.arange(RET_QK)[:, None] // RET_DK
    ci = np.arange(RET_WIDTH)[None, :] // RET_DV
    cdec = np.where(ri == ci, np.exp(np.repeat(lg, RET_DK) * C)[:, None], 0.0).astype(np.float32)
    return jnp.asarray(dec), jnp.asarray(cdec)


def _const_spec(shape, single_buffer=False):
    mode = pl.Buffered(1) if single_buffer else None
    return pl.BlockSpec(shape, lambda *_: (0,) * len(shape), pipeline_mode=mode)


@jax.jit
def kernel(x, norm_g, w_in, ret_gn_g, rwkv_mu, w_lora_up, w0, a_lora_up, a0, k_k, k_a, r_k,
           rwkv_gn_g, rwkv_gn_b, w_out, final_norm_g):
    B, T, D = x.shape
    assert D == D_MODEL and T % CHUNK == 0 and T % PROJ_TILE == 0 and B % BATCH_BLOCK == 0
    assert norm_g.shape[0] == 1, "single-layer block"
    n_tok = B * T
    xf = x.reshape(n_tok, D)
    params = pltpu.CompilerParams(dimension_semantics=("arbitrary",), vmem_limit_bytes=VMEM_LIMIT)
    tiles_per_seq = T // PROJ_TILE
    cos, sin = _rope_tables(T)

    p = pl.pallas_call(
        functools.partial(_in_proj_kernel, tiles_per_seq),
        grid=(n_tok // PROJ_TILE,),
        in_specs=[pl.BlockSpec((PROJ_TILE, D), lambda i: (i, 0)),
                  _const_spec((1, D)),
                  _const_spec((D, IN_COLS), single_buffer=True),
                  pl.BlockSpec((PROJ_TILE, RET_QK), lambda i: (i % tiles_per_seq, 0)),
                  pl.BlockSpec((PROJ_TILE, RET_QK), lambda i: (i % tiles_per_seq, 0)),
                  _const_spec((1, RWKV_COLS))],
        out_specs=pl.BlockSpec((PROJ_TILE, IN_COLS), lambda i: (i, 0)),
        out_shape=jax.ShapeDtypeStruct((n_tok, IN_COLS), f32),
        scratch_shapes=[pltpu.VMEM((D, IN_COLS), bf16),
                        pltpu.VMEM((1, RWKV_COLS), f32)],
        compiler_params=params,
        name="in_proj",
    )(xf, norm_g[0][None, :], w_in[0], cos, sin, rwkv_mu[0][None, :])

    dec, cdec = _retention_constants()
    zeros = jnp.zeros((LORA, RWKV_WIDTH), f32)
    lora_w = jnp.concatenate(
        [jnp.concatenate([w_lora_up[0], zeros], axis=1),
         jnp.concatenate([zeros, a_lora_up[0]], axis=1)], axis=0).astype(bf16)
    vecs = jnp.stack([w0[0], a0[0], k_k[0], k_a[0], r_k[0].reshape(-1), rwkv_gn_g[0], rwkv_gn_b[0],
                      jnp.zeros((RWKV_WIDTH,), f32)])
    seg_ids = np.arange(GROUP) // RWKV_HEAD
    seg = jnp.asarray(seg_ids[:, None] == seg_ids[None, :], bf16)

    n_chunks = T // CHUNK
    last = n_chunks - 1
    return pl.pallas_call(
        _mixer_kernel,
        grid=(B // BATCH_BLOCK, n_chunks + 1),
        in_specs=[pl.BlockSpec((BATCH_BLOCK, CHUNK, IN_COLS), lambda b, c: (b, jnp.minimum(c, last), 0)),
                  _const_spec((3, CHUNK, RET_QK)),
                  _const_spec((RET_QK, RET_WIDTH)),
                  _const_spec((2 * LORA, 2 * RWKV_WIDTH)),
                  _const_spec((8, RWKV_WIDTH)),
                  _const_spec((1, RET_WIDTH)),
                  _const_spec((GROUP, GROUP)),
                  pl.BlockSpec((BATCH_BLOCK, CHUNK, D), lambda b, c: (b, jnp.maximum(c - 1, 0), 0)),
                  _const_spec((D, D), single_buffer=True),
                  _const_spec((1, D))],
        out_specs=pl.BlockSpec((BATCH_BLOCK, CHUNK, D), lambda b, c: (b, jnp.maximum(c - 1, 0), 0)),
        out_shape=jax.ShapeDtypeStruct((B, T, D), f32),
        scratch_shapes=[pltpu.VMEM((BATCH_BLOCK, CHUNK, D), bf16),
                        pltpu.VMEM((BATCH_BLOCK, RWKV_WIDTH // GROUP, GROUP, GROUP), f32),
                        pltpu.VMEM((BATCH_BLOCK, RET_QK, RET_WIDTH), f32),
                        pltpu.VMEM((D, D), bf16)],
        compiler_params=pltpu.CompilerParams(dimension_semantics=("arbitrary", "arbitrary"),
                                             vmem_limit_bytes=VMEM_LIMIT),
        name="mixers",
    )(p.reshape(B, T, IN_COLS), dec, cdec, lora_w, vecs,
      ret_gn_g[0][None, :], seg, x, w_out[0], final_norm_g[None, :])
```

```python
import functools

import numpy as np
import jax
import jax.numpy as jnp
from jax import lax
from jax.experimental import pallas as pl
from jax.experimental.pallas import tpu as pltpu

D_MODEL = 1024
CHUNK = 64
RET_HEADS = 4
RET_DK = 64
RET_DV = 128
RET_QK = RET_HEADS * RET_DK
RET_WIDTH = RET_HEADS * RET_DV
RWKV_WIDTH = 512
RWKV_HEAD = 64
LORA = 64
RET_COLS = 2 * RET_QK + 2 * RET_WIDTH
RWKV_COLS = 4 * RWKV_WIDTH + 2 * LORA
IN_COLS = RET_COLS + RWKV_COLS
ROPE_BASE = 10000.0
RMS_EPS = 1e-6
RET_GN_EPS = 1e-5
RWKV_GN_EPS = 64e-5
GROUP = 256
PROJ_TILE = 512
BATCH_BLOCK = 8
VMEM_LIMIT = 56 * 1024 * 1024

f32 = jnp.float32
bf16 = jnp.bfloat16


def _mm(a, b):
    return jnp.dot(a.astype(bf16), b.astype(bf16), preferred_element_type=f32)


def _mm_nt(a, b):
    return lax.dot_general(a.astype(bf16), b.astype(bf16), (((1,), (1,)), ((), ())),
                           preferred_element_type=f32)


def _split2(x):
    hi = x.astype(bf16)
    lo = (x - hi.astype(f32)).astype(bf16)
    return hi, lo


def _block_diag(x):
    x = x.astype(bf16)
    rows, lanes = x.shape
    zero = jnp.zeros((rows, 128), bf16)
    cols = []
    if lanes == 4 * 128:
        for j in range(4):
            cols.append(jnp.concatenate([x[:, j * 128:(j + 1) * 128] if i == j else zero for i in range(4)],
                                        axis=0))
    else:
        low = lax.broadcasted_iota(jnp.int32, (rows, 128), 1) < 64
        for j in range(2):
            xj = x[:, j * 128:(j + 1) * 128]
            pair = [jnp.where(low, xj, zero), jnp.where(low, zero, xj)]
            cols.append(jnp.concatenate([zero] * (2 * j) + pair + [zero] * (2 - 2 * j), axis=0))
    return jnp.concatenate(cols, axis=1)


def _sigmoid(x):
    return 1.0 / (1.0 + jnp.exp(-x))


def _in_proj_kernel(tiles_per_seq, x_ref, g_ref, w_ref, cos_ref, sin_ref, mu_ref, p_ref, wb_ref, carry_ref):
    i = pl.program_id(0)

    @pl.when(i == 0)
    def _():
        wb_ref[...] = w_ref[...].astype(bf16)

    x = x_ref[...]
    u = (x * lax.rsqrt(jnp.mean(x * x, axis=-1, keepdims=True) + RMS_EPS) * g_ref[...]).astype(bf16)
    rows = x.shape[0]
    seq_start = (i % tiles_per_seq) == 0
    W = RWKV_WIDTH

    def rope(pg):
        cos = jnp.concatenate([cos_ref[...]] * 2, axis=1)
        sin = jnp.concatenate([sin_ref[...]] * 2, axis=1)
        half = (lax.broadcasted_iota(jnp.int32, pg.shape, 1) & (RET_DK - 1)) < RET_DK // 2
        lanes = pg.shape[1]
        swapped = jnp.where(half, pltpu.roll(pg, lanes - RET_DK // 2, 1), pltpu.roll(pg, RET_DK // 2, 1))
        return pg * cos + swapped * sin

    def silu(pg):
        return pg * _sigmoid(pg)

    def shifted(lo, hi):
        def fn(pg):
            first = jnp.where(seq_start, 0.0, carry_ref[:, lo:hi])
            row0 = lax.broadcasted_iota(jnp.int32, pg.shape, 0) == 0
            prev = jnp.where(row0, first, pltpu.roll(pg, 1, 0))
            carry_ref[:, lo:hi] = pg[rows - 1:rows, :]
            return pg + mu_ref[:, lo:hi] * (prev - pg)
        return fn

    groups = [(0, 2 * RET_QK, rope),
              (2 * RET_QK, 2 * RET_QK + RET_WIDTH, lambda pg: pg),
              (2 * RET_QK + RET_WIDTH, RET_COLS, silu),
              (RET_COLS, RET_COLS + W, shifted(0, W)),
              (RET_COLS + W, RET_COLS + 2 * W, shifted(W, 2 * W)),
              (RET_COLS + 2 * W, RET_COLS + 3 * W, shifted(2 * W, 3 * W)),
              (RET_COLS + 3 * W, RET_COLS + 4 * W, lambda pg: silu(shifted(3 * W, 4 * W)(pg))),
              (RET_COLS + 4 * W, IN_COLS, shifted(4 * W, RWKV_COLS))]
    pending = None
    for lo, hi, fn in groups:
        acc = jnp.dot(u, wb_ref[:, lo:hi], preferred_element_type=f32)
        if pending is not None:
            p_ref[:, pending[1]:pending[2]] = pending[3](pending[0])
        pending = (acc, lo, hi, fn)
    p_ref[:, pending[1]:pending[2]] = pending[3](pending[0])


def _head_blocks(s):
    s = s.astype(bf16)
    zero = jnp.zeros(s.shape[1:], bf16)
    return jnp.concatenate([jnp.concatenate([s[h] if i == h else zero for i in range(RET_HEADS)], axis=0)
                            for h in range(RET_HEADS)], axis=1)


def _retention_chunks(qs, ks, vs, states, decay4, qdec, kdec, head_decay):
    n = range(len(qs))
    hs = range(RET_HEADS)
    scores = [_mm_nt(qs[i], _block_diag(ks[i])) * decay4 for i in n]
    out = [_mm(jnp.concatenate([scores[i], qs[i] * qdec], axis=1),
               jnp.concatenate([_block_diag(vs[i]), _head_blocks(states[i])], axis=0)) for i in n]
    kts = [(ks[i] * kdec).T for i in n]
    kv = [[_mm(kts[i][h * RET_DK:(h + 1) * RET_DK], vs[i][:, h * RET_DV:(h + 1) * RET_DV]) for h in hs]
          for i in n]
    s_new = [jnp.stack([states[i][h] * head_decay[h] + kv[i][h] for h in hs]) for i in n]
    return out, s_new


def _rwkv_chunks(rs, ks, vs, kkns, aas, lws, m_cs):
    C = CHUNK
    n = range(len(rs))
    ti = lax.broadcasted_iota(jnp.int32, (C, C), 0)
    si = lax.broadcasted_iota(jnp.int32, (C, C), 1)
    tri_incl = jnp.where(si <= ti, 1.0, 0.0).astype(bf16)
    tri_incl = jnp.concatenate([tri_incl, tri_incl], axis=1)
    cum = [jnp.dot(tri_incl, jnp.concatenate(_split2(lws[i]), axis=0), preferred_element_type=f32) for i in n]
    g_inc = [jnp.exp(cum[i]) for i in n]
    g_exc = [jnp.exp(cum[i] - lws[i]) for i in n]
    g_inv = [jnp.exp(-cum[i]) for i in n]
    g_end = [jnp.exp(cum[i][C - 1:C, :] - cum[i]) for i in n]
    beta = [kkns[i] * aas[i] for i in n]
    lhs = [jnp.concatenate([-kkns[i] * g_exc[i], rs[i] * g_inc[i]], axis=0) for i in n]
    amat = [_mm_nt(lhs[i], jnp.concatenate([_block_diag(beta[i] * g_inv[i]),
                                            _block_diag(ks[i] * g_inv[i])], axis=0)) for i in n]
    ti = lax.broadcasted_iota(jnp.int32, (C, GROUP), 0)
    si = lax.broadcasted_iota(jnp.int32, (C, GROUP), 1) & (C - 1)
    strict = si < ti
    incl = si <= ti
    a_ab = [jnp.where(strict, amat[i][:C, :GROUP], 0.0) for i in n]
    a_ak = [jnp.where(strict, amat[i][:C, GROUP:], 0.0) for i in n]
    a_rb = [jnp.where(incl, amat[i][C:, :GROUP], 0.0) for i in n]
    a_rk = [jnp.where(incl, amat[i][C:, GROUP:], 0.0) for i in n]
    mv = [_mm_nt(lhs[i], _block_diag(m_cs[i])) for i in n]
    av = [_mm(jnp.concatenate([a_ak[i], a_rk[i]], axis=0), _block_diag(vs[i])) for i in n]
    eye = jnp.where(si == ti, 1.0, 0.0)
    t_inv = [eye + a_ab[i] for i in n]
    s_pow = [_mm(a_ab[i], _block_diag(a_ab[i])) for i in n]
    for _ in range(4):
        st = [_mm(jnp.concatenate([s_pow[i], t_inv[i]], axis=0), _block_diag(s_pow[i])) for i in n]
        s_pow = [st[i][:C] for i in n]
        t_inv = [t_inv[i] + st[i][C:] for i in n]
    t_inv = [t_inv[i] + _mm(t_inv[i], _block_diag(s_pow[i])) for i in n]
    u = [_mm(t_inv[i], _block_diag(mv[i][:C] + av[i][:C])) for i in n]
    out = [mv[i][C:] + av[i][C:] + _mm(a_rb[i], _block_diag(u[i])) for i in n]
    uvt = [jnp.concatenate([u[i], vs[i]], axis=0).T for i in n]
    key_end = [jnp.concatenate([beta[i] * g_end[i], ks[i] * g_end[i]], axis=0) for i in n]
    upd = [[_mm(uvt[i][h * RWKV_HEAD:(h + 1) * RWKV_HEAD], key_end[i][:, (h // 2) * 128:(h // 2 + 1) * 128])
            for h in range(GROUP // RWKV_HEAD)] for i in n]
    low = lax.broadcasted_iota(jnp.int32, (RWKV_HEAD, 128), 1) < RWKV_HEAD
    m_new = [m_cs[i] * g_inc[i][C - 1:C, :]
             + jnp.concatenate([jnp.where(low, upd[i][2 * j], upd[i][2 * j + 1]) for j in range(GROUP // 128)],
                               axis=1) for i in n]
    return out, m_new


def _mix_chunk(p_ref, dec_ref, lora_ref, vec_ref, rgn_ref, seg_ref, y_ref, m_ref, s_ref):
    C = CHUNK
    nb = BATCH_BLOCK
    R = nb * C
    W = RWKV_WIDTH
    NG = W // GROUP

    def rows(x, bi):
        return x[bi * C:(bi + 1) * C]

    def rwkv_cols(lo, hi):
        return p_ref[:, :, RET_COLS + lo:RET_COLS + hi].reshape(R, hi - lo)

    r = rwkv_cols(0, W)
    kr = rwkv_cols(W, 2 * W)
    vr = rwkv_cols(2 * W, 3 * W)
    gate_rw = rwkv_cols(3 * W, 4 * W)
    xwa = rwkv_cols(4 * W, RWKV_COLS)
    lane = lax.broadcasted_iota(jnp.int32, xwa.shape, 1)
    lora = _mm(jnp.where(lane < LORA, jnp.tanh(xwa), xwa), lora_ref[...])
    w0, a0, k_k, k_a, r_k, gn_g, gn_b = (vec_ref[i:i + 1, :] for i in range(7))
    lw = -np.float32(np.exp(-0.5)) * _sigmoid(w0 + lora[:, :W])
    a = _sigmoid(a0 + lora[:, W:])
    seg = seg_ref[...]

    def segsum(x):
        xs = jnp.concatenate([x[:, g * GROUP:(g + 1) * GROUP] for g in range(NG)], axis=0)
        tot = jnp.dot(xs.astype(bf16), seg, preferred_element_type=f32)
        return jnp.concatenate([tot[g * R:(g + 1) * R] for g in range(NG)], axis=1)

    kk = kr * k_k
    kkn = kk * lax.rsqrt(jnp.maximum(segsum(kk * kk), 1e-24))
    kmod = kr * (1.0 + (a - 1.0) * k_a)

    q = p_ref[:, :, 0:RET_QK].reshape(R, RET_QK)
    k = p_ref[:, :, RET_QK:2 * RET_QK].reshape(R, RET_QK)
    v = p_ref[:, :, 2 * RET_QK:2 * RET_QK + RET_WIDTH].reshape(R, RET_WIDTH)
    rets, s_new = _retention_chunks([rows(q, bi) for bi in range(nb)], [rows(k, bi) for bi in range(nb)],
                                    [rows(v, bi) for bi in range(nb)], [s_ref[bi] for bi in range(nb)],
                                    dec_ref[0], dec_ref[1], dec_ref[2], _RET_CHUNK_DECAY)
    for bi in range(nb):
        s_ref[bi] = s_new[bi]
    ret = jnp.concatenate(rets, axis=0)
    rgn = rgn_ref[...]
    for h in range(RET_HEADS):
        sl = slice(h * RET_DV, (h + 1) * RET_DV)
        xh = ret[:, sl]
        d = xh - jnp.mean(xh, axis=-1, keepdims=True)
        var = jnp.mean(d * d, axis=-1, keepdims=True)
        gh = p_ref[:, :, 2 * RET_QK + RET_WIDTH + h * RET_DV:2 * RET_QK + RET_WIDTH + (h + 1) * RET_DV]
        gh = gh.reshape(R, RET_DV)
        yh = gh * (d * lax.rsqrt(var + RET_GN_EPS) * rgn[:, sl])
        y_ref[:, :, sl] = yh.reshape(nb, C, RET_DV).astype(y_ref.dtype)

    chains = [(bi, g) for bi in range(nb) for g in range(NG)]

    def pick(x):
        return [x[bi * C:(bi + 1) * C, g * GROUP:(g + 1) * GROUP] for bi, g in chains]

    outs, m_new = _rwkv_chunks(pick(r), pick(kmod), pick(vr), pick(kkn), pick(a), pick(lw),
                               [m_ref[bi, g] for bi, g in chains])
    for i, (bi, g) in enumerate(chains):
        m_ref[bi, g] = m_new[i]
    o = jnp.concatenate([jnp.concatenate(outs[bi * NG:(bi + 1) * NG], axis=1) for bi in range(nb)], axis=0)
    d = o - segsum(o) * (1.0 / RWKV_HEAD)
    var = segsum(d * d) * (1.0 / RWKV_HEAD)
    o = d * lax.rsqrt(var + RWKV_GN_EPS) * gn_g + gn_b
    bonus = segsum(r * kmod * r_k) * vr
    y_rw = gate_rw * (o + bonus)
    y_ref[:, :, RET_WIDTH:] = y_rw.reshape(nb, C, W).astype(y_ref.dtype)


def _project_chunk(x_ref, y_ref, w_ref, g_ref, o_ref):
    nb, C, D = x_ref.shape
    y = y_ref[...].reshape(nb * C, D)
    h = x_ref[...].reshape(nb * C, D) + jnp.dot(y, w_ref[...], preferred_element_type=f32)
    out = h * lax.rsqrt(jnp.mean(h * h, axis=-1, keepdims=True) + RMS_EPS) * g_ref[...]
    o_ref[...] = out.reshape(nb, C, D)


def _mixer_kernel(p_ref, dec_ref, lora_ref, vec_ref, rgn_ref,
                  seg_ref, x_ref, wout_ref, fg_ref, o_ref, y_ref, m_ref, s_ref, woutb_ref):
    @pl.when(pl.program_id(1) == 0)
    def _():
        m_ref[...] = jnp.zeros_like(m_ref)
        s_ref[...] = jnp.zeros_like(s_ref)
        y_ref[...] = jnp.zeros_like(y_ref)
        woutb_ref[...] = wout_ref[...].astype(bf16)

    _project_chunk(x_ref, y_ref, woutb_ref, fg_ref, o_ref)
    _mix_chunk(p_ref, dec_ref, lora_ref, vec_ref, rgn_ref, seg_ref, y_ref, m_ref, s_ref)


def _rope_tables(seq):
    half = RET_DK // 2
    expo = -jnp.arange(half, dtype=f32) / f32(half)
    freqs = jnp.exp(expo * f32(np.log(ROPE_BASE)))
    ang = jnp.arange(seq, dtype=jnp.int32).astype(f32)[:, None] * freqs[None, :]
    cos = jnp.cos(ang)
    sin = jnp.sin(ang)
    cos_full = jnp.tile(jnp.concatenate([cos, cos], axis=1), (1, RET_HEADS))
    sin_signed = jnp.tile(jnp.concatenate([-sin, sin], axis=1), (1, RET_HEADS))
    return cos_full, sin_signed


_RET_LOG_GAMMA = np.log(1.0 - np.exp2(-5.0 - np.arange(RET_HEADS, dtype=np.float64)))
_RET_CHUNK_DECAY = tuple(np.float32(v) for v in np.exp(_RET_LOG_GAMMA * CHUNK))


def _retention_constants():
    C = CHUNK
    lg = _RET_LOG_GAMMA
    lane_lg = np.repeat(lg, RET_DK)[None, :]
    n = np.arange(C, dtype=np.float64)[:, None]
    m = np.tile(np.arange(C, dtype=np.float64), RET_HEADS)[None, :]
    scale = RET_DK ** -0.5
    decay4 = scale * np.exp(lane_lg * np.abs(n - m))
    qdec = np.exp(lane_lg * (n + 1.0)) * np.ones((1, RET_QK))
    kdec = scale * np.exp(lane_lg * (C - 1.0 - n)) * np.ones((1, RET_QK))
    return jnp.asarray(np.stack([decay4, qdec, kdec]).astype(np.float32))


def _const_spec(shape, single_buffer=False):
    mode = pl.Buffered(1) if single_buffer else None
    return pl.BlockSpec(shape, lambda *_: (0,) * len(shape), pipeline_mode=mode)


@jax.jit
def kernel(x, norm_g, w_in, ret_gn_g, rwkv_mu, w_lora_up, w0, a_lora_up, a0, k_k, k_a, r_k,
           rwkv_gn_g, rwkv_gn_b, w_out, final_norm_g):
    B, T, D = x.shape
    assert D == D_MODEL and T % CHUNK == 0 and T % PROJ_TILE == 0 and B % BATCH_BLOCK == 0
    assert norm_g.shape[0] == 1, "single-layer block"
    n_tok = B * T
    xf = x.reshape(n_tok, D)
    params = pltpu.CompilerParams(dimension_semantics=("arbitrary",), vmem_limit_bytes=VMEM_LIMIT)
    tiles_per_seq = T // PROJ_TILE
    cos, sin = _rope_tables(T)

    p = pl.pallas_call(
        functools.partial(_in_proj_kernel, tiles_per_seq),
        grid=(n_tok // PROJ_TILE,),
        in_specs=[pl.BlockSpec((PROJ_TILE, D), lambda i: (i, 0)),
                  _const_spec((1, D)),
                  _const_spec((D, IN_COLS), single_buffer=True),
                  pl.BlockSpec((PROJ_TILE, RET_QK), lambda i: (i % tiles_per_seq, 0)),
                  pl.BlockSpec((PROJ_TILE, RET_QK), lambda i: (i % tiles_per_seq, 0)),
                  _const_spec((1, RWKV_COLS))],
        out_specs=pl.BlockSpec((PROJ_TILE, IN_COLS), lambda i: (i, 0)),
        out_shape=jax.ShapeDtypeStruct((n_tok, IN_COLS), f32),
        scratch_shapes=[pltpu.VMEM((D, IN_COLS), bf16),
                        pltpu.VMEM((1, RWKV_COLS), f32)],
        compiler_params=params,
        name="in_proj",
    )(xf, norm_g[0][None, :], w_in[0], cos, sin, rwkv_mu[0][None, :])

    dec = _retention_constants()
    zeros = jnp.zeros((LORA, RWKV_WIDTH), f32)
    lora_w = jnp.concatenate(
        [jnp.concatenate([w_lora_up[0], zeros], axis=1),
         jnp.concatenate([zeros, a_lora_up[0]], axis=1)], axis=0).astype(bf16)
    vecs = jnp.stack([w0[0], a0[0], k_k[0], k_a[0], r_k[0].reshape(-1), rwkv_gn_g[0], rwkv_gn_b[0],
                      jnp.zeros((RWKV_WIDTH,), f32)])
    seg_ids = np.arange(GROUP) // RWKV_HEAD
    seg = jnp.asarray(seg_ids[:, None] == seg_ids[None, :], bf16)

    n_chunks = T // CHUNK
    last = n_chunks - 1
    return pl.pallas_call(
        _mixer_kernel,
        grid=(B // BATCH_BLOCK, n_chunks + 1),
        in_specs=[pl.BlockSpec((BATCH_BLOCK, CHUNK, IN_COLS), lambda b, c: (b, jnp.minimum(c, last), 0)),
                  _const_spec((3, CHUNK, RET_QK)),
                  _const_spec((2 * LORA, 2 * RWKV_WIDTH)),
                  _const_spec((8, RWKV_WIDTH)),
                  _const_spec((1, RET_WIDTH)),
                  _const_spec((GROUP, GROUP)),
                  pl.BlockSpec((BATCH_BLOCK, CHUNK, D), lambda b, c: (b, jnp.maximum(c - 1, 0), 0)),
                  _const_spec((D, D), single_buffer=True),
                  _const_spec((1, D))],
        out_specs=pl.BlockSpec((BATCH_BLOCK, CHUNK, D), lambda b, c: (b, jnp.maximum(c - 1, 0), 0)),
        out_shape=jax.ShapeDtypeStruct((B, T, D), f32),
        scratch_shapes=[pltpu.VMEM((BATCH_BLOCK, CHUNK, D), bf16),
                        pltpu.VMEM((BATCH_BLOCK, RWKV_WIDTH // GROUP, RWKV_HEAD, GROUP), f32),
                        pltpu.VMEM((BATCH_BLOCK, RET_HEADS, RET_DK, RET_DV), f32),
                        pltpu.VMEM((D, D), bf16)],
        compiler_params=pltpu.CompilerParams(dimension_semantics=("arbitrary", "arbitrary"),
                                             vmem_limit_bytes=VMEM_LIMIT),
        name="mixers",
    )(p.reshape(B, T, IN_COLS), dec, lora_w, vecs,
      ret_gn_g[0][None, :], seg, x, w_out[0], final_norm_g[None, :])
```

```python
import functools

import numpy as np
import jax
import jax.numpy as jnp
from jax import lax
from jax.experimental import pallas as pl
from jax.experimental.pallas import tpu as pltpu

D_MODEL = 1024
CHUNK = 64
RET_HEADS = 4
RET_DK = 64
RET_DV = 128
RET_QK = RET_HEADS * RET_DK
RET_WIDTH = RET_HEADS * RET_DV
RWKV_WIDTH = 512
RWKV_HEAD = 64
LORA = 64
RET_COLS = 2 * RET_QK + 2 * RET_WIDTH
RWKV_COLS = 4 * RWKV_WIDTH + 2 * LORA
IN_COLS = RET_COLS + RWKV_COLS
ROPE_BASE = 10000.0
RMS_EPS = 1e-6
RET_GN_EPS = 1e-5
RWKV_GN_EPS = 64e-5
GROUP = 256
PROJ_TILE = 512
BATCH_BLOCK = 8
RESULT_LAG = 4
VMEM_LIMIT = 56 * 1024 * 1024

f32 = jnp.float32
bf16 = jnp.bfloat16


def _mm(a, b):
    return jnp.dot(a.astype(bf16), b.astype(bf16), preferred_element_type=f32)


def _mm_nt(a, b):
    return lax.dot_general(a.astype(bf16), b.astype(bf16), (((1,), (1,)), ((), ())),
                           preferred_element_type=f32)


def _split2(x):
    hi = x.astype(bf16)
    lo = (x - hi.astype(f32)).astype(bf16)
    return hi, lo


def _block_diag(x):
    x = x.astype(bf16)
    rows, lanes = x.shape
    zero = jnp.zeros((rows, 128), bf16)
    cols = []
    if lanes == 4 * 128:
        for j in range(4):
            cols.append(jnp.concatenate([x[:, j * 128:(j + 1) * 128] if i == j else zero for i in range(4)],
                                        axis=0))
    else:
        low = lax.broadcasted_iota(jnp.int32, (rows, 128), 1) < 64
        for j in range(2):
            xj = x[:, j * 128:(j + 1) * 128]
            pair = [jnp.where(low, xj, zero), jnp.where(low, zero, xj)]
            cols.append(jnp.concatenate([zero] * (2 * j) + pair + [zero] * (2 - 2 * j), axis=0))
    return jnp.concatenate(cols, axis=1)


def _sigmoid(x):
    return 1.0 / (1.0 + jnp.exp(-x))


def _in_proj_kernel(tiles_per_seq, x_ref, g_ref, w_ref, cos_ref, sin_ref, mu_ref, p_ref, wb_ref, carry_ref):
    i = pl.program_id(0)

    @pl.when(i == 0)
    def _():
        wb_ref[...] = w_ref[...].astype(bf16)

    x = x_ref[...]
    u = (x * lax.rsqrt(jnp.mean(x * x, axis=-1, keepdims=True) + RMS_EPS) * g_ref[...]).astype(bf16)
    rows = x.shape[0]
    seq_start = (i % tiles_per_seq) == 0
    W = RWKV_WIDTH

    def rope(pg):
        cos = jnp.concatenate([cos_ref[...]] * 2, axis=1)
        sin = jnp.concatenate([sin_ref[...]] * 2, axis=1)
        half = (lax.broadcasted_iota(jnp.int32, pg.shape, 1) & (RET_DK - 1)) < RET_DK // 2
        lanes = pg.shape[1]
        swapped = jnp.where(half, pltpu.roll(pg, lanes - RET_DK // 2, 1), pltpu.roll(pg, RET_DK // 2, 1))
        return pg * cos + swapped * sin

    def silu(pg):
        return pg * _sigmoid(pg)

    def shifted(lo, hi):
        def fn(pg):
            first = jnp.where(seq_start, 0.0, carry_ref[:, lo:hi])
            row0 = lax.broadcasted_iota(jnp.int32, pg.shape, 0) == 0
            prev = jnp.where(row0, first, pltpu.roll(pg, 1, 0))
            carry_ref[:, lo:hi] = pg[rows - 1:rows, :]
            return pg + mu_ref[:, lo:hi] * (prev - pg)
        return fn

    groups = [(0, 2 * RET_QK, rope),
              (2 * RET_QK, 2 * RET_QK + RET_WIDTH, lambda pg: pg),
              (2 * RET_QK + RET_WIDTH, RET_COLS, silu),
              (RET_COLS, RET_COLS + W, shifted(0, W)),
              (RET_COLS + W, RET_COLS + 2 * W, shifted(W, 2 * W)),
              (RET_COLS + 2 * W, RET_COLS + 3 * W, shifted(2 * W, 3 * W)),
              (RET_COLS + 3 * W, RET_COLS + 4 * W, lambda pg: silu(shifted(3 * W, 4 * W)(pg))),
              (RET_COLS + 4 * W, IN_COLS, shifted(4 * W, RWKV_COLS))]
    pending = None
    for lo, hi, fn in groups:
        acc = jnp.dot(u, wb_ref[:, lo:hi], preferred_element_type=f32)
        if pending is not None:
            p_ref[:, pending[1]:pending[2]] = pending[3](pending[0])
        pending = (acc, lo, hi, fn)
    p_ref[:, pending[1]:pending[2]] = pending[3](pending[0])


def _head_blocks(s):
    s = s.astype(bf16)
    zero = jnp.zeros(s.shape[1:], bf16)
    return jnp.concatenate([jnp.concatenate([s[h] if i == h else zero for i in range(RET_HEADS)], axis=0)
                            for h in range(RET_HEADS)], axis=1)


def _retention_chunks(qs, ks, vs, states, decay4, qdec, kdec, head_decay):
    n = range(len(qs))
    hs = range(RET_HEADS)
    scores = [_mm_nt(qs[i], _block_diag(ks[i])) * decay4 for i in n]
    out = [_mm(jnp.concatenate([scores[i], qs[i] * qdec], axis=1),
               jnp.concatenate([_block_diag(vs[i]), _head_blocks(states[i])], axis=0)) for i in n]
    kts = [(ks[i] * kdec).T for i in n]
    kv = [[_mm(kts[i][h * RET_DK:(h + 1) * RET_DK], vs[i][:, h * RET_DV:(h + 1) * RET_DV]) for h in hs]
          for i in n]
    s_new = [jnp.stack([states[i][h] * head_decay[h] + kv[i][h] for h in hs]) for i in n]
    return out, s_new


def _rwkv_chunks(rs, ks, vs, kkns, aas, lws, m_cs):
    C = CHUNK
    n = len(rs)
    heads = GROUP // RWKV_HEAD
    ti = lax.broadcasted_iota(jnp.int32, (C, C), 0)
    si = lax.broadcasted_iota(jnp.int32, (C, C), 1)
    tri_incl = jnp.where(si <= ti, 1.0, 0.0).astype(bf16)
    tri_incl = jnp.concatenate([tri_incl, tri_incl], axis=1)
    ti = lax.broadcasted_iota(jnp.int32, (C, GROUP), 0)
    si = lax.broadcasted_iota(jnp.int32, (C, GROUP), 1) & (C - 1)
    strict = si < ti
    incl = si <= ti
    eye = jnp.where(si == ti, 1.0, 0.0)
    low = lax.broadcasted_iota(jnp.int32, (RWKV_HEAD, 128), 1) < RWKV_HEAD

    def slots():
        return [None] * n

    def staged(produce, consume):
        for i in range(n + RESULT_LAG):
            if i < n:
                produce(i)
            if i >= RESULT_LAG:
                consume(i - RESULT_LAG)

    cum, lhs, rhs, decay_end, key_end = slots(), slots(), slots(), slots(), slots()

    def cum_matmul(i):
        cum[i] = jnp.dot(tri_incl, jnp.concatenate(_split2(lws[i]), axis=0), preferred_element_type=f32)

    def scale_operands(i):
        g_inc = jnp.exp(cum[i])
        g_inv = jnp.exp(-cum[i])
        g_end = jnp.exp(cum[i][C - 1:C, :] - cum[i])
        beta = kkns[i] * aas[i]
        lhs[i] = jnp.concatenate([-kkns[i] * jnp.exp(cum[i] - lws[i]), rs[i] * g_inc], axis=0).astype(bf16)
        rhs[i] = jnp.concatenate([_block_diag(beta * g_inv), _block_diag(ks[i] * g_inv)], axis=0)
        decay_end[i] = g_inc[C - 1:C, :]
        key_end[i] = jnp.concatenate([beta * g_end, ks[i] * g_end], axis=0).astype(bf16)

    staged(cum_matmul, scale_operands)

    amat, mv, a_ab, a_k, a_rb, t_inv = slots(), slots(), slots(), slots(), slots(), slots()

    def score_matmuls(i):
        amat[i] = _mm_nt(lhs[i], rhs[i])
        mv[i] = _mm_nt(lhs[i], _block_diag(m_cs[i]))

    def mask_scores(i):
        a_ab[i] = jnp.where(strict, amat[i][:C, :GROUP], 0.0).astype(bf16)
        a_rb[i] = jnp.where(incl, amat[i][C:, :GROUP], 0.0).astype(bf16)
        a_k[i] = jnp.concatenate([jnp.where(strict, amat[i][:C, GROUP:], 0.0),
                                  jnp.where(incl, amat[i][C:, GROUP:], 0.0)], axis=0).astype(bf16)
        t_inv[i] = eye + jnp.where(strict, amat[i][:C, :GROUP], 0.0)

    staged(score_matmuls, mask_scores)

    av, st, st_lhs, pow_bd = slots(), slots(), slots(), slots()

    def first_square(i):
        av[i] = _mm(a_k[i], _block_diag(vs[i]))
        st[i] = _mm(a_ab[i], _block_diag(a_ab[i]))

    def after_first_square(i):
        s_b = st[i].astype(bf16)
        st_lhs[i] = jnp.concatenate([s_b, t_inv[i].astype(bf16)], axis=0)
        pow_bd[i] = _block_diag(s_b)

    staged(first_square, after_first_square)

    def square(i):
        st[i] = _mm(st_lhs[i], pow_bd[i])

    def after_square(i):
        s_b = st[i][:C].astype(bf16)
        t_inv[i] = t_inv[i] + st[i][C:]
        st_lhs[i] = jnp.concatenate([s_b, t_inv[i].astype(bf16)], axis=0)
        pow_bd[i] = _block_diag(s_b)

    for _ in range(4):
        staged(square, after_square)

    t_b, y_bd = slots(), slots()

    def last_product(i):
        st[i] = _mm(st_lhs[i][C:], pow_bd[i])

    def after_last_product(i):
        t_b[i] = (t_inv[i] + st[i]).astype(bf16)
        y_bd[i] = _block_diag(mv[i][:C] + av[i][:C])

    staged(last_product, after_last_product)

    u, u_bd, uvt = slots(), slots(), slots()

    def solve(i):
        u[i] = _mm(t_b[i], y_bd[i])

    def after_solve(i):
        u_bd[i] = _block_diag(u[i])
        uvt[i] = jnp.concatenate([u[i], vs[i]], axis=0).T.astype(bf16)

    staged(solve, after_solve)

    o_u, upd, out, m_new = slots(), slots(), slots(), slots()

    def output_and_update(i):
        o_u[i] = _mm(a_rb[i], u_bd[i])
        upd[i] = [_mm(uvt[i][h * RWKV_HEAD:(h + 1) * RWKV_HEAD], key_end[i][:, (h // 2) * 128:(h // 2 + 1) * 128])
                  for h in range(heads)]

    def finish(i):
        out[i] = mv[i][C:] + av[i][C:] + o_u[i]
        m_new[i] = m_cs[i] * decay_end[i] + jnp.concatenate(
            [jnp.where(low, upd[i][2 * j], upd[i][2 * j + 1]) for j in range(heads // 2)], axis=1)

    staged(output_and_update, finish)
    return out, m_new


def _mix_chunk(p_ref, dec_ref, lora_ref, vec_ref, rgn_ref, seg_ref, y_ref, m_ref, s_ref):
    C = CHUNK
    nb = BATCH_BLOCK
    R = nb * C
    W = RWKV_WIDTH
    NG = W // GROUP

    def rows(x, bi):
        return x[bi * C:(bi + 1) * C]

    def rwkv_cols(lo, hi):
        return p_ref[:, :, RET_COLS + lo:RET_COLS + hi].reshape(R, hi - lo)

    r = rwkv_cols(0, W)
    kr = rwkv_cols(W, 2 * W)
    vr = rwkv_cols(2 * W, 3 * W)
    gate_rw = rwkv_cols(3 * W, 4 * W)
    xwa = rwkv_cols(4 * W, RWKV_COLS)
    lane = lax.broadcasted_iota(jnp.int32, xwa.shape, 1)
    lora = _mm(jnp.where(lane < LORA, jnp.tanh(xwa), xwa), lora_ref[...])
    w0, a0, k_k, k_a, r_k, gn_g, gn_b = (vec_ref[i:i + 1, :] for i in range(7))
    lw = -np.float32(np.exp(-0.5)) * _sigmoid(w0 + lora[:, :W])
    a = _sigmoid(a0 + lora[:, W:])
    seg = seg_ref[...]

    def segsum(x):
        xs = jnp.concatenate([x[:, g * GROUP:(g + 1) * GROUP] for g in range(NG)], axis=0)
        tot = jnp.dot(xs.astype(bf16), seg, preferred_element_type=f32)
        return jnp.concatenate([tot[g * R:(g + 1) * R] for g in range(NG)], axis=1)

    kk = kr * k_k
    kkn = kk * lax.rsqrt(jnp.maximum(segsum(kk * kk), 1e-24))
    kmod = kr * (1.0 + (a - 1.0) * k_a)

    q = p_ref[:, :, 0:RET_QK].reshape(R, RET_QK)
    k = p_ref[:, :, RET_QK:2 * RET_QK].reshape(R, RET_QK)
    v = p_ref[:, :, 2 * RET_QK:2 * RET_QK + RET_WIDTH].reshape(R, RET_WIDTH)
    rets, s_new = _retention_chunks([rows(q, bi) for bi in range(nb)], [rows(k, bi) for bi in range(nb)],
                                    [rows(v, bi) for bi in range(nb)], [s_ref[bi] for bi in range(nb)],
                                    dec_ref[0], dec_ref[1], dec_ref[2], _RET_CHUNK_DECAY)
    for bi in range(nb):
        s_ref[bi] = s_new[bi]
    ret = jnp.concatenate(rets, axis=0)
    rgn = rgn_ref[...]
    for h in range(RET_HEADS):
        sl = slice(h * RET_DV, (h + 1) * RET_DV)
        xh = ret[:, sl]
        d = xh - jnp.mean(xh, axis=-1, keepdims=True)
        var = jnp.mean(d * d, axis=-1, keepdims=True)
        gh = p_ref[:, :, 2 * RET_QK + RET_WIDTH + h * RET_DV:2 * RET_QK + RET_WIDTH + (h + 1) * RET_DV]
        gh = gh.reshape(R, RET_DV)
        yh = gh * (d * lax.rsqrt(var + RET_GN_EPS) * rgn[:, sl])
        y_ref[:, :, sl] = yh.reshape(nb, C, RET_DV).astype(y_ref.dtype)

    chains = [(bi, g) for bi in range(nb) for g in range(NG)]

    def pick(x):
        return [x[bi * C:(bi + 1) * C, g * GROUP:(g + 1) * GROUP] for bi, g in chains]

    outs, m_new = _rwkv_chunks(pick(r), pick(kmod), pick(vr), pick(kkn), pick(a), pick(lw),
                               [m_ref[bi, g] for bi, g in chains])
    for i, (bi, g) in enumerate(chains):
        m_ref[bi, g] = m_new[i]
    o = jnp.concatenate([jnp.concatenate(outs[bi * NG:(bi + 1) * NG], axis=1) for bi in range(nb)], axis=0)
    d = o - segsum(o) * (1.0 / RWKV_HEAD)
    var = segsum(d * d) * (1.0 / RWKV_HEAD)
    o = d * lax.rsqrt(var + RWKV_GN_EPS) * gn_g + gn_b
    bonus = segsum(r * kmod * r_k) * vr
    y_rw = gate_rw * (o + bonus)
    y_ref[:, :, RET_WIDTH:] = y_rw.reshape(nb, C, W).astype(y_ref.dtype)


def _project_chunk(x_ref, y_ref, w_ref, g_ref, o_ref):
    nb, C, D = x_ref.shape
    y = y_ref[...].reshape(nb * C, D)
    h = x_ref[...].reshape(nb * C, D) + jnp.dot(y, w_ref[...], preferred_element_type=f32)
    out = h * lax.rsqrt(jnp.mean(h * h, axis=-1, keepdims=True) + RMS_EPS) * g_ref[...]
    o_ref[...] = out.reshape(nb, C, D)


def _mixer_kernel(p_ref, dec_ref, lora_ref, vec_ref, rgn_ref,
                  seg_ref, x_ref, wout_ref, fg_ref, o_ref, y_ref, m_ref, s_ref, woutb_ref):
    @pl.when(pl.program_id(1) == 0)
    def _():
        m_ref[...] = jnp.zeros_like(m_ref)
        s_ref[...] = jnp.zeros_like(s_ref)
        y_ref[...] = jnp.zeros_like(y_ref)
        woutb_ref[...] = wout_ref[...].astype(bf16)

    _project_chunk(x_ref, y_ref, woutb_ref, fg_ref, o_ref)
    _mix_chunk(p_ref, dec_ref, lora_ref, vec_ref, rgn_ref, seg_ref, y_ref, m_ref, s_ref)


def _rope_tables(seq):
    half = RET_DK // 2
    expo = -jnp.arange(half, dtype=f32) / f32(half)
    freqs = jnp.exp(expo * f32(np.log(ROPE_BASE)))
    ang = jnp.arange(seq, dtype=jnp.int32).astype(f32)[:, None] * freqs[None, :]
    cos = jnp.cos(ang)
    sin = jnp.sin(ang)
    cos_full = jnp.tile(jnp.concatenate([cos, cos], axis=1), (1, RET_HEADS))
    sin_signed = jnp.tile(jnp.concatenate([-sin, sin], axis=1), (1, RET_HEADS))
    return cos_full, sin_signed


_RET_LOG_GAMMA = np.log(1.0 - np.exp2(-5.0 - np.arange(RET_HEADS, dtype=np.float64)))
_RET_CHUNK_DECAY = tuple(np.float32(v) for v in np.exp(_RET_LOG_GAMMA * CHUNK))


def _retention_constants():
    C = CHUNK
    lg = _RET_LOG_GAMMA
    lane_lg = np.repeat(lg, RET_DK)[None, :]
    n = np.arange(C, dtype=np.float64)[:, None]
    m = np.tile(np.arange(C, dtype=np.float64), RET_HEADS)[None, :]
    scale = RET_DK ** -0.5
    decay4 = scale * np.exp(lane_lg * np.abs(n - m))
    qdec = np.exp(lane_lg * (n + 1.0)) * np.ones((1, RET_QK))
    kdec = scale * np.exp(lane_lg * (C - 1.0 - n)) * np.ones((1, RET_QK))
    return jnp.asarray(np.stack([decay4, qdec, kdec]).astype(np.float32))


def _const_spec(shape, single_buffer=False):
    mode = pl.Buffered(1) if single_buffer else None
    return pl.BlockSpec(shape, lambda *_: (0,) * len(shape), pipeline_mode=mode)


@jax.jit
def kernel(x, norm_g, w_in, ret_gn_g, rwkv_mu, w_lora_up, w0, a_lora_up, a0, k_k, k_a, r_k,
           rwkv_gn_g, rwkv_gn_b, w_out, final_norm_g):
    B, T, D = x.shape
    assert D == D_MODEL and T % CHUNK == 0 and T % PROJ_TILE == 0 and B % BATCH_BLOCK == 0
    assert norm_g.shape[0] == 1, "single-layer block"
    n_tok = B * T
    xf = x.reshape(n_tok, D)
    params = pltpu.CompilerParams(dimension_semantics=("arbitrary",), vmem_limit_bytes=VMEM_LIMIT)
    tiles_per_seq = T // PROJ_TILE
    cos, sin = _rope_tables(T)

    p = pl.pallas_call(
        functools.partial(_in_proj_kernel, tiles_per_seq),
        grid=(n_tok // PROJ_TILE,),
        in_specs=[pl.BlockSpec((PROJ_TILE, D), lambda i: (i, 0)),
                  _const_spec((1, D)),
                  _const_spec((D, IN_COLS), single_buffer=True),
                  pl.BlockSpec((PROJ_TILE, RET_QK), lambda i: (i % tiles_per_seq, 0)),
                  pl.BlockSpec((PROJ_TILE, RET_QK), lambda i: (i % tiles_per_seq, 0)),
                  _const_spec((1, RWKV_COLS))],
        out_specs=pl.BlockSpec((PROJ_TILE, IN_COLS), lambda i: (i, 0)),
        out_shape=jax.ShapeDtypeStruct((n_tok, IN_COLS), f32),
        scratch_shapes=[pltpu.VMEM((D, IN_COLS), bf16),
                        pltpu.VMEM((1, RWKV_COLS), f32)],
        compiler_params=params,
        name="in_proj",
    )(xf, norm_g[0][None, :], w_in[0], cos, sin, rwkv_mu[0][None, :])

    dec = _retention_constants()
    zeros = jnp.zeros((LORA, RWKV_WIDTH), f32)
    lora_w = jnp.concatenate(
        [jnp.concatenate([w_lora_up[0], zeros], axis=1),
         jnp.concatenate([zeros, a_lora_up[0]], axis=1)], axis=0).astype(bf16)
    vecs = jnp.stack([w0[0], a0[0], k_k[0], k_a[0], r_k[0].reshape(-1), rwkv_gn_g[0], rwkv_gn_b[0],
                      jnp.zeros((RWKV_WIDTH,), f32)])
    seg_ids = np.arange(GROUP) // RWKV_HEAD
    seg = jnp.asarray(seg_ids[:, None] == seg_ids[None, :], bf16)

    n_chunks = T // CHUNK
    last = n_chunks - 1
    return pl.pallas_call(
        _mixer_kernel,
        grid=(B // BATCH_BLOCK, n_chunks + 1),
        in_specs=[pl.BlockSpec((BATCH_BLOCK, CHUNK, IN_COLS), lambda b, c: (b, jnp.minimum(c, last), 0)),
                  _const_spec((3, CHUNK, RET_QK)),
                  _const_spec((2 * LORA, 2 * RWKV_WIDTH)),
                  _const_spec((8, RWKV_WIDTH)),
                  _const_spec((1, RET_WIDTH)),
                  _const_spec((GROUP, GROUP)),
                  pl.BlockSpec((BATCH_BLOCK, CHUNK, D), lambda b, c: (b, jnp.maximum(c - 1, 0), 0)),
                  _const_spec((D, D), single_buffer=True),
                  _const_spec((1, D))],
        out_specs=pl.BlockSpec((BATCH_BLOCK, CHUNK, D), lambda b, c: (b, jnp.maximum(c - 1, 0), 0)),
        out_shape=jax.ShapeDtypeStruct((B, T, D), f32),
        scratch_shapes=[pltpu.VMEM((BATCH_BLOCK, CHUNK, D), bf16),
                        pltpu.VMEM((BATCH_BLOCK, RWKV_WIDTH // GROUP, RWKV_HEAD, GROUP), f32),
                        pltpu.VMEM((BATCH_BLOCK, RET_HEADS, RET_DK, RET_DV), f32),
                        pltpu.VMEM((D, D), bf16)],
        compiler_params=pltpu.CompilerParams(dimension_semantics=("arbitrary", "arbitrary"),
                                             vmem_limit_bytes=VMEM_LIMIT),
        name="mixers",
    )(p.reshape(B, T, IN_COLS), dec, lora_w, vecs,
      ret_gn_g[0][None, :], seg, x, w_out[0], final_norm_g[None, :])
```

```python
import functools

import numpy as np
import jax
import jax.numpy as jnp
from jax import lax
from jax.experimental import pallas as pl
from jax.experimental.pallas import tpu as pltpu

D_MODEL = 1024
CHUNK = 64
RET_HEADS = 4
RET_DK = 64
RET_DV = 128
RET_QK = RET_HEADS * RET_DK
RET_WIDTH = RET_HEADS * RET_DV
RWKV_WIDTH = 512
RWKV_HEAD = 64
LORA = 64
RET_COLS = 2 * RET_QK + 2 * RET_WIDTH
RWKV_COLS = 4 * RWKV_WIDTH + 2 * LORA
IN_COLS = RET_COLS + RWKV_COLS
ROPE_BASE = 10000.0
RMS_EPS = 1e-6
RET_GN_EPS = 1e-5
RWKV_GN_EPS = 64e-5
GROUP = 256
PROJ_TILE = 512
BATCH_BLOCK = 8
RESULT_LAG = 4
VMEM_LIMIT = 56 * 1024 * 1024

f32 = jnp.float32
bf16 = jnp.bfloat16


def _mm(a, b):
    return jnp.dot(a.astype(bf16), b.astype(bf16), preferred_element_type=f32)


def _mm_nt(a, b):
    return lax.dot_general(a.astype(bf16), b.astype(bf16), (((1,), (1,)), ((), ())),
                           preferred_element_type=f32)


def _split2(x):
    hi = x.astype(bf16)
    lo = (x - hi.astype(f32)).astype(bf16)
    return hi, lo


def _block_diag(x):
    x = x.astype(bf16)
    rows, lanes = x.shape
    zero = jnp.zeros((rows, 128), bf16)
    cols = []
    if lanes == 4 * 128:
        for j in range(4):
            cols.append(jnp.concatenate([x[:, j * 128:(j + 1) * 128] if i == j else zero for i in range(4)],
                                        axis=0))
    else:
        low = lax.broadcasted_iota(jnp.int32, (rows, 128), 1) < 64
        for j in range(2):
            xj = x[:, j * 128:(j + 1) * 128]
            pair = [jnp.where(low, xj, zero), jnp.where(low, zero, xj)]
            cols.append(jnp.concatenate([zero] * (2 * j) + pair + [zero] * (2 - 2 * j), axis=0))
    return jnp.concatenate(cols, axis=1)


def _sigmoid(x):
    return 1.0 / (1.0 + jnp.exp(-x))


def _in_proj_kernel(tiles_per_seq, x_ref, g_ref, w_ref, cos_ref, sin_ref, mu_ref, p_ref, wb_ref, carry_ref):
    i = pl.program_id(0)

    @pl.when(i == 0)
    def _():
        wb_ref[...] = w_ref[...].astype(bf16)

    x = x_ref[...]
    u = (x * lax.rsqrt(jnp.mean(x * x, axis=-1, keepdims=True) + RMS_EPS) * g_ref[...]).astype(bf16)
    rows = x.shape[0]
    seq_start = (i % tiles_per_seq) == 0
    W = RWKV_WIDTH

    def rope(pg):
        cos = jnp.concatenate([cos_ref[...]] * 2, axis=1)
        sin = jnp.concatenate([sin_ref[...]] * 2, axis=1)
        half = (lax.broadcasted_iota(jnp.int32, pg.shape, 1) & (RET_DK - 1)) < RET_DK // 2
        lanes = pg.shape[1]
        swapped = jnp.where(half, pltpu.roll(pg, lanes - RET_DK // 2, 1), pltpu.roll(pg, RET_DK // 2, 1))
        return pg * cos + swapped * sin

    def silu(pg):
        return pg * _sigmoid(pg)

    def shifted(lo, hi):
        def fn(pg):
            first = jnp.where(seq_start, 0.0, carry_ref[:, lo:hi])
            row0 = lax.broadcasted_iota(jnp.int32, pg.shape, 0) == 0
            prev = jnp.where(row0, first, pltpu.roll(pg, 1, 0))
            carry_ref[:, lo:hi] = pg[rows - 1:rows, :]
            return pg + mu_ref[:, lo:hi] * (prev - pg)
        return fn

    groups = [(0, 2 * RET_QK, rope),
              (2 * RET_QK, 2 * RET_QK + RET_WIDTH, lambda pg: pg),
              (2 * RET_QK + RET_WIDTH, RET_COLS, silu),
              (RET_COLS, RET_COLS + W, shifted(0, W)),
              (RET_COLS + W, RET_COLS + 2 * W, shifted(W, 2 * W)),
              (RET_COLS + 2 * W, RET_COLS + 3 * W, shifted(2 * W, 3 * W)),
              (RET_COLS + 3 * W, RET_COLS + 4 * W, lambda pg: silu(shifted(3 * W, 4 * W)(pg))),
              (RET_COLS + 4 * W, IN_COLS, shifted(4 * W, RWKV_COLS))]
    pending = None
    for lo, hi, fn in groups:
        acc = jnp.dot(u, wb_ref[:, lo:hi], preferred_element_type=f32)
        if pending is not None:
            p_ref[:, pending[1]:pending[2]] = pending[3](pending[0])
        pending = (acc, lo, hi, fn)
    p_ref[:, pending[1]:pending[2]] = pending[3](pending[0])


def _head_blocks(s):
    s = s.astype(bf16)
    zero = jnp.zeros(s.shape[1:], bf16)
    return jnp.concatenate([jnp.concatenate([s[h] if i == h else zero for i in range(RET_HEADS)], axis=0)
                            for h in range(RET_HEADS)], axis=1)


def _staged(n, produce, consume):
    for i in range(n + RESULT_LAG):
        if i < n:
            produce(i)
        if i >= RESULT_LAG:
            consume(i - RESULT_LAG)


def _retention_chunks(qs, ks, vs, states, decay4, qdec, kdec, head_decay):
    n = len(qs)
    hs = range(RET_HEADS)
    scores, lhs, kts, out, kv, s_new = ([None] * n for _ in range(6))

    def score_matmul(i):
        scores[i] = _mm_nt(qs[i], _block_diag(ks[i]))

    def after_scores(i):
        lhs[i] = jnp.concatenate([scores[i] * decay4, qs[i] * qdec], axis=1).astype(bf16)
        kts[i] = (ks[i] * kdec).T.astype(bf16)

    _staged(n, score_matmul, after_scores)

    def value_matmuls(i):
        out[i] = _mm(lhs[i], jnp.concatenate([_block_diag(vs[i]), _head_blocks(states[i])], axis=0))
        kv[i] = [_mm(kts[i][h * RET_DK:(h + 1) * RET_DK], vs[i][:, h * RET_DV:(h + 1) * RET_DV]) for h in hs]

    def update_state(i):
        s_new[i] = jnp.stack([states[i][h] * head_decay[h] + kv[i][h] for h in hs])

    _staged(n, value_matmuls, update_state)
    return out, s_new


def _rwkv_chunks(rs, ks, vs, kkns, aas, lws, m_cs):
    C = CHUNK
    n = len(rs)
    heads = GROUP // RWKV_HEAD
    ti = lax.broadcasted_iota(jnp.int32, (C, C), 0)
    si = lax.broadcasted_iota(jnp.int32, (C, C), 1)
    tri_incl = jnp.where(si <= ti, 1.0, 0.0).astype(bf16)
    tri_incl = jnp.concatenate([tri_incl, tri_incl], axis=1)
    ti = lax.broadcasted_iota(jnp.int32, (C, GROUP), 0)
    si = lax.broadcasted_iota(jnp.int32, (C, GROUP), 1) & (C - 1)
    strict = si < ti
    incl = si <= ti
    eye = jnp.where(si == ti, 1.0, 0.0)
    low = lax.broadcasted_iota(jnp.int32, (RWKV_HEAD, 128), 1) < RWKV_HEAD

    def slots():
        return [None] * n

    def staged(produce, consume):
        _staged(n, produce, consume)

    cum, lhs, rhs, decay_end, key_end = slots(), slots(), slots(), slots(), slots()

    def cum_matmul(i):
        cum[i] = jnp.dot(tri_incl, jnp.concatenate(_split2(lws[i]), axis=0), preferred_element_type=f32)

    def scale_operands(i):
        g_inc = jnp.exp(cum[i])
        g_inv = jnp.exp(-cum[i])
        g_end = jnp.exp(cum[i][C - 1:C, :] - cum[i])
        beta = kkns[i] * aas[i]
        lhs[i] = jnp.concatenate([-kkns[i] * jnp.exp(cum[i] - lws[i]), rs[i] * g_inc], axis=0).astype(bf16)
        rhs[i] = jnp.concatenate([_block_diag(beta * g_inv), _block_diag(ks[i] * g_inv)], axis=0)
        decay_end[i] = g_inc[C - 1:C, :]
        key_end[i] = jnp.concatenate([beta * g_end, ks[i] * g_end], axis=0).astype(bf16)

    staged(cum_matmul, scale_operands)

    amat, mv, a_ab, a_k, a_rb, t_inv = slots(), slots(), slots(), slots(), slots(), slots()

    def score_matmuls(i):
        amat[i] = _mm_nt(lhs[i], rhs[i])
        mv[i] = _mm_nt(lhs[i], _block_diag(m_cs[i]))

    def mask_scores(i):
        a_ab[i] = jnp.where(strict, amat[i][:C, :GROUP], 0.0).astype(bf16)
        a_rb[i] = jnp.where(incl, amat[i][C:, :GROUP], 0.0).astype(bf16)
        a_k[i] = jnp.concatenate([jnp.where(strict, amat[i][:C, GROUP:], 0.0),
                                  jnp.where(incl, amat[i][C:, GROUP:], 0.0)], axis=0).astype(bf16)
        t_inv[i] = eye + jnp.where(strict, amat[i][:C, :GROUP], 0.0)

    staged(score_matmuls, mask_scores)

    av, st, st_lhs, pow_bd = slots(), slots(), slots(), slots()

    def first_square(i):
        av[i] = _mm(a_k[i], _block_diag(vs[i]))
        st[i] = _mm(a_ab[i], _block_diag(a_ab[i]))

    def after_first_square(i):
        s_b = st[i].astype(bf16)
        st_lhs[i] = jnp.concatenate([s_b, t_inv[i].astype(bf16)], axis=0)
        pow_bd[i] = _block_diag(s_b)

    staged(first_square, after_first_square)

    def square(i):
        st[i] = _mm(st_lhs[i], pow_bd[i])

    def after_square(i):
        s_b = st[i][:C].astype(bf16)
        t_inv[i] = t_inv[i] + st[i][C:]
        st_lhs[i] = jnp.concatenate([s_b, t_inv[i].astype(bf16)], axis=0)
        pow_bd[i] = _block_diag(s_b)

    for _ in range(4):
        staged(square, after_square)

    t_b, y_bd = slots(), slots()

    def last_product(i):
        st[i] = _mm(st_lhs[i][C:], pow_bd[i])

    def after_last_product(i):
        t_b[i] = (t_inv[i] + st[i]).astype(bf16)
        y_bd[i] = _block_diag(mv[i][:C] + av[i][:C])

    staged(last_product, after_last_product)

    u, u_bd, uvt = slots(), slots(), slots()

    def solve(i):
        u[i] = _mm(t_b[i], y_bd[i])

    def after_solve(i):
        u_bd[i] = _block_diag(u[i])
        uvt[i] = jnp.concatenate([u[i], vs[i]], axis=0).T.astype(bf16)

    staged(solve, after_solve)

    o_u, upd, out, m_new = slots(), slots(), slots(), slots()

    def output_and_update(i):
        o_u[i] = _mm(a_rb[i], u_bd[i])
        upd[i] = [_mm(uvt[i][h * RWKV_HEAD:(h + 1) * RWKV_HEAD], key_end[i][:, (h // 2) * 128:(h // 2 + 1) * 128])
                  for h in range(heads)]

    def finish(i):
        out[i] = mv[i][C:] + av[i][C:] + o_u[i]
        m_new[i] = m_cs[i] * decay_end[i] + jnp.concatenate(
            [jnp.where(low, upd[i][2 * j], upd[i][2 * j + 1]) for j in range(heads // 2)], axis=1)

    staged(output_and_update, finish)
    return out, m_new


def _mix_chunk(p_ref, dec_ref, lora_ref, vec_ref, rgn_ref, seg_ref, y_ref, m_ref, s_ref):
    C = CHUNK
    nb = BATCH_BLOCK
    R = nb * C
    W = RWKV_WIDTH
    NG = W // GROUP

    def rows(x, bi):
        return x[bi * C:(bi + 1) * C]

    def rwkv_cols(lo, hi):
        return p_ref[:, :, RET_COLS + lo:RET_COLS + hi].reshape(R, hi - lo)

    r = rwkv_cols(0, W)
    kr = rwkv_cols(W, 2 * W)
    vr = rwkv_cols(2 * W, 3 * W)
    gate_rw = rwkv_cols(3 * W, 4 * W)
    xwa = rwkv_cols(4 * W, RWKV_COLS)
    lane = lax.broadcasted_iota(jnp.int32, xwa.shape, 1)
    lora = _mm(jnp.where(lane < LORA, jnp.tanh(xwa), xwa), lora_ref[...])
    w0, a0, k_k, k_a, r_k, gn_g, gn_b = (vec_ref[i:i + 1, :] for i in range(7))
    lw = -np.float32(np.exp(-0.5)) * _sigmoid(w0 + lora[:, :W])
    a = _sigmoid(a0 + lora[:, W:])
    seg = seg_ref[...]

    def segsum(x):
        xs = jnp.concatenate([x[:, g * GROUP:(g + 1) * GROUP] for g in range(NG)], axis=0)
        tot = jnp.dot(xs.astype(bf16), seg, preferred_element_type=f32)
        return jnp.concatenate([tot[g * R:(g + 1) * R] for g in range(NG)], axis=1)

    kk = kr * k_k
    kkn = kk * lax.rsqrt(jnp.maximum(segsum(kk * kk), 1e-24))
    kmod = kr * (1.0 + (a - 1.0) * k_a)

    q = p_ref[:, :, 0:RET_QK].reshape(R, RET_QK)
    k = p_ref[:, :, RET_QK:2 * RET_QK].reshape(R, RET_QK)
    v = p_ref[:, :, 2 * RET_QK:2 * RET_QK + RET_WIDTH].reshape(R, RET_WIDTH)
    rets, s_new = _retention_chunks([rows(q, bi) for bi in range(nb)], [rows(k, bi) for bi in range(nb)],
                                    [rows(v, bi) for bi in range(nb)], [s_ref[bi] for bi in range(nb)],
                                    dec_ref[0], dec_ref[1], dec_ref[2], _RET_CHUNK_DECAY)
    for bi in range(nb):
        s_ref[bi] = s_new[bi]
    ret = jnp.concatenate(rets, axis=0)
    rgn = rgn_ref[...]
    for h in range(RET_HEADS):
        sl = slice(h * RET_DV, (h + 1) * RET_DV)
        xh = ret[:, sl]
        d = xh - jnp.mean(xh, axis=-1, keepdims=True)
        var = jnp.mean(d * d, axis=-1, keepdims=True)
        gh = p_ref[:, :, 2 * RET_QK + RET_WIDTH + h * RET_DV:2 * RET_QK + RET_WIDTH + (h + 1) * RET_DV]
        gh = gh.reshape(R, RET_DV)
        yh = gh * (d * lax.rsqrt(var + RET_GN_EPS) * rgn[:, sl])
        y_ref[:, :, sl] = yh.reshape(nb, C, RET_DV).astype(y_ref.dtype)

    chains = [(bi, g) for bi in range(nb) for g in range(NG)]

    def pick(x):
        return [x[bi * C:(bi + 1) * C, g * GROUP:(g + 1) * GROUP] for bi, g in chains]

    outs, m_new = _rwkv_chunks(pick(r), pick(kmod), pick(vr), pick(kkn), pick(a), pick(lw),
                               [m_ref[bi, g] for bi, g in chains])
    for i, (bi, g) in enumerate(chains):
        m_ref[bi, g] = m_new[i]
    o = jnp.concatenate([jnp.concatenate(outs[bi * NG:(bi + 1) * NG], axis=1) for bi in range(nb)], axis=0)
    d = o - segsum(o) * (1.0 / RWKV_HEAD)
    var = segsum(d * d) * (1.0 / RWKV_HEAD)
    o = d * lax.rsqrt(var + RWKV_GN_EPS) * gn_g + gn_b
    bonus = segsum(r * kmod * r_k) * vr
    y_rw = gate_rw * (o + bonus)
    y_ref[:, :, RET_WIDTH:] = y_rw.reshape(nb, C, W).astype(y_ref.dtype)


def _project_chunk(x_ref, y_ref, w_ref, g_ref, o_ref):
    nb, C, D = x_ref.shape
    y = y_ref[...].reshape(nb * C, D)
    h = x_ref[...].reshape(nb * C, D) + jnp.dot(y, w_ref[...], preferred_element_type=f32)
    out = h * lax.rsqrt(jnp.mean(h * h, axis=-1, keepdims=True) + RMS_EPS) * g_ref[...]
    o_ref[...] = out.reshape(nb, C, D)


def _mixer_kernel(p_ref, dec_ref, lora_ref, vec_ref, rgn_ref,
                  seg_ref, x_ref, wout_ref, fg_ref, o_ref, y_ref, m_ref, s_ref, woutb_ref):
    @pl.when(pl.program_id(1) == 0)
    def _():
        m_ref[...] = jnp.zeros_like(m_ref)
        s_ref[...] = jnp.zeros_like(s_ref)
        y_ref[...] = jnp.zeros_like(y_ref)
        woutb_ref[...] = wout_ref[...].astype(bf16)

    _project_chunk(x_ref, y_ref, woutb_ref, fg_ref, o_ref)
    _mix_chunk(p_ref, dec_ref, lora_ref, vec_ref, rgn_ref, seg_ref, y_ref, m_ref, s_ref)


def _rope_tables(seq):
    half = RET_DK // 2
    expo = -jnp.arange(half, dtype=f32) / f32(half)
    freqs = jnp.exp(expo * f32(np.log(ROPE_BASE)))
    ang = jnp.arange(seq, dtype=jnp.int32).astype(f32)[:, None] * freqs[None, :]
    cos = jnp.cos(ang)
    sin = jnp.sin(ang)
    cos_full = jnp.tile(jnp.concatenate([cos, cos], axis=1), (1, RET_HEADS))
    sin_signed = jnp.tile(jnp.concatenate([-sin, sin], axis=1), (1, RET_HEADS))
    return cos_full, sin_signed


_RET_LOG_GAMMA = np.log(1.0 - np.exp2(-5.0 - np.arange(RET_HEADS, dtype=np.float64)))
_RET_CHUNK_DECAY = tuple(np.float32(v) for v in np.exp(_RET_LOG_GAMMA * CHUNK))


def _retention_constants():
    C = CHUNK
    lg = _RET_LOG_GAMMA
    lane_lg = np.repeat(lg, RET_DK)[None, :]
    n = np.arange(C, dtype=np.float64)[:, None]
    m = np.tile(np.arange(C, dtype=np.float64), RET_HEADS)[None, :]
    scale = RET_DK ** -0.5
    decay4 = scale * np.exp(lane_lg * np.abs(n - m))
    qdec = np.exp(lane_lg * (n + 1.0)) * np.ones((1, RET_QK))
    kdec = scale * np.exp(lane_lg * (C - 1.0 - n)) * np.ones((1, RET_QK))
    return jnp.asarray(np.stack([decay4, qdec, kdec]).astype(np.float32))


def _const_spec(shape, single_buffer=False):
    mode = pl.Buffered(1) if single_buffer else None
    return pl.BlockSpec(shape, lambda *_: (0,) * len(shape), pipeline_mode=mode)


@jax.jit
def kernel(x, norm_g, w_in, ret_gn_g, rwkv_mu, w_lora_up, w0, a_lora_up, a0, k_k, k_a, r_k,
           rwkv_gn_g, rwkv_gn_b, w_out, final_norm_g):
    B, T, D = x.shape
    assert D == D_MODEL and T % CHUNK == 0 and T % PROJ_TILE == 0 and B % BATCH_BLOCK == 0
    assert norm_g.shape[0] == 1, "single-layer block"
    n_tok = B * T
    xf = x.reshape(n_tok, D)
    params = pltpu.CompilerParams(dimension_semantics=("arbitrary",), vmem_limit_bytes=VMEM_LIMIT)
    tiles_per_seq = T // PROJ_TILE
    cos, sin = _rope_tables(T)

    p = pl.pallas_call(
        functools.partial(_in_proj_kernel, tiles_per_seq),
        grid=(n_tok // PROJ_TILE,),
        in_specs=[pl.BlockSpec((PROJ_TILE, D), lambda i: (i, 0)),
                  _const_spec((1, D)),
                  _const_spec((D, IN_COLS), single_buffer=True),
                  pl.BlockSpec((PROJ_TILE, RET_QK), lambda i: (i % tiles_per_seq, 0)),
                  pl.BlockSpec((PROJ_TILE, RET_QK), lambda i: (i % tiles_per_seq, 0)),
                  _const_spec((1, RWKV_COLS))],
        out_specs=pl.BlockSpec((PROJ_TILE, IN_COLS), lambda i: (i, 0)),
        out_shape=jax.ShapeDtypeStruct((n_tok, IN_COLS), f32),
        scratch_shapes=[pltpu.VMEM((D, IN_COLS), bf16),
                        pltpu.VMEM((1, RWKV_COLS), f32)],
        compiler_params=params,
        name="in_proj",
    )(xf, norm_g[0][None, :], w_in[0], cos, sin, rwkv_mu[0][None, :])

    dec = _retention_constants()
    zeros = jnp.zeros((LORA, RWKV_WIDTH), f32)
    lora_w = jnp.concatenate(
        [jnp.concatenate([w_lora_up[0], zeros], axis=1),
         jnp.concatenate([zeros, a_lora_up[0]], axis=1)], axis=0).astype(bf16)
    vecs = jnp.stack([w0[0], a0[0], k_k[0], k_a[0], r_k[0].reshape(-1), rwkv_gn_g[0], rwkv_gn_b[0],
                      jnp.zeros((RWKV_WIDTH,), f32)])
    seg_ids = np.arange(GROUP) // RWKV_HEAD
    seg = jnp.asarray(seg_ids[:, None] == seg_ids[None, :], bf16)

    n_chunks = T // CHUNK
    last = n_chunks - 1
    return pl.pallas_call(
        _mixer_kernel,
        grid=(B // BATCH_BLOCK, n_chunks + 1),
        in_specs=[pl.BlockSpec((BATCH_BLOCK, CHUNK, IN_COLS), lambda b, c: (b, jnp.minimum(c, last), 0)),
                  _const_spec((3, CHUNK, RET_QK)),
                  _const_spec((2 * LORA, 2 * RWKV_WIDTH)),
                  _const_spec((8, RWKV_WIDTH)),
                  _const_spec((1, RET_WIDTH)),
                  _const_spec((GROUP, GROUP)),
                  pl.BlockSpec((BATCH_BLOCK, CHUNK, D), lambda b, c: (b, jnp.maximum(c - 1, 0), 0)),
                  _const_spec((D, D), single_buffer=True),
                  _const_spec((1, D))],
        out_specs=pl.BlockSpec((BATCH_BLOCK, CHUNK, D), lambda b, c: (b, jnp.maximum(c - 1, 0), 0)),
        out_shape=jax.ShapeDtypeStruct((B, T, D), f32),
        scratch_shapes=[pltpu.VMEM((BATCH_BLOCK, CHUNK, D), bf16),
                        pltpu.VMEM((BATCH_BLOCK, RWKV_WIDTH // GROUP, RWKV_HEAD, GROUP), f32),
                        pltpu.VMEM((BATCH_BLOCK, RET_HEADS, RET_DK, RET_DV), f32),
                        pltpu.VMEM((D, D), bf16)],
        compiler_params=pltpu.CompilerParams(dimension_semantics=("arbitrary", "arbitrary"),
                                             vmem_limit_bytes=VMEM_LIMIT),
        name="mixers",
    )(p.reshape(B, T, IN_COLS), dec, lora_w, vecs,
      ret_gn_g[0][None, :], seg, x, w_out[0], final_norm_g[None, :])
```

```python
import functools

import numpy as np
import jax
import jax.numpy as jnp
from jax import lax
from jax.experimental import pallas as pl
from jax.experimental.pallas import tpu as pltpu

D_MODEL = 1024
CHUNK = 64
RET_HEADS = 4
RET_DK = 64
RET_DV = 128
RET_QK = RET_HEADS * RET_DK
RET_WIDTH = RET_HEADS * RET_DV
RWKV_WIDTH = 512
RWKV_HEAD = 64
LORA = 64
RET_COLS = 2 * RET_QK + 2 * RET_WIDTH
RWKV_COLS = 4 * RWKV_WIDTH + 2 * LORA
IN_COLS = RET_COLS + RWKV_COLS
ROPE_BASE = 10000.0
RMS_EPS = 1e-6
RET_GN_EPS = 1e-5
RWKV_GN_EPS = 64e-5
GROUP = 256
PROJ_TILE = 512
BATCH_BLOCK = 8
RESULT_LAG = 4
VMEM_LIMIT = 56 * 1024 * 1024

f32 = jnp.float32
bf16 = jnp.bfloat16


def _mm(a, b):
    return jnp.dot(a.astype(bf16), b.astype(bf16), preferred_element_type=f32)


def _mm_nt(a, b):
    return lax.dot_general(a.astype(bf16), b.astype(bf16), (((1,), (1,)), ((), ())),
                           preferred_element_type=f32)


def _split2(x):
    hi = x.astype(bf16)
    lo = (x - hi.astype(f32)).astype(bf16)
    return hi, lo


def _block_diag(x):
    x = x.astype(bf16)
    rows, lanes = x.shape
    zero = jnp.zeros((rows, 128), bf16)
    cols = []
    if lanes == 4 * 128:
        for j in range(4):
            cols.append(jnp.concatenate([x[:, j * 128:(j + 1) * 128] if i == j else zero for i in range(4)],
                                        axis=0))
    else:
        low = lax.broadcasted_iota(jnp.int32, (rows, 128), 1) < 64
        for j in range(2):
            xj = x[:, j * 128:(j + 1) * 128]
            pair = [jnp.where(low, xj, zero), jnp.where(low, zero, xj)]
            cols.append(jnp.concatenate([zero] * (2 * j) + pair + [zero] * (2 - 2 * j), axis=0))
    return jnp.concatenate(cols, axis=1)


def _sigmoid(x):
    return 1.0 / (1.0 + jnp.exp(-x))


def _in_proj_pieces(x_ref, g_ref, w_ref, cos_ref, sin_ref, mu_ref, p_ref, carry_ref):
    nb, C, D = x_ref.shape
    R = nb * C
    W = RWKV_WIDTH
    x = x_ref[...].reshape(R, D)
    u = (x * lax.rsqrt(jnp.mean(x * x, axis=-1, keepdims=True) + RMS_EPS) * g_ref[...]).astype(bf16)
    yield

    def rope(pg):
        cos = jnp.concatenate([jnp.concatenate([cos_ref[...]] * 2, axis=1)] * nb, axis=0)
        sin = jnp.concatenate([jnp.concatenate([sin_ref[...]] * 2, axis=1)] * nb, axis=0)
        half = (lax.broadcasted_iota(jnp.int32, pg.shape, 1) & (RET_DK - 1)) < RET_DK // 2
        lanes = pg.shape[1]
        swapped = jnp.where(half, pltpu.roll(pg, lanes - RET_DK // 2, 1), pltpu.roll(pg, RET_DK // 2, 1))
        return pg * cos + swapped * sin

    def silu(pg):
        return pg * _sigmoid(pg)

    def shifted(lo, hi):
        def fn(pg):
            row = lax.broadcasted_iota(jnp.int32, pg.shape, 0)
            prev = pltpu.roll(pg, 1, 0)
            for bi in range(nb):
                prev = jnp.where(row == bi * C, carry_ref[bi, :, lo:hi], prev)
                carry_ref[bi, :, lo:hi] = pg[(bi + 1) * C - 1:(bi + 1) * C, :]
            return pg + mu_ref[:, lo:hi] * (prev - pg)
        return fn

    groups = [(0, 2 * RET_QK, rope),
              (2 * RET_QK, 2 * RET_QK + RET_WIDTH, lambda pg: pg),
              (2 * RET_QK + RET_WIDTH, RET_COLS, silu),
              (RET_COLS, RET_COLS + W, shifted(0, W)),
              (RET_COLS + W, RET_COLS + 2 * W, shifted(W, 2 * W)),
              (RET_COLS + 2 * W, RET_COLS + 3 * W, shifted(2 * W, 3 * W)),
              (RET_COLS + 3 * W, RET_COLS + 4 * W, lambda pg: silu(shifted(3 * W, 4 * W)(pg))),
              (RET_COLS + 4 * W, IN_COLS, shifted(4 * W, RWKV_COLS))]
    pending = None
    for lo, hi, fn in groups:
        acc = jnp.dot(u, w_ref[:, lo:hi], preferred_element_type=f32)
        if pending is not None:
            p_ref[:, :, pending[1]:pending[2]] = pending[3](pending[0]).reshape(nb, C, pending[2] - pending[1])
        pending = (acc, lo, hi, fn)
        yield
    p_ref[:, :, pending[1]:pending[2]] = pending[3](pending[0]).reshape(nb, C, pending[2] - pending[1])


def _head_blocks(s):
    s = s.astype(bf16)
    zero = jnp.zeros(s.shape[1:], bf16)
    return jnp.concatenate([jnp.concatenate([s[h] if i == h else zero for i in range(RET_HEADS)], axis=0)
                            for h in range(RET_HEADS)], axis=1)


def _staged(n, produce, consume):
    for i in range(n + RESULT_LAG):
        if i < n:
            produce(i)
        if i >= RESULT_LAG:
            consume(i - RESULT_LAG)


def _retention_chunks(qs, ks, vs, states, decay4, qdec, kdec, head_decay):
    n = range(len(qs))
    hs = range(RET_HEADS)
    scores = [_mm_nt(qs[i], _block_diag(ks[i])) * decay4 for i in n]
    out = [_mm(jnp.concatenate([scores[i], qs[i] * qdec], axis=1),
               jnp.concatenate([_block_diag(vs[i]), _head_blocks(states[i])], axis=0)) for i in n]
    kts = [(ks[i] * kdec).T for i in n]
    kv = [[_mm(kts[i][h * RET_DK:(h + 1) * RET_DK], vs[i][:, h * RET_DV:(h + 1) * RET_DV]) for h in hs]
          for i in n]
    s_new = [jnp.stack([states[i][h] * head_decay[h] + kv[i][h] for h in hs]) for i in n]
    return out, s_new


def _rwkv_chunks(rs, ks, vs, kkns, aas, lws, m_cs, filler):
    C = CHUNK
    n = len(rs)
    heads = GROUP // RWKV_HEAD
    ti = lax.broadcasted_iota(jnp.int32, (C, C), 0)
    si = lax.broadcasted_iota(jnp.int32, (C, C), 1)
    tri_incl = jnp.where(si <= ti, 1.0, 0.0).astype(bf16)
    tri_incl = jnp.concatenate([tri_incl, tri_incl], axis=1)
    ti = lax.broadcasted_iota(jnp.int32, (C, GROUP), 0)
    si = lax.broadcasted_iota(jnp.int32, (C, GROUP), 1) & (C - 1)
    strict = si < ti
    incl = si <= ti
    eye = jnp.where(si == ti, 1.0, 0.0)
    low = lax.broadcasted_iota(jnp.int32, (RWKV_HEAD, 128), 1) < RWKV_HEAD

    def slots():
        return [None] * n

    def staged(produce, consume):
        _staged(n, produce, consume)
        next(filler, None)

    cum, lhs, rhs, decay_end, key_end = slots(), slots(), slots(), slots(), slots()

    def cum_matmul(i):
        cum[i] = jnp.dot(tri_incl, jnp.concatenate(_split2(lws[i]), axis=0), preferred_element_type=f32)

    def scale_operands(i):
        g_inc = jnp.exp(cum[i])
        g_inv = jnp.exp(-cum[i])
        g_end = jnp.exp(cum[i][C - 1:C, :] - cum[i])
        beta = kkns[i] * aas[i]
        lhs[i] = jnp.concatenate([-kkns[i] * jnp.exp(cum[i] - lws[i]), rs[i] * g_inc], axis=0).astype(bf16)
        rhs[i] = jnp.concatenate([_block_diag(beta * g_inv), _block_diag(ks[i] * g_inv)], axis=0)
        decay_end[i] = g_inc[C - 1:C, :]
        key_end[i] = jnp.concatenate([beta * g_end, ks[i] * g_end], axis=0).astype(bf16)

    staged(cum_matmul, scale_operands)

    amat, mv, a_ab, a_k, a_rb, t_inv = slots(), slots(), slots(), slots(), slots(), slots()

    def score_matmuls(i):
        amat[i] = _mm_nt(lhs[i], rhs[i])
        mv[i] = _mm_nt(lhs[i], _block_diag(m_cs[i]))

    def mask_scores(i):
        a_ab[i] = jnp.where(strict, amat[i][:C, :GROUP], 0.0).astype(bf16)
        a_rb[i] = jnp.where(incl, amat[i][C:, :GROUP], 0.0).astype(bf16)
        a_k[i] = jnp.concatenate([jnp.where(strict, amat[i][:C, GROUP:], 0.0),
                                  jnp.where(incl, amat[i][C:, GROUP:], 0.0)], axis=0).astype(bf16)
        t_inv[i] = eye + jnp.where(strict, amat[i][:C, :GROUP], 0.0)

    staged(score_matmuls, mask_scores)

    av, st, st_lhs, pow_bd = slots(), slots(), slots(), slots()

    def first_square(i):
        av[i] = _mm(a_k[i], _block_diag(vs[i]))
        st[i] = _mm(a_ab[i], _block_diag(a_ab[i]))

    def after_first_square(i):
        s_b = st[i].astype(bf16)
        st_lhs[i] = jnp.concatenate([s_b, t_inv[i].astype(bf16)], axis=0)
        pow_bd[i] = _block_diag(s_b)

    staged(first_square, after_first_square)

    def square(i):
        st[i] = _mm(st_lhs[i], pow_bd[i])

    def after_square(i):
        s_b = st[i][:C].astype(bf16)
        t_inv[i] = t_inv[i] + st[i][C:]
        st_lhs[i] = jnp.concatenate([s_b, t_inv[i].astype(bf16)], axis=0)
        pow_bd[i] = _block_diag(s_b)

    for _ in range(4):
        staged(square, after_square)

    t_b, y_bd = slots(), slots()

    def last_product(i):
        st[i] = _mm(st_lhs[i][C:], pow_bd[i])

    def after_last_product(i):
        t_b[i] = (t_inv[i] + st[i]).astype(bf16)
        y_bd[i] = _block_diag(mv[i][:C] + av[i][:C])

    staged(last_product, after_last_product)

    u, u_bd, uvt = slots(), slots(), slots()

    def solve(i):
        u[i] = _mm(t_b[i], y_bd[i])

    def after_solve(i):
        u_bd[i] = _block_diag(u[i])
        uvt[i] = jnp.concatenate([u[i], vs[i]], axis=0).T.astype(bf16)

    staged(solve, after_solve)

    o_u, upd, out, m_new = slots(), slots(), slots(), slots()

    def output_and_update(i):
        o_u[i] = _mm(a_rb[i], u_bd[i])
        upd[i] = [_mm(uvt[i][h * RWKV_HEAD:(h + 1) * RWKV_HEAD], key_end[i][:, (h // 2) * 128:(h // 2 + 1) * 128])
                  for h in range(heads)]

    def finish(i):
        out[i] = mv[i][C:] + av[i][C:] + o_u[i]
        m_new[i] = m_cs[i] * decay_end[i] + jnp.concatenate(
            [jnp.where(low, upd[i][2 * j], upd[i][2 * j + 1]) for j in range(heads // 2)], axis=1)

    staged(output_and_update, finish)
    return out, m_new


def _mix_chunk(p_ref, dec_ref, lora_ref, vec_ref, rgn_ref, seg_ref, y_ref, m_ref, s_ref, filler):
    C = CHUNK
    nb = BATCH_BLOCK
    R = nb * C
    W = RWKV_WIDTH
    NG = W // GROUP

    def rows(x, bi):
        return x[bi * C:(bi + 1) * C]

    def rwkv_cols(lo, hi):
        return p_ref[:, :, RET_COLS + lo:RET_COLS + hi].reshape(R, hi - lo)

    r = rwkv_cols(0, W)
    kr = rwkv_cols(W, 2 * W)
    vr = rwkv_cols(2 * W, 3 * W)
    gate_rw = rwkv_cols(3 * W, 4 * W)
    xwa = rwkv_cols(4 * W, RWKV_COLS)
    lane = lax.broadcasted_iota(jnp.int32, xwa.shape, 1)
    lora = _mm(jnp.where(lane < LORA, jnp.tanh(xwa), xwa), lora_ref[...])
    w0, a0, k_k, k_a, r_k, gn_g, gn_b = (vec_ref[i:i + 1, :] for i in range(7))
    lw = -np.float32(np.exp(-0.5)) * _sigmoid(w0 + lora[:, :W])
    a = _sigmoid(a0 + lora[:, W:])
    seg = seg_ref[...]

    def segsum(x):
        xs = jnp.concatenate([x[:, g * GROUP:(g + 1) * GROUP] for g in range(NG)], axis=0)
        tot = jnp.dot(xs.astype(bf16), seg, preferred_element_type=f32)
        return jnp.concatenate([tot[g * R:(g + 1) * R] for g in range(NG)], axis=1)

    kk = kr * k_k
    kkn = kk * lax.rsqrt(jnp.maximum(segsum(kk * kk), 1e-24))
    kmod = kr * (1.0 + (a - 1.0) * k_a)

    q = p_ref[:, :, 0:RET_QK].reshape(R, RET_QK)
    k = p_ref[:, :, RET_QK:2 * RET_QK].reshape(R, RET_QK)
    v = p_ref[:, :, 2 * RET_QK:2 * RET_QK + RET_WIDTH].reshape(R, RET_WIDTH)
    rets, s_new = _retention_chunks([rows(q, bi) for bi in range(nb)], [rows(k, bi) for bi in range(nb)],
                                    [rows(v, bi) for bi in range(nb)], [s_ref[bi] for bi in range(nb)],
                                    dec_ref[0], dec_ref[1], dec_ref[2], _RET_CHUNK_DECAY)
    for bi in range(nb):
        s_ref[bi] = s_new[bi]
    ret = jnp.concatenate(rets, axis=0)
    rgn = rgn_ref[...]
    for h in range(RET_HEADS):
        sl = slice(h * RET_DV, (h + 1) * RET_DV)
        xh = ret[:, sl]
        d = xh - jnp.mean(xh, axis=-1, keepdims=True)
        var = jnp.mean(d * d, axis=-1, keepdims=True)
        gh = p_ref[:, :, 2 * RET_QK + RET_WIDTH + h * RET_DV:2 * RET_QK + RET_WIDTH + (h + 1) * RET_DV]
        gh = gh.reshape(R, RET_DV)
        yh = gh * (d * lax.rsqrt(var + RET_GN_EPS) * rgn[:, sl])
        y_ref[:, :, sl] = yh.reshape(nb, C, RET_DV).astype(y_ref.dtype)

    chains = [(bi, g) for bi in range(nb) for g in range(NG)]

    def pick(x):
        return [x[bi * C:(bi + 1) * C, g * GROUP:(g + 1) * GROUP] for bi, g in chains]

    outs, m_new = _rwkv_chunks(pick(r), pick(kmod), pick(vr), pick(kkn), pick(a), pick(lw),
                               [m_ref[bi, g] for bi, g in chains], filler)
    for i, (bi, g) in enumerate(chains):
        m_ref[bi, g] = m_new[i]
    o = jnp.concatenate([jnp.concatenate(outs[bi * NG:(bi + 1) * NG], axis=1) for bi in range(nb)], axis=0)
    d = o - segsum(o) * (1.0 / RWKV_HEAD)
    var = segsum(d * d) * (1.0 / RWKV_HEAD)
    o = d * lax.rsqrt(var + RWKV_GN_EPS) * gn_g + gn_b
    bonus = segsum(r * kmod * r_k) * vr
    y_rw = gate_rw * (o + bonus)
    y_ref[:, :, RET_WIDTH:] = y_rw.reshape(nb, C, W).astype(y_ref.dtype)


def _project_chunk(x_ref, y_ref, w_ref, g_ref, o_ref):
    nb, C, D = x_ref.shape
    y = y_ref[...].reshape(nb * C, D)
    h = x_ref[...].reshape(nb * C, D) + jnp.dot(y, w_ref[...], preferred_element_type=f32)
    out = h * lax.rsqrt(jnp.mean(h * h, axis=-1, keepdims=True) + RMS_EPS) * g_ref[...]
    o_ref[...] = out.reshape(nb, C, D)


def _block_kernel(xn_ref, ng_ref, win_ref, cos_ref, sin_ref, mu_ref, dec_ref, lora_ref, vec_ref, rgn_ref,
                  seg_ref, x_ref, wout_ref, fg_ref, o_ref, p_ref, carry_ref, y_ref, m_ref, s_ref, woutb_ref):
    @pl.when(pl.program_id(1) == 0)
    def _():
        m_ref[...] = jnp.zeros_like(m_ref)
        s_ref[...] = jnp.zeros_like(s_ref)
        y_ref[...] = jnp.zeros_like(y_ref)
        p_ref[...] = jnp.zeros_like(p_ref)
        carry_ref[...] = jnp.zeros_like(carry_ref)
        woutb_ref[...] = wout_ref[...].astype(bf16)

    _project_chunk(x_ref, y_ref, woutb_ref, fg_ref, o_ref)
    filler = _in_proj_pieces(xn_ref, ng_ref, win_ref, cos_ref, sin_ref, mu_ref, p_ref, carry_ref)
    _mix_chunk(p_ref, dec_ref, lora_ref, vec_ref, rgn_ref, seg_ref, y_ref, m_ref, s_ref, filler)
    for _ in filler:
        pass


def _rope_tables(seq):
    half = RET_DK // 2
    expo = -jnp.arange(half, dtype=f32) / f32(half)
    freqs = jnp.exp(expo * f32(np.log(ROPE_BASE)))
    ang = jnp.arange(seq, dtype=jnp.int32).astype(f32)[:, None] * freqs[None, :]
    cos = jnp.cos(ang)
    sin = jnp.sin(ang)
    cos_full = jnp.tile(jnp.concatenate([cos, cos], axis=1), (1, RET_HEADS))
    sin_signed = jnp.tile(jnp.concatenate([-sin, sin], axis=1), (1, RET_HEADS))
    return cos_full, sin_signed


_RET_LOG_GAMMA = np.log(1.0 - np.exp2(-5.0 - np.arange(RET_HEADS, dtype=np.float64)))
_RET_CHUNK_DECAY = tuple(np.float32(v) for v in np.exp(_RET_LOG_GAMMA * CHUNK))


def _retention_constants():
    C = CHUNK
    lg = _RET_LOG_GAMMA
    lane_lg = np.repeat(lg, RET_DK)[None, :]
    n = np.arange(C, dtype=np.float64)[:, None]
    m = np.tile(np.arange(C, dtype=np.float64), RET_HEADS)[None, :]
    scale = RET_DK ** -0.5
    decay4 = scale * np.exp(lane_lg * np.abs(n - m))
    qdec = np.exp(lane_lg * (n + 1.0)) * np.ones((1, RET_QK))
    kdec = scale * np.exp(lane_lg * (C - 1.0 - n)) * np.ones((1, RET_QK))
    return jnp.asarray(np.stack([decay4, qdec, kdec]).astype(np.float32))


def _const_spec(shape, single_buffer=False):
    mode = pl.Buffered(1) if single_buffer else None
    return pl.BlockSpec(shape, lambda *_: (0,) * len(shape), pipeline_mode=mode)


@jax.jit
def kernel(x, norm_g, w_in, ret_gn_g, rwkv_mu, w_lora_up, w0, a_lora_up, a0, k_k, k_a, r_k,
           rwkv_gn_g, rwkv_gn_b, w_out, final_norm_g):
    B, T, D = x.shape
    assert D == D_MODEL and T % CHUNK == 0 and B % BATCH_BLOCK == 0
    assert norm_g.shape[0] == 1, "single-layer block"
    cos, sin = _rope_tables(T)
    dec = _retention_constants()
    zeros = jnp.zeros((LORA, RWKV_WIDTH), f32)
    lora_w = jnp.concatenate(
        [jnp.concatenate([w_lora_up[0], zeros], axis=1),
         jnp.concatenate([zeros, a_lora_up[0]], axis=1)], axis=0).astype(bf16)
    vecs = jnp.stack([w0[0], a0[0], k_k[0], k_a[0], r_k[0].reshape(-1), rwkv_gn_g[0], rwkv_gn_b[0],
                      jnp.zeros((RWKV_WIDTH,), f32)])
    seg_ids = np.arange(GROUP) // RWKV_HEAD
    seg = jnp.asarray(seg_ids[:, None] == seg_ids[None, :], bf16)

    n_chunks = T // CHUNK
    last = n_chunks - 1
    def behind(c):
        return jnp.clip(c - 2, 0, last)

    return pl.pallas_call(
        _block_kernel,
        grid=(B // BATCH_BLOCK, n_chunks + 2),
        in_specs=[pl.BlockSpec((BATCH_BLOCK, CHUNK, D), lambda b, c: (b, jnp.minimum(c, last), 0)),
                  _const_spec((1, D)),
                  _const_spec((D, IN_COLS), single_buffer=True),
                  pl.BlockSpec((CHUNK, RET_QK), lambda b, c: (jnp.minimum(c, last), 0)),
                  pl.BlockSpec((CHUNK, RET_QK), lambda b, c: (jnp.minimum(c, last), 0)),
                  _const_spec((1, RWKV_COLS)),
                  _const_spec((3, CHUNK, RET_QK)),
                  _const_spec((2 * LORA, 2 * RWKV_WIDTH)),
                  _const_spec((8, RWKV_WIDTH)),
                  _const_spec((1, RET_WIDTH)),
                  _const_spec((GROUP, GROUP)),
                  pl.BlockSpec((BATCH_BLOCK, CHUNK, D), lambda b, c: (b, behind(c), 0)),
                  _const_spec((D, D), single_buffer=True),
                  _const_spec((1, D))],
        out_specs=pl.BlockSpec((BATCH_BLOCK, CHUNK, D), lambda b, c: (b, behind(c), 0)),
        out_shape=jax.ShapeDtypeStruct((B, T, D), f32),
        scratch_shapes=[pltpu.VMEM((BATCH_BLOCK, CHUNK, IN_COLS), f32),
                        pltpu.VMEM((BATCH_BLOCK, 1, RWKV_COLS), f32),
                        pltpu.VMEM((BATCH_BLOCK, CHUNK, D), bf16),
                        pltpu.VMEM((BATCH_BLOCK, RWKV_WIDTH // GROUP, RWKV_HEAD, GROUP), f32),
                        pltpu.VMEM((BATCH_BLOCK, RET_HEADS, RET_DK, RET_DV), f32),
                        pltpu.VMEM((D, D), bf16)],
        compiler_params=pltpu.CompilerParams(dimension_semantics=("arbitrary", "arbitrary"),
                                             vmem_limit_bytes=VMEM_LIMIT),
        name="block",
    )(x, norm_g[0][None, :], w_in[0].astype(bf16), cos, sin, rwkv_mu[0][None, :], dec, lora_w, vecs,
      ret_gn_g[0][None, :], seg, x, w_out[0], final_norm_g[None, :])
```

```python
import functools

import numpy as np
import jax
import jax.numpy as jnp
from jax import lax
from jax.experimental import pallas as pl
from jax.experimental.pallas import tpu as pltpu

D_MODEL = 1024
CHUNK = 64
RET_HEADS = 4
RET_DK = 64
RET_DV = 128
RET_QK = RET_HEADS * RET_DK
RET_WIDTH = RET_HEADS * RET_DV
RWKV_WIDTH = 512
RWKV_HEAD = 64
LORA = 64
RET_COLS = 2 * RET_QK + 2 * RET_WIDTH
RWKV_COLS = 4 * RWKV_WIDTH + 2 * LORA
IN_COLS = RET_COLS + RWKV_COLS
ROPE_BASE = 10000.0
RMS_EPS = 1e-6
RET_GN_EPS = 1e-5
RWKV_GN_EPS = 64e-5
GROUP = 256
PROJ_TILE = 512
BATCH_BLOCK = 8
RESULT_LAG = 4
VMEM_LIMIT = 56 * 1024 * 1024

f32 = jnp.float32
bf16 = jnp.bfloat16


def _mm(a, b):
    return jnp.dot(a.astype(bf16), b.astype(bf16), preferred_element_type=f32)


def _mm_nt(a, b):
    return lax.dot_general(a.astype(bf16), b.astype(bf16), (((1,), (1,)), ((), ())),
                           preferred_element_type=f32)


def _split2(x):
    hi = x.astype(bf16)
    lo = (x - hi.astype(f32)).astype(bf16)
    return hi, lo


def _block_diag(x):
    x = x.astype(bf16)
    rows, lanes = x.shape
    zero = jnp.zeros((rows, 128), bf16)
    cols = []
    if lanes == 4 * 128:
        for j in range(4):
            cols.append(jnp.concatenate([x[:, j * 128:(j + 1) * 128] if i == j else zero for i in range(4)],
                                        axis=0))
    else:
        low = lax.broadcasted_iota(jnp.int32, (rows, 128), 1) < 64
        for j in range(2):
            xj = x[:, j * 128:(j + 1) * 128]
            pair = [jnp.where(low, xj, zero), jnp.where(low, zero, xj)]
            cols.append(jnp.concatenate([zero] * (2 * j) + pair + [zero] * (2 - 2 * j), axis=0))
    return jnp.concatenate(cols, axis=1)


def _sigmoid(x):
    return 1.0 / (1.0 + jnp.exp(-x))


def _in_proj_kernel(tiles_per_seq, x_ref, g_ref, w_ref, cos_ref, sin_ref, mu_ref, p_ref, wb_ref, carry_ref):
    i = pl.program_id(0)

    @pl.when(i == 0)
    def _():
        wb_ref[...] = w_ref[...].astype(bf16)

    x = x_ref[...]
    u = (x * lax.rsqrt(jnp.mean(x * x, axis=-1, keepdims=True) + RMS_EPS) * g_ref[...]).astype(bf16)
    rows = x.shape[0]
    seq_start = (i % tiles_per_seq) == 0
    W = RWKV_WIDTH

    def rope(pg):
        cos = jnp.concatenate([cos_ref[...]] * 2, axis=1)
        sin = jnp.concatenate([sin_ref[...]] * 2, axis=1)
        half = (lax.broadcasted_iota(jnp.int32, pg.shape, 1) & (RET_DK - 1)) < RET_DK // 2
        lanes = pg.shape[1]
        swapped = jnp.where(half, pltpu.roll(pg, lanes - RET_DK // 2, 1), pltpu.roll(pg, RET_DK // 2, 1))
        return pg * cos + swapped * sin

    def silu(pg):
        return pg * _sigmoid(pg)

    def shifted(lo, hi):
        def fn(pg):
            first = jnp.where(seq_start, 0.0, carry_ref[:, lo:hi])
            row0 = lax.broadcasted_iota(jnp.int32, pg.shape, 0) == 0
            prev = jnp.where(row0, first, pltpu.roll(pg, 1, 0))
            carry_ref[:, lo:hi] = pg[rows - 1:rows, :]
            return pg + mu_ref[:, lo:hi] * (prev - pg)
        return fn

    groups = [(0, 2 * RET_QK, rope),
              (2 * RET_QK, 2 * RET_QK + RET_WIDTH, lambda pg: pg),
              (2 * RET_QK + RET_WIDTH, RET_COLS, silu),
              (RET_COLS, RET_COLS + W, shifted(0, W)),
              (RET_COLS + W, RET_COLS + 2 * W, shifted(W, 2 * W)),
              (RET_COLS + 2 * W, RET_COLS + 3 * W, shifted(2 * W, 3 * W)),
              (RET_COLS + 3 * W, RET_COLS + 4 * W, lambda pg: silu(shifted(3 * W, 4 * W)(pg))),
              (RET_COLS + 4 * W, IN_COLS, shifted(4 * W, RWKV_COLS))]
    pending = None
    for lo, hi, fn in groups:
        acc = jnp.dot(u, wb_ref[:, lo:hi], preferred_element_type=f32)
        if pending is not None:
            p_ref[:, pending[1]:pending[2]] = pending[3](pending[0])
        pending = (acc, lo, hi, fn)
    p_ref[:, pending[1]:pending[2]] = pending[3](pending[0])


def _head_blocks(s):
    s = s.astype(bf16)
    zero = jnp.zeros(s.shape[1:], bf16)
    return jnp.concatenate([jnp.concatenate([s[h] if i == h else zero for i in range(RET_HEADS)], axis=0)
                            for h in range(RET_HEADS)], axis=1)


def _staged(n, produce, consume):
    for i in range(n + RESULT_LAG):
        if i < n:
            produce(i)
        if i >= RESULT_LAG:
            consume(i - RESULT_LAG)


def _retention_chunks(qs, ks, vs, states, decay4, qdec, kdec, head_decay):
    n = range(len(qs))
    hs = range(RET_HEADS)
    scores = [_mm_nt(qs[i], _block_diag(ks[i])) * decay4 for i in n]
    out = [_mm(jnp.concatenate([scores[i], qs[i] * qdec], axis=1),
               jnp.concatenate([_block_diag(vs[i]), _head_blocks(states[i])], axis=0)) for i in n]
    kts = [(ks[i] * kdec).T for i in n]
    kv = [[_mm(kts[i][h * RET_DK:(h + 1) * RET_DK], vs[i][:, h * RET_DV:(h + 1) * RET_DV]) for h in hs]
          for i in n]
    s_new = [jnp.stack([states[i][h] * head_decay[h] + kv[i][h] for h in hs]) for i in n]
    return out, s_new


def _rwkv_chunks(rs, ks, vs, kkns, aas, lws, m_cs):
    C = CHUNK
    n = len(rs)
    heads = GROUP // RWKV_HEAD
    ti = lax.broadcasted_iota(jnp.int32, (C, C), 0)
    si = lax.broadcasted_iota(jnp.int32, (C, C), 1)
    tri_incl = jnp.where(si <= ti, 1.0, 0.0).astype(bf16)
    tri_incl = jnp.concatenate([tri_incl, tri_incl], axis=1)
    ti = lax.broadcasted_iota(jnp.int32, (C, GROUP), 0)
    si = lax.broadcasted_iota(jnp.int32, (C, GROUP), 1) & (C - 1)
    strict = si < ti
    incl = si <= ti
    eye = jnp.where(si == ti, 1.0, 0.0)
    low = lax.broadcasted_iota(jnp.int32, (RWKV_HEAD, 128), 1) < RWKV_HEAD

    def slots():
        return [None] * n

    def staged(produce, consume):
        _staged(n, produce, consume)

    cum, lhs, rhs, decay_end, key_end = slots(), slots(), slots(), slots(), slots()

    def cum_matmul(i):
        cum[i] = jnp.dot(tri_incl, jnp.concatenate(_split2(lws[i]), axis=0), preferred_element_type=f32)

    def scale_operands(i):
        g_inc = jnp.exp(cum[i])
        g_inv = jnp.exp(-cum[i])
        g_end = jnp.exp(cum[i][C - 1:C, :] - cum[i])
        beta = kkns[i] * aas[i]
        lhs[i] = jnp.concatenate([-kkns[i] * jnp.exp(cum[i] - lws[i]), rs[i] * g_inc], axis=0).astype(bf16)
        rhs[i] = jnp.concatenate([_block_diag(beta * g_inv), _block_diag(ks[i] * g_inv)], axis=0)
        decay_end[i] = g_inc[C - 1:C, :]
        key_end[i] = jnp.concatenate([beta * g_end, ks[i] * g_end], axis=0).astype(bf16)

    staged(cum_matmul, scale_operands)

    amat, mv, a_ab, a_k, a_rb, t_inv = slots(), slots(), slots(), slots(), slots(), slots()

    def score_matmuls(i):
        amat[i] = _mm_nt(lhs[i], rhs[i])
        mv[i] = _mm_nt(lhs[i], _block_diag(m_cs[i]))

    def mask_scores(i):
        a_ab[i] = jnp.where(strict, amat[i][:C, :GROUP], 0.0).astype(bf16)
        a_rb[i] = jnp.where(incl, amat[i][C:, :GROUP], 0.0).astype(bf16)
        a_k[i] = jnp.concatenate([jnp.where(strict, amat[i][:C, GROUP:], 0.0),
                                  jnp.where(incl, amat[i][C:, GROUP:], 0.0)], axis=0).astype(bf16)
        t_inv[i] = eye + jnp.where(strict, amat[i][:C, :GROUP], 0.0)

    staged(score_matmuls, mask_scores)

    av, st, st_lhs, pow_bd = slots(), slots(), slots(), slots()

    def first_square(i):
        av[i] = _mm(a_k[i], _block_diag(vs[i]))
        st[i] = _mm(a_ab[i], _block_diag(a_ab[i]))

    def after_first_square(i):
        s_b = st[i].astype(bf16)
        st_lhs[i] = jnp.concatenate([s_b, t_inv[i].astype(bf16)], axis=0)
        pow_bd[i] = _block_diag(s_b)

    staged(first_square, after_first_square)

    def square(i):
        st[i] = _mm(st_lhs[i], pow_bd[i])

    def after_square(i):
        s_b = st[i][:C].astype(bf16)
        t_inv[i] = t_inv[i] + st[i][C:]
        st_lhs[i] = jnp.concatenate([s_b, t_inv[i].astype(bf16)], axis=0)
        pow_bd[i] = _block_diag(s_b)

    for _ in range(4):
        staged(square, after_square)

    t_b, y_bd = slots(), slots()

    def last_product(i):
        st[i] = _mm(st_lhs[i][C:], pow_bd[i])

    def after_last_product(i):
        t_b[i] = (t_inv[i] + st[i]).astype(bf16)
        y_bd[i] = _block_diag(mv[i][:C] + av[i][:C])

    staged(last_product, after_last_product)

    u, u_bd, uvt = slots(), slots(), slots()

    def solve(i):
        u[i] = _mm(t_b[i], y_bd[i])

    def after_solve(i):
        u_bd[i] = _block_diag(u[i])
        uvt[i] = jnp.concatenate([u[i], vs[i]], axis=0).T.astype(bf16)

    staged(solve, after_solve)

    o_u, upd, out, m_new = slots(), slots(), slots(), slots()

    def output_and_update(i):
        o_u[i] = _mm(a_rb[i], u_bd[i])
        upd[i] = [_mm(uvt[i][h * RWKV_HEAD:(h + 1) * RWKV_HEAD], key_end[i][:, (h // 2) * 128:(h // 2 + 1) * 128])
                  for h in range(heads)]

    def finish(i):
        out[i] = mv[i][C:] + av[i][C:] + o_u[i]
        m_new[i] = m_cs[i] * decay_end[i] + jnp.concatenate(
            [jnp.where(low, upd[i][2 * j], upd[i][2 * j + 1]) for j in range(heads // 2)], axis=1)

    staged(output_and_update, finish)
    return out, m_new


def _mix_chunk(p_ref, dec_ref, lora_ref, vec_ref, rgn_ref, seg_ref, y_ref, m_ref, s_ref):
    C = CHUNK
    nb = BATCH_BLOCK
    R = nb * C
    W = RWKV_WIDTH
    NG = W // GROUP

    def rows(x, bi):
        return x[bi * C:(bi + 1) * C]

    def rwkv_cols(lo, hi):
        return p_ref[:, :, RET_COLS + lo:RET_COLS + hi].reshape(R, hi - lo)

    r = rwkv_cols(0, W)
    kr = rwkv_cols(W, 2 * W)
    vr = rwkv_cols(2 * W, 3 * W)
    gate_rw = rwkv_cols(3 * W, 4 * W)
    xwa = rwkv_cols(4 * W, RWKV_COLS)
    lane = lax.broadcasted_iota(jnp.int32, xwa.shape, 1)
    lora = _mm(jnp.where(lane < LORA, jnp.tanh(xwa), xwa), lora_ref[...])
    w0, a0, k_k, k_a, r_k, gn_g, gn_b = (vec_ref[i:i + 1, :] for i in range(7))
    lw = -np.float32(np.exp(-0.5)) * _sigmoid(w0 + lora[:, :W])
    a = _sigmoid(a0 + lora[:, W:])
    seg = seg_ref[...]

    def segsum(x):
        xs = jnp.concatenate([x[:, g * GROUP:(g + 1) * GROUP] for g in range(NG)], axis=0)
        tot = jnp.dot(xs.astype(bf16), seg, preferred_element_type=f32)
        return jnp.concatenate([tot[g * R:(g + 1) * R] for g in range(NG)], axis=1)

    kk = kr * k_k
    kkn = kk * lax.rsqrt(jnp.maximum(segsum(kk * kk), 1e-24))
    kmod = kr * (1.0 + (a - 1.0) * k_a)

    q = p_ref[:, :, 0:RET_QK].reshape(R, RET_QK)
    k = p_ref[:, :, RET_QK:2 * RET_QK].reshape(R, RET_QK)
    v = p_ref[:, :, 2 * RET_QK:2 * RET_QK + RET_WIDTH].reshape(R, RET_WIDTH)
    rets, s_new = _retention_chunks([rows(q, bi) for bi in range(nb)], [rows(k, bi) for bi in range(nb)],
                                    [rows(v, bi) for bi in range(nb)], [s_ref[bi] for bi in range(nb)],
                                    dec_ref[0], dec_ref[1], dec_ref[2], _RET_CHUNK_DECAY)
    for bi in range(nb):
        s_ref[bi] = s_new[bi]
    ret = jnp.concatenate(rets, axis=0)
    rgn = rgn_ref[...]
    for h in range(RET_HEADS):
        sl = slice(h * RET_DV, (h + 1) * RET_DV)
        xh = ret[:, sl]
        d = xh - jnp.mean(xh, axis=-1, keepdims=True)
        var = jnp.mean(d * d, axis=-1, keepdims=True)
        gh = p_ref[:, :, 2 * RET_QK + RET_WIDTH + h * RET_DV:2 * RET_QK + RET_WIDTH + (h + 1) * RET_DV]
        gh = gh.reshape(R, RET_DV)
        yh = gh * (d * lax.rsqrt(var + RET_GN_EPS) * rgn[:, sl])
        y_ref[:, :, sl] = yh.reshape(nb, C, RET_DV).astype(y_ref.dtype)

    chains = [(bi, g) for bi in range(nb) for g in range(NG)]

    def pick(x):
        return [x[bi * C:(bi + 1) * C, g * GROUP:(g + 1) * GROUP] for bi, g in chains]

    outs, m_new = _rwkv_chunks(pick(r), pick(kmod), pick(vr), pick(kkn), pick(a), pick(lw),
                               [m_ref[bi, g] for bi, g in chains])
    for i, (bi, g) in enumerate(chains):
        m_ref[bi, g] = m_new[i]
    o = jnp.concatenate([jnp.concatenate(outs[bi * NG:(bi + 1) * NG], axis=1) for bi in range(nb)], axis=0)
    d = o - segsum(o) * (1.0 / RWKV_HEAD)
    var = segsum(d * d) * (1.0 / RWKV_HEAD)
    o = d * lax.rsqrt(var + RWKV_GN_EPS) * gn_g + gn_b
    bonus = segsum(r * kmod * r_k) * vr
    y_rw = gate_rw * (o + bonus)
    y_ref[:, :, RET_WIDTH:] = y_rw.reshape(nb, C, W).astype(y_ref.dtype)


def _project_chunk(x_ref, y_ref, w_ref, g_ref, o_ref):
    nb, C, D = x_ref.shape
    y = y_ref[...].reshape(nb * C, D)
    h = x_ref[...].reshape(nb * C, D) + jnp.dot(y, w_ref[...], preferred_element_type=f32)
    out = h * lax.rsqrt(jnp.mean(h * h, axis=-1, keepdims=True) + RMS_EPS) * g_ref[...]
    o_ref[...] = out.reshape(nb, C, D)


def _mixer_kernel(p_ref, dec_ref, lora_ref, vec_ref, rgn_ref,
                  seg_ref, x_ref, wout_ref, fg_ref, o_ref, y_ref, m_ref, s_ref, woutb_ref):
    @pl.when(pl.program_id(1) == 0)
    def _():
        m_ref[...] = jnp.zeros_like(m_ref)
        s_ref[...] = jnp.zeros_like(s_ref)
        woutb_ref[...] = wout_ref[...].astype(bf16)

    _mix_chunk(p_ref, dec_ref, lora_ref, vec_ref, rgn_ref, seg_ref, y_ref, m_ref, s_ref)
    _project_chunk(x_ref, y_ref, woutb_ref, fg_ref, o_ref)


def _rope_tables(seq):
    half = RET_DK // 2
    expo = -jnp.arange(half, dtype=f32) / f32(half)
    freqs = jnp.exp(expo * f32(np.log(ROPE_BASE)))
    ang = jnp.arange(seq, dtype=jnp.int32).astype(f32)[:, None] * freqs[None, :]
    cos = jnp.cos(ang)
    sin = jnp.sin(ang)
    cos_full = jnp.tile(jnp.concatenate([cos, cos], axis=1), (1, RET_HEADS))
    sin_signed = jnp.tile(jnp.concatenate([-sin, sin], axis=1), (1, RET_HEADS))
    return cos_full, sin_signed


_RET_LOG_GAMMA = np.log(1.0 - np.exp2(-5.0 - np.arange(RET_HEADS, dtype=np.float64)))
_RET_CHUNK_DECAY = tuple(np.float32(v) for v in np.exp(_RET_LOG_GAMMA * CHUNK))


def _retention_constants():
    C = CHUNK
    lg = _RET_LOG_GAMMA
    lane_lg = np.repeat(lg, RET_DK)[None, :]
    n = np.arange(C, dtype=np.float64)[:, None]
    m = np.tile(np.arange(C, dtype=np.float64), RET_HEADS)[None, :]
    scale = RET_DK ** -0.5
    decay4 = scale * np.exp(lane_lg * np.abs(n - m))
    qdec = np.exp(lane_lg * (n + 1.0)) * np.ones((1, RET_QK))
    kdec = scale * np.exp(lane_lg * (C - 1.0 - n)) * np.ones((1, RET_QK))
    return jnp.asarray(np.stack([decay4, qdec, kdec]).astype(np.float32))


def _const_spec(shape, single_buffer=False):
    mode = pl.Buffered(1) if single_buffer else None
    return pl.BlockSpec(shape, lambda *_: (0,) * len(shape), pipeline_mode=mode)


@jax.jit
def kernel(x, norm_g, w_in, ret_gn_g, rwkv_mu, w_lora_up, w0, a_lora_up, a0, k_k, k_a, r_k,
           rwkv_gn_g, rwkv_gn_b, w_out, final_norm_g):
    B, T, D = x.shape
    assert D == D_MODEL and T % CHUNK == 0 and T % PROJ_TILE == 0 and B % BATCH_BLOCK == 0
    assert norm_g.shape[0] == 1, "single-layer block"
    n_tok = B * T
    xf = x.reshape(n_tok, D)
    params = pltpu.CompilerParams(dimension_semantics=("arbitrary",), vmem_limit_bytes=VMEM_LIMIT)
    tiles_per_seq = T // PROJ_TILE
    cos, sin = _rope_tables(T)

    p = pl.pallas_call(
        functools.partial(_in_proj_kernel, tiles_per_seq),
        grid=(n_tok // PROJ_TILE,),
        in_specs=[pl.BlockSpec((PROJ_TILE, D), lambda i: (i, 0)),
                  _const_spec((1, D)),
                  _const_spec((D, IN_COLS), single_buffer=True),
                  pl.BlockSpec((PROJ_TILE, RET_QK), lambda i: (i % tiles_per_seq, 0)),
                  pl.BlockSpec((PROJ_TILE, RET_QK), lambda i: (i % tiles_per_seq, 0)),
                  _const_spec((1, RWKV_COLS))],
        out_specs=pl.BlockSpec((PROJ_TILE, IN_COLS), lambda i: (i, 0)),
        out_shape=jax.ShapeDtypeStruct((n_tok, IN_COLS), f32),
        scratch_shapes=[pltpu.VMEM((D, IN_COLS), bf16),
                        pltpu.VMEM((1, RWKV_COLS), f32)],
        compiler_params=params,
        name="in_proj",
    )(xf, norm_g[0][None, :], w_in[0], cos, sin, rwkv_mu[0][None, :])

    dec = _retention_constants()
    zeros = jnp.zeros((LORA, RWKV_WIDTH), f32)
    lora_w = jnp.concatenate(
        [jnp.concatenate([w_lora_up[0], zeros], axis=1),
         jnp.concatenate([zeros, a_lora_up[0]], axis=1)], axis=0).astype(bf16)
    vecs = jnp.stack([w0[0], a0[0], k_k[0], k_a[0], r_k[0].reshape(-1), rwkv_gn_g[0], rwkv_gn_b[0],
                      jnp.zeros((RWKV_WIDTH,), f32)])
    seg_ids = np.arange(GROUP) // RWKV_HEAD
    seg = jnp.asarray(seg_ids[:, None] == seg_ids[None, :], bf16)

    return pl.pallas_call(
        _mixer_kernel,
        grid=(B // BATCH_BLOCK, T // CHUNK),
        in_specs=[pl.BlockSpec((BATCH_BLOCK, CHUNK, IN_COLS), lambda b, c: (b, c, 0)),
                  _const_spec((3, CHUNK, RET_QK)),
                  _const_spec((2 * LORA, 2 * RWKV_WIDTH)),
                  _const_spec((8, RWKV_WIDTH)),
                  _const_spec((1, RET_WIDTH)),
                  _const_spec((GROUP, GROUP)),
                  pl.BlockSpec((BATCH_BLOCK, CHUNK, D), lambda b, c: (b, c, 0)),
                  _const_spec((D, D), single_buffer=True),
                  _const_spec((1, D))],
        out_specs=pl.BlockSpec((BATCH_BLOCK, CHUNK, D), lambda b, c: (b, c, 0)),
        out_shape=jax.ShapeDtypeStruct((B, T, D), f32),
        scratch_shapes=[pltpu.VMEM((BATCH_BLOCK, CHUNK, D), bf16),
                        pltpu.VMEM((BATCH_BLOCK, RWKV_WIDTH // GROUP, RWKV_HEAD, GROUP), f32),
                        pltpu.VMEM((BATCH_BLOCK, RET_HEADS, RET_DK, RET_DV), f32),
                        pltpu.VMEM((D, D), bf16)],
        compiler_params=pltpu.CompilerParams(dimension_semantics=("arbitrary", "arbitrary"),
                                             vmem_limit_bytes=VMEM_LIMIT),
        name="mixers",
    )(p.reshape(B, T, IN_COLS), dec, lora_w, vecs,
      ret_gn_g[0][None, :], seg, x, w_out[0], final_norm_g[None, :])
```

```python
import functools

import numpy as np
import jax
import jax.numpy as jnp
from jax import lax
from jax.experimental import pallas as pl
from jax.experimental.pallas import tpu as pltpu

D_MODEL = 1024
CHUNK = 64
RET_HEADS = 4
RET_DK = 64
RET_DV = 128
RET_QK = RET_HEADS * RET_DK
RET_WIDTH = RET_HEADS * RET_DV
RWKV_WIDTH = 512
RWKV_HEAD = 64
LORA = 64
RET_COLS = 2 * RET_QK + 2 * RET_WIDTH
RWKV_COLS = 4 * RWKV_WIDTH + 2 * LORA
IN_COLS = RET_COLS + RWKV_COLS
ROPE_BASE = 10000.0
RMS_EPS = 1e-6
RET_GN_EPS = 1e-5
RWKV_GN_EPS = 64e-5
LANES = 128
GROUP = 256
PROJ_TILE = 512
BATCH_BLOCK = 8
RESULT_LAG = 4
VMEM_LIMIT = 56 * 1024 * 1024

f32 = jnp.float32
bf16 = jnp.bfloat16


def _mm(a, b):
    return jnp.dot(a.astype(bf16), b.astype(bf16), preferred_element_type=f32)


def _mm_nt(a, b):
    return lax.dot_general(a.astype(bf16), b.astype(bf16), (((1,), (1,)), ((), ())),
                           preferred_element_type=f32)


def _split2(x):
    hi = x.astype(bf16)
    lo = (x - hi.astype(f32)).astype(bf16)
    return hi, lo


def _block_diag(x):
    x = x.astype(bf16)
    rows, lanes = x.shape
    assert lanes in (2 * LANES, 4 * LANES)
    zero = jnp.zeros((rows, LANES), bf16)
    cols = []
    if lanes == 4 * LANES:
        for j in range(4):
            cols.append(jnp.concatenate([x[:, j * LANES:(j + 1) * LANES] if i == j else zero for i in range(4)],
                                        axis=0))
    else:
        low = lax.broadcasted_iota(jnp.int32, (rows, LANES), 1) < LANES // 2
        for j in range(2):
            xj = x[:, j * LANES:(j + 1) * LANES]
            pair = [jnp.where(low, xj, zero), jnp.where(low, zero, xj)]
            cols.append(jnp.concatenate([zero] * (2 * j) + pair + [zero] * (2 - 2 * j), axis=0))
    return jnp.concatenate(cols, axis=1)


def _sigmoid(x):
    return 1.0 / (1.0 + jnp.exp(-x))


def _in_proj_kernel(tiles_per_seq, x_ref, g_ref, w_ref, cos_ref, sin_ref, mu_ref, p_ref, wb_ref, carry_ref):
    i = pl.program_id(0)

    @pl.when(i == 0)
    def _():
        wb_ref[...] = w_ref[...].astype(bf16)

    x = x_ref[...]
    u = (x * lax.rsqrt(jnp.mean(x * x, axis=-1, keepdims=True) + RMS_EPS) * g_ref[...]).astype(bf16)
    rows = x.shape[0]
    seq_start = (i % tiles_per_seq) == 0
    W = RWKV_WIDTH

    def rope(pg):
        cos = jnp.concatenate([cos_ref[...]] * 2, axis=1)
        sin = jnp.concatenate([sin_ref[...]] * 2, axis=1)
        half = (lax.broadcasted_iota(jnp.int32, pg.shape, 1) & (RET_DK - 1)) < RET_DK // 2
        lanes = pg.shape[1]
        swapped = jnp.where(half, pltpu.roll(pg, lanes - RET_DK // 2, 1), pltpu.roll(pg, RET_DK // 2, 1))
        return pg * cos + swapped * sin

    def silu(pg):
        return pg * _sigmoid(pg)

    def shifted(lo, hi):
        def fn(pg):
            first = jnp.where(seq_start, 0.0, carry_ref[:, lo:hi])
            row0 = lax.broadcasted_iota(jnp.int32, pg.shape, 0) == 0
            prev = jnp.where(row0, first, pltpu.roll(pg, 1, 0))
            carry_ref[:, lo:hi] = pg[rows - 1:rows, :]
            return pg + mu_ref[:, lo:hi] * (prev - pg)
        return fn

    groups = [(0, 2 * RET_QK, rope),
              (2 * RET_QK, 2 * RET_QK + RET_WIDTH, lambda pg: pg),
              (2 * RET_QK + RET_WIDTH, RET_COLS, silu),
              (RET_COLS, RET_COLS + W, shifted(0, W)),
              (RET_COLS + W, RET_COLS + 2 * W, shifted(W, 2 * W)),
              (RET_COLS + 2 * W, RET_COLS + 3 * W, shifted(2 * W, 3 * W)),
              (RET_COLS + 3 * W, RET_COLS + 4 * W, lambda pg: silu(shifted(3 * W, 4 * W)(pg))),
              (RET_COLS + 4 * W, IN_COLS, shifted(4 * W, RWKV_COLS))]
    pending = None
    for lo, hi, fn in groups:
        acc = jnp.dot(u, wb_ref[:, lo:hi], preferred_element_type=f32)
        if pending is not None:
            p_ref[:, pending[1]:pending[2]] = pending[3](pending[0])
        pending = (acc, lo, hi, fn)
    p_ref[:, pending[1]:pending[2]] = pending[3](pending[0])


def _head_blocks(s):
    s = s.astype(bf16)
    zero = jnp.zeros(s.shape[1:], bf16)
    return jnp.concatenate([jnp.concatenate([s[h] if i == h else zero for i in range(RET_HEADS)], axis=0)
                            for h in range(RET_HEADS)], axis=1)


def _staged(n, produce, consume):
    for i in range(n + RESULT_LAG):
        if i < n:
            produce(i)
        if i >= RESULT_LAG:
            consume(i - RESULT_LAG)


def _retention_chunks(qs, ks, vs, states, decay4, qdec, kdec, head_decay):
    n = range(len(qs))
    hs = range(RET_HEADS)
    scores = [_mm_nt(qs[i], _block_diag(ks[i])) * decay4 for i in n]
    out = [_mm(jnp.concatenate([scores[i], qs[i] * qdec], axis=1),
               jnp.concatenate([_block_diag(vs[i]), _head_blocks(states[i])], axis=0)) for i in n]
    kts = [(ks[i] * kdec).T for i in n]
    kv = [[_mm(kts[i][h * RET_DK:(h + 1) * RET_DK], vs[i][:, h * RET_DV:(h + 1) * RET_DV]) for h in hs]
          for i in n]
    s_new = [jnp.stack([states[i][h] * head_decay[h] + kv[i][h] for h in hs]) for i in n]
    return out, s_new


def _rwkv_chunks(rs, ks, vs, kkns, aas, lws, m_cs):
    C = CHUNK
    n = len(rs)
    heads = GROUP // RWKV_HEAD
    ti = lax.broadcasted_iota(jnp.int32, (C, C), 0)
    si = lax.broadcasted_iota(jnp.int32, (C, C), 1)
    tri_incl = jnp.where(si <= ti, 1.0, 0.0).astype(bf16)
    tri_incl = jnp.concatenate([tri_incl, tri_incl], axis=1)
    ti = lax.broadcasted_iota(jnp.int32, (C, GROUP), 0)
    si = lax.broadcasted_iota(jnp.int32, (C, GROUP), 1) & (C - 1)
    strict = si < ti
    incl = si <= ti
    eye = jnp.where(si == ti, 1.0, 0.0)
    low = lax.broadcasted_iota(jnp.int32, (RWKV_HEAD, LANES), 1) < RWKV_HEAD

    def slots():
        return [None] * n

    def staged(produce, consume):
        _staged(n, produce, consume)

    cum, lhs, rhs, decay_end, key_end = slots(), slots(), slots(), slots(), slots()

    def cum_matmul(i):
        cum[i] = jnp.dot(tri_incl, jnp.concatenate(_split2(lws[i]), axis=0), preferred_element_type=f32)

    def scale_operands(i):
        g_inc = jnp.exp(cum[i])
        g_inv = jnp.exp(-cum[i])
        beta_inv = kkns[i] * aas[i] * g_inv
        k_inv = ks[i] * g_inv
        lhs[i] = jnp.concatenate([-kkns[i] * jnp.exp(cum[i] - lws[i]), rs[i] * g_inc], axis=0).astype(bf16)
        rhs[i] = jnp.concatenate([_block_diag(beta_inv), _block_diag(k_inv)], axis=0)
        decay_end[i] = g_inc[C - 1:C, :]
        key_end[i] = jnp.concatenate([beta_inv * decay_end[i], k_inv * decay_end[i]], axis=0).astype(bf16)

    staged(cum_matmul, scale_operands)

    amat, mv, a_ab, a_k, a_rb, t_inv = slots(), slots(), slots(), slots(), slots(), slots()

    def score_matmuls(i):
        amat[i] = _mm_nt(lhs[i], rhs[i])
        mv[i] = _mm_nt(lhs[i], _block_diag(m_cs[i]))

    def mask_scores(i):
        lower = jnp.where(strict, amat[i][:C, :GROUP], 0.0)
        a_ab[i] = lower.astype(bf16)
        a_rb[i] = jnp.where(incl, amat[i][C:, :GROUP], 0.0).astype(bf16)
        a_k[i] = jnp.concatenate([jnp.where(strict, amat[i][:C, GROUP:], 0.0),
                                  jnp.where(incl, amat[i][C:, GROUP:], 0.0)], axis=0).astype(bf16)
        t_inv[i] = eye + lower

    staged(score_matmuls, mask_scores)

    av, st, st_lhs, pow_bd = slots(), slots(), slots(), slots()

    def first_square(i):
        av[i] = _mm(a_k[i], _block_diag(vs[i]))
        st[i] = _mm(a_ab[i], _block_diag(a_ab[i]))

    def after_first_square(i):
        s_b = st[i].astype(bf16)
        st_lhs[i] = jnp.concatenate([s_b, t_inv[i].astype(bf16)], axis=0)
        pow_bd[i] = _block_diag(s_b)

    staged(first_square, after_first_square)

    def square(i):
        st[i] = _mm(st_lhs[i], pow_bd[i])

    def after_square(i):
        s_b = st[i][:C].astype(bf16)
        t_inv[i] = t_inv[i] + st[i][C:]
        st_lhs[i] = jnp.concatenate([s_b, t_inv[i].astype(bf16)], axis=0)
        pow_bd[i] = _block_diag(s_b)

    for _ in range(4):
        staged(square, after_square)

    t_b, y_bd = slots(), slots()

    def last_product(i):
        st[i] = _mm(st_lhs[i][C:], pow_bd[i])

    def after_last_product(i):
        t_b[i] = (t_inv[i] + st[i]).astype(bf16)
        y_bd[i] = _block_diag(mv[i][:C] + av[i][:C])

    staged(last_product, after_last_product)

    u, u_bd, uvt = slots(), slots(), slots()

    def solve(i):
        u[i] = _mm(t_b[i], y_bd[i])

    def after_solve(i):
        u_bd[i] = _block_diag(u[i])
        uvt[i] = jnp.concatenate([u[i], vs[i]], axis=0).T.astype(bf16)

    staged(solve, after_solve)

    o_u, upd, out, m_new = slots(), slots(), slots(), slots()

    def output_and_update(i):
        o_u[i] = _mm(a_rb[i], u_bd[i])
        upd[i] = [_mm(uvt[i][h * RWKV_HEAD:(h + 1) * RWKV_HEAD],
                      key_end[i][:, (h // 2) * LANES:(h // 2 + 1) * LANES]) for h in range(heads)]

    def finish(i):
        out[i] = mv[i][C:] + av[i][C:] + o_u[i]
        m_new[i] = m_cs[i] * decay_end[i] + jnp.concatenate(
            [jnp.where(low, upd[i][2 * j], upd[i][2 * j + 1]) for j in range(heads // 2)], axis=1)

    staged(output_and_update, finish)
    return out, m_new


def _mix_chunk(p_ref, dec_ref, lora_ref, vec_ref, rgn_ref, seg_ref, y_ref, m_ref, s_ref):
    C = CHUNK
    nb = BATCH_BLOCK
    R = nb * C
    W = RWKV_WIDTH
    NG = W // GROUP

    def rows(x, bi):
        return x[bi * C:(bi + 1) * C]

    def rwkv_cols(lo, hi):
        return p_ref[:, :, RET_COLS + lo:RET_COLS + hi].reshape(R, hi - lo)

    r = rwkv_cols(0, W)
    kr = rwkv_cols(W, 2 * W)
    vr = rwkv_cols(2 * W, 3 * W)
    gate_rw = rwkv_cols(3 * W, 4 * W)
    xwa = rwkv_cols(4 * W, RWKV_COLS)
    lane = lax.broadcasted_iota(jnp.int32, xwa.shape, 1)
    lora = _mm(jnp.where(lane < LORA, jnp.tanh(xwa), xwa), lora_ref[...])
    w0, a0, k_k, k_a, r_k, gn_g, gn_b = (vec_ref[i:i + 1, :] for i in range(7))
    lw = -np.float32(np.exp(-0.5)) * _sigmoid(w0 + lora[:, :W])
    a = _sigmoid(a0 + lora[:, W:])
    seg = seg_ref[...]

    def segsum(x):
        xs = jnp.concatenate([x[:, g * GROUP:(g + 1) * GROUP] for g in range(NG)], axis=0)
        tot = jnp.dot(xs.astype(bf16), seg, preferred_element_type=f32)
        return jnp.concatenate([tot[g * R:(g + 1) * R] for g in range(NG)], axis=1)

    kk = kr * k_k
    kkn = kk * lax.rsqrt(jnp.maximum(segsum(kk * kk), 1e-24))
    kmod = kr * (1.0 + (a - 1.0) * k_a)

    q = p_ref[:, :, 0:RET_QK].reshape(R, RET_QK)
    k = p_ref[:, :, RET_QK:2 * RET_QK].reshape(R, RET_QK)
    v = p_ref[:, :, 2 * RET_QK:2 * RET_QK + RET_WIDTH].reshape(R, RET_WIDTH)
    rets, s_new = _retention_chunks([rows(q, bi) for bi in range(nb)], [rows(k, bi) for bi in range(nb)],
                                    [rows(v, bi) for bi in range(nb)], [s_ref[bi] for bi in range(nb)],
                                    dec_ref[0], dec_ref[1], dec_ref[2], _RET_CHUNK_DECAY)
    for bi in range(nb):
        s_ref[bi] = s_new[bi]
    ret = jnp.concatenate(rets, axis=0)
    rgn = rgn_ref[...]
    for h in range(RET_HEADS):
        sl = slice(h * RET_DV, (h + 1) * RET_DV)
        xh = ret[:, sl]
        d = xh - jnp.mean(xh, axis=-1, keepdims=True)
        var = jnp.mean(d * d, axis=-1, keepdims=True)
        gh = p_ref[:, :, 2 * RET_QK + RET_WIDTH + h * RET_DV:2 * RET_QK + RET_WIDTH + (h + 1) * RET_DV]
        gh = gh.reshape(R, RET_DV)
        yh = gh * (d * lax.rsqrt(var + RET_GN_EPS) * rgn[:, sl])
        y_ref[:, :, sl] = yh.reshape(nb, C, RET_DV).astype(y_ref.dtype)

    chains = [(bi, g) for bi in range(nb) for g in range(NG)]

    def pick(x):
        return [x[bi * C:(bi + 1) * C, g * GROUP:(g + 1) * GROUP] for bi, g in chains]

    outs, m_new = _rwkv_chunks(pick(r), pick(kmod), pick(vr), pick(kkn), pick(a), pick(lw),
                               [m_ref[bi, g] for bi, g in chains])
    for i, (bi, g) in enumerate(chains):
        m_ref[bi, g] = m_new[i]
    o = jnp.concatenate([jnp.concatenate(outs[bi * NG:(bi + 1) * NG], axis=1) for bi in range(nb)], axis=0)
    d = o - segsum(o) * (1.0 / RWKV_HEAD)
    var = segsum(d * d) * (1.0 / RWKV_HEAD)
    o = d * lax.rsqrt(var + RWKV_GN_EPS) * gn_g + gn_b
    bonus = segsum(r * kmod * r_k) * vr
    y_rw = gate_rw * (o + bonus)
    y_ref[:, :, RET_WIDTH:] = y_rw.reshape(nb, C, W).astype(y_ref.dtype)


def _project_chunk(x_ref, y_ref, w_ref, g_ref, o_ref):
    nb, C, D = x_ref.shape
    y = y_ref[...].reshape(nb * C, D)
    h = x_ref[...].reshape(nb * C, D) + jnp.dot(y, w_ref[...], preferred_element_type=f32)
    out = h * lax.rsqrt(jnp.mean(h * h, axis=-1, keepdims=True) + RMS_EPS) * g_ref[...]
    o_ref[...] = out.reshape(nb, C, D)


def _mixer_kernel(p_ref, dec_ref, lora_ref, vec_ref, rgn_ref,
                  seg_ref, x_ref, wout_ref, fg_ref, o_ref, y_ref, m_ref, s_ref, woutb_ref):
    @pl.when(pl.program_id(1) == 0)
    def _():
        m_ref[...] = jnp.zeros_like(m_ref)
        s_ref[...] = jnp.zeros_like(s_ref)
        woutb_ref[...] = wout_ref[...].astype(bf16)

    _mix_chunk(p_ref, dec_ref, lora_ref, vec_ref, rgn_ref, seg_ref, y_ref, m_ref, s_ref)
    _project_chunk(x_ref, y_ref, woutb_ref, fg_ref, o_ref)


def _rope_tables(seq):
    half = RET_DK // 2
    expo = -jnp.arange(half, dtype=f32) / f32(half)
    freqs = jnp.exp(expo * f32(np.log(ROPE_BASE)))
    ang = jnp.arange(seq, dtype=jnp.int32).astype(f32)[:, None] * freqs[None, :]
    cos = jnp.cos(ang)
    sin = jnp.sin(ang)
    cos_full = jnp.tile(jnp.concatenate([cos, cos], axis=1), (1, RET_HEADS))
    sin_signed = jnp.tile(jnp.concatenate([-sin, sin], axis=1), (1, RET_HEADS))
    return cos_full, sin_signed


_RET_LOG_GAMMA = np.log(1.0 - np.exp2(-5.0 - np.arange(RET_HEADS, dtype=np.float64)))
_RET_CHUNK_DECAY = tuple(np.float32(v) for v in np.exp(_RET_LOG_GAMMA * CHUNK))


def _retention_constants():
    C = CHUNK
    lg = _RET_LOG_GAMMA
    lane_lg = np.repeat(lg, RET_DK)[None, :]
    n = np.arange(C, dtype=np.float64)[:, None]
    m = np.tile(np.arange(C, dtype=np.float64), RET_HEADS)[None, :]
    scale = RET_DK ** -0.5
    decay4 = scale * np.exp(lane_lg * np.abs(n - m))
    qdec = np.exp(lane_lg * (n + 1.0)) * np.ones((1, RET_QK))
    kdec = scale * np.exp(lane_lg * (C - 1.0 - n)) * np.ones((1, RET_QK))
    return jnp.asarray(np.stack([decay4, qdec, kdec]).astype(np.float32))


def _const_spec(shape, single_buffer=False):
    mode = pl.Buffered(1) if single_buffer else None
    return pl.BlockSpec(shape, lambda *_: (0,) * len(shape), pipeline_mode=mode)


@jax.jit
def kernel(x, norm_g, w_in, ret_gn_g, rwkv_mu, w_lora_up, w0, a_lora_up, a0, k_k, k_a, r_k,
           rwkv_gn_g, rwkv_gn_b, w_out, final_norm_g):
    B, T, D = x.shape
    assert D == D_MODEL and T % CHUNK == 0 and T % PROJ_TILE == 0 and B % BATCH_BLOCK == 0
    assert norm_g.shape[0] == 1, "single-layer block"
    n_tok = B * T
    xf = x.reshape(n_tok, D)
    params = pltpu.CompilerParams(dimension_semantics=("arbitrary",), vmem_limit_bytes=VMEM_LIMIT)
    tiles_per_seq = T // PROJ_TILE
    cos, sin = _rope_tables(T)

    p = pl.pallas_call(
        functools.partial(_in_proj_kernel, tiles_per_seq),
        grid=(n_tok // PROJ_TILE,),
        in_specs=[pl.BlockSpec((PROJ_TILE, D), lambda i: (i, 0)),
                  _const_spec((1, D)),
                  _const_spec((D, IN_COLS), single_buffer=True),
                  pl.BlockSpec((PROJ_TILE, RET_QK), lambda i: (i % tiles_per_seq, 0)),
                  pl.BlockSpec((PROJ_TILE, RET_QK), lambda i: (i % tiles_per_seq, 0)),
                  _const_spec((1, RWKV_COLS))],
        out_specs=pl.BlockSpec((PROJ_TILE, IN_COLS), lambda i: (i, 0)),
        out_shape=jax.ShapeDtypeStruct((n_tok, IN_COLS), f32),
        scratch_shapes=[pltpu.VMEM((D, IN_COLS), bf16),
                        pltpu.VMEM((1, RWKV_COLS), f32)],
        compiler_params=params,
        name="in_proj",
    )(xf, norm_g[0][None, :], w_in[0], cos, sin, rwkv_mu[0][None, :])

    dec = _retention_constants()
    zeros = jnp.zeros((LORA, RWKV_WIDTH), f32)
    lora_w = jnp.concatenate(
        [jnp.concatenate([w_lora_up[0], zeros], axis=1),
         jnp.concatenate([zeros, a_lora_up[0]], axis=1)], axis=0).astype(bf16)
    vecs = jnp.stack([w0[0], a0[0], k_k[0], k_a[0], r_k[0].reshape(-1), rwkv_gn_g[0], rwkv_gn_b[0],
                      jnp.zeros((RWKV_WIDTH,), f32)])
    seg_ids = np.arange(GROUP) // RWKV_HEAD
    seg = jnp.asarray(seg_ids[:, None] == seg_ids[None, :], bf16)

    return pl.pallas_call(
        _mixer_kernel,
        grid=(B // BATCH_BLOCK, T // CHUNK),
        in_specs=[pl.BlockSpec((BATCH_BLOCK, CHUNK, IN_COLS), lambda b, c: (b, c, 0)),
                  _const_spec((3, CHUNK, RET_QK)),
                  _const_spec((2 * LORA, 2 * RWKV_WIDTH)),
                  _const_spec((8, RWKV_WIDTH)),
                  _const_spec((1, RET_WIDTH)),
                  _const_spec((GROUP, GROUP)),
                  pl.BlockSpec((BATCH_BLOCK, CHUNK, D), lambda b, c: (b, c, 0)),
                  _const_spec((D, D), single_buffer=True),
                  _const_spec((1, D))],
        out_specs=pl.BlockSpec((BATCH_BLOCK, CHUNK, D), lambda b, c: (b, c, 0)),
        out_shape=jax.ShapeDtypeStruct((B, T, D), f32),
        scratch_shapes=[pltpu.VMEM((BATCH_BLOCK, CHUNK, D), bf16),
                        pltpu.VMEM((BATCH_BLOCK, RWKV_WIDTH // GROUP, RWKV_HEAD, GROUP), f32),
                        pltpu.VMEM((BATCH_BLOCK, RET_HEADS, RET_DK, RET_DV), f32),
                        pltpu.VMEM((D, D), bf16)],
        compiler_params=pltpu.CompilerParams(dimension_semantics=("arbitrary", "arbitrary"),
                                             vmem_limit_bytes=VMEM_LIMIT),
        name="mixers",
    )(p.reshape(B, T, IN_COLS), dec, lora_w, vecs,
      ret_gn_g[0][None, :], seg, x, w_out[0], final_norm_g[None, :])
```

```python
import functools

import numpy as np
import jax
import jax.numpy as jnp
from jax import lax
from jax.experimental import pallas as pl
from jax.experimental.pallas import tpu as pltpu

D_MODEL = 1024
CHUNK = 64
RET_HEADS = 4
RET_DK = 64
RET_DV = 128
RET_QK = RET_HEADS * RET_DK
RET_WIDTH = RET_HEADS * RET_DV
RWKV_WIDTH = 512
RWKV_HEAD = 64
LORA = 64
RET_COLS = 2 * RET_QK + 2 * RET_WIDTH
RWKV_COLS = 4 * RWKV_WIDTH + 2 * LORA
IN_COLS = RET_COLS + RWKV_COLS
ROPE_BASE = 10000.0
RMS_EPS = 1e-6
RET_GN_EPS = 1e-5
RWKV_GN_EPS = 64e-5
LANES = 128
GROUP = 256
PROJ_TILE = 512
BATCH_BLOCK = 8
RESULT_LAG = 4
VMEM_LIMIT = 56 * 1024 * 1024

f32 = jnp.float32
bf16 = jnp.bfloat16


def _mm(a, b):
    return jnp.dot(a.astype(bf16), b.astype(bf16), preferred_element_type=f32)


def _mm_nt(a, b):
    return lax.dot_general(a.astype(bf16), b.astype(bf16), (((1,), (1,)), ((), ())),
                           preferred_element_type=f32)


def _split2(x):
    hi = x.astype(bf16)
    lo = (x - hi.astype(f32)).astype(bf16)
    return hi, lo


def _block_diag(x):
    x = x.astype(bf16)
    rows, lanes = x.shape
    assert lanes in (2 * LANES, 4 * LANES)
    zero = jnp.zeros((rows, LANES), bf16)
    cols = []
    if lanes == 4 * LANES:
        for j in range(4):
            cols.append(jnp.concatenate([x[:, j * LANES:(j + 1) * LANES] if i == j else zero for i in range(4)],
                                        axis=0))
    else:
        low = lax.broadcasted_iota(jnp.int32, (rows, LANES), 1) < LANES // 2
        for j in range(2):
            xj = x[:, j * LANES:(j + 1) * LANES]
            pair = [jnp.where(low, xj, zero), jnp.where(low, zero, xj)]
            cols.append(jnp.concatenate([zero] * (2 * j) + pair + [zero] * (2 - 2 * j), axis=0))
    return jnp.concatenate(cols, axis=1)


def _sigmoid(x):
    return 1.0 / (1.0 + jnp.exp(-x))


def _in_proj_kernel(tiles_per_seq, x_ref, g_ref, w_ref, cos_ref, sin_ref, mu_ref, p_ref, wb_ref, carry_ref):
    i = pl.program_id(0)

    @pl.when(i == 0)
    def _():
        wb_ref[...] = w_ref[...].astype(bf16)

    x = x_ref[...]
    u = (x * lax.rsqrt(jnp.mean(x * x, axis=-1, keepdims=True) + RMS_EPS) * g_ref[...]).astype(bf16)
    rows = x.shape[0]
    seq_start = (i % tiles_per_seq) == 0
    W = RWKV_WIDTH

    def rope(pg):
        cos = jnp.concatenate([cos_ref[...]] * 2, axis=1)
        sin = jnp.concatenate([sin_ref[...]] * 2, axis=1)
        half = (lax.broadcasted_iota(jnp.int32, pg.shape, 1) & (RET_DK - 1)) < RET_DK // 2
        lanes = pg.shape[1]
        swapped = jnp.where(half, pltpu.roll(pg, lanes - RET_DK // 2, 1), pltpu.roll(pg, RET_DK // 2, 1))
        return pg * cos + swapped * sin

    def silu(pg):
        return pg * _sigmoid(pg)

    def shifted(lo, hi):
        def fn(pg):
            first = jnp.where(seq_start, 0.0, carry_ref[:, lo:hi])
            row0 = lax.broadcasted_iota(jnp.int32, pg.shape, 0) == 0
            prev = jnp.where(row0, first, pltpu.roll(pg, 1, 0))
            carry_ref[:, lo:hi] = pg[rows - 1:rows, :]
            return pg + mu_ref[:, lo:hi] * (prev - pg)
        return fn

    groups = [(0, 2 * RET_QK, rope),
              (2 * RET_QK, 2 * RET_QK + RET_WIDTH, lambda pg: pg),
              (2 * RET_QK + RET_WIDTH, RET_COLS, silu),
              (RET_COLS, RET_COLS + W, shifted(0, W)),
              (RET_COLS + W, RET_COLS + 2 * W, shifted(W, 2 * W)),
              (RET_COLS + 2 * W, RET_COLS + 3 * W, shifted(2 * W, 3 * W)),
              (RET_COLS + 3 * W, RET_COLS + 4 * W, lambda pg: silu(shifted(3 * W, 4 * W)(pg))),
              (RET_COLS + 4 * W, IN_COLS, shifted(4 * W, RWKV_COLS))]
    pending = None
    for lo, hi, fn in groups:
        acc = jnp.dot(u, wb_ref[:, lo:hi], preferred_element_type=f32)
        if pending is not None:
            p_ref[:, pending[1]:pending[2]] = pending[3](pending[0])
        pending = (acc, lo, hi, fn)
    p_ref[:, pending[1]:pending[2]] = pending[3](pending[0])


def _head_blocks(s):
    s = s.astype(bf16)
    zero = jnp.zeros(s.shape[1:], bf16)
    return jnp.concatenate([jnp.concatenate([s[h] if i == h else zero for i in range(RET_HEADS)], axis=0)
                            for h in range(RET_HEADS)], axis=1)


def _staged(n, produce, consume):
    for i in range(n + RESULT_LAG):
        if i < n:
            produce(i)
        if i >= RESULT_LAG:
            consume(i - RESULT_LAG)


def _retention_chunks(qs, ks, vs, states, decay4, qdec, kdec, head_decay):
    n = range(len(qs))
    hs = range(RET_HEADS)
    scores = [_mm_nt(qs[i], _block_diag(ks[i])) * decay4 for i in n]
    out = [_mm(jnp.concatenate([scores[i], qs[i] * qdec], axis=1),
               jnp.concatenate([_block_diag(vs[i]), _head_blocks(states[i])], axis=0)) for i in n]
    kts = [(ks[i] * kdec).T for i in n]
    kv = [[_mm(kts[i][h * RET_DK:(h + 1) * RET_DK], vs[i][:, h * RET_DV:(h + 1) * RET_DV]) for h in hs]
          for i in n]
    s_new = [jnp.stack([states[i][h] * head_decay[h] + kv[i][h] for h in hs]) for i in n]
    return out, s_new


def _rwkv_chunks(rs, ks, vs, kkns, aas, lws, m_cs):
    C = CHUNK
    n = len(rs)
    heads = GROUP // RWKV_HEAD
    ti = lax.broadcasted_iota(jnp.int32, (C, C), 0)
    si = lax.broadcasted_iota(jnp.int32, (C, C), 1)
    tri_incl = jnp.where(si <= ti, 1.0, 0.0).astype(bf16)
    tri_incl = jnp.concatenate([tri_incl, tri_incl], axis=1)
    ti = lax.broadcasted_iota(jnp.int32, (C, GROUP), 0)
    si = lax.broadcasted_iota(jnp.int32, (C, GROUP), 1) & (C - 1)
    strict = si < ti
    incl = si <= ti
    eye = jnp.where(si == ti, 1.0, 0.0)
    low = lax.broadcasted_iota(jnp.int32, (RWKV_HEAD, LANES), 1) < RWKV_HEAD

    def slots():
        return [None] * n

    def staged(produce, consume):
        _staged(n, produce, consume)

    cum, lhs, rhs, decay_end, key_end = slots(), slots(), slots(), slots(), slots()

    def cum_matmul(i):
        cum[i] = jnp.dot(tri_incl, jnp.concatenate(_split2(lws[i]), axis=0), preferred_element_type=f32)

    def scale_operands(i):
        g_inc = jnp.exp(cum[i])
        g_inv = jnp.exp(-cum[i])
        g_end = jnp.exp(cum[i][C - 1:C, :] - cum[i])
        beta = kkns[i] * aas[i]
        lhs[i] = jnp.concatenate([-kkns[i] * jnp.exp(cum[i] - lws[i]), rs[i] * g_inc], axis=0).astype(bf16)
        rhs[i] = jnp.concatenate([_block_diag(beta * g_inv), _block_diag(ks[i] * g_inv)], axis=0)
        decay_end[i] = g_inc[C - 1:C, :]
        key_end[i] = jnp.concatenate([beta * g_end, ks[i] * g_end], axis=0).astype(bf16)

    staged(cum_matmul, scale_operands)

    amat, mv, a_ab, a_k, a_rb, t_inv = slots(), slots(), slots(), slots(), slots(), slots()

    def score_matmuls(i):
        amat[i] = _mm_nt(lhs[i], rhs[i])
        mv[i] = _mm_nt(lhs[i], _block_diag(m_cs[i]))

    def mask_scores(i):
        lower = jnp.where(strict, amat[i][:C, :GROUP], 0.0)
        a_ab[i] = lower.astype(bf16)
        a_rb[i] = jnp.where(incl, amat[i][C:, :GROUP], 0.0).astype(bf16)
        a_k[i] = jnp.concatenate([jnp.where(strict, amat[i][:C, GROUP:], 0.0),
                                  jnp.where(incl, amat[i][C:, GROUP:], 0.0)], axis=0).astype(bf16)
        t_inv[i] = eye + lower

    staged(score_matmuls, mask_scores)

    av, st, st_lhs, pow_bd = slots(), slots(), slots(), slots()

    def first_square(i):
        av[i] = _mm(a_k[i], _block_diag(vs[i]))
        st[i] = _mm(a_ab[i], _block_diag(a_ab[i]))

    def after_first_square(i):
        s_b = st[i].astype(bf16)
        st_lhs[i] = jnp.concatenate([s_b, t_inv[i].astype(bf16)], axis=0)
        pow_bd[i] = _block_diag(s_b)

    staged(first_square, after_first_square)

    def square(i):
        st[i] = _mm(st_lhs[i], pow_bd[i])

    def after_square(i):
        s_b = st[i][:C].astype(bf16)
        t_inv[i] = t_inv[i] + st[i][C:]
        st_lhs[i] = jnp.concatenate([s_b, t_inv[i].astype(bf16)], axis=0)
        pow_bd[i] = _block_diag(s_b)

    for _ in range(4):
        staged(square, after_square)

    t_b, y_bd = slots(), slots()

    def last_product(i):
        st[i] = _mm(st_lhs[i][C:], pow_bd[i])

    def after_last_product(i):
        t_b[i] = (t_inv[i] + st[i]).astype(bf16)
        y_bd[i] = _block_diag(mv[i][:C] + av[i][:C])

    staged(last_product, after_last_product)

    u, u_bd, uvt = slots(), slots(), slots()

    def solve(i):
        u[i] = _mm(t_b[i], y_bd[i])

    def after_solve(i):
        u_bd[i] = _block_diag(u[i])
        uvt[i] = jnp.concatenate([u[i], vs[i]], axis=0).T.astype(bf16)

    staged(solve, after_solve)

    o_u, upd, out, m_new = slots(), slots(), slots(), slots()

    def output_and_update(i):
        o_u[i] = _mm(a_rb[i], u_bd[i])
        upd[i] = [_mm(uvt[i][h * RWKV_HEAD:(h + 1) * RWKV_HEAD],
                      key_end[i][:, (h // 2) * LANES:(h // 2 + 1) * LANES]) for h in range(heads)]

    def finish(i):
        out[i] = mv[i][C:] + av[i][C:] + o_u[i]
        m_new[i] = m_cs[i] * decay_end[i] + jnp.concatenate(
            [jnp.where(low, upd[i][2 * j], upd[i][2 * j + 1]) for j in range(heads // 2)], axis=1)

    staged(output_and_update, finish)
    return out, m_new


def _mix_chunk(p_ref, dec_ref, lora_ref, vec_ref, rgn_ref, seg_ref, y_ref, m_ref, s_ref):
    C = CHUNK
    nb = BATCH_BLOCK
    R = nb * C
    W = RWKV_WIDTH
    NG = W // GROUP

    def rows(x, bi):
        return x[bi * C:(bi + 1) * C]

    def rwkv_cols(lo, hi):
        return p_ref[:, :, RET_COLS + lo:RET_COLS + hi].reshape(R, hi - lo)

    r = rwkv_cols(0, W)
    kr = rwkv_cols(W, 2 * W)
    vr = rwkv_cols(2 * W, 3 * W)
    gate_rw = rwkv_cols(3 * W, 4 * W)
    xwa = rwkv_cols(4 * W, RWKV_COLS)
    lane = lax.broadcasted_iota(jnp.int32, xwa.shape, 1)
    lora = _mm(jnp.where(lane < LORA, jnp.tanh(xwa), xwa), lora_ref[...])
    w0, a0, k_k, k_a, r_k, gn_g, gn_b = (vec_ref[i:i + 1, :] for i in range(7))
    lw = -np.float32(np.exp(-0.5)) * _sigmoid(w0 + lora[:, :W])
    a = _sigmoid(a0 + lora[:, W:])
    seg = seg_ref[...]

    def segsum(x):
        xs = jnp.concatenate([x[:, g * GROUP:(g + 1) * GROUP] for g in range(NG)], axis=0)
        tot = jnp.dot(xs.astype(bf16), seg, preferred_element_type=f32)
        return jnp.concatenate([tot[g * R:(g + 1) * R] for g in range(NG)], axis=1)

    kk = kr * k_k
    kkn = kk * lax.rsqrt(jnp.maximum(segsum(kk * kk), 1e-24))
    kmod = kr * (1.0 + (a - 1.0) * k_a)

    q = p_ref[:, :, 0:RET_QK].reshape(R, RET_QK)
    k = p_ref[:, :, RET_QK:2 * RET_QK].reshape(R, RET_QK)
    v = p_ref[:, :, 2 * RET_QK:2 * RET_QK + RET_WIDTH].reshape(R, RET_WIDTH)
    rets, s_new = _retention_chunks([rows(q, bi) for bi in range(nb)], [rows(k, bi) for bi in range(nb)],
                                    [rows(v, bi) for bi in range(nb)], [s_ref[bi] for bi in range(nb)],
                                    dec_ref[0], dec_ref[1], dec_ref[2], _RET_CHUNK_DECAY)
    for bi in range(nb):
        s_ref[bi] = s_new[bi]
    ret = jnp.concatenate(rets, axis=0)
    rgn = rgn_ref[...]
    for h in range(RET_HEADS):
        sl = slice(h * RET_DV, (h + 1) * RET_DV)
        xh = ret[:, sl]
        d = xh - jnp.mean(xh, axis=-1, keepdims=True)
        var = jnp.mean(d * d, axis=-1, keepdims=True)
        gh = p_ref[:, :, 2 * RET_QK + RET_WIDTH + h * RET_DV:2 * RET_QK + RET_WIDTH + (h + 1) * RET_DV]
        gh = gh.reshape(R, RET_DV)
        yh = gh * (d * lax.rsqrt(var + RET_GN_EPS) * rgn[:, sl])
        y_ref[:, :, sl] = yh.reshape(nb, C, RET_DV).astype(y_ref.dtype)

    chains = [(bi, g) for bi in range(nb) for g in range(NG)]

    def pick(x):
        return [x[bi * C:(bi + 1) * C, g * GROUP:(g + 1) * GROUP] for bi, g in chains]

    outs, m_new = _rwkv_chunks(pick(r), pick(kmod), pick(vr), pick(kkn), pick(a), pick(lw),
                               [m_ref[bi, g] for bi, g in chains])
    for i, (bi, g) in enumerate(chains):
        m_ref[bi, g] = m_new[i]
    o = jnp.concatenate([jnp.concatenate(outs[bi * NG:(bi + 1) * NG], axis=1) for bi in range(nb)], axis=0)
    d = o - segsum(o) * (1.0 / RWKV_HEAD)
    var = segsum(d * d) * (1.0 / RWKV_HEAD)
    o = d * lax.rsqrt(var + RWKV_GN_EPS) * gn_g + gn_b
    bonus = segsum(r * kmod * r_k) * vr
    y_rw = gate_rw * (o + bonus)
    y_ref[:, :, RET_WIDTH:] = y_rw.reshape(nb, C, W).astype(y_ref.dtype)


def _project_chunk(x_ref, y_ref, w_ref, g_ref, o_ref):
    nb, C, D = x_ref.shape
    y = y_ref[...].reshape(nb * C, D)
    h = x_ref[...].reshape(nb * C, D) + jnp.dot(y, w_ref[...], preferred_element_type=f32)
    out = h * lax.rsqrt(jnp.mean(h * h, axis=-1, keepdims=True) + RMS_EPS) * g_ref[...]
    o_ref[...] = out.reshape(nb, C, D)


def _mixer_kernel(p_ref, dec_ref, lora_ref, vec_ref, rgn_ref,
                  seg_ref, x_ref, wout_ref, fg_ref, o_ref, y_ref, m_ref, s_ref, woutb_ref):
    @pl.when(pl.program_id(1) == 0)
    def _():
        m_ref[...] = jnp.zeros_like(m_ref)
        s_ref[...] = jnp.zeros_like(s_ref)
        woutb_ref[...] = wout_ref[...].astype(bf16)

    _mix_chunk(p_ref, dec_ref, lora_ref, vec_ref, rgn_ref, seg_ref, y_ref, m_ref, s_ref)
    _project_chunk(x_ref, y_ref, woutb_ref, fg_ref, o_ref)


def _rope_tables(seq):
    half = RET_DK // 2
    expo = -jnp.arange(half, dtype=f32) / f32(half)
    freqs = jnp.exp(expo * f32(np.log(ROPE_BASE)))
    ang = jnp.arange(seq, dtype=jnp.int32).astype(f32)[:, None] * freqs[None, :]
    cos = jnp.cos(ang)
    sin = jnp.sin(ang)
    cos_full = jnp.tile(jnp.concatenate([cos, cos], axis=1), (1, RET_HEADS))
    sin_signed = jnp.tile(jnp.concatenate([-sin, sin], axis=1), (1, RET_HEADS))
    return cos_full, sin_signed


_RET_LOG_GAMMA = np.log(1.0 - np.exp2(-5.0 - np.arange(RET_HEADS, dtype=np.float64)))
_RET_CHUNK_DECAY = tuple(np.float32(v) for v in np.exp(_RET_LOG_GAMMA * CHUNK))


def _retention_constants():
    C = CHUNK
    lg = _RET_LOG_GAMMA
    lane_lg = np.repeat(lg, RET_DK)[None, :]
    n = np.arange(C, dtype=np.float64)[:, None]
    m = np.tile(np.arange(C, dtype=np.float64), RET_HEADS)[None, :]
    scale = RET_DK ** -0.5
    decay4 = scale * np.exp(lane_lg * np.abs(n - m))
    qdec = np.exp(lane_lg * (n + 1.0)) * np.ones((1, RET_QK))
    kdec = scale * np.exp(lane_lg * (C - 1.0 - n)) * np.ones((1, RET_QK))
    return jnp.asarray(np.stack([decay4, qdec, kdec]).astype(np.float32))


def _const_spec(shape, single_buffer=False):
    mode = pl.Buffered(1) if single_buffer else None
    return pl.BlockSpec(shape, lambda *_: (0,) * len(shape), pipeline_mode=mode)


@jax.jit
def kernel(x, norm_g, w_in, ret_gn_g, rwkv_mu, w_lora_up, w0, a_lora_up, a0, k_k, k_a, r_k,
           rwkv_gn_g, rwkv_gn_b, w_out, final_norm_g):
    B, T, D = x.shape
    assert D == D_MODEL and T % CHUNK == 0 and T % PROJ_TILE == 0 and B % BATCH_BLOCK == 0
    assert norm_g.shape[0] == 1, "single-layer block"
    n_tok = B * T
    xf = x.reshape(n_tok, D)
    params = pltpu.CompilerParams(dimension_semantics=("arbitrary",), vmem_limit_bytes=VMEM_LIMIT)
    tiles_per_seq = T // PROJ_TILE
    cos, sin = _rope_tables(T)

    p = pl.pallas_call(
        functools.partial(_in_proj_kernel, tiles_per_seq),
        grid=(n_tok // PROJ_TILE,),
        in_specs=[pl.BlockSpec((PROJ_TILE, D), lambda i: (i, 0)),
                  _const_spec((1, D)),
                  _const_spec((D, IN_COLS), single_buffer=True),
                  pl.BlockSpec((PROJ_TILE, RET_QK), lambda i: (i % tiles_per_seq, 0)),
                  pl.BlockSpec((PROJ_TILE, RET_QK), lambda i: (i % tiles_per_seq, 0)),
                  _const_spec((1, RWKV_COLS))],
        out_specs=pl.BlockSpec((PROJ_TILE, IN_COLS), lambda i: (i, 0)),
        out_shape=jax.ShapeDtypeStruct((n_tok, IN_COLS), f32),
        scratch_shapes=[pltpu.VMEM((D, IN_COLS), bf16),
                        pltpu.VMEM((1, RWKV_COLS), f32)],
        compiler_params=params,
        name="in_proj",
    )(xf, norm_g[0][None, :], w_in[0], cos, sin, rwkv_mu[0][None, :])

    dec = _retention_constants()
    zeros = jnp.zeros((LORA, RWKV_WIDTH), f32)
    lora_w = jnp.concatenate(
        [jnp.concatenate([w_lora_up[0], zeros], axis=1),
         jnp.concatenate([zeros, a_lora_up[0]], axis=1)], axis=0).astype(bf16)
    vecs = jnp.stack([w0[0], a0[0], k_k[0], k_a[0], r_k[0].reshape(-1), rwkv_gn_g[0], rwkv_gn_b[0],
                      jnp.zeros((RWKV_WIDTH,), f32)])
    seg_ids = np.arange(GROUP) // RWKV_HEAD
    seg = jnp.asarray(seg_ids[:, None] == seg_ids[None, :], bf16)

    return pl.pallas_call(
        _mixer_kernel,
        grid=(B // BATCH_BLOCK, T // CHUNK),
        in_specs=[pl.BlockSpec((BATCH_BLOCK, CHUNK, IN_COLS), lambda b, c: (b, c, 0)),
                  _const_spec((3, CHUNK, RET_QK)),
                  _const_spec((2 * LORA, 2 * RWKV_WIDTH)),
                  _const_spec((8, RWKV_WIDTH)),
                  _const_spec((1, RET_WIDTH)),
                  _const_spec((GROUP, GROUP)),
                  pl.BlockSpec((BATCH_BLOCK, CHUNK, D), lambda b, c: (b, c, 0)),
                  _const_spec((D, D), single_buffer=True),
                  _const_spec((1, D))],
        out_specs=pl.BlockSpec((BATCH_BLOCK, CHUNK, D), lambda b, c: (b, c, 0)),
        out_shape=jax.ShapeDtypeStruct((B, T, D), f32),
        scratch_shapes=[pltpu.VMEM((BATCH_BLOCK, CHUNK, D), bf16),
                        pltpu.VMEM((BATCH_BLOCK, RWKV_WIDTH // GROUP, RWKV_HEAD, GROUP), f32),
                        pltpu.VMEM((BATCH_BLOCK, RET_HEADS, RET_DK, RET_DV), f32),
                        pltpu.VMEM((D, D), bf16)],
        compiler_params=pltpu.CompilerParams(dimension_semantics=("arbitrary", "arbitrary"),
                                             vmem_limit_bytes=VMEM_LIMIT),
        name="mixers",
    )(p.reshape(B, T, IN_COLS), dec, lora_w, vecs,
      ret_gn_g[0][None, :], seg, x, w_out[0], final_norm_g[None, :])
```

```python
import functools

import numpy as np
import jax
import jax.numpy as jnp
from jax import lax
from jax.experimental import pallas as pl
from jax.experimental.pallas import tpu as pltpu

D_MODEL = 1024
CHUNK = 64
RET_HEADS = 4
RET_DK = 64
RET_DV = 128
RET_QK = RET_HEADS * RET_DK
RET_WIDTH = RET_HEADS * RET_DV
RWKV_WIDTH = 512
RWKV_HEAD = 64
LORA = 64
RET_COLS = 2 * RET_QK + 2 * RET_WIDTH
RWKV_COLS = 4 * RWKV_WIDTH + 2 * LORA
IN_COLS = RET_COLS + RWKV_COLS
ROPE_BASE = 10000.0
RMS_EPS = 1e-6
RET_GN_EPS = 1e-5
RWKV_GN_EPS = 64e-5
LANES = 128
GROUP = 256
PROJ_TILE = 512
BATCH_BLOCK = 8
RESULT_LAG = 4
VMEM_LIMIT = 56 * 1024 * 1024

f32 = jnp.float32
bf16 = jnp.bfloat16


def _mm(a, b):
    return jnp.dot(a.astype(bf16), b.astype(bf16), preferred_element_type=f32)


def _mm_nt(a, b):
    return lax.dot_general(a.astype(bf16), b.astype(bf16), (((1,), (1,)), ((), ())),
                           preferred_element_type=f32)


def _split2(x):
    hi = x.astype(bf16)
    lo = (x - hi.astype(f32)).astype(bf16)
    return hi, lo


def _block_diag(x):
    x = x.astype(bf16)
    rows, lanes = x.shape
    assert lanes in (2 * LANES, 4 * LANES)
    zero = jnp.zeros((rows, LANES), bf16)
    cols = []
    if lanes == 4 * LANES:
        for j in range(4):
            cols.append(jnp.concatenate([x[:, j * LANES:(j + 1) * LANES] if i == j else zero for i in range(4)],
                                        axis=0))
    else:
        low = lax.broadcasted_iota(jnp.int32, (rows, LANES), 1) < LANES // 2
        for j in range(2):
            xj = x[:, j * LANES:(j + 1) * LANES]
            pair = [jnp.where(low, xj, zero), jnp.where(low, zero, xj)]
            cols.append(jnp.concatenate([zero] * (2 * j) + pair + [zero] * (2 - 2 * j), axis=0))
    return jnp.concatenate(cols, axis=1)


def _sigmoid(x):
    return 1.0 / (1.0 + jnp.exp(-x))


def _in_proj_kernel(tiles_per_seq, x_ref, g_ref, w_ref, cos_ref, sin_ref, mu_ref, p_ref, wb_ref, carry_ref):
    i = pl.program_id(0)

    @pl.when(i == 0)
    def _():
        wb_ref[...] = w_ref[...].astype(bf16)

    x = x_ref[...]
    u = (x * lax.rsqrt(jnp.mean(x * x, axis=-1, keepdims=True) + RMS_EPS) * g_ref[...]).astype(bf16)
    rows = x.shape[0]
    seq_start = (i % tiles_per_seq) == 0
    W = RWKV_WIDTH

    def rope(pg):
        cos = jnp.concatenate([cos_ref[...]] * 2, axis=1)
        sin = jnp.concatenate([sin_ref[...]] * 2, axis=1)
        half = (lax.broadcasted_iota(jnp.int32, pg.shape, 1) & (RET_DK - 1)) < RET_DK // 2
        lanes = pg.shape[1]
        swapped = jnp.where(half, pltpu.roll(pg, lanes - RET_DK // 2, 1), pltpu.roll(pg, RET_DK // 2, 1))
        return pg * cos + swapped * sin

    def silu(pg):
        return pg * _sigmoid(pg)

    def shifted(lo, hi):
        def fn(pg):
            first = jnp.where(seq_start, 0.0, carry_ref[:, lo:hi])
            row0 = lax.broadcasted_iota(jnp.int32, pg.shape, 0) == 0
            prev = jnp.where(row0, first, pltpu.roll(pg, 1, 0))
            carry_ref[:, lo:hi] = pg[rows - 1:rows, :]
            return pg + mu_ref[:, lo:hi] * (prev - pg)
        return fn

    groups = [(0, 2 * RET_QK, rope),
              (2 * RET_QK, 2 * RET_QK + RET_WIDTH, lambda pg: pg),
              (2 * RET_QK + RET_WIDTH, RET_COLS, silu),
              (RET_COLS, RET_COLS + W, shifted(0, W)),
              (RET_COLS + W, RET_COLS + 2 * W, shifted(W, 2 * W)),
              (RET_COLS + 2 * W, RET_COLS + 3 * W, shifted(2 * W, 3 * W)),
              (RET_COLS + 3 * W, RET_COLS + 4 * W, lambda pg: silu(shifted(3 * W, 4 * W)(pg))),
              (RET_COLS + 4 * W, IN_COLS, shifted(4 * W, RWKV_COLS))]
    pending = None
    for lo, hi, fn in groups:
        acc = jnp.dot(u, wb_ref[:, lo:hi], preferred_element_type=f32)
        if pending is not None:
            p_ref[:, pending[1]:pending[2]] = pending[3](pending[0])
        pending = (acc, lo, hi, fn)
    p_ref[:, pending[1]:pending[2]] = pending[3](pending[0])


def _staged(n, produce, consume):
    for i in range(n + RESULT_LAG):
        if i < n:
            produce(i)
        if i >= RESULT_LAG:
            consume(i - RESULT_LAG)


def _retention_chunks(qs, ks, vs, states, decay4, qdec, kdec, head_decay):
    n = range(len(qs))
    hs = range(RET_HEADS)
    scores = [_mm_nt(qs[i], _block_diag(ks[i])) * decay4 for i in n]
    qds = [qs[i] * qdec for i in n]
    low = lax.broadcasted_iota(jnp.int32, (CHUNK, LANES), 1) < RET_DK

    def head_lhs(sc, qd, h):
        sl = slice((h // 2) * LANES, (h // 2 + 1) * LANES)
        if h % 2 == 0:
            return jnp.where(low, sc[:, sl], pltpu.roll(qd[:, sl], RET_DK, 1))
        return jnp.where(low, pltpu.roll(sc[:, sl], RET_DK, 1), qd[:, sl])

    out = [jnp.concatenate(
        [_mm(head_lhs(scores[i], qds[i], h),
             jnp.concatenate([vs[i][:, h * RET_DV:(h + 1) * RET_DV].astype(bf16), states[i][h].astype(bf16)],
                             axis=0)) for h in hs], axis=1) for i in n]
    kts = [(ks[i] * kdec).T for i in n]
    kv = [[_mm(kts[i][h * RET_DK:(h + 1) * RET_DK], vs[i][:, h * RET_DV:(h + 1) * RET_DV]) for h in hs]
          for i in n]
    s_new = [jnp.stack([states[i][h] * head_decay[h] + kv[i][h] for h in hs]) for i in n]
    return out, s_new


def _rwkv_chunks(rs, ks, vs, kkns, aas, lws, m_cs):
    C = CHUNK
    n = len(rs)
    heads = GROUP // RWKV_HEAD
    ti = lax.broadcasted_iota(jnp.int32, (C, C), 0)
    si = lax.broadcasted_iota(jnp.int32, (C, C), 1)
    tri_incl = jnp.where(si <= ti, 1.0, 0.0).astype(bf16)
    tri_incl = jnp.concatenate([tri_incl, tri_incl], axis=1)
    ti = lax.broadcasted_iota(jnp.int32, (C, GROUP), 0)
    si = lax.broadcasted_iota(jnp.int32, (C, GROUP), 1) & (C - 1)
    strict = si < ti
    incl = si <= ti
    eye = jnp.where(si == ti, 1.0, 0.0)
    low = lax.broadcasted_iota(jnp.int32, (RWKV_HEAD, LANES), 1) < RWKV_HEAD

    def slots():
        return [None] * n

    def staged(produce, consume):
        _staged(n, produce, consume)

    cum, lhs, rhs, decay_end, key_end = slots(), slots(), slots(), slots(), slots()

    def cum_matmul(i):
        cum[i] = jnp.dot(tri_incl, jnp.concatenate(_split2(lws[i]), axis=0), preferred_element_type=f32)

    def scale_operands(i):
        g_inc = jnp.exp(cum[i])
        g_inv = jnp.exp(-cum[i])
        g_end = jnp.exp(cum[i][C - 1:C, :] - cum[i])
        beta = kkns[i] * aas[i]
        lhs[i] = jnp.concatenate([-kkns[i] * jnp.exp(cum[i] - lws[i]), rs[i] * g_inc], axis=0).astype(bf16)
        rhs[i] = jnp.concatenate([_block_diag(beta * g_inv), _block_diag(ks[i] * g_inv)], axis=0)
        decay_end[i] = g_inc[C - 1:C, :]
        key_end[i] = jnp.concatenate([beta * g_end, ks[i] * g_end], axis=0).astype(bf16)

    staged(cum_matmul, scale_operands)

    amat, mv, a_ab, a_k, a_rb, t_inv = slots(), slots(), slots(), slots(), slots(), slots()

    def score_matmuls(i):
        amat[i] = _mm_nt(lhs[i], rhs[i])
        mv[i] = _mm_nt(lhs[i], _block_diag(m_cs[i]))

    def mask_scores(i):
        lower = jnp.where(strict, amat[i][:C, :GROUP], 0.0)
        a_ab[i] = lower.astype(bf16)
        a_rb[i] = jnp.where(incl, amat[i][C:, :GROUP], 0.0).astype(bf16)
        a_k[i] = jnp.concatenate([jnp.where(strict, amat[i][:C, GROUP:], 0.0),
                                  jnp.where(incl, amat[i][C:, GROUP:], 0.0)], axis=0).astype(bf16)
        t_inv[i] = eye + lower

    staged(score_matmuls, mask_scores)

    av, st, st_lhs, pow_bd = slots(), slots(), slots(), slots()

    def first_square(i):
        av[i] = _mm(a_k[i], _block_diag(vs[i]))
        st[i] = _mm(a_ab[i], _block_diag(a_ab[i]))

    def after_first_square(i):
        s_b = st[i].astype(bf16)
        st_lhs[i] = jnp.concatenate([s_b, t_inv[i].astype(bf16)], axis=0)
        pow_bd[i] = _block_diag(s_b)

    staged(first_square, after_first_square)

    def square(i):
        st[i] = _mm(st_lhs[i], pow_bd[i])

    def after_square(i):
        s_b = st[i][:C].astype(bf16)
        t_inv[i] = t_inv[i] + st[i][C:]
        st_lhs[i] = jnp.concatenate([s_b, t_inv[i].astype(bf16)], axis=0)
        pow_bd[i] = _block_diag(s_b)

    for _ in range(4):
        staged(square, after_square)

    t_b, y_bd = slots(), slots()

    def last_product(i):
        st[i] = _mm(st_lhs[i][C:], pow_bd[i])

    def after_last_product(i):
        t_b[i] = (t_inv[i] + st[i]).astype(bf16)
        y_bd[i] = _block_diag(mv[i][:C] + av[i][:C])

    staged(last_product, after_last_product)

    u, u_bd, uvt = slots(), slots(), slots()

    def solve(i):
        u[i] = _mm(t_b[i], y_bd[i])

    def after_solve(i):
        u_bd[i] = _block_diag(u[i])
        uvt[i] = jnp.concatenate([u[i], vs[i]], axis=0).T.astype(bf16)

    staged(solve, after_solve)

    o_u, upd, out, m_new = slots(), slots(), slots(), slots()

    def output_and_update(i):
        o_u[i] = _mm(a_rb[i], u_bd[i])
        upd[i] = [_mm(uvt[i][h * RWKV_HEAD:(h + 1) * RWKV_HEAD],
                      key_end[i][:, (h // 2) * LANES:(h // 2 + 1) * LANES]) for h in range(heads)]

    def finish(i):
        out[i] = mv[i][C:] + av[i][C:] + o_u[i]
        m_new[i] = m_cs[i] * decay_end[i] + jnp.concatenate(
            [jnp.where(low, upd[i][2 * j], upd[i][2 * j + 1]) for j in range(heads // 2)], axis=1)

    staged(output_and_update, finish)
    return out, m_new


def _mix_chunk(p_ref, dec_ref, lora_ref, vec_ref, rgn_ref, seg_ref, y_ref, m_ref, s_ref):
    C = CHUNK
    nb = BATCH_BLOCK
    R = nb * C
    W = RWKV_WIDTH
    NG = W // GROUP

    def rows(x, bi):
        return x[bi * C:(bi + 1) * C]

    def rwkv_cols(lo, hi):
        return p_ref[:, :, RET_COLS + lo:RET_COLS + hi].reshape(R, hi - lo)

    r = rwkv_cols(0, W)
    kr = rwkv_cols(W, 2 * W)
    vr = rwkv_cols(2 * W, 3 * W)
    gate_rw = rwkv_cols(3 * W, 4 * W)
    xwa = rwkv_cols(4 * W, RWKV_COLS)
    lane = lax.broadcasted_iota(jnp.int32, xwa.shape, 1)
    lora = _mm(jnp.where(lane < LORA, jnp.tanh(xwa), xwa), lora_ref[...])
    w0, a0, k_k, k_a, r_k, gn_g, gn_b = (vec_ref[i:i + 1, :] for i in range(7))
    lw = -np.float32(np.exp(-0.5)) * _sigmoid(w0 + lora[:, :W])
    a = _sigmoid(a0 + lora[:, W:])
    seg = seg_ref[...]

    def segsum(x):
        xs = jnp.concatenate([x[:, g * GROUP:(g + 1) * GROUP] for g in range(NG)], axis=0)
        tot = jnp.dot(xs.astype(bf16), seg, preferred_element_type=f32)
        return jnp.concatenate([tot[g * R:(g + 1) * R] for g in range(NG)], axis=1)

    kk = kr * k_k
    kkn = kk * lax.rsqrt(jnp.maximum(segsum(kk * kk), 1e-24))
    kmod = kr * (1.0 + (a - 1.0) * k_a)

    q = p_ref[:, :, 0:RET_QK].reshape(R, RET_QK)
    k = p_ref[:, :, RET_QK:2 * RET_QK].reshape(R, RET_QK)
    v = p_ref[:, :, 2 * RET_QK:2 * RET_QK + RET_WIDTH].reshape(R, RET_WIDTH)
    rets, s_new = _retention_chunks([rows(q, bi) for bi in range(nb)], [rows(k, bi) for bi in range(nb)],
                                    [rows(v, bi) for bi in range(nb)], [s_ref[bi] for bi in range(nb)],
                                    dec_ref[0], dec_ref[1], dec_ref[2], _RET_CHUNK_DECAY)
    for bi in range(nb):
        s_ref[bi] = s_new[bi]
    ret = jnp.concatenate(rets, axis=0)
    rgn = rgn_ref[...]
    for h in range(RET_HEADS):
        sl = slice(h * RET_DV, (h + 1) * RET_DV)
        xh = ret[:, sl]
        d = xh - jnp.mean(xh, axis=-1, keepdims=True)
        var = jnp.mean(d * d, axis=-1, keepdims=True)
        gh = p_ref[:, :, 2 * RET_QK + RET_WIDTH + h * RET_DV:2 * RET_QK + RET_WIDTH + (h + 1) * RET_DV]
        gh = gh.reshape(R, RET_DV)
        yh = gh * (d * lax.rsqrt(var + RET_GN_EPS) * rgn[:, sl])
        y_ref[:, :, sl] = yh.reshape(nb, C, RET_DV).astype(y_ref.dtype)

    chains = [(bi, g) for bi in range(nb) for g in range(NG)]

    def pick(x):
        return [x[bi * C:(bi + 1) * C, g * GROUP:(g + 1) * GROUP] for bi, g in chains]

    outs, m_new = _rwkv_chunks(pick(r), pick(kmod), pick(vr), pick(kkn), pick(a), pick(lw),
                               [m_ref[bi, g] for bi, g in chains])
    for i, (bi, g) in enumerate(chains):
        m_ref[bi, g] = m_new[i]
    o = jnp.concatenate([jnp.concatenate(outs[bi * NG:(bi + 1) * NG], axis=1) for bi in range(nb)], axis=0)
    d = o - segsum(o) * (1.0 / RWKV_HEAD)
    var = segsum(d * d) * (1.0 / RWKV_HEAD)
    o = d * lax.rsqrt(var + RWKV_GN_EPS) * gn_g + gn_b
    bonus = segsum(r * kmod * r_k) * vr
    y_rw = gate_rw * (o + bonus)
    y_ref[:, :, RET_WIDTH:] = y_rw.reshape(nb, C, W).astype(y_ref.dtype)


def _project_chunk(x_ref, y_ref, w_ref, g_ref, o_ref):
    nb, C, D = x_ref.shape
    y = y_ref[...].reshape(nb * C, D)
    h = x_ref[...].reshape(nb * C, D) + jnp.dot(y, w_ref[...], preferred_element_type=f32)
    out = h * lax.rsqrt(jnp.mean(h * h, axis=-1, keepdims=True) + RMS_EPS) * g_ref[...]
    o_ref[...] = out.reshape(nb, C, D)


def _mixer_kernel(p_ref, dec_ref, lora_ref, vec_ref, rgn_ref,
                  seg_ref, x_ref, wout_ref, fg_ref, o_ref, y_ref, m_ref, s_ref, woutb_ref):
    @pl.when(pl.program_id(1) == 0)
    def _():
        m_ref[...] = jnp.zeros_like(m_ref)
        s_ref[...] = jnp.zeros_like(s_ref)
        woutb_ref[...] = wout_ref[...].astype(bf16)

    _mix_chunk(p_ref, dec_ref, lora_ref, vec_ref, rgn_ref, seg_ref, y_ref, m_ref, s_ref)
    _project_chunk(x_ref, y_ref, woutb_ref, fg_ref, o_ref)


def _rope_tables(seq):
    half = RET_DK // 2
    expo = -jnp.arange(half, dtype=f32) / f32(half)
    freqs = jnp.exp(expo * f32(np.log(ROPE_BASE)))
    ang = jnp.arange(seq, dtype=jnp.int32).astype(f32)[:, None] * freqs[None, :]
    cos = jnp.cos(ang)
    sin = jnp.sin(ang)
    cos_full = jnp.tile(jnp.concatenate([cos, cos], axis=1), (1, RET_HEADS))
    sin_signed = jnp.tile(jnp.concatenate([-sin, sin], axis=1), (1, RET_HEADS))
    return cos_full, sin_signed


_RET_LOG_GAMMA = np.log(1.0 - np.exp2(-5.0 - np.arange(RET_HEADS, dtype=np.float64)))
_RET_CHUNK_DECAY = tuple(np.float32(v) for v in np.exp(_RET_LOG_GAMMA * CHUNK))


def _retention_constants():
    C = CHUNK
    lg = _RET_LOG_GAMMA
    lane_lg = np.repeat(lg, RET_DK)[None, :]
    n = np.arange(C, dtype=np.float64)[:, None]
    m = np.tile(np.arange(C, dtype=np.float64), RET_HEADS)[None, :]
    scale = RET_DK ** -0.5
    decay4 = scale * np.exp(lane_lg * np.abs(n - m))
    qdec = np.exp(lane_lg * (n + 1.0)) * np.ones((1, RET_QK))
    kdec = scale * np.exp(lane_lg * (C - 1.0 - n)) * np.ones((1, RET_QK))
    return jnp.asarray(np.stack([decay4, qdec, kdec]).astype(np.float32))


def _const_spec(shape, single_buffer=False):
    mode = pl.Buffered(1) if single_buffer else None
    return pl.BlockSpec(shape, lambda *_: (0,) * len(shape), pipeline_mode=mode)


@jax.jit
def kernel(x, norm_g, w_in, ret_gn_g, rwkv_mu, w_lora_up, w0, a_lora_up, a0, k_k, k_a, r_k,
           rwkv_gn_g, rwkv_gn_b, w_out, final_norm_g):
    B, T, D = x.shape
    assert D == D_MODEL and T % CHUNK == 0 and T % PROJ_TILE == 0 and B % BATCH_BLOCK == 0
    assert norm_g.shape[0] == 1, "single-layer block"
    n_tok = B * T
    xf = x.reshape(n_tok, D)
    params = pltpu.CompilerParams(dimension_semantics=("arbitrary",), vmem_limit_bytes=VMEM_LIMIT)
    tiles_per_seq = T // PROJ_TILE
    cos, sin = _rope_tables(T)

    p = pl.pallas_call(
        functools.partial(_in_proj_kernel, tiles_per_seq),
        grid=(n_tok // PROJ_TILE,),
        in_specs=[pl.BlockSpec((PROJ_TILE, D), lambda i: (i, 0)),
                  _const_spec((1, D)),
                  _const_spec((D, IN_COLS), single_buffer=True),
                  pl.BlockSpec((PROJ_TILE, RET_QK), lambda i: (i % tiles_per_seq, 0)),
                  pl.BlockSpec((PROJ_TILE, RET_QK), lambda i: (i % tiles_per_seq, 0)),
                  _const_spec((1, RWKV_COLS))],
        out_specs=pl.BlockSpec((PROJ_TILE, IN_COLS), lambda i: (i, 0)),
        out_shape=jax.ShapeDtypeStruct((n_tok, IN_COLS), f32),
        scratch_shapes=[pltpu.VMEM((D, IN_COLS), bf16),
                        pltpu.VMEM((1, RWKV_COLS), f32)],
        compiler_params=params,
        name="in_proj",
    )(xf, norm_g[0][None, :], w_in[0], cos, sin, rwkv_mu[0][None, :])

    dec = _retention_constants()
    zeros = jnp.zeros((LORA, RWKV_WIDTH), f32)
    lora_w = jnp.concatenate(
        [jnp.concatenate([w_lora_up[0], zeros], axis=1),
         jnp.concatenate([zeros, a_lora_up[0]], axis=1)], axis=0).astype(bf16)
    vecs = jnp.stack([w0[0], a0[0], k_k[0], k_a[0], r_k[0].reshape(-1), rwkv_gn_g[0], rwkv_gn_b[0],
                      jnp.zeros((RWKV_WIDTH,), f32)])
    seg_ids = np.arange(GROUP) // RWKV_HEAD
    seg = jnp.asarray(seg_ids[:, None] == seg_ids[None, :], bf16)

    return pl.pallas_call(
        _mixer_kernel,
        grid=(B // BATCH_BLOCK, T // CHUNK),
        in_specs=[pl.BlockSpec((BATCH_BLOCK, CHUNK, IN_COLS), lambda b, c: (b, c, 0)),
                  _const_spec((3, CHUNK, RET_QK)),
                  _const_spec((2 * LORA, 2 * RWKV_WIDTH)),
                  _const_spec((8, RWKV_WIDTH)),
                  _const_spec((1, RET_WIDTH)),
                  _const_spec((GROUP, GROUP)),
                  pl.BlockSpec((BATCH_BLOCK, CHUNK, D), lambda b, c: (b, c, 0)),
                  _const_spec((D, D), single_buffer=True),
                  _const_spec((1, D))],
        out_specs=pl.BlockSpec((BATCH_BLOCK, CHUNK, D), lambda b, c: (b, c, 0)),
        out_shape=jax.ShapeDtypeStruct((B, T, D), f32),
        scratch_shapes=[pltpu.VMEM((BATCH_BLOCK, CHUNK, D), bf16),
                        pltpu.VMEM((BATCH_BLOCK, RWKV_WIDTH // GROUP, RWKV_HEAD, GROUP), f32),
                        pltpu.VMEM((BATCH_BLOCK, RET_HEADS, RET_DK, RET_DV), f32),
                        pltpu.VMEM((D, D), bf16)],
        compiler_params=pltpu.CompilerParams(dimension_semantics=("arbitrary", "arbitrary"),
                                             vmem_limit_bytes=VMEM_LIMIT),
        name="mixers",
    )(p.reshape(B, T, IN_COLS), dec, lora_w, vecs,
      ret_gn_g[0][None, :], seg, x, w_out[0], final_norm_g[None, :])
```

```python
import functools

import numpy as np
import jax
import jax.numpy as jnp
from jax import lax
from jax.experimental import pallas as pl
from jax.experimental.pallas import tpu as pltpu

D_MODEL = 1024
CHUNK = 64
RET_HEADS = 4
RET_DK = 64
RET_DV = 128
RET_QK = RET_HEADS * RET_DK
RET_WIDTH = RET_HEADS * RET_DV
RWKV_WIDTH = 512
RWKV_HEAD = 64
LORA = 64
RET_COLS = 2 * RET_QK + 2 * RET_WIDTH
RWKV_COLS = 4 * RWKV_WIDTH + 2 * LORA
IN_COLS = RET_COLS + RWKV_COLS
ROPE_BASE = 10000.0
RMS_EPS = 1e-6
RET_GN_EPS = 1e-5
RWKV_GN_EPS = 64e-5
LANES = 128
GROUP = 256
PROJ_TILE = 512
BATCH_BLOCK = 8
RESULT_LAG = 4
VMEM_LIMIT = 56 * 1024 * 1024

f32 = jnp.float32
bf16 = jnp.bfloat16


def _mm(a, b):
    return jnp.dot(a.astype(bf16), b.astype(bf16), preferred_element_type=f32)


def _mm_nt(a, b):
    return lax.dot_general(a.astype(bf16), b.astype(bf16), (((1,), (1,)), ((), ())),
                           preferred_element_type=f32)


def _split2(x):
    hi = x.astype(bf16)
    lo = (x - hi.astype(f32)).astype(bf16)
    return hi, lo


def _block_diag(x):
    x = x.astype(bf16)
    rows, lanes = x.shape
    assert lanes in (2 * LANES, 4 * LANES)
    zero = jnp.zeros((rows, LANES), bf16)
    cols = []
    if lanes == 4 * LANES:
        for j in range(4):
            cols.append(jnp.concatenate([x[:, j * LANES:(j + 1) * LANES] if i == j else zero for i in range(4)],
                                        axis=0))
    else:
        low = lax.broadcasted_iota(jnp.int32, (rows, LANES), 1) < LANES // 2
        for j in range(2):
            xj = x[:, j * LANES:(j + 1) * LANES]
            pair = [jnp.where(low, xj, zero), jnp.where(low, zero, xj)]
            cols.append(jnp.concatenate([zero] * (2 * j) + pair + [zero] * (2 - 2 * j), axis=0))
    return jnp.concatenate(cols, axis=1)


def _sigmoid(x):
    return 1.0 / (1.0 + jnp.exp(-x))


def _in_proj_kernel(tiles_per_seq, x_ref, g_ref, w_ref, cos_ref, sin_ref, mu_ref, p_ref, wb_ref, carry_ref):
    i = pl.program_id(0)

    @pl.when(i == 0)
    def _():
        wb_ref[...] = w_ref[...].astype(bf16)

    x = x_ref[...]
    u = (x * lax.rsqrt(jnp.mean(x * x, axis=-1, keepdims=True) + RMS_EPS) * g_ref[...]).astype(bf16)
    rows = x.shape[0]
    seq_start = (i % tiles_per_seq) == 0
    W = RWKV_WIDTH

    def rope(pg):
        cos = jnp.concatenate([cos_ref[...]] * (pg.shape[1] // LANES), axis=1)
        sin = jnp.concatenate([sin_ref[...]] * (pg.shape[1] // LANES), axis=1)
        half = (lax.broadcasted_iota(jnp.int32, pg.shape, 1) & (RET_DK - 1)) < RET_DK // 2
        lanes = pg.shape[1]
        swapped = jnp.where(half, pltpu.roll(pg, lanes - RET_DK // 2, 1), pltpu.roll(pg, RET_DK // 2, 1))
        return pg * cos + swapped * sin

    def silu(pg):
        return pg * _sigmoid(pg)

    def shifted(lo, hi):
        def fn(pg):
            first = jnp.where(seq_start, 0.0, carry_ref[:, lo:hi])
            row0 = lax.broadcasted_iota(jnp.int32, pg.shape, 0) == 0
            prev = jnp.where(row0, first, pltpu.roll(pg, 1, 0))
            carry_ref[:, lo:hi] = pg[rows - 1:rows, :]
            return pg + mu_ref[:, lo:hi] * (prev - pg)
        return fn

    groups = [(0, 2 * RET_QK, rope),
              (2 * RET_QK, 2 * RET_QK + RET_WIDTH, lambda pg: pg),
              (2 * RET_QK + RET_WIDTH, RET_COLS, silu),
              (RET_COLS, RET_COLS + W, shifted(0, W)),
              (RET_COLS + W, RET_COLS + 2 * W, shifted(W, 2 * W)),
              (RET_COLS + 2 * W, RET_COLS + 3 * W, shifted(2 * W, 3 * W)),
              (RET_COLS + 3 * W, RET_COLS + 4 * W, lambda pg: silu(shifted(3 * W, 4 * W)(pg))),
              (RET_COLS + 4 * W, IN_COLS, shifted(4 * W, RWKV_COLS))]
    pending = None
    for lo, hi, fn in groups:
        acc = jnp.dot(u, wb_ref[:, lo:hi], preferred_element_type=f32)
        if pending is not None:
            p_ref[:, pending[1]:pending[2]] = pending[3](pending[0])
        pending = (acc, lo, hi, fn)
    p_ref[:, pending[1]:pending[2]] = pending[3](pending[0])


def _staged(n, produce, consume):
    for i in range(n + RESULT_LAG):
        if i < n:
            produce(i)
        if i >= RESULT_LAG:
            consume(i - RESULT_LAG)


def _retention_chunks(qs, ks, vs, states, decay4, qdec, kdec, head_decay):
    n = range(len(qs))
    hs = range(RET_HEADS)
    scores = [_mm_nt(qs[i], _block_diag(ks[i])) * decay4 for i in n]
    qds = [qs[i] * qdec for i in n]
    low = lax.broadcasted_iota(jnp.int32, (CHUNK, LANES), 1) < RET_DK

    def head_lhs(sc, qd, h):
        sl = slice((h // 2) * LANES, (h // 2 + 1) * LANES)
        if h % 2 == 0:
            return jnp.where(low, sc[:, sl], pltpu.roll(qd[:, sl], RET_DK, 1))
        return jnp.where(low, pltpu.roll(sc[:, sl], RET_DK, 1), qd[:, sl])

    out = [jnp.concatenate(
        [_mm(head_lhs(scores[i], qds[i], h),
             jnp.concatenate([vs[i][:, h * RET_DV:(h + 1) * RET_DV].astype(bf16), states[i][h].astype(bf16)],
                             axis=0)) for h in hs], axis=1) for i in n]
    kts = [(ks[i] * kdec).T for i in n]
    kv = [[_mm(kts[i][h * RET_DK:(h + 1) * RET_DK], vs[i][:, h * RET_DV:(h + 1) * RET_DV]) for h in hs]
          for i in n]
    s_new = [jnp.stack([states[i][h] * head_decay[h] + kv[i][h] for h in hs]) for i in n]
    return out, s_new


def _rwkv_chunks(rs, ks, vs, kkns, aas, lws, m_cs):
    C = CHUNK
    n = len(rs)
    heads = GROUP // RWKV_HEAD
    ti = lax.broadcasted_iota(jnp.int32, (C, C), 0)
    si = lax.broadcasted_iota(jnp.int32, (C, C), 1)
    tri_incl = jnp.where(si <= ti, 1.0, 0.0).astype(bf16)
    tri_incl = jnp.concatenate([tri_incl, tri_incl], axis=1)
    ti = lax.broadcasted_iota(jnp.int32, (C, GROUP), 0)
    si = lax.broadcasted_iota(jnp.int32, (C, GROUP), 1) & (C - 1)
    strict = si < ti
    incl = si <= ti
    eye = jnp.where(si == ti, 1.0, 0.0)
    low = lax.broadcasted_iota(jnp.int32, (RWKV_HEAD, LANES), 1) < RWKV_HEAD

    def slots():
        return [None] * n

    def staged(produce, consume):
        _staged(n, produce, consume)

    cum, lhs, rhs, decay_end, key_end = slots(), slots(), slots(), slots(), slots()

    def cum_matmul(i):
        cum[i] = jnp.dot(tri_incl, jnp.concatenate(_split2(lws[i]), axis=0), preferred_element_type=f32)

    def scale_operands(i):
        g_inc = jnp.exp(cum[i])
        g_inv = jnp.exp(-cum[i])
        g_end = jnp.exp(cum[i][C - 1:C, :] - cum[i])
        beta = kkns[i] * aas[i]
        lhs[i] = jnp.concatenate([-kkns[i] * jnp.exp(cum[i] - lws[i]), rs[i] * g_inc], axis=0).astype(bf16)
        rhs[i] = jnp.concatenate([_block_diag(beta * g_inv), _block_diag(ks[i] * g_inv)], axis=0)
        decay_end[i] = g_inc[C - 1:C, :]
        key_end[i] = jnp.concatenate([beta * g_end, ks[i] * g_end], axis=0).astype(bf16)

    staged(cum_matmul, scale_operands)

    amat, mv, a_ab, a_k, a_rb, t_inv = slots(), slots(), slots(), slots(), slots(), slots()

    def score_matmuls(i):
        amat[i] = _mm_nt(lhs[i], rhs[i])
        mv[i] = _mm_nt(lhs[i], _block_diag(m_cs[i]))

    def mask_scores(i):
        lower = jnp.where(strict, amat[i][:C, :GROUP], 0.0)
        a_ab[i] = lower.astype(bf16)
        a_rb[i] = jnp.where(incl, amat[i][C:, :GROUP], 0.0).astype(bf16)
        a_k[i] = jnp.concatenate([jnp.where(strict, amat[i][:C, GROUP:], 0.0),
                                  jnp.where(incl, amat[i][C:, GROUP:], 0.0)], axis=0).astype(bf16)
        t_inv[i] = eye + lower

    staged(score_matmuls, mask_scores)

    av, st, st_lhs, pow_bd = slots(), slots(), slots(), slots()

    def first_square(i):
        av[i] = _mm(a_k[i], _block_diag(vs[i]))
        st[i] = _mm(a_ab[i], _block_diag(a_ab[i]))

    def after_first_square(i):
        s_b = st[i].astype(bf16)
        st_lhs[i] = jnp.concatenate([s_b, t_inv[i].astype(bf16)], axis=0)
        pow_bd[i] = _block_diag(s_b)

    staged(first_square, after_first_square)

    def square(i):
        st[i] = _mm(st_lhs[i], pow_bd[i])

    def after_square(i):
        s_b = st[i][:C].astype(bf16)
        t_inv[i] = t_inv[i] + st[i][C:]
        st_lhs[i] = jnp.concatenate([s_b, t_inv[i].astype(bf16)], axis=0)
        pow_bd[i] = _block_diag(s_b)

    for _ in range(4):
        staged(square, after_square)

    t_b, y_bd = slots(), slots()

    def last_product(i):
        st[i] = _mm(st_lhs[i][C:], pow_bd[i])

    def after_last_product(i):
        t_b[i] = (t_inv[i] + st[i]).astype(bf16)
        y_bd[i] = _block_diag(mv[i][:C] + av[i][:C])

    staged(last_product, after_last_product)

    u, u_bd, uvt = slots(), slots(), slots()

    def solve(i):
        u[i] = _mm(t_b[i], y_bd[i])

    def after_solve(i):
        u_bd[i] = _block_diag(u[i])
        uvt[i] = jnp.concatenate([u[i], vs[i]], axis=0).T.astype(bf16)

    staged(solve, after_solve)

    o_u, upd, out, m_new = slots(), slots(), slots(), slots()

    def output_and_update(i):
        o_u[i] = _mm(a_rb[i], u_bd[i])
        upd[i] = [_mm(uvt[i][h * RWKV_HEAD:(h + 1) * RWKV_HEAD],
                      key_end[i][:, (h // 2) * LANES:(h // 2 + 1) * LANES]) for h in range(heads)]

    def finish(i):
        out[i] = mv[i][C:] + av[i][C:] + o_u[i]
        m_new[i] = m_cs[i] * decay_end[i] + jnp.concatenate(
            [jnp.where(low, upd[i][2 * j], upd[i][2 * j + 1]) for j in range(heads // 2)], axis=1)

    staged(output_and_update, finish)
    return out, m_new


def _mix_chunk(p_ref, dec_ref, lora_ref, vec_refs, rgn_ref, seg_ref, y_ref, m_ref, s_ref):
    C = CHUNK
    nb = BATCH_BLOCK
    R = nb * C
    W = RWKV_WIDTH
    NG = W // GROUP

    def rows(x, bi):
        return x[bi * C:(bi + 1) * C]

    def rwkv_cols(lo, hi):
        return p_ref[:, :, RET_COLS + lo:RET_COLS + hi].reshape(R, hi - lo)

    r = rwkv_cols(0, W)
    kr = rwkv_cols(W, 2 * W)
    vr = rwkv_cols(2 * W, 3 * W)
    gate_rw = rwkv_cols(3 * W, 4 * W)
    xwa = rwkv_cols(4 * W, RWKV_COLS)
    lane = lax.broadcasted_iota(jnp.int32, xwa.shape, 1)
    lora = _mm(jnp.where(lane < LORA, jnp.tanh(xwa), xwa), lora_ref[...])
    w0, a0, k_k, k_a, r_k, gn_g, gn_b = (ref[...] for ref in vec_refs)
    lw = -np.float32(np.exp(-0.5)) * _sigmoid(w0 + lora[:, :W])
    a = _sigmoid(a0 + lora[:, W:])
    seg = seg_ref[...]

    def segsum(x):
        xs = jnp.concatenate([x[:, g * GROUP:(g + 1) * GROUP] for g in range(NG)], axis=0)
        tot = jnp.dot(xs.astype(bf16), seg, preferred_element_type=f32)
        return jnp.concatenate([tot[g * R:(g + 1) * R] for g in range(NG)], axis=1)

    kk = kr * k_k
    kkn = kk * lax.rsqrt(jnp.maximum(segsum(kk * kk), 1e-24))
    kmod = kr * (1.0 + (a - 1.0) * k_a)

    q = p_ref[:, :, 0:RET_QK].reshape(R, RET_QK)
    k = p_ref[:, :, RET_QK:2 * RET_QK].reshape(R, RET_QK)
    v = p_ref[:, :, 2 * RET_QK:2 * RET_QK + RET_WIDTH].reshape(R, RET_WIDTH)
    rets, s_new = _retention_chunks([rows(q, bi) for bi in range(nb)], [rows(k, bi) for bi in range(nb)],
                                    [rows(v, bi) for bi in range(nb)], [s_ref[bi] for bi in range(nb)],
                                    dec_ref[0], dec_ref[1], dec_ref[2], _RET_CHUNK_DECAY)
    for bi in range(nb):
        s_ref[bi] = s_new[bi]
    ret = jnp.concatenate(rets, axis=0)
    rgn = rgn_ref[...]
    for h in range(RET_HEADS):
        sl = slice(h * RET_DV, (h + 1) * RET_DV)
        xh = ret[:, sl]
        d = xh - jnp.mean(xh, axis=-1, keepdims=True)
        var = jnp.mean(d * d, axis=-1, keepdims=True)
        gh = p_ref[:, :, 2 * RET_QK + RET_WIDTH + h * RET_DV:2 * RET_QK + RET_WIDTH + (h + 1) * RET_DV]
        gh = gh.reshape(R, RET_DV)
        yh = gh * (d * lax.rsqrt(var + RET_GN_EPS) * rgn[:, sl])
        y_ref[:, :, sl] = yh.reshape(nb, C, RET_DV).astype(y_ref.dtype)

    chains = [(bi, g) for bi in range(nb) for g in range(NG)]

    def pick(x):
        return [x[bi * C:(bi + 1) * C, g * GROUP:(g + 1) * GROUP] for bi, g in chains]

    outs, m_new = _rwkv_chunks(pick(r), pick(kmod), pick(vr), pick(kkn), pick(a), pick(lw),
                               [m_ref[bi, g] for bi, g in chains])
    for i, (bi, g) in enumerate(chains):
        m_ref[bi, g] = m_new[i]
    o = jnp.concatenate([jnp.concatenate(outs[bi * NG:(bi + 1) * NG], axis=1) for bi in range(nb)], axis=0)
    d = o - segsum(o) * (1.0 / RWKV_HEAD)
    var = segsum(d * d) * (1.0 / RWKV_HEAD)
    o = d * lax.rsqrt(var + RWKV_GN_EPS) * gn_g + gn_b
    bonus = segsum(r * kmod * r_k) * vr
    y_rw = gate_rw * (o + bonus)
    y_ref[:, :, RET_WIDTH:] = y_rw.reshape(nb, C, W).astype(y_ref.dtype)


def _project_chunk(x_ref, y_ref, w_ref, g_ref, o_ref):
    nb, C, D = x_ref.shape
    y = y_ref[...].reshape(nb * C, D)
    h = x_ref[...].reshape(nb * C, D) + jnp.dot(y, w_ref[...], preferred_element_type=f32)
    out = h * lax.rsqrt(jnp.mean(h * h, axis=-1, keepdims=True) + RMS_EPS) * g_ref[...]
    o_ref[...] = out.reshape(nb, C, D)


def _mixer_kernel(p_ref, dec_ref, wlora_ref, alora_ref, w0_ref, a0_ref, kk_ref, ka_ref, rk_ref, gng_ref, gnb_ref,
                  rgn_ref, seg_ref, x_ref, wout_ref, fg_ref, o_ref, y_ref, m_ref, s_ref, woutb_ref, lora_ref):
    @pl.when(pl.program_id(1) == 0)
    def _():
        m_ref[...] = jnp.zeros_like(m_ref)
        s_ref[...] = jnp.zeros_like(s_ref)
        woutb_ref[...] = wout_ref[...].astype(bf16)
        zero = jnp.zeros((LORA, RWKV_WIDTH), bf16)
        lora_ref[...] = jnp.concatenate(
            [jnp.concatenate([wlora_ref[...].astype(bf16), zero], axis=1),
             jnp.concatenate([zero, alora_ref[...].astype(bf16)], axis=1)], axis=0)

    vec_refs = (w0_ref, a0_ref, kk_ref, ka_ref, rk_ref, gng_ref, gnb_ref)
    _mix_chunk(p_ref, dec_ref, lora_ref, vec_refs, rgn_ref, seg_ref, y_ref, m_ref, s_ref)
    _project_chunk(x_ref, y_ref, woutb_ref, fg_ref, o_ref)


def _rope_tables(seq):
    half = RET_DK // 2
    expo = -jnp.arange(half, dtype=f32) / f32(half)
    freqs = jnp.exp(expo * f32(np.log(ROPE_BASE)))
    ang = jnp.arange(seq, dtype=jnp.int32).astype(f32)[:, None] * freqs[None, :]
    cos = jnp.cos(ang)
    sin = jnp.sin(ang)
    cos_full = jnp.tile(jnp.concatenate([cos, cos], axis=1), (1, LANES // RET_DK))
    sin_signed = jnp.tile(jnp.concatenate([-sin, sin], axis=1), (1, LANES // RET_DK))
    return cos_full, sin_signed


_RET_LOG_GAMMA = np.log(1.0 - np.exp2(-5.0 - np.arange(RET_HEADS, dtype=np.float64)))
_RET_CHUNK_DECAY = tuple(np.float32(v) for v in np.exp(_RET_LOG_GAMMA * CHUNK))


def _retention_constants():
    C = CHUNK
    lg = _RET_LOG_GAMMA
    lane_lg = np.repeat(lg, RET_DK)[None, :]
    n = np.arange(C, dtype=np.float64)[:, None]
    m = np.tile(np.arange(C, dtype=np.float64), RET_HEADS)[None, :]
    scale = RET_DK ** -0.5
    decay4 = scale * np.exp(lane_lg * np.abs(n - m))
    qdec = np.exp(lane_lg * (n + 1.0)) * np.ones((1, RET_QK))
    kdec = scale * np.exp(lane_lg * (C - 1.0 - n)) * np.ones((1, RET_QK))
    return jnp.asarray(np.stack([decay4, qdec, kdec]).astype(np.float32))


def _const_spec(shape, single_buffer=False):
    mode = pl.Buffered(1) if single_buffer else None
    return pl.BlockSpec(shape, lambda *_: (0,) * len(shape), pipeline_mode=mode)


@jax.jit
def kernel(x, norm_g, w_in, ret_gn_g, rwkv_mu, w_lora_up, w0, a_lora_up, a0, k_k, k_a, r_k,
           rwkv_gn_g, rwkv_gn_b, w_out, final_norm_g):
    B, T, D = x.shape
    assert D == D_MODEL and T % CHUNK == 0 and T % PROJ_TILE == 0 and B % BATCH_BLOCK == 0
    assert norm_g.shape[0] == 1, "single-layer block"
    n_tok = B * T
    xf = x.reshape(n_tok, D)
    params = pltpu.CompilerParams(dimension_semantics=("arbitrary",), vmem_limit_bytes=VMEM_LIMIT)
    tiles_per_seq = T // PROJ_TILE
    cos, sin = _rope_tables(T)

    p = pl.pallas_call(
        functools.partial(_in_proj_kernel, tiles_per_seq),
        grid=(n_tok // PROJ_TILE,),
        in_specs=[pl.BlockSpec((PROJ_TILE, D), lambda i: (i, 0)),
                  _const_spec((1, D)),
                  _const_spec((D, IN_COLS), single_buffer=True),
                  pl.BlockSpec((PROJ_TILE, LANES), lambda i: (i % tiles_per_seq, 0)),
                  pl.BlockSpec((PROJ_TILE, LANES), lambda i: (i % tiles_per_seq, 0)),
                  _const_spec((1, RWKV_COLS))],
        out_specs=pl.BlockSpec((PROJ_TILE, IN_COLS), lambda i: (i, 0)),
        out_shape=jax.ShapeDtypeStruct((n_tok, IN_COLS), f32),
        scratch_shapes=[pltpu.VMEM((D, IN_COLS), bf16),
                        pltpu.VMEM((1, RWKV_COLS), f32)],
        compiler_params=params,
        name="in_proj",
    )(xf, norm_g[0][None, :], w_in[0], cos, sin, rwkv_mu[0][None, :])

    dec = _retention_constants()
    vecs = [v.reshape(1, RWKV_WIDTH) for v in (w0, a0, k_k, k_a, r_k, rwkv_gn_g, rwkv_gn_b)]
    seg_ids = np.arange(GROUP) // RWKV_HEAD
    seg = jnp.asarray(seg_ids[:, None] == seg_ids[None, :], bf16)

    return pl.pallas_call(
        _mixer_kernel,
        grid=(B // BATCH_BLOCK, T // CHUNK),
        in_specs=[pl.BlockSpec((BATCH_BLOCK, CHUNK, IN_COLS), lambda b, c: (b, c, 0)),
                  _const_spec((3, CHUNK, RET_QK)),
                  _const_spec((LORA, RWKV_WIDTH)),
                  _const_spec((LORA, RWKV_WIDTH))]
                 + [_const_spec((1, RWKV_WIDTH))] * len(vecs)
                 + [_const_spec((1, RET_WIDTH)),
                  _const_spec((GROUP, GROUP)),
                  pl.BlockSpec((BATCH_BLOCK, CHUNK, D), lambda b, c: (b, c, 0)),
                  _const_spec((D, D), single_buffer=True),
                  _const_spec((1, D))],
        out_specs=pl.BlockSpec((BATCH_BLOCK, CHUNK, D), lambda b, c: (b, c, 0)),
        out_shape=jax.ShapeDtypeStruct((B, T, D), f32),
        scratch_shapes=[pltpu.VMEM((BATCH_BLOCK, CHUNK, D), bf16),
                        pltpu.VMEM((BATCH_BLOCK, RWKV_WIDTH // GROUP, RWKV_HEAD, GROUP), f32),
                        pltpu.VMEM((BATCH_BLOCK, RET_HEADS, RET_DK, RET_DV), f32),
                        pltpu.VMEM((D, D), bf16),
                        pltpu.VMEM((2 * LORA, 2 * RWKV_WIDTH), bf16)],
        compiler_params=pltpu.CompilerParams(dimension_semantics=("arbitrary", "arbitrary"),
                                             vmem_limit_bytes=VMEM_LIMIT),
        name="mixers",
    )(p.reshape(B, T, IN_COLS), dec, w_lora_up[0], a_lora_up[0], *vecs,
      ret_gn_g[0][None, :], seg, x, w_out[0], final_norm_g[None, :])
```

```python
import functools

import numpy as np
import jax
import jax.numpy as jnp
from jax import lax
from jax.experimental import pallas as pl
from jax.experimental.pallas import tpu as pltpu

D_MODEL = 1024
CHUNK = 64
RET_HEADS = 4
RET_DK = 64
RET_DV = 128
RET_QK = RET_HEADS * RET_DK
RET_WIDTH = RET_HEADS * RET_DV
RWKV_WIDTH = 512
RWKV_HEAD = 64
LORA = 64
RET_COLS = 2 * RET_QK + 2 * RET_WIDTH
RWKV_COLS = 4 * RWKV_WIDTH + 2 * LORA
IN_COLS = RET_COLS + RWKV_COLS
ROPE_BASE = 10000.0
RMS_EPS = 1e-6
RET_GN_EPS = 1e-5
RWKV_GN_EPS = 64e-5
LANES = 128
GROUP = 256
PROJ_TILE = 512
BATCH_BLOCK = 8
RESULT_LAG = 4
VMEM_LIMIT = 56 * 1024 * 1024

f32 = jnp.float32
bf16 = jnp.bfloat16


def _mm(a, b):
    return jnp.dot(a.astype(bf16), b.astype(bf16), preferred_element_type=f32)


def _mm_nt(a, b):
    return lax.dot_general(a.astype(bf16), b.astype(bf16), (((1,), (1,)), ((), ())),
                           preferred_element_type=f32)


def _split2(x):
    hi = x.astype(bf16)
    lo = (x - hi.astype(f32)).astype(bf16)
    return hi, lo


def _block_diag(x):
    x = x.astype(bf16)
    rows, lanes = x.shape
    assert lanes in (2 * LANES, 4 * LANES)
    zero = jnp.zeros((rows, LANES), bf16)
    cols = []
    if lanes == 4 * LANES:
        for j in range(4):
            cols.append(jnp.concatenate([x[:, j * LANES:(j + 1) * LANES] if i == j else zero for i in range(4)],
                                        axis=0))
    else:
        low = lax.broadcasted_iota(jnp.int32, (rows, LANES), 1) < LANES // 2
        for j in range(2):
            xj = x[:, j * LANES:(j + 1) * LANES]
            pair = [jnp.where(low, xj, zero), jnp.where(low, zero, xj)]
            cols.append(jnp.concatenate([zero] * (2 * j) + pair + [zero] * (2 - 2 * j), axis=0))
    return jnp.concatenate(cols, axis=1)


def _sigmoid(x):
    return 1.0 / (1.0 + jnp.exp(-x))


def _in_proj_kernel(tiles_per_seq, x_ref, g_ref, w_ref, cos_ref, sin_ref, mu_ref, p_ref, wb_ref, carry_ref):
    i = pl.program_id(0)

    @pl.when(i == 0)
    def _():
        wb_ref[...] = w_ref[...].astype(bf16)

    x = x_ref[...]
    u = (x * lax.rsqrt(jnp.mean(x * x, axis=-1, keepdims=True) + RMS_EPS) * g_ref[...]).astype(bf16)
    rows = x.shape[0]
    seq_start = (i % tiles_per_seq) == 0
    W = RWKV_WIDTH

    def rope(pg):
        cos = jnp.concatenate([cos_ref[...]] * (pg.shape[1] // LANES), axis=1)
        sin = jnp.concatenate([sin_ref[...]] * (pg.shape[1] // LANES), axis=1)
        half = (lax.broadcasted_iota(jnp.int32, pg.shape, 1) & (RET_DK - 1)) < RET_DK // 2
        lanes = pg.shape[1]
        swapped = jnp.where(half, pltpu.roll(pg, lanes - RET_DK // 2, 1), pltpu.roll(pg, RET_DK // 2, 1))
        return pg * cos + swapped * sin

    def silu(pg):
        return pg * _sigmoid(pg)

    def shifted(lo, hi):
        def fn(pg):
            first = jnp.where(seq_start, 0.0, carry_ref[:, lo:hi])
            row0 = lax.broadcasted_iota(jnp.int32, pg.shape, 0) == 0
            prev = jnp.where(row0, first, pltpu.roll(pg, 1, 0))
            carry_ref[:, lo:hi] = pg[rows - 1:rows, :]
            return pg + mu_ref[:, lo:hi] * (prev - pg)
        return fn

    groups = [(0, 2 * RET_QK, rope),
              (2 * RET_QK, 2 * RET_QK + RET_WIDTH, lambda pg: pg),
              (2 * RET_QK + RET_WIDTH, RET_COLS, silu),
              (RET_COLS, RET_COLS + W, shifted(0, W)),
              (RET_COLS + W, RET_COLS + 2 * W, shifted(W, 2 * W)),
              (RET_COLS + 2 * W, RET_COLS + 3 * W, shifted(2 * W, 3 * W)),
              (RET_COLS + 3 * W, RET_COLS + 4 * W, lambda pg: silu(shifted(3 * W, 4 * W)(pg))),
              (RET_COLS + 4 * W, IN_COLS, shifted(4 * W, RWKV_COLS))]
    pending = None
    for lo, hi, fn in groups:
        acc = jnp.dot(u, wb_ref[:, lo:hi], preferred_element_type=f32)
        if pending is not None:
            p_ref[:, pending[1]:pending[2]] = pending[3](pending[0])
        pending = (acc, lo, hi, fn)
    p_ref[:, pending[1]:pending[2]] = pending[3](pending[0])


def _staged(n, produce, consume):
    for i in range(n + RESULT_LAG):
        if i < n:
            produce(i)
        if i >= RESULT_LAG:
            consume(i - RESULT_LAG)


def _retention_chunks(qs, ks, vs, states, decay4, qdec, kdec, head_decay):
    n = range(len(qs))
    hs = range(RET_HEADS)
    scores = [_mm_nt(qs[i], _block_diag(ks[i])) * decay4 for i in n]
    qds = [qs[i] * qdec for i in n]
    low = lax.broadcasted_iota(jnp.int32, (CHUNK, LANES), 1) < RET_DK

    def head_lhs(sc, qd, h):
        sl = slice((h // 2) * LANES, (h // 2 + 1) * LANES)
        if h % 2 == 0:
            return jnp.where(low, sc[:, sl], pltpu.roll(qd[:, sl], RET_DK, 1))
        return jnp.where(low, pltpu.roll(sc[:, sl], RET_DK, 1), qd[:, sl])

    out = [jnp.concatenate(
        [_mm(head_lhs(scores[i], qds[i], h),
             jnp.concatenate([vs[i][:, h * RET_DV:(h + 1) * RET_DV].astype(bf16), states[i][h].astype(bf16)],
                             axis=0)) for h in hs], axis=1) for i in n]
    kts = [(ks[i] * kdec).T for i in n]
    kv = [[_mm(kts[i][h * RET_DK:(h + 1) * RET_DK], vs[i][:, h * RET_DV:(h + 1) * RET_DV]) for h in hs]
          for i in n]
    s_new = [jnp.stack([states[i][h] * head_decay[h] + kv[i][h] for h in hs]) for i in n]
    return out, s_new


def _rwkv_chunks(rs, ks, vs, kkns, aas, lws, m_cs):
    C = CHUNK
    n = len(rs)
    heads = GROUP // RWKV_HEAD
    ti = lax.broadcasted_iota(jnp.int32, (C, C), 0)
    si = lax.broadcasted_iota(jnp.int32, (C, C), 1)
    tri_incl = jnp.where(si <= ti, 1.0, 0.0).astype(bf16)
    tri_incl = jnp.concatenate([tri_incl, tri_incl], axis=1)
    ti = lax.broadcasted_iota(jnp.int32, (C, GROUP), 0)
    si = lax.broadcasted_iota(jnp.int32, (C, GROUP), 1) & (C - 1)
    strict = si < ti
    incl = si <= ti
    eye = jnp.where(si == ti, 1.0, 0.0)
    low = lax.broadcasted_iota(jnp.int32, (RWKV_HEAD, LANES), 1) < RWKV_HEAD

    def slots():
        return [None] * n

    def staged(produce, consume):
        _staged(n, produce, consume)

    cum, lhs, rhs, decay_end, key_end = slots(), slots(), slots(), slots(), slots()

    def cum_matmul(i):
        cum[i] = jnp.dot(tri_incl, jnp.concatenate(_split2(lws[i]), axis=0), preferred_element_type=f32)

    def scale_operands(i):
        g_inc = jnp.exp(cum[i])
        g_inv = jnp.exp(-cum[i])
        g_end = jnp.exp(cum[i][C - 1:C, :] - cum[i])
        beta = kkns[i] * aas[i]
        lhs[i] = jnp.concatenate([-kkns[i] * jnp.exp(cum[i] - lws[i]), rs[i]() * g_inc], axis=0).astype(bf16)
        rhs[i] = jnp.concatenate([_block_diag(beta * g_inv), _block_diag(ks[i] * g_inv)], axis=0)
        decay_end[i] = g_inc[C - 1:C, :]
        key_end[i] = jnp.concatenate([beta * g_end, ks[i] * g_end], axis=0).astype(bf16)

    staged(cum_matmul, scale_operands)

    amat, mv, a_ab, a_k, a_rb, t_inv = slots(), slots(), slots(), slots(), slots(), slots()

    def score_matmuls(i):
        amat[i] = _mm_nt(lhs[i], rhs[i])
        mv[i] = _mm_nt(lhs[i], _block_diag(m_cs[i]))

    def mask_scores(i):
        lower = jnp.where(strict, amat[i][:C, :GROUP], 0.0)
        a_ab[i] = lower.astype(bf16)
        a_rb[i] = jnp.where(incl, amat[i][C:, :GROUP], 0.0).astype(bf16)
        a_k[i] = jnp.concatenate([jnp.where(strict, amat[i][:C, GROUP:], 0.0),
                                  jnp.where(incl, amat[i][C:, GROUP:], 0.0)], axis=0).astype(bf16)
        t_inv[i] = eye + lower

    staged(score_matmuls, mask_scores)

    av, st, st_lhs, pow_bd = slots(), slots(), slots(), slots()

    def first_square(i):
        av[i] = _mm(a_k[i], _block_diag(vs[i]()))
        st[i] = _mm(a_ab[i], _block_diag(a_ab[i]))

    def after_first_square(i):
        s_b = st[i].astype(bf16)
        st_lhs[i] = jnp.concatenate([s_b, t_inv[i].astype(bf16)], axis=0)
        pow_bd[i] = _block_diag(s_b)

    staged(first_square, after_first_square)

    def square(i):
        st[i] = _mm(st_lhs[i], pow_bd[i])

    def after_square(i):
        s_b = st[i][:C].astype(bf16)
        t_inv[i] = t_inv[i] + st[i][C:]
        st_lhs[i] = jnp.concatenate([s_b, t_inv[i].astype(bf16)], axis=0)
        pow_bd[i] = _block_diag(s_b)

    for _ in range(4):
        staged(square, after_square)

    t_b, y_bd = slots(), slots()

    def last_product(i):
        st[i] = _mm(st_lhs[i][C:], pow_bd[i])

    def after_last_product(i):
        t_b[i] = (t_inv[i] + st[i]).astype(bf16)
        y_bd[i] = _block_diag(mv[i][:C] + av[i][:C])

    staged(last_product, after_last_product)

    u, u_bd, uvt = slots(), slots(), slots()

    def solve(i):
        u[i] = _mm(t_b[i], y_bd[i])

    def after_solve(i):
        u_bd[i] = _block_diag(u[i])
        uvt[i] = jnp.concatenate([u[i], vs[i]()], axis=0).T.astype(bf16)

    staged(solve, after_solve)

    o_u, upd, out, m_new = slots(), slots(), slots(), slots()

    def output_and_update(i):
        o_u[i] = _mm(a_rb[i], u_bd[i])
        upd[i] = [_mm(uvt[i][h * RWKV_HEAD:(h + 1) * RWKV_HEAD],
                      key_end[i][:, (h // 2) * LANES:(h // 2 + 1) * LANES]) for h in range(heads)]

    def finish(i):
        out[i] = mv[i][C:] + av[i][C:] + o_u[i]
        m_new[i] = m_cs[i] * decay_end[i] + jnp.concatenate(
            [jnp.where(low, upd[i][2 * j], upd[i][2 * j + 1]) for j in range(heads // 2)], axis=1)

    staged(output_and_update, finish)
    return out, m_new


def _mix_chunk(p_ref, dec_ref, lora_ref, vec_refs, rgn_ref, seg_ref, y_ref, m_ref, s_ref):
    C = CHUNK
    nb = BATCH_BLOCK
    R = nb * C
    W = RWKV_WIDTH
    NG = W // GROUP

    def rows(x, bi):
        return x[bi * C:(bi + 1) * C]

    def rwkv_cols(lo, hi):
        return p_ref[:, :, RET_COLS + lo:RET_COLS + hi].reshape(R, hi - lo)

    kr = rwkv_cols(W, 2 * W)
    xwa = rwkv_cols(4 * W, RWKV_COLS)
    lane = lax.broadcasted_iota(jnp.int32, xwa.shape, 1)
    lora = _mm(jnp.where(lane < LORA, jnp.tanh(xwa), xwa), lora_ref[...])
    w0, a0, k_k, k_a, r_k, gn_g, gn_b = (ref[...] for ref in vec_refs)
    lw = -np.float32(np.exp(-0.5)) * _sigmoid(w0 + lora[:, :W])
    a = _sigmoid(a0 + lora[:, W:])
    seg = seg_ref[...]

    def segsum(x):
        xs = jnp.concatenate([x[:, g * GROUP:(g + 1) * GROUP] for g in range(NG)], axis=0)
        tot = jnp.dot(xs.astype(bf16), seg, preferred_element_type=f32)
        return jnp.concatenate([tot[g * R:(g + 1) * R] for g in range(NG)], axis=1)

    kk = kr * k_k
    kkn = kk * lax.rsqrt(jnp.maximum(segsum(kk * kk), 1e-24))
    kmod = kr * (1.0 + (a - 1.0) * k_a)

    q = p_ref[:, :, 0:RET_QK].reshape(R, RET_QK)
    k = p_ref[:, :, RET_QK:2 * RET_QK].reshape(R, RET_QK)
    v = p_ref[:, :, 2 * RET_QK:2 * RET_QK + RET_WIDTH].reshape(R, RET_WIDTH)
    rets, s_new = _retention_chunks([rows(q, bi) for bi in range(nb)], [rows(k, bi) for bi in range(nb)],
                                    [rows(v, bi) for bi in range(nb)], [s_ref[bi] for bi in range(nb)],
                                    dec_ref[0], dec_ref[1], dec_ref[2], _RET_CHUNK_DECAY)
    for bi in range(nb):
        s_ref[bi] = s_new[bi]
    ret = jnp.concatenate(rets, axis=0)
    rgn = rgn_ref[...]
    for h in range(RET_HEADS):
        sl = slice(h * RET_DV, (h + 1) * RET_DV)
        xh = ret[:, sl]
        d = xh - jnp.mean(xh, axis=-1, keepdims=True)
        var = jnp.mean(d * d, axis=-1, keepdims=True)
        gh = p_ref[:, :, 2 * RET_QK + RET_WIDTH + h * RET_DV:2 * RET_QK + RET_WIDTH + (h + 1) * RET_DV]
        gh = gh.reshape(R, RET_DV)
        yh = gh * (d * lax.rsqrt(var + RET_GN_EPS) * rgn[:, sl])
        y_ref[:, :, sl] = yh.reshape(nb, C, RET_DV).astype(y_ref.dtype)

    chains = [(bi, g) for bi in range(nb) for g in range(NG)]

    def pick(x):
        return [x[bi * C:(bi + 1) * C, g * GROUP:(g + 1) * GROUP] for bi, g in chains]

    def readers(lo):
        def reader(bi, g):
            start = RET_COLS + lo + g * GROUP
            return lambda: p_ref[bi, :, start:start + GROUP]
        return [reader(bi, g) for bi, g in chains]

    outs, m_new = _rwkv_chunks(readers(0), pick(kmod), readers(2 * W), pick(kkn), pick(a), pick(lw),
                               [m_ref[bi, g] for bi, g in chains])
    for i, (bi, g) in enumerate(chains):
        m_ref[bi, g] = m_new[i]
    o = jnp.concatenate([jnp.concatenate(outs[bi * NG:(bi + 1) * NG], axis=1) for bi in range(nb)], axis=0)
    d = o - segsum(o) * (1.0 / RWKV_HEAD)
    var = segsum(d * d) * (1.0 / RWKV_HEAD)
    o = d * lax.rsqrt(var + RWKV_GN_EPS) * gn_g + gn_b
    bonus = segsum(rwkv_cols(0, W) * kmod * r_k) * rwkv_cols(2 * W, 3 * W)
    y_rw = rwkv_cols(3 * W, 4 * W) * (o + bonus)
    y_ref[:, :, RET_WIDTH:] = y_rw.reshape(nb, C, W).astype(y_ref.dtype)


def _project_chunk(x_ref, y_ref, w_ref, g_ref, o_ref):
    nb, C, D = x_ref.shape
    y = y_ref[...].reshape(nb * C, D)
    h = x_ref[...].reshape(nb * C, D) + jnp.dot(y, w_ref[...], preferred_element_type=f32)
    out = h * lax.rsqrt(jnp.mean(h * h, axis=-1, keepdims=True) + RMS_EPS) * g_ref[...]
    o_ref[...] = out.reshape(nb, C, D)


def _mixer_kernel(p_ref, dec_ref, wlora_ref, alora_ref, w0_ref, a0_ref, kk_ref, ka_ref, rk_ref, gng_ref, gnb_ref,
                  rgn_ref, seg_ref, x_ref, wout_ref, fg_ref, o_ref, y_ref, m_ref, s_ref, woutb_ref, lora_ref):
    @pl.when(pl.program_id(1) == 0)
    def _():
        m_ref[...] = jnp.zeros_like(m_ref)
        s_ref[...] = jnp.zeros_like(s_ref)
        woutb_ref[...] = wout_ref[...].astype(bf16)
        zero = jnp.zeros((LORA, RWKV_WIDTH), bf16)
        lora_ref[...] = jnp.concatenate(
            [jnp.concatenate([wlora_ref[...].astype(bf16), zero], axis=1),
             jnp.concatenate([zero, alora_ref[...].astype(bf16)], axis=1)], axis=0)

    vec_refs = (w0_ref, a0_ref, kk_ref, ka_ref, rk_ref, gng_ref, gnb_ref)
    _mix_chunk(p_ref, dec_ref, lora_ref, vec_refs, rgn_ref, seg_ref, y_ref, m_ref, s_ref)
    _project_chunk(x_ref, y_ref, woutb_ref, fg_ref, o_ref)


def _rope_tables(seq):
    half = RET_DK // 2
    expo = -jnp.arange(half, dtype=f32) / f32(half)
    freqs = jnp.exp(expo * f32(np.log(ROPE_BASE)))
    ang = jnp.arange(seq, dtype=jnp.int32).astype(f32)[:, None] * freqs[None, :]
    cos = jnp.cos(ang)
    sin = jnp.sin(ang)
    cos_full = jnp.tile(jnp.concatenate([cos, cos], axis=1), (1, LANES // RET_DK))
    sin_signed = jnp.tile(jnp.concatenate([-sin, sin], axis=1), (1, LANES // RET_DK))
    return cos_full, sin_signed


_RET_LOG_GAMMA = np.log(1.0 - np.exp2(-5.0 - np.arange(RET_HEADS, dtype=np.float64)))
_RET_CHUNK_DECAY = tuple(np.float32(v) for v in np.exp(_RET_LOG_GAMMA * CHUNK))


def _retention_constants():
    C = CHUNK
    lg = _RET_LOG_GAMMA
    lane_lg = np.repeat(lg, RET_DK)[None, :]
    n = np.arange(C, dtype=np.float64)[:, None]
    m = np.tile(np.arange(C, dtype=np.float64), RET_HEADS)[None, :]
    scale = RET_DK ** -0.5
    decay4 = scale * np.exp(lane_lg * np.abs(n - m))
    qdec = np.exp(lane_lg * (n + 1.0)) * np.ones((1, RET_QK))
    kdec = scale * np.exp(lane_lg * (C - 1.0 - n)) * np.ones((1, RET_QK))
    return jnp.asarray(np.stack([decay4, qdec, kdec]).astype(np.float32))


def _const_spec(shape, single_buffer=False):
    mode = pl.Buffered(1) if single_buffer else None
    return pl.BlockSpec(shape, lambda *_: (0,) * len(shape), pipeline_mode=mode)


@jax.jit
def kernel(x, norm_g, w_in, ret_gn_g, rwkv_mu, w_lora_up, w0, a_lora_up, a0, k_k, k_a, r_k,
           rwkv_gn_g, rwkv_gn_b, w_out, final_norm_g):
    B, T, D = x.shape
    assert D == D_MODEL and T % CHUNK == 0 and T % PROJ_TILE == 0 and B % BATCH_BLOCK == 0
    assert norm_g.shape[0] == 1, "single-layer block"
    n_tok = B * T
    xf = x.reshape(n_tok, D)
    params = pltpu.CompilerParams(dimension_semantics=("arbitrary",), vmem_limit_bytes=VMEM_LIMIT)
    tiles_per_seq = T // PROJ_TILE
    cos, sin = _rope_tables(T)

    p = pl.pallas_call(
        functools.partial(_in_proj_kernel, tiles_per_seq),
        grid=(n_tok // PROJ_TILE,),
        in_specs=[pl.BlockSpec((PROJ_TILE, D), lambda i: (i, 0)),
                  _const_spec((1, D)),
                  _const_spec((D, IN_COLS), single_buffer=True),
                  pl.BlockSpec((PROJ_TILE, LANES), lambda i: (i % tiles_per_seq, 0)),
                  pl.BlockSpec((PROJ_TILE, LANES), lambda i: (i % tiles_per_seq, 0)),
                  _const_spec((1, RWKV_COLS))],
        out_specs=pl.BlockSpec((PROJ_TILE, IN_COLS), lambda i: (i, 0)),
        out_shape=jax.ShapeDtypeStruct((n_tok, IN_COLS), f32),
        scratch_shapes=[pltpu.VMEM((D, IN_COLS), bf16),
                        pltpu.VMEM((1, RWKV_COLS), f32)],
        compiler_params=params,
        name="in_proj",
    )(xf, norm_g[0][None, :], w_in[0], cos, sin, rwkv_mu[0][None, :])

    dec = _retention_constants()
    vecs = [v.reshape(1, RWKV_WIDTH) for v in (w0, a0, k_k, k_a, r_k, rwkv_gn_g, rwkv_gn_b)]
    seg_ids = np.arange(GROUP) // RWKV_HEAD
    seg = jnp.asarray(seg_ids[:, None] == seg_ids[None, :], bf16)

    return pl.pallas_call(
        _mixer_kernel,
        grid=(B // BATCH_BLOCK, T // CHUNK),
        in_specs=[pl.BlockSpec((BATCH_BLOCK, CHUNK, IN_COLS), lambda b, c: (b, c, 0)),
                  _const_spec((3, CHUNK, RET_QK)),
                  _const_spec((LORA, RWKV_WIDTH)),
                  _const_spec((LORA, RWKV_WIDTH))]
                 + [_const_spec((1, RWKV_WIDTH))] * len(vecs)
                 + [_const_spec((1, RET_WIDTH)),
                  _const_spec((GROUP, GROUP)),
                  pl.BlockSpec((BATCH_BLOCK, CHUNK, D), lambda b, c: (b, c, 0)),
                  _const_spec((D, D), single_buffer=True),
                  _const_spec((1, D))],
        out_specs=pl.BlockSpec((BATCH_BLOCK, CHUNK, D), lambda b, c: (b, c, 0)),
        out_shape=jax.ShapeDtypeStruct((B, T, D), f32),
        scratch_shapes=[pltpu.VMEM((BATCH_BLOCK, CHUNK, D), bf16),
                        pltpu.VMEM((BATCH_BLOCK, RWKV_WIDTH // GROUP, RWKV_HEAD, GROUP), f32),
                        pltpu.VMEM((BATCH_BLOCK, RET_HEADS, RET_DK, RET_DV), f32),
                        pltpu.VMEM((D, D), bf16),
                        pltpu.VMEM((2 * LORA, 2 * RWKV_WIDTH), bf16)],
        compiler_params=pltpu.CompilerParams(dimension_semantics=("arbitrary", "arbitrary"),
                                             vmem_limit_bytes=VMEM_LIMIT),
        name="mixers",
    )(p.reshape(B, T, IN_COLS), dec, w_lora_up[0], a_lora_up[0], *vecs,
      ret_gn_g[0][None, :], seg, x, w_out[0], final_norm_g[None, :])
```

```python
import functools

import numpy as np
import jax
import jax.numpy as jnp
from jax import lax
from jax.experimental import pallas as pl
from jax.experimental.pallas import tpu as pltpu

D_MODEL = 1024
CHUNK = 64
RET_HEADS = 4
RET_DK = 64
RET_DV = 128
RET_QK = RET_HEADS * RET_DK
RET_WIDTH = RET_HEADS * RET_DV
RWKV_WIDTH = 512
RWKV_HEAD = 64
LORA = 64
RET_COLS = 2 * RET_QK + 2 * RET_WIDTH
RWKV_COLS = 4 * RWKV_WIDTH + 2 * LORA
IN_COLS = RET_COLS + RWKV_COLS
ROPE_BASE = 10000.0
RMS_EPS = 1e-6
RET_GN_EPS = 1e-5
RWKV_GN_EPS = 64e-5
LANES = 128
GROUP = 256
PROJ_TILE = 512
BATCH_BLOCK = 8
RESULT_LAG = 4
VMEM_LIMIT = 56 * 1024 * 1024

f32 = jnp.float32
bf16 = jnp.bfloat16


def _mm(a, b):
    return jnp.dot(a.astype(bf16), b.astype(bf16), preferred_element_type=f32)


def _mm_nt(a, b):
    return lax.dot_general(a.astype(bf16), b.astype(bf16), (((1,), (1,)), ((), ())),
                           preferred_element_type=f32)


def _split2(x):
    hi = x.astype(bf16)
    lo = (x - hi.astype(f32)).astype(bf16)
    return hi, lo


def _block_diag(x):
    x = x.astype(bf16)
    rows, lanes = x.shape
    assert lanes in (2 * LANES, 4 * LANES)
    zero = jnp.zeros((rows, LANES), bf16)
    cols = []
    if lanes == 4 * LANES:
        for j in range(4):
            cols.append(jnp.concatenate([x[:, j * LANES:(j + 1) * LANES] if i == j else zero for i in range(4)],
                                        axis=0))
    else:
        low = lax.broadcasted_iota(jnp.int32, (rows, LANES), 1) < LANES // 2
        for j in range(2):
            xj = x[:, j * LANES:(j + 1) * LANES]
            pair = [jnp.where(low, xj, zero), jnp.where(low, zero, xj)]
            cols.append(jnp.concatenate([zero] * (2 * j) + pair + [zero] * (2 - 2 * j), axis=0))
    return jnp.concatenate(cols, axis=1)


def _sigmoid(x):
    return 1.0 / (1.0 + jnp.exp(-x))


def _in_proj_kernel(tiles_per_seq, x_ref, g_ref, w_ref, cos_ref, sin_ref, mu_ref, p_ref, wb_ref, carry_ref):
    i = pl.program_id(0)

    @pl.when(i == 0)
    def _():
        wb_ref[...] = w_ref[...].astype(bf16)

    x = x_ref[...]
    u = (x * lax.rsqrt(jnp.mean(x * x, axis=-1, keepdims=True) + RMS_EPS) * g_ref[...]).astype(bf16)
    rows = x.shape[0]
    seq_start = (i % tiles_per_seq) == 0
    W = RWKV_WIDTH

    def rope(pg):
        cos = jnp.concatenate([cos_ref[...]] * (pg.shape[1] // LANES), axis=1)
        sin = jnp.concatenate([sin_ref[...]] * (pg.shape[1] // LANES), axis=1)
        half = (lax.broadcasted_iota(jnp.int32, pg.shape, 1) & (RET_DK - 1)) < RET_DK // 2
        lanes = pg.shape[1]
        swapped = jnp.where(half, pltpu.roll(pg, lanes - RET_DK // 2, 1), pltpu.roll(pg, RET_DK // 2, 1))
        return pg * cos + swapped * sin

    def silu(pg):
        return pg * _sigmoid(pg)

    def shifted(lo, hi):
        def fn(pg):
            first = jnp.where(seq_start, 0.0, carry_ref[:, lo:hi])
            row0 = lax.broadcasted_iota(jnp.int32, pg.shape, 0) == 0
            prev = jnp.where(row0, first, pltpu.roll(pg, 1, 0))
            carry_ref[:, lo:hi] = pg[rows - 1:rows, :]
            return pg + mu_ref[:, lo:hi] * (prev - pg)
        return fn

    groups = [(0, 2 * RET_QK, rope),
              (2 * RET_QK, 2 * RET_QK + RET_WIDTH, lambda pg: pg),
              (2 * RET_QK + RET_WIDTH, RET_COLS, silu),
              (RET_COLS, RET_COLS + W, shifted(0, W)),
              (RET_COLS + W, RET_COLS + 2 * W, shifted(W, 2 * W)),
              (RET_COLS + 2 * W, RET_COLS + 3 * W, shifted(2 * W, 3 * W)),
              (RET_COLS + 3 * W, RET_COLS + 4 * W, lambda pg: silu(shifted(3 * W, 4 * W)(pg))),
              (RET_COLS + 4 * W, IN_COLS, shifted(4 * W, RWKV_COLS))]
    pending = None
    for lo, hi, fn in groups:
        acc = jnp.dot(u, wb_ref[:, lo:hi], preferred_element_type=f32)
        if pending is not None:
            p_ref[:, pending[1]:pending[2]] = pending[3](pending[0])
        pending = (acc, lo, hi, fn)
    p_ref[:, pending[1]:pending[2]] = pending[3](pending[0])


def _staged(n, produce, consume):
    for i in range(n + RESULT_LAG):
        if i < n:
            produce(i)
        if i >= RESULT_LAG:
            consume(i - RESULT_LAG)


def _retention_chunks(qs, ks, vs, states, decay4, qdec, kdec, head_decay):
    n = range(len(qs))
    hs = range(RET_HEADS)
    scores = [_mm_nt(qs[i], _block_diag(ks[i])) * decay4 for i in n]
    qds = [qs[i] * qdec for i in n]
    low = lax.broadcasted_iota(jnp.int32, (CHUNK, LANES), 1) < RET_DK

    def head_lhs(sc, qd, h):
        sl = slice((h // 2) * LANES, (h // 2 + 1) * LANES)
        if h % 2 == 0:
            return jnp.where(low, sc[:, sl], pltpu.roll(qd[:, sl], RET_DK, 1))
        return jnp.where(low, pltpu.roll(sc[:, sl], RET_DK, 1), qd[:, sl])

    out = [jnp.concatenate(
        [_mm(head_lhs(scores[i], qds[i], h),
             jnp.concatenate([vs[i][:, h * RET_DV:(h + 1) * RET_DV].astype(bf16), states[i][h].astype(bf16)],
                             axis=0)) for h in hs], axis=1) for i in n]
    kts = [(ks[i] * kdec).T for i in n]
    kv = [[_mm(kts[i][h * RET_DK:(h + 1) * RET_DK], vs[i][:, h * RET_DV:(h + 1) * RET_DV]) for h in hs]
          for i in n]
    s_new = [jnp.stack([states[i][h] * head_decay[h] + kv[i][h] for h in hs]) for i in n]
    return out, s_new


def _rwkv_chunks(rs, ks, vs, kkns, aas, lws, m_cs):
    C = CHUNK
    n = len(rs)
    heads = GROUP // RWKV_HEAD
    ti = lax.broadcasted_iota(jnp.int32, (C, C), 0)
    si = lax.broadcasted_iota(jnp.int32, (C, C), 1)
    tri_incl = jnp.where(si <= ti, 1.0, 0.0).astype(bf16)
    tri_incl = jnp.concatenate([tri_incl, tri_incl], axis=1)
    ti = lax.broadcasted_iota(jnp.int32, (C, GROUP), 0)
    si = lax.broadcasted_iota(jnp.int32, (C, GROUP), 1) & (C - 1)
    strict = si < ti
    incl = si <= ti
    eye = jnp.where(si == ti, 1.0, 0.0)
    low = lax.broadcasted_iota(jnp.int32, (RWKV_HEAD, LANES), 1) < RWKV_HEAD

    def slots():
        return [None] * n

    def staged(produce, consume):
        _staged(n, produce, consume)

    cum, lhs, rhs, decay_end, key_end = slots(), slots(), slots(), slots(), slots()

    def cum_matmul(i):
        cum[i] = jnp.dot(tri_incl, jnp.concatenate(_split2(lws[i]), axis=0), preferred_element_type=f32)

    def scale_operands(i):
        g_inc = jnp.exp(cum[i])
        g_inv = jnp.exp(-cum[i])
        g_end = jnp.exp(cum[i][C - 1:C, :] - cum[i])
        beta = kkns[i] * aas[i]
        lhs[i] = jnp.concatenate([-kkns[i] * jnp.exp(cum[i] - lws[i]), rs[i]() * g_inc], axis=0).astype(bf16)
        rhs[i] = jnp.concatenate([_block_diag(beta * g_inv), _block_diag(ks[i] * g_inv)], axis=0)
        decay_end[i] = g_inc[C - 1:C, :]
        key_end[i] = jnp.concatenate([beta * g_end, ks[i] * g_end], axis=0).astype(bf16)

    staged(cum_matmul, scale_operands)

    amat, mv, a_ab, a_k, a_rb, t_inv = slots(), slots(), slots(), slots(), slots(), slots()

    def score_matmuls(i):
        amat[i] = _mm_nt(lhs[i], rhs[i])

    def mask_scores(i):
        lower = jnp.where(strict, amat[i][:C, :GROUP], 0.0)
        a_ab[i] = lower.astype(bf16)
        a_rb[i] = jnp.where(incl, amat[i][C:, :GROUP], 0.0).astype(bf16)
        a_k[i] = jnp.concatenate([jnp.where(strict, amat[i][:C, GROUP:], 0.0),
                                  jnp.where(incl, amat[i][C:, GROUP:], 0.0)], axis=0).astype(bf16)
        t_inv[i] = eye + lower

    staged(score_matmuls, mask_scores)

    av, st, st_lhs, pow_bd = slots(), slots(), slots(), slots()

    def first_square(i):
        st[i] = _mm(a_ab[i], _block_diag(a_ab[i]))

    def after_first_square(i):
        s_b = st[i].astype(bf16)
        st_lhs[i] = jnp.concatenate([s_b, t_inv[i].astype(bf16)], axis=0)
        pow_bd[i] = _block_diag(s_b)

    staged(first_square, after_first_square)

    def square(i):
        st[i] = _mm(st_lhs[i], pow_bd[i])

    def after_square(i):
        s_b = st[i][:C].astype(bf16)
        t_inv[i] = t_inv[i] + st[i][C:]
        st_lhs[i] = jnp.concatenate([s_b, t_inv[i].astype(bf16)], axis=0)
        pow_bd[i] = _block_diag(s_b)

    for _ in range(4):
        staged(square, after_square)

    t_b, y_bd = slots(), slots()

    def last_product(i):
        st[i] = _mm(st_lhs[i][C:], pow_bd[i])
        mv[i] = _mm_nt(lhs[i], _block_diag(m_cs[i]))
        av[i] = _mm(a_k[i], _block_diag(vs[i]()))

    def after_last_product(i):
        t_b[i] = (t_inv[i] + st[i]).astype(bf16)
        y_bd[i] = _block_diag(mv[i][:C] + av[i][:C])

    staged(last_product, after_last_product)

    u, u_bd, uvt = slots(), slots(), slots()

    def solve(i):
        u[i] = _mm(t_b[i], y_bd[i])

    def after_solve(i):
        u_bd[i] = _block_diag(u[i])
        uvt[i] = jnp.concatenate([u[i], vs[i]()], axis=0).T.astype(bf16)

    staged(solve, after_solve)

    o_u, upd, out, m_new = slots(), slots(), slots(), slots()

    def output_and_update(i):
        o_u[i] = _mm(a_rb[i], u_bd[i])
        upd[i] = [_mm(uvt[i][h * RWKV_HEAD:(h + 1) * RWKV_HEAD],
                      key_end[i][:, (h // 2) * LANES:(h // 2 + 1) * LANES]) for h in range(heads)]

    def finish(i):
        out[i] = mv[i][C:] + av[i][C:] + o_u[i]
        m_new[i] = m_cs[i] * decay_end[i] + jnp.concatenate(
            [jnp.where(low, upd[i][2 * j], upd[i][2 * j + 1]) for j in range(heads // 2)], axis=1)

    staged(output_and_update, finish)
    return out, m_new


def _mix_chunk(p_ref, dec_ref, lora_ref, vec_refs, rgn_ref, seg_ref, y_ref, m_ref, s_ref):
    C = CHUNK
    nb = BATCH_BLOCK
    R = nb * C
    W = RWKV_WIDTH
    NG = W // GROUP

    def rows(x, bi):
        return x[bi * C:(bi + 1) * C]

    def rwkv_cols(lo, hi):
        return p_ref[:, :, RET_COLS + lo:RET_COLS + hi].reshape(R, hi - lo)

    kr = rwkv_cols(W, 2 * W)
    xwa = rwkv_cols(4 * W, RWKV_COLS)
    lane = lax.broadcasted_iota(jnp.int32, xwa.shape, 1)
    lora = _mm(jnp.where(lane < LORA, jnp.tanh(xwa), xwa), lora_ref[...])
    w0, a0, k_k, k_a, r_k, gn_g, gn_b = (ref[...] for ref in vec_refs)
    lw = -np.float32(np.exp(-0.5)) * _sigmoid(w0 + lora[:, :W])
    a = _sigmoid(a0 + lora[:, W:])
    seg = seg_ref[...]

    def segsum(x):
        xs = jnp.concatenate([x[:, g * GROUP:(g + 1) * GROUP] for g in range(NG)], axis=0)
        tot = jnp.dot(xs.astype(bf16), seg, preferred_element_type=f32)
        return jnp.concatenate([tot[g * R:(g + 1) * R] for g in range(NG)], axis=1)

    kk = kr * k_k
    kkn = kk * lax.rsqrt(jnp.maximum(segsum(kk * kk), 1e-24))
    kmod = kr * (1.0 + (a - 1.0) * k_a)

    q = p_ref[:, :, 0:RET_QK].reshape(R, RET_QK)
    k = p_ref[:, :, RET_QK:2 * RET_QK].reshape(R, RET_QK)
    v = p_ref[:, :, 2 * RET_QK:2 * RET_QK + RET_WIDTH].reshape(R, RET_WIDTH)
    rets, s_new = _retention_chunks([rows(q, bi) for bi in range(nb)], [rows(k, bi) for bi in range(nb)],
                                    [rows(v, bi) for bi in range(nb)], [s_ref[bi] for bi in range(nb)],
                                    dec_ref[0], dec_ref[1], dec_ref[2], _RET_CHUNK_DECAY)
    for bi in range(nb):
        s_ref[bi] = s_new[bi]
    ret = jnp.concatenate(rets, axis=0)
    rgn = rgn_ref[...]
    for h in range(RET_HEADS):
        sl = slice(h * RET_DV, (h + 1) * RET_DV)
        xh = ret[:, sl]
        d = xh - jnp.mean(xh, axis=-1, keepdims=True)
        var = jnp.mean(d * d, axis=-1, keepdims=True)
        gh = p_ref[:, :, 2 * RET_QK + RET_WIDTH + h * RET_DV:2 * RET_QK + RET_WIDTH + (h + 1) * RET_DV]
        gh = gh.reshape(R, RET_DV)
        yh = gh * (d * lax.rsqrt(var + RET_GN_EPS) * rgn[:, sl])
        y_ref[:, :, sl] = yh.reshape(nb, C, RET_DV).astype(y_ref.dtype)

    chains = [(bi, g) for bi in range(nb) for g in range(NG)]

    def pick(x):
        return [x[bi * C:(bi + 1) * C, g * GROUP:(g + 1) * GROUP] for bi, g in chains]

    def readers(lo):
        def reader(bi, g):
            start = RET_COLS + lo + g * GROUP
            return lambda: p_ref[bi, :, start:start + GROUP]
        return [reader(bi, g) for bi, g in chains]

    outs, m_new = _rwkv_chunks(readers(0), pick(kmod), readers(2 * W), pick(kkn), pick(a), pick(lw),
                               [m_ref[bi, g] for bi, g in chains])
    for i, (bi, g) in enumerate(chains):
        m_ref[bi, g] = m_new[i]
    o = jnp.concatenate([jnp.concatenate(outs[bi * NG:(bi + 1) * NG], axis=1) for bi in range(nb)], axis=0)
    d = o - segsum(o) * (1.0 / RWKV_HEAD)
    var = segsum(d * d) * (1.0 / RWKV_HEAD)
    o = d * lax.rsqrt(var + RWKV_GN_EPS) * gn_g + gn_b
    bonus = segsum(rwkv_cols(0, W) * kmod * r_k) * rwkv_cols(2 * W, 3 * W)
    y_rw = rwkv_cols(3 * W, 4 * W) * (o + bonus)
    y_ref[:, :, RET_WIDTH:] = y_rw.reshape(nb, C, W).astype(y_ref.dtype)


def _project_chunk(x_ref, y_ref, w_ref, g_ref, o_ref):
    nb, C, D = x_ref.shape
    y = y_ref[...].reshape(nb * C, D)
    h = x_ref[...].reshape(nb * C, D) + jnp.dot(y, w_ref[...], preferred_element_type=f32)
    out = h * lax.rsqrt(jnp.mean(h * h, axis=-1, keepdims=True) + RMS_EPS) * g_ref[...]
    o_ref[...] = out.reshape(nb, C, D)


def _mixer_kernel(p_ref, dec_ref, wlora_ref, alora_ref, w0_ref, a0_ref, kk_ref, ka_ref, rk_ref, gng_ref, gnb_ref,
                  rgn_ref, seg_ref, x_ref, wout_ref, fg_ref, o_ref, y_ref, m_ref, s_ref, woutb_ref, lora_ref):
    @pl.when(pl.program_id(1) == 0)
    def _():
        m_ref[...] = jnp.zeros_like(m_ref)
        s_ref[...] = jnp.zeros_like(s_ref)
        woutb_ref[...] = wout_ref[...].astype(bf16)
        zero = jnp.zeros((LORA, RWKV_WIDTH), bf16)
        lora_ref[...] = jnp.concatenate(
            [jnp.concatenate([wlora_ref[...].astype(bf16), zero], axis=1),
             jnp.concatenate([zero, alora_ref[...].astype(bf16)], axis=1)], axis=0)

    vec_refs = (w0_ref, a0_ref, kk_ref, ka_ref, rk_ref, gng_ref, gnb_ref)
    _mix_chunk(p_ref, dec_ref, lora_ref, vec_refs, rgn_ref, seg_ref, y_ref, m_ref, s_ref)
    _project_chunk(x_ref, y_ref, woutb_ref, fg_ref, o_ref)


def _rope_tables(seq):
    half = RET_DK // 2
    expo = -jnp.arange(half, dtype=f32) / f32(half)
    freqs = jnp.exp(expo * f32(np.log(ROPE_BASE)))
    ang = jnp.arange(seq, dtype=jnp.int32).astype(f32)[:, None] * freqs[None, :]
    cos = jnp.cos(ang)
    sin = jnp.sin(ang)
    cos_full = jnp.tile(jnp.concatenate([cos, cos], axis=1), (1, LANES // RET_DK))
    sin_signed = jnp.tile(jnp.concatenate([-sin, sin], axis=1), (1, LANES // RET_DK))
    return cos_full, sin_signed


_RET_LOG_GAMMA = np.log(1.0 - np.exp2(-5.0 - np.arange(RET_HEADS, dtype=np.float64)))
_RET_CHUNK_DECAY = tuple(np.float32(v) for v in np.exp(_RET_LOG_GAMMA * CHUNK))


def _retention_constants():
    C = CHUNK
    lg = _RET_LOG_GAMMA
    lane_lg = np.repeat(lg, RET_DK)[None, :]
    n = np.arange(C, dtype=np.float64)[:, None]
    m = np.tile(np.arange(C, dtype=np.float64), RET_HEADS)[None, :]
    scale = RET_DK ** -0.5
    decay4 = scale * np.exp(lane_lg * np.abs(n - m))
    qdec = np.exp(lane_lg * (n + 1.0)) * np.ones((1, RET_QK))
    kdec = scale * np.exp(lane_lg * (C - 1.0 - n)) * np.ones((1, RET_QK))
    return jnp.asarray(np.stack([decay4, qdec, kdec]).astype(np.float32))


def _const_spec(shape, single_buffer=False):
    mode = pl.Buffered(1) if single_buffer else None
    return pl.BlockSpec(shape, lambda *_: (0,) * len(shape), pipeline_mode=mode)


@jax.jit
def kernel(x, norm_g, w_in, ret_gn_g, rwkv_mu, w_lora_up, w0, a_lora_up, a0, k_k, k_a, r_k,
           rwkv_gn_g, rwkv_gn_b, w_out, final_norm_g):
    B, T, D = x.shape
    assert D == D_MODEL and T % CHUNK == 0 and T % PROJ_TILE == 0 and B % BATCH_BLOCK == 0
    assert norm_g.shape[0] == 1, "single-layer block"
    n_tok = B * T
    xf = x.reshape(n_tok, D)
    params = pltpu.CompilerParams(dimension_semantics=("arbitrary",), vmem_limit_bytes=VMEM_LIMIT)
    tiles_per_seq = T // PROJ_TILE
    cos, sin = _rope_tables(T)

    p = pl.pallas_call(
        functools.partial(_in_proj_kernel, tiles_per_seq),
        grid=(n_tok // PROJ_TILE,),
        in_specs=[pl.BlockSpec((PROJ_TILE, D), lambda i: (i, 0)),
                  _const_spec((1, D)),
                  _const_spec((D, IN_COLS), single_buffer=True),
                  pl.BlockSpec((PROJ_TILE, LANES), lambda i: (i % tiles_per_seq, 0)),
                  pl.BlockSpec((PROJ_TILE, LANES), lambda i: (i % tiles_per_seq, 0)),
                  _const_spec((1, RWKV_COLS))],
        out_specs=pl.BlockSpec((PROJ_TILE, IN_COLS), lambda i: (i, 0)),
        out_shape=jax.ShapeDtypeStruct((n_tok, IN_COLS), f32),
        scratch_shapes=[pltpu.VMEM((D, IN_COLS), bf16),
                        pltpu.VMEM((1, RWKV_COLS), f32)],
        compiler_params=params,
        name="in_proj",
    )(xf, norm_g[0][None, :], w_in[0], cos, sin, rwkv_mu[0][None, :])

    dec = _retention_constants()
    vecs = [v.reshape(1, RWKV_WIDTH) for v in (w0, a0, k_k, k_a, r_k, rwkv_gn_g, rwkv_gn_b)]
    seg_ids = np.arange(GROUP) // RWKV_HEAD
    seg = jnp.asarray(seg_ids[:, None] == seg_ids[None, :], bf16)

    return pl.pallas_call(
        _mixer_kernel,
        grid=(B // BATCH_BLOCK, T // CHUNK),
        in_specs=[pl.BlockSpec((BATCH_BLOCK, CHUNK, IN_COLS), lambda b, c: (b, c, 0)),
                  _const_spec((3, CHUNK, RET_QK)),
                  _const_spec((LORA, RWKV_WIDTH)),
                  _const_spec((LORA, RWKV_WIDTH))]
                 + [_const_spec((1, RWKV_WIDTH))] * len(vecs)
                 + [_const_spec((1, RET_WIDTH)),
                  _const_spec((GROUP, GROUP)),
                  pl.BlockSpec((BATCH_BLOCK, CHUNK, D), lambda b, c: (b, c, 0)),
                  _const_spec((D, D), single_buffer=True),
                  _const_spec((1, D))],
        out_specs=pl.BlockSpec((BATCH_BLOCK, CHUNK, D), lambda b, c: (b, c, 0)),
        out_shape=jax.ShapeDtypeStruct((B, T, D), f32),
        scratch_shapes=[pltpu.VMEM((BATCH_BLOCK, CHUNK, D), bf16),
                        pltpu.VMEM((BATCH_BLOCK, RWKV_WIDTH // GROUP, RWKV_HEAD, GROUP), f32),
                        pltpu.VMEM((BATCH_BLOCK, RET_HEADS, RET_DK, RET_DV), f32),
                        pltpu.VMEM((D, D), bf16),
                        pltpu.VMEM((2 * LORA, 2 * RWKV_WIDTH), bf16)],
        compiler_params=pltpu.CompilerParams(dimension_semantics=("arbitrary", "arbitrary"),
                                             vmem_limit_bytes=VMEM_LIMIT),
        name="mixers",
    )(p.reshape(B, T, IN_COLS), dec, w_lora_up[0], a_lora_up[0], *vecs,
      ret_gn_g[0][None, :], seg, x, w_out[0], final_norm_g[None, :])
```

```python
import functools

import numpy as np
import jax
import jax.numpy as jnp
from jax import lax
from jax.experimental import pallas as pl
from jax.experimental.pallas import tpu as pltpu

D_MODEL = 1024
CHUNK = 64
RET_HEADS = 4
RET_DK = 64
RET_DV = 128
RET_QK = RET_HEADS * RET_DK
RET_WIDTH = RET_HEADS * RET_DV
RWKV_WIDTH = 512
RWKV_HEAD = 64
LORA = 64
RET_COLS = 2 * RET_QK + 2 * RET_WIDTH
RWKV_COLS = 4 * RWKV_WIDTH + 2 * LORA
IN_COLS = RET_COLS + RWKV_COLS
ROPE_BASE = 10000.0
RMS_EPS = 1e-6
RET_GN_EPS = 1e-5
RWKV_GN_EPS = 64e-5
LANES = 128
GROUP = 256
PROJ_TILE = 512
BATCH_BLOCK = 8
RESULT_LAG = 4
VMEM_LIMIT = 56 * 1024 * 1024

f32 = jnp.float32
bf16 = jnp.bfloat16


def _mm(a, b):
    return jnp.dot(a.astype(bf16), b.astype(bf16), preferred_element_type=f32)


def _mm_nt(a, b):
    return lax.dot_general(a.astype(bf16), b.astype(bf16), (((1,), (1,)), ((), ())),
                           preferred_element_type=f32)


def _split2(x):
    hi = x.astype(bf16)
    lo = (x - hi.astype(f32)).astype(bf16)
    return hi, lo


def _block_diag(x):
    x = x.astype(bf16)
    rows, lanes = x.shape
    assert lanes in (2 * LANES, 4 * LANES)
    zero = jnp.zeros((rows, LANES), bf16)
    cols = []
    if lanes == 4 * LANES:
        for j in range(4):
            cols.append(jnp.concatenate([x[:, j * LANES:(j + 1) * LANES] if i == j else zero for i in range(4)],
                                        axis=0))
    else:
        low = lax.broadcasted_iota(jnp.int32, (rows, LANES), 1) < LANES // 2
        for j in range(2):
            xj = x[:, j * LANES:(j + 1) * LANES]
            pair = [jnp.where(low, xj, zero), jnp.where(low, zero, xj)]
            cols.append(jnp.concatenate([zero] * (2 * j) + pair + [zero] * (2 - 2 * j), axis=0))
    return jnp.concatenate(cols, axis=1)


def _sigmoid(x):
    return 1.0 / (1.0 + jnp.exp(-x))


def _in_proj_kernel(tiles_per_seq, x_ref, g_ref, w_ref, cos_ref, sin_ref, mu_ref, p_ref, wb_ref, carry_ref):
    i = pl.program_id(0)

    @pl.when(i == 0)
    def _():
        wb_ref[...] = w_ref[...].astype(bf16)

    x = x_ref[...]
    u = (x * lax.rsqrt(jnp.mean(x * x, axis=-1, keepdims=True) + RMS_EPS) * g_ref[...]).astype(bf16)
    rows = x.shape[0]
    seq_start = (i % tiles_per_seq) == 0
    W = RWKV_WIDTH

    def rope(pg):
        cos = jnp.concatenate([cos_ref[...]] * (pg.shape[1] // LANES), axis=1)
        sin = jnp.concatenate([sin_ref[...]] * (pg.shape[1] // LANES), axis=1)
        half = (lax.broadcasted_iota(jnp.int32, pg.shape, 1) & (RET_DK - 1)) < RET_DK // 2
        lanes = pg.shape[1]
        swapped = jnp.where(half, pltpu.roll(pg, lanes - RET_DK // 2, 1), pltpu.roll(pg, RET_DK // 2, 1))
        return pg * cos + swapped * sin

    def silu(pg):
        return pg * _sigmoid(pg)

    def shifted(lo, hi):
        def fn(pg):
            first = jnp.where(seq_start, 0.0, carry_ref[:, lo:hi])
            row0 = lax.broadcasted_iota(jnp.int32, pg.shape, 0) == 0
            prev = jnp.where(row0, first, pltpu.roll(pg, 1, 0))
            carry_ref[:, lo:hi] = pg[rows - 1:rows, :]
            return pg + mu_ref[:, lo:hi] * (prev - pg)
        return fn

    groups = [(0, 2 * RET_QK, rope),
              (2 * RET_QK, 2 * RET_QK + RET_WIDTH, lambda pg: pg),
              (2 * RET_QK + RET_WIDTH, RET_COLS, silu),
              (RET_COLS, RET_COLS + W, shifted(0, W)),
              (RET_COLS + W, RET_COLS + 2 * W, shifted(W, 2 * W)),
              (RET_COLS + 2 * W, RET_COLS + 3 * W, shifted(2 * W, 3 * W)),
              (RET_COLS + 3 * W, RET_COLS + 4 * W, lambda pg: silu(shifted(3 * W, 4 * W)(pg))),
              (RET_COLS + 4 * W, IN_COLS, shifted(4 * W, RWKV_COLS))]
    pending = None
    for lo, hi, fn in groups:
        acc = jnp.dot(u, wb_ref[:, lo:hi], preferred_element_type=f32)
        if pending is not None:
            p_ref[:, pending[1]:pending[2]] = pending[3](pending[0])
        pending = (acc, lo, hi, fn)
    p_ref[:, pending[1]:pending[2]] = pending[3](pending[0])


def _staged(n, produce, consume):
    for i in range(n + RESULT_LAG):
        if i < n:
            produce(i)
        if i >= RESULT_LAG:
            consume(i - RESULT_LAG)


def _retention_chunks(qs, ks, vs, states, decay4, qdec, kdec, head_decay):
    n = range(len(qs))
    hs = range(RET_HEADS)
    scores = [_mm_nt(qs[i], _block_diag(ks[i])) * decay4 for i in n]
    qds = [qs[i] * qdec for i in n]
    low = lax.broadcasted_iota(jnp.int32, (CHUNK, LANES), 1) < RET_DK

    def head_lhs(sc, qd, h):
        sl = slice((h // 2) * LANES, (h // 2 + 1) * LANES)
        if h % 2 == 0:
            return jnp.where(low, sc[:, sl], pltpu.roll(qd[:, sl], RET_DK, 1))
        return jnp.where(low, pltpu.roll(sc[:, sl], RET_DK, 1), qd[:, sl])

    out = [jnp.concatenate(
        [_mm(head_lhs(scores[i], qds[i], h),
             jnp.concatenate([vs[i][:, h * RET_DV:(h + 1) * RET_DV].astype(bf16), states[i][h].astype(bf16)],
                             axis=0)) for h in hs], axis=1) for i in n]
    kts = [(ks[i] * kdec).T for i in n]
    kv = [[_mm(kts[i][h * RET_DK:(h + 1) * RET_DK], vs[i][:, h * RET_DV:(h + 1) * RET_DV]) for h in hs]
          for i in n]
    s_new = [jnp.stack([states[i][h] * head_decay[h] + kv[i][h] for h in hs]) for i in n]
    return out, s_new


def _rwkv_chunks(rs, ks, vs, kkns, aas, lws, m_cs, fillers):
    C = CHUNK
    n = len(rs)
    heads = GROUP // RWKV_HEAD
    ti = lax.broadcasted_iota(jnp.int32, (C, C), 0)
    si = lax.broadcasted_iota(jnp.int32, (C, C), 1)
    tri_incl = jnp.where(si <= ti, 1.0, 0.0).astype(bf16)
    tri_incl = jnp.concatenate([tri_incl, tri_incl], axis=1)
    ti = lax.broadcasted_iota(jnp.int32, (C, GROUP), 0)
    si = lax.broadcasted_iota(jnp.int32, (C, GROUP), 1) & (C - 1)
    strict = si < ti
    incl = si <= ti
    eye = jnp.where(si == ti, 1.0, 0.0)
    low = lax.broadcasted_iota(jnp.int32, (RWKV_HEAD, LANES), 1) < RWKV_HEAD

    def slots():
        return [None] * n

    def staged(produce, consume):
        _staged(n, produce, consume)

    cum, lhs, rhs, decay_end, key_end = slots(), slots(), slots(), slots(), slots()

    def cum_matmul(i):
        cum[i] = jnp.dot(tri_incl, jnp.concatenate(_split2(lws[i]), axis=0), preferred_element_type=f32)

    def scale_operands(i):
        g_inc = jnp.exp(cum[i])
        g_inv = jnp.exp(-cum[i])
        g_end = jnp.exp(cum[i][C - 1:C, :] - cum[i])
        beta = kkns[i] * aas[i]
        lhs[i] = jnp.concatenate([-kkns[i] * jnp.exp(cum[i] - lws[i]), rs[i]() * g_inc], axis=0).astype(bf16)
        rhs[i] = jnp.concatenate([_block_diag(beta * g_inv), _block_diag(ks[i] * g_inv)], axis=0)
        decay_end[i] = g_inc[C - 1:C, :]
        key_end[i] = jnp.concatenate([beta * g_end, ks[i] * g_end], axis=0).astype(bf16)

    staged(cum_matmul, scale_operands)

    amat, mv, a_ab, a_k, a_rb, t_inv = slots(), slots(), slots(), slots(), slots(), slots()

    def score_matmuls(i):
        amat[i] = _mm_nt(lhs[i], rhs[i])
        mv[i] = _mm_nt(lhs[i], _block_diag(m_cs[i]))

    def mask_scores(i):
        lower = jnp.where(strict, amat[i][:C, :GROUP], 0.0)
        a_ab[i] = lower.astype(bf16)
        a_rb[i] = jnp.where(incl, amat[i][C:, :GROUP], 0.0).astype(bf16)
        a_k[i] = jnp.concatenate([jnp.where(strict, amat[i][:C, GROUP:], 0.0),
                                  jnp.where(incl, amat[i][C:, GROUP:], 0.0)], axis=0).astype(bf16)
        t_inv[i] = eye + lower

    staged(score_matmuls, mask_scores)

    av, st, st_lhs, pow_bd = slots(), slots(), slots(), slots()

    def first_square(i):
        av[i] = _mm(a_k[i], _block_diag(vs[i]()))
        st[i] = _mm(a_ab[i], _block_diag(a_ab[i]))

    def after_first_square(i):
        s_b = st[i].astype(bf16)
        st_lhs[i] = jnp.concatenate([s_b, t_inv[i].astype(bf16)], axis=0)
        pow_bd[i] = _block_diag(s_b)

    staged(first_square, after_first_square)

    def square(i):
        st[i] = _mm(st_lhs[i], pow_bd[i])

    def after_square(i):
        s_b = st[i][:C].astype(bf16)
        t_inv[i] = t_inv[i] + st[i][C:]
        st_lhs[i] = jnp.concatenate([s_b, t_inv[i].astype(bf16)], axis=0)
        pow_bd[i] = _block_diag(s_b)

    for _ in range(4):
        staged(square, after_square)
        if fillers:
            fillers.pop(0)()

    t_b, y_bd = slots(), slots()

    def last_product(i):
        st[i] = _mm(st_lhs[i][C:], pow_bd[i])

    def after_last_product(i):
        t_b[i] = (t_inv[i] + st[i]).astype(bf16)
        y_bd[i] = _block_diag(mv[i][:C] + av[i][:C])

    staged(last_product, after_last_product)

    u, u_bd, uvt = slots(), slots(), slots()

    def solve(i):
        u[i] = _mm(t_b[i], y_bd[i])

    def after_solve(i):
        u_bd[i] = _block_diag(u[i])
        uvt[i] = jnp.concatenate([u[i], vs[i]()], axis=0).T.astype(bf16)

    staged(solve, after_solve)

    o_u, upd, out, m_new = slots(), slots(), slots(), slots()

    def output_and_update(i):
        o_u[i] = _mm(a_rb[i], u_bd[i])
        upd[i] = [_mm(uvt[i][h * RWKV_HEAD:(h + 1) * RWKV_HEAD],
                      key_end[i][:, (h // 2) * LANES:(h // 2 + 1) * LANES]) for h in range(heads)]

    def finish(i):
        out[i] = mv[i][C:] + av[i][C:] + o_u[i]
        m_new[i] = m_cs[i] * decay_end[i] + jnp.concatenate(
            [jnp.where(low, upd[i][2 * j], upd[i][2 * j + 1]) for j in range(heads // 2)], axis=1)

    staged(output_and_update, finish)
    return out, m_new


def _mix_chunk(p_ref, dec_ref, lora_ref, vec_refs, rgn_ref, seg_ref, y_ref, m_ref, s_ref):
    C = CHUNK
    nb = BATCH_BLOCK
    R = nb * C
    W = RWKV_WIDTH
    NG = W // GROUP

    def rows(x, bi):
        return x[bi * C:(bi + 1) * C]

    def rwkv_cols(lo, hi):
        return p_ref[:, :, RET_COLS + lo:RET_COLS + hi].reshape(R, hi - lo)

    kr = rwkv_cols(W, 2 * W)
    xwa = rwkv_cols(4 * W, RWKV_COLS)
    lane = lax.broadcasted_iota(jnp.int32, xwa.shape, 1)
    lora = _mm(jnp.where(lane < LORA, jnp.tanh(xwa), xwa), lora_ref[...])
    w0, a0, k_k, k_a, r_k, gn_g, gn_b = (ref[...] for ref in vec_refs)
    lw = -np.float32(np.exp(-0.5)) * _sigmoid(w0 + lora[:, :W])
    a = _sigmoid(a0 + lora[:, W:])
    seg = seg_ref[...]

    def segsum(x):
        xs = jnp.concatenate([x[:, g * GROUP:(g + 1) * GROUP] for g in range(NG)], axis=0)
        tot = jnp.dot(xs.astype(bf16), seg, preferred_element_type=f32)
        return jnp.concatenate([tot[g * R:(g + 1) * R] for g in range(NG)], axis=1)

    kk = kr * k_k
    kkn = kk * lax.rsqrt(jnp.maximum(segsum(kk * kk), 1e-24))
    kmod = kr * (1.0 + (a - 1.0) * k_a)

    q = p_ref[:, :, 0:RET_QK].reshape(R, RET_QK)
    k = p_ref[:, :, RET_QK:2 * RET_QK].reshape(R, RET_QK)
    v = p_ref[:, :, 2 * RET_QK:2 * RET_QK + RET_WIDTH].reshape(R, RET_WIDTH)
    rets, s_new = _retention_chunks([rows(q, bi) for bi in range(nb)], [rows(k, bi) for bi in range(nb)],
                                    [rows(v, bi) for bi in range(nb)], [s_ref[bi] for bi in range(nb)],
                                    dec_ref[0], dec_ref[1], dec_ref[2], _RET_CHUNK_DECAY)
    for bi in range(nb):
        s_ref[bi] = s_new[bi]
    ret = jnp.concatenate(rets, axis=0)
    rgn = rgn_ref[...]

    def retention_head_output(h):
        sl = slice(h * RET_DV, (h + 1) * RET_DV)
        xh = ret[:, sl]
        d = xh - jnp.mean(xh, axis=-1, keepdims=True)
        var = jnp.mean(d * d, axis=-1, keepdims=True)
        gh = p_ref[:, :, 2 * RET_QK + RET_WIDTH + h * RET_DV:2 * RET_QK + RET_WIDTH + (h + 1) * RET_DV]
        gh = gh.reshape(R, RET_DV)
        yh = gh * (d * lax.rsqrt(var + RET_GN_EPS) * rgn[:, sl])
        y_ref[:, :, sl] = yh.reshape(nb, C, RET_DV).astype(y_ref.dtype)

    fillers = [functools.partial(retention_head_output, h) for h in range(RET_HEADS)]

    chains = [(bi, g) for bi in range(nb) for g in range(NG)]

    def pick(x):
        return [x[bi * C:(bi + 1) * C, g * GROUP:(g + 1) * GROUP] for bi, g in chains]

    def readers(lo):
        def reader(bi, g):
            start = RET_COLS + lo + g * GROUP
            return lambda: p_ref[bi, :, start:start + GROUP]
        return [reader(bi, g) for bi, g in chains]

    outs, m_new = _rwkv_chunks(readers(0), pick(kmod), readers(2 * W), pick(kkn), pick(a), pick(lw),
                               [m_ref[bi, g] for bi, g in chains], fillers)
    assert not fillers
    for i, (bi, g) in enumerate(chains):
        m_ref[bi, g] = m_new[i]
    o = jnp.concatenate([jnp.concatenate(outs[bi * NG:(bi + 1) * NG], axis=1) for bi in range(nb)], axis=0)
    d = o - segsum(o) * (1.0 / RWKV_HEAD)
    var = segsum(d * d) * (1.0 / RWKV_HEAD)
    o = d * lax.rsqrt(var + RWKV_GN_EPS) * gn_g + gn_b
    bonus = segsum(rwkv_cols(0, W) * kmod * r_k) * rwkv_cols(2 * W, 3 * W)
    y_rw = rwkv_cols(3 * W, 4 * W) * (o + bonus)
    y_ref[:, :, RET_WIDTH:] = y_rw.reshape(nb, C, W).astype(y_ref.dtype)


def _project_chunk(x_ref, y_ref, w_ref, g_ref, o_ref):
    nb, C, D = x_ref.shape
    y = y_ref[...].reshape(nb * C, D)
    h = x_ref[...].reshape(nb * C, D) + jnp.dot(y, w_ref[...], preferred_element_type=f32)
    out = h * lax.rsqrt(jnp.mean(h * h, axis=-1, keepdims=True) + RMS_EPS) * g_ref[...]
    o_ref[...] = out.reshape(nb, C, D)


def _mixer_kernel(p_ref, dec_ref, wlora_ref, alora_ref, w0_ref, a0_ref, kk_ref, ka_ref, rk_ref, gng_ref, gnb_ref,
                  rgn_ref, seg_ref, x_ref, wout_ref, fg_ref, o_ref, y_ref, m_ref, s_ref, woutb_ref, lora_ref):
    @pl.when(pl.program_id(1) == 0)
    def _():
        m_ref[...] = jnp.zeros_like(m_ref)
        s_ref[...] = jnp.zeros_like(s_ref)
        woutb_ref[...] = wout_ref[...].astype(bf16)
        zero = jnp.zeros((LORA, RWKV_WIDTH), bf16)
        lora_ref[...] = jnp.concatenate(
            [jnp.concatenate([wlora_ref[...].astype(bf16), zero], axis=1),
             jnp.concatenate([zero, alora_ref[...].astype(bf16)], axis=1)], axis=0)

    vec_refs = (w0_ref, a0_ref, kk_ref, ka_ref, rk_ref, gng_ref, gnb_ref)
    _mix_chunk(p_ref, dec_ref, lora_ref, vec_refs, rgn_ref, seg_ref, y_ref, m_ref, s_ref)
    _project_chunk(x_ref, y_ref, woutb_ref, fg_ref, o_ref)


def _rope_tables(seq):
    half = RET_DK // 2
    expo = -jnp.arange(half, dtype=f32) / f32(half)
    freqs = jnp.exp(expo * f32(np.log(ROPE_BASE)))
    ang = jnp.arange(seq, dtype=jnp.int32).astype(f32)[:, None] * freqs[None, :]
    cos = jnp.cos(ang)
    sin = jnp.sin(ang)
    cos_full = jnp.tile(jnp.concatenate([cos, cos], axis=1), (1, LANES // RET_DK))
    sin_signed = jnp.tile(jnp.concatenate([-sin, sin], axis=1), (1, LANES // RET_DK))
    return cos_full, sin_signed


_RET_LOG_GAMMA = np.log(1.0 - np.exp2(-5.0 - np.arange(RET_HEADS, dtype=np.float64)))
_RET_CHUNK_DECAY = tuple(np.float32(v) for v in np.exp(_RET_LOG_GAMMA * CHUNK))


def _retention_constants():
    C = CHUNK
    lg = _RET_LOG_GAMMA
    lane_lg = np.repeat(lg, RET_DK)[None, :]
    n = np.arange(C, dtype=np.float64)[:, None]
    m = np.tile(np.arange(C, dtype=np.float64), RET_HEADS)[None, :]
    scale = RET_DK ** -0.5
    decay4 = scale * np.exp(lane_lg * np.abs(n - m))
    qdec = np.exp(lane_lg * (n + 1.0)) * np.ones((1, RET_QK))
    kdec = scale * np.exp(lane_lg * (C - 1.0 - n)) * np.ones((1, RET_QK))
    return jnp.asarray(np.stack([decay4, qdec, kdec]).astype(np.float32))


def _const_spec(shape, single_buffer=False):
    mode = pl.Buffered(1) if single_buffer else None
    return pl.BlockSpec(shape, lambda *_: (0,) * len(shape), pipeline_mode=mode)


@jax.jit
def kernel(x, norm_g, w_in, ret_gn_g, rwkv_mu, w_lora_up, w0, a_lora_up, a0, k_k, k_a, r_k,
           rwkv_gn_g, rwkv_gn_b, w_out, final_norm_g):
    B, T, D = x.shape
    assert D == D_MODEL and T % CHUNK == 0 and T % PROJ_TILE == 0 and B % BATCH_BLOCK == 0
    assert norm_g.shape[0] == 1, "single-layer block"
    n_tok = B * T
    xf = x.reshape(n_tok, D)
    params = pltpu.CompilerParams(dimension_semantics=("arbitrary",), vmem_limit_bytes=VMEM_LIMIT)
    tiles_per_seq = T // PROJ_TILE
    cos, sin = _rope_tables(T)

    p = pl.pallas_call(
        functools.partial(_in_proj_kernel, tiles_per_seq),
        grid=(n_tok // PROJ_TILE,),
        in_specs=[pl.BlockSpec((PROJ_TILE, D), lambda i: (i, 0)),
                  _const_spec((1, D)),
                  _const_spec((D, IN_COLS), single_buffer=True),
                  pl.BlockSpec((PROJ_TILE, LANES), lambda i: (i % tiles_per_seq, 0)),
                  pl.BlockSpec((PROJ_TILE, LANES), lambda i: (i % tiles_per_seq, 0)),
                  _const_spec((1, RWKV_COLS))],
        out_specs=pl.BlockSpec((PROJ_TILE, IN_COLS), lambda i: (i, 0)),
        out_shape=jax.ShapeDtypeStruct((n_tok, IN_COLS), f32),
        scratch_shapes=[pltpu.VMEM((D, IN_COLS), bf16),
                        pltpu.VMEM((1, RWKV_COLS), f32)],
        compiler_params=params,
        name="in_proj",
    )(xf, norm_g[0][None, :], w_in[0], cos, sin, rwkv_mu[0][None, :])

    dec = _retention_constants()
    vecs = [v.reshape(1, RWKV_WIDTH) for v in (w0, a0, k_k, k_a, r_k, rwkv_gn_g, rwkv_gn_b)]
    seg_ids = np.arange(GROUP) // RWKV_HEAD
    seg = jnp.asarray(seg_ids[:, None] == seg_ids[None, :], bf16)

    return pl.pallas_call(
        _mixer_kernel,
        grid=(B // BATCH_BLOCK, T // CHUNK),
        in_specs=[pl.BlockSpec((BATCH_BLOCK, CHUNK, IN_COLS), lambda b, c: (b, c, 0)),
                  _const_spec((3, CHUNK, RET_QK)),
                  _const_spec((LORA, RWKV_WIDTH)),
                  _const_spec((LORA, RWKV_WIDTH))]
                 + [_const_spec((1, RWKV_WIDTH))] * len(vecs)
                 + [_const_spec((1, RET_WIDTH)),
                  _const_spec((GROUP, GROUP)),
                  pl.BlockSpec((BATCH_BLOCK, CHUNK, D), lambda b, c: (b, c, 0)),
                  _const_spec((D, D), single_buffer=True),
                  _const_spec((1, D))],
        out_specs=pl.BlockSpec((BATCH_BLOCK, CHUNK, D), lambda b, c: (b, c, 0)),
        out_shape=jax.ShapeDtypeStruct((B, T, D), f32),
        scratch_shapes=[pltpu.VMEM((BATCH_BLOCK, CHUNK, D), bf16),
                        pltpu.VMEM((BATCH_BLOCK, RWKV_WIDTH // GROUP, RWKV_HEAD, GROUP), f32),
                        pltpu.VMEM((BATCH_BLOCK, RET_HEADS, RET_DK, RET_DV), f32),
                        pltpu.VMEM((D, D), bf16),
                        pltpu.VMEM((2 * LORA, 2 * RWKV_WIDTH), bf16)],
        compiler_params=pltpu.CompilerParams(dimension_semantics=("arbitrary", "arbitrary"),
                                             vmem_limit_bytes=VMEM_LIMIT),
        name="mixers",
    )(p.reshape(B, T, IN_COLS), dec, w_lora_up[0], a_lora_up[0], *vecs,
      ret_gn_g[0][None, :], seg, x, w_out[0], final_norm_g[None, :])
```

```python
import functools

import numpy as np
import jax
import jax.numpy as jnp
from jax import lax
from jax.experimental import pallas as pl
from jax.experimental.pallas import tpu as pltpu

D_MODEL = 1024
CHUNK = 64
RET_HEADS = 4
RET_DK = 64
RET_DV = 128
RET_QK = RET_HEADS * RET_DK
RET_WIDTH = RET_HEADS * RET_DV
RWKV_WIDTH = 512
RWKV_HEAD = 64
LORA = 64
RET_COLS = 2 * RET_QK + 2 * RET_WIDTH
RWKV_COLS = 4 * RWKV_WIDTH + 2 * LORA
IN_COLS = RET_COLS + RWKV_COLS
ROPE_BASE = 10000.0
RMS_EPS = 1e-6
RET_GN_EPS = 1e-5
RWKV_GN_EPS = 64e-5
LANES = 128
GROUP = 256
PROJ_TILE = 512
BATCH_BLOCK = 8
RESULT_LAG = 4
INV_BASE = 8
VMEM_LIMIT = 56 * 1024 * 1024

f32 = jnp.float32
bf16 = jnp.bfloat16


def _mm(a, b):
    return jnp.dot(a.astype(bf16), b.astype(bf16), preferred_element_type=f32)


def _mm_nt(a, b):
    return lax.dot_general(a.astype(bf16), b.astype(bf16), (((1,), (1,)), ((), ())),
                           preferred_element_type=f32)


def _split2(x):
    hi = x.astype(bf16)
    lo = (x - hi.astype(f32)).astype(bf16)
    return hi, lo


def _block_diag(x):
    x = x.astype(bf16)
    rows, lanes = x.shape
    assert lanes in (2 * LANES, 4 * LANES)
    zero = jnp.zeros((rows, LANES), bf16)
    cols = []
    if lanes == 4 * LANES:
        for j in range(4):
            cols.append(jnp.concatenate([x[:, j * LANES:(j + 1) * LANES] if i == j else zero for i in range(4)],
                                        axis=0))
    else:
        low = lax.broadcasted_iota(jnp.int32, (rows, LANES), 1) < LANES // 2
        for j in range(2):
            xj = x[:, j * LANES:(j + 1) * LANES]
            pair = [jnp.where(low, xj, zero), jnp.where(low, zero, xj)]
            cols.append(jnp.concatenate([zero] * (2 * j) + pair + [zero] * (2 - 2 * j), axis=0))
    return jnp.concatenate(cols, axis=1)


def _sigmoid(x):
    return 1.0 / (1.0 + jnp.exp(-x))


def _in_proj_kernel(tiles_per_seq, x_ref, g_ref, w_ref, cos_ref, sin_ref, mu_ref, p_ref, wb_ref, carry_ref):
    i = pl.program_id(0)

    @pl.when(i == 0)
    def _():
        wb_ref[...] = w_ref[...].astype(bf16)

    x = x_ref[...]
    u = (x * lax.rsqrt(jnp.mean(x * x, axis=-1, keepdims=True) + RMS_EPS) * g_ref[...]).astype(bf16)
    rows = x.shape[0]
    seq_start = (i % tiles_per_seq) == 0
    W = RWKV_WIDTH

    def rope(pg):
        cos = jnp.concatenate([cos_ref[...]] * (pg.shape[1] // LANES), axis=1)
        sin = jnp.concatenate([sin_ref[...]] * (pg.shape[1] // LANES), axis=1)
        half = (lax.broadcasted_iota(jnp.int32, pg.shape, 1) & (RET_DK - 1)) < RET_DK // 2
        lanes = pg.shape[1]
        swapped = jnp.where(half, pltpu.roll(pg, lanes - RET_DK // 2, 1), pltpu.roll(pg, RET_DK // 2, 1))
        return pg * cos + swapped * sin

    def silu(pg):
        return pg * _sigmoid(pg)

    def shifted(lo, hi):
        def fn(pg):
            first = jnp.where(seq_start, 0.0, carry_ref[:, lo:hi])
            row0 = lax.broadcasted_iota(jnp.int32, pg.shape, 0) == 0
            prev = jnp.where(row0, first, pltpu.roll(pg, 1, 0))
            carry_ref[:, lo:hi] = pg[rows - 1:rows, :]
            return pg + mu_ref[:, lo:hi] * (prev - pg)
        return fn

    groups = [(0, 2 * RET_QK, rope),
              (2 * RET_QK, 2 * RET_QK + RET_WIDTH, lambda pg: pg),
              (2 * RET_QK + RET_WIDTH, RET_COLS, silu),
              (RET_COLS, RET_COLS + W, shifted(0, W)),
              (RET_COLS + W, RET_COLS + 2 * W, shifted(W, 2 * W)),
              (RET_COLS + 2 * W, RET_COLS + 3 * W, shifted(2 * W, 3 * W)),
              (RET_COLS + 3 * W, RET_COLS + 4 * W, lambda pg: silu(shifted(3 * W, 4 * W)(pg))),
              (RET_COLS + 4 * W, IN_COLS, shifted(4 * W, RWKV_COLS))]
    pending = None
    for lo, hi, fn in groups:
        acc = jnp.dot(u, wb_ref[:, lo:hi], preferred_element_type=f32)
        if pending is not None:
            p_ref[:, pending[1]:pending[2]] = pending[3](pending[0])
        pending = (acc, lo, hi, fn)
    p_ref[:, pending[1]:pending[2]] = pending[3](pending[0])


def _staged(n, produce, consume):
    for i in range(n + RESULT_LAG):
        if i < n:
            produce(i)
        if i >= RESULT_LAG:
            consume(i - RESULT_LAG)


def _retention_chunks(qs, ks, vs, states, decay4, qdec, kdec, head_decay):
    n = range(len(qs))
    hs = range(RET_HEADS)
    scores = [_mm_nt(qs[i], _block_diag(ks[i])) * decay4 for i in n]
    qds = [qs[i] * qdec for i in n]
    low = lax.broadcasted_iota(jnp.int32, (CHUNK, LANES), 1) < RET_DK

    def head_lhs(sc, qd, h):
        sl = slice((h // 2) * LANES, (h // 2 + 1) * LANES)
        if h % 2 == 0:
            return jnp.where(low, sc[:, sl], pltpu.roll(qd[:, sl], RET_DK, 1))
        return jnp.where(low, pltpu.roll(sc[:, sl], RET_DK, 1), qd[:, sl])

    out = [jnp.concatenate(
        [_mm(head_lhs(scores[i], qds[i], h),
             jnp.concatenate([vs[i][:, h * RET_DV:(h + 1) * RET_DV].astype(bf16), states[i][h].astype(bf16)],
                             axis=0)) for h in hs], axis=1) for i in n]
    kts = [(ks[i] * kdec).T for i in n]
    kv = [[_mm(kts[i][h * RET_DK:(h + 1) * RET_DK], vs[i][:, h * RET_DV:(h + 1) * RET_DV]) for h in hs]
          for i in n]
    s_new = [jnp.stack([states[i][h] * head_decay[h] + kv[i][h] for h in hs]) for i in n]
    return out, s_new


def _rwkv_chunks(rs, ks, vs, kkns, aas, lws, m_cs):
    C = CHUNK
    n = len(rs)
    heads = GROUP // RWKV_HEAD
    ti = lax.broadcasted_iota(jnp.int32, (C, C), 0)
    si = lax.broadcasted_iota(jnp.int32, (C, C), 1)
    tri_incl = jnp.where(si <= ti, 1.0, 0.0).astype(bf16)
    tri_incl = jnp.concatenate([tri_incl, tri_incl], axis=1)
    ti = lax.broadcasted_iota(jnp.int32, (C, GROUP), 0)
    si = lax.broadcasted_iota(jnp.int32, (C, GROUP), 1) & (C - 1)
    strict = si < ti
    incl = si <= ti
    eye = jnp.where(si == ti, 1.0, 0.0)
    low = lax.broadcasted_iota(jnp.int32, (RWKV_HEAD, LANES), 1) < RWKV_HEAD

    def slots():
        return [None] * n

    def staged(produce, consume):
        _staged(n, produce, consume)

    cum, lhs, rhs, decay_end, key_end = slots(), slots(), slots(), slots(), slots()

    def cum_matmul(i):
        cum[i] = jnp.dot(tri_incl, jnp.concatenate(_split2(lws[i]), axis=0), preferred_element_type=f32)

    def scale_operands(i):
        g_inc = jnp.exp(cum[i])
        g_inv = jnp.exp(-cum[i])
        g_end = jnp.exp(cum[i][C - 1:C, :] - cum[i])
        beta = kkns[i] * aas[i]
        lhs[i] = jnp.concatenate([-kkns[i] * jnp.exp(cum[i] - lws[i]), rs[i]() * g_inc], axis=0).astype(bf16)
        rhs[i] = jnp.concatenate([_block_diag(beta * g_inv), _block_diag(ks[i] * g_inv)], axis=0)
        decay_end[i] = g_inc[C - 1:C, :]
        key_end[i] = jnp.concatenate([beta * g_end, ks[i] * g_end], axis=0).astype(bf16)

    staged(cum_matmul, scale_operands)

    amat, mv, a_ab, a_k, a_rb, t_inv = slots(), slots(), slots(), slots(), slots(), slots()

    def score_matmuls(i):
        amat[i] = _mm_nt(lhs[i], rhs[i])
        mv[i] = _mm_nt(lhs[i], _block_diag(m_cs[i]))

    def same_block(width):
        shift = width.bit_length() - 1
        return (ti >> shift) == (si >> shift)

    a_off = slots()

    def mask_scores(i):
        lower = jnp.where(strict, amat[i][:C, :GROUP], 0.0)
        base = jnp.where(same_block(INV_BASE), lower, 0.0)
        a_ab[i] = base.astype(bf16)
        a_off[i] = lower - base
        a_rb[i] = jnp.where(incl, amat[i][C:, :GROUP], 0.0).astype(bf16)
        a_k[i] = jnp.concatenate([jnp.where(strict, amat[i][:C, GROUP:], 0.0),
                                  jnp.where(incl, amat[i][C:, GROUP:], 0.0)], axis=0).astype(bf16)
        t_inv[i] = eye + base

    staged(score_matmuls, mask_scores)

    av, st, st_lhs, pow_bd = slots(), slots(), slots(), slots()

    def first_square(i):
        av[i] = _mm(a_k[i], _block_diag(vs[i]()))
        st[i] = _mm(a_ab[i], _block_diag(a_ab[i]))

    def after_first_square(i):
        s_b = st[i].astype(bf16)
        st_lhs[i] = jnp.concatenate([s_b, t_inv[i].astype(bf16)], axis=0)
        pow_bd[i] = _block_diag(s_b)

    staged(first_square, after_first_square)

    def square(i):
        st[i] = _mm(st_lhs[i], pow_bd[i])

    def after_square(i):
        s_b = st[i][:C].astype(bf16)
        t_inv[i] = t_inv[i] + st[i][C:]
        st_lhs[i] = jnp.concatenate([s_b, t_inv[i].astype(bf16)], axis=0)
        pow_bd[i] = _block_diag(s_b)

    for _ in range(INV_BASE.bit_length() - 3):
        staged(square, after_square)

    def last_product(i):
        st[i] = _mm(st_lhs[i][C:], pow_bd[i])

    def after_last_product(i):
        t_inv[i] = t_inv[i] + st[i]

    staged(last_product, after_last_product)

    width = INV_BASE
    while width < C:
        joined = same_block(2 * width)
        off_b, t_cur = slots(), slots()

        def off_times_t(i):
            t_cur[i] = t_inv[i].astype(bf16)
            st[i] = _mm(jnp.where(joined, a_off[i], 0.0), _block_diag(t_cur[i]))

        def after_off_times_t(i):
            off_b[i] = _block_diag(st[i])

        staged(off_times_t, after_off_times_t)

        def t_times(i):
            st[i] = _mm(t_cur[i], off_b[i])

        def after_t_times(i):
            t_inv[i] = t_inv[i] + st[i]
            a_off[i] = jnp.where(joined, 0.0, a_off[i])

        staged(t_times, after_t_times)
        width *= 2

    t_b, y_bd = slots(), slots()
    for i in range(n):
        t_b[i] = t_inv[i].astype(bf16)
        y_bd[i] = _block_diag(mv[i][:C] + av[i][:C])

    u, u_bd, uvt = slots(), slots(), slots()

    def solve(i):
        u[i] = _mm(t_b[i], y_bd[i])

    def after_solve(i):
        u_bd[i] = _block_diag(u[i])
        uvt[i] = jnp.concatenate([u[i], vs[i]()], axis=0).T.astype(bf16)

    staged(solve, after_solve)

    o_u, upd, out, m_new = slots(), slots(), slots(), slots()

    def output_and_update(i):
        o_u[i] = _mm(a_rb[i], u_bd[i])
        upd[i] = [_mm(uvt[i][h * RWKV_HEAD:(h + 1) * RWKV_HEAD],
                      key_end[i][:, (h // 2) * LANES:(h // 2 + 1) * LANES]) for h in range(heads)]

    def finish(i):
        out[i] = mv[i][C:] + av[i][C:] + o_u[i]
        m_new[i] = m_cs[i] * decay_end[i] + jnp.concatenate(
            [jnp.where(low, upd[i][2 * j], upd[i][2 * j + 1]) for j in range(heads // 2)], axis=1)

    staged(output_and_update, finish)
    return out, m_new


def _mix_chunk(p_ref, dec_ref, lora_ref, vec_refs, rgn_ref, seg_ref, y_ref, m_ref, s_ref):
    C = CHUNK
    nb = BATCH_BLOCK
    R = nb * C
    W = RWKV_WIDTH
    NG = W // GROUP

    def rows(x, bi):
        return x[bi * C:(bi + 1) * C]

    def rwkv_cols(lo, hi):
        return p_ref[:, :, RET_COLS + lo:RET_COLS + hi].reshape(R, hi - lo)

    kr = rwkv_cols(W, 2 * W)
    xwa = rwkv_cols(4 * W, RWKV_COLS)
    lane = lax.broadcasted_iota(jnp.int32, xwa.shape, 1)
    lora = _mm(jnp.where(lane < LORA, jnp.tanh(xwa), xwa), lora_ref[...])
    w0, a0, k_k, k_a, r_k, gn_g, gn_b = (ref[...] for ref in vec_refs)
    lw = -np.float32(np.exp(-0.5)) * _sigmoid(w0 + lora[:, :W])
    a = _sigmoid(a0 + lora[:, W:])
    seg = seg_ref[...]

    def segsum(x):
        xs = jnp.concatenate([x[:, g * GROUP:(g + 1) * GROUP] for g in range(NG)], axis=0)
        tot = jnp.dot(xs.astype(bf16), seg, preferred_element_type=f32)
        return jnp.concatenate([tot[g * R:(g + 1) * R] for g in range(NG)], axis=1)

    kk = kr * k_k
    kkn = kk * lax.rsqrt(jnp.maximum(segsum(kk * kk), 1e-24))
    kmod = kr * (1.0 + (a - 1.0) * k_a)

    q = p_ref[:, :, 0:RET_QK].reshape(R, RET_QK)
    k = p_ref[:, :, RET_QK:2 * RET_QK].reshape(R, RET_QK)
    v = p_ref[:, :, 2 * RET_QK:2 * RET_QK + RET_WIDTH].reshape(R, RET_WIDTH)
    rets, s_new = _retention_chunks([rows(q, bi) for bi in range(nb)], [rows(k, bi) for bi in range(nb)],
                                    [rows(v, bi) for bi in range(nb)], [s_ref[bi] for bi in range(nb)],
                                    dec_ref[0], dec_ref[1], dec_ref[2], _RET_CHUNK_DECAY)
    for bi in range(nb):
        s_ref[bi] = s_new[bi]
    ret = jnp.concatenate(rets, axis=0)
    rgn = rgn_ref[...]
    for h in range(RET_HEADS):
        sl = slice(h * RET_DV, (h + 1) * RET_DV)
        xh = ret[:, sl]
        d = xh - jnp.mean(xh, axis=-1, keepdims=True)
        var = jnp.mean(d * d, axis=-1, keepdims=True)
        gh = p_ref[:, :, 2 * RET_QK + RET_WIDTH + h * RET_DV:2 * RET_QK + RET_WIDTH + (h + 1) * RET_DV]
        gh = gh.reshape(R, RET_DV)
        yh = gh * (d * lax.rsqrt(var + RET_GN_EPS) * rgn[:, sl])
        y_ref[:, :, sl] = yh.reshape(nb, C, RET_DV).astype(y_ref.dtype)

    chains = [(bi, g) for bi in range(nb) for g in range(NG)]

    def pick(x):
        return [x[bi * C:(bi + 1) * C, g * GROUP:(g + 1) * GROUP] for bi, g in chains]

    def readers(lo):
        def reader(bi, g):
            start = RET_COLS + lo + g * GROUP
            return lambda: p_ref[bi, :, start:start + GROUP]
        return [reader(bi, g) for bi, g in chains]

    outs, m_new = _rwkv_chunks(readers(0), pick(kmod), readers(2 * W), pick(kkn), pick(a), pick(lw),
                               [m_ref[bi, g] for bi, g in chains])
    for i, (bi, g) in enumerate(chains):
        m_ref[bi, g] = m_new[i]
    o = jnp.concatenate([jnp.concatenate(outs[bi * NG:(bi + 1) * NG], axis=1) for bi in range(nb)], axis=0)
    d = o - segsum(o) * (1.0 / RWKV_HEAD)
    var = segsum(d * d) * (1.0 / RWKV_HEAD)
    o = d * lax.rsqrt(var + RWKV_GN_EPS) * gn_g + gn_b
    bonus = segsum(rwkv_cols(0, W) * kmod * r_k) * rwkv_cols(2 * W, 3 * W)
    y_rw = rwkv_cols(3 * W, 4 * W) * (o + bonus)
    y_ref[:, :, RET_WIDTH:] = y_rw.reshape(nb, C, W).astype(y_ref.dtype)


def _project_chunk(x_ref, y_ref, w_ref, g_ref, o_ref):
    nb, C, D = x_ref.shape
    y = y_ref[...].reshape(nb * C, D)
    h = x_ref[...].reshape(nb * C, D) + jnp.dot(y, w_ref[...], preferred_element_type=f32)
    out = h * lax.rsqrt(jnp.mean(h * h, axis=-1, keepdims=True) + RMS_EPS) * g_ref[...]
    o_ref[...] = out.reshape(nb, C, D)


def _mixer_kernel(p_ref, dec_ref, wlora_ref, alora_ref, w0_ref, a0_ref, kk_ref, ka_ref, rk_ref, gng_ref, gnb_ref,
                  rgn_ref, seg_ref, x_ref, wout_ref, fg_ref, o_ref, y_ref, m_ref, s_ref, woutb_ref, lora_ref):
    @pl.when(pl.program_id(1) == 0)
    def _():
        m_ref[...] = jnp.zeros_like(m_ref)
        s_ref[...] = jnp.zeros_like(s_ref)
        woutb_ref[...] = wout_ref[...].astype(bf16)
        zero = jnp.zeros((LORA, RWKV_WIDTH), bf16)
        lora_ref[...] = jnp.concatenate(
            [jnp.concatenate([wlora_ref[...].astype(bf16), zero], axis=1),
             jnp.concatenate([zero, alora_ref[...].astype(bf16)], axis=1)], axis=0)

    vec_refs = (w0_ref, a0_ref, kk_ref, ka_ref, rk_ref, gng_ref, gnb_ref)
    _mix_chunk(p_ref, dec_ref, lora_ref, vec_refs, rgn_ref, seg_ref, y_ref, m_ref, s_ref)
    _project_chunk(x_ref, y_ref, woutb_ref, fg_ref, o_ref)


def _rope_tables(seq):
    half = RET_DK // 2
    expo = -jnp.arange(half, dtype=f32) / f32(half)
    freqs = jnp.exp(expo * f32(np.log(ROPE_BASE)))
    ang = jnp.arange(seq, dtype=jnp.int32).astype(f32)[:, None] * freqs[None, :]
    cos = jnp.cos(ang)
    sin = jnp.sin(ang)
    cos_full = jnp.tile(jnp.concatenate([cos, cos], axis=1), (1, LANES // RET_DK))
    sin_signed = jnp.tile(jnp.concatenate([-sin, sin], axis=1), (1, LANES // RET_DK))
    return cos_full, sin_signed


_RET_LOG_GAMMA = np.log(1.0 - np.exp2(-5.0 - np.arange(RET_HEADS, dtype=np.float64)))
_RET_CHUNK_DECAY = tuple(np.float32(v) for v in np.exp(_RET_LOG_GAMMA * CHUNK))


def _retention_constants():
    C = CHUNK
    lg = _RET_LOG_GAMMA
    lane_lg = np.repeat(lg, RET_DK)[None, :]
    n = np.arange(C, dtype=np.float64)[:, None]
    m = np.tile(np.arange(C, dtype=np.float64), RET_HEADS)[None, :]
    scale = RET_DK ** -0.5
    decay4 = scale * np.exp(lane_lg * np.abs(n - m))
    qdec = np.exp(lane_lg * (n + 1.0)) * np.ones((1, RET_QK))
    kdec = scale * np.exp(lane_lg * (C - 1.0 - n)) * np.ones((1, RET_QK))
    return jnp.asarray(np.stack([decay4, qdec, kdec]).astype(np.float32))


def _const_spec(shape, single_buffer=False):
    mode = pl.Buffered(1) if single_buffer else None
    return pl.BlockSpec(shape, lambda *_: (0,) * len(shape), pipeline_mode=mode)


@jax.jit
def kernel(x, norm_g, w_in, ret_gn_g, rwkv_mu, w_lora_up, w0, a_lora_up, a0, k_k, k_a, r_k,
           rwkv_gn_g, rwkv_gn_b, w_out, final_norm_g):
    B, T, D = x.shape
    assert D == D_MODEL and T % CHUNK == 0 and T % PROJ_TILE == 0 and B % BATCH_BLOCK == 0
    assert norm_g.shape[0] == 1, "single-layer block"
    n_tok = B * T
    xf = x.reshape(n_tok, D)
    params = pltpu.CompilerParams(dimension_semantics=("arbitrary",), vmem_limit_bytes=VMEM_LIMIT)
    tiles_per_seq = T // PROJ_TILE
    cos, sin = _rope_tables(T)

    p = pl.pallas_call(
        functools.partial(_in_proj_kernel, tiles_per_seq),
        grid=(n_tok // PROJ_TILE,),
        in_specs=[pl.BlockSpec((PROJ_TILE, D), lambda i: (i, 0)),
                  _const_spec((1, D)),
                  _const_spec((D, IN_COLS), single_buffer=True),
                  pl.BlockSpec((PROJ_TILE, LANES), lambda i: (i % tiles_per_seq, 0)),
                  pl.BlockSpec((PROJ_TILE, LANES), lambda i: (i % tiles_per_seq, 0)),
                  _const_spec((1, RWKV_COLS))],
        out_specs=pl.BlockSpec((PROJ_TILE, IN_COLS), lambda i: (i, 0)),
        out_shape=jax.ShapeDtypeStruct((n_tok, IN_COLS), f32),
        scratch_shapes=[pltpu.VMEM((D, IN_COLS), bf16),
                        pltpu.VMEM((1, RWKV_COLS), f32)],
        compiler_params=params,
        name="in_proj",
    )(xf, norm_g[0][None, :], w_in[0], cos, sin, rwkv_mu[0][None, :])

    dec = _retention_constants()
    vecs = [v.reshape(1, RWKV_WIDTH) for v in (w0, a0, k_k, k_a, r_k, rwkv_gn_g, rwkv_gn_b)]
    seg_ids = np.arange(GROUP) // RWKV_HEAD
    seg = jnp.asarray(seg_ids[:, None] == seg_ids[None, :], bf16)

    return pl.pallas_call(
        _mixer_kernel,
        grid=(B // BATCH_BLOCK, T // CHUNK),
        in_specs=[pl.BlockSpec((BATCH_BLOCK, CHUNK, IN_COLS), lambda b, c: (b, c, 0)),
                  _const_spec((3, CHUNK, RET_QK)),
                  _const_spec((LORA, RWKV_WIDTH)),
                  _const_spec((LORA, RWKV_WIDTH))]
                 + [_const_spec((1, RWKV_WIDTH))] * len(vecs)
                 + [_const_spec((1, RET_WIDTH)),
                  _const_spec((GROUP, GROUP)),
                  pl.BlockSpec((BATCH_BLOCK, CHUNK, D), lambda b, c: (b, c, 0)),
                  _const_spec((D, D), single_buffer=True),
                  _const_spec((1, D))],
        out_specs=pl.BlockSpec((BATCH_BLOCK, CHUNK, D), lambda b, c: (b, c, 0)),
        out_shape=jax.ShapeDtypeStruct((B, T, D), f32),
        scratch_shapes=[pltpu.VMEM((BATCH_BLOCK, CHUNK, D), bf16),
                        pltpu.VMEM((BATCH_BLOCK, RWKV_WIDTH // GROUP, RWKV_HEAD, GROUP), f32),
                        pltpu.VMEM((BATCH_BLOCK, RET_HEADS, RET_DK, RET_DV), f32),
                        pltpu.VMEM((D, D), bf16),
                        pltpu.VMEM((2 * LORA, 2 * RWKV_WIDTH), bf16)],
        compiler_params=pltpu.CompilerParams(dimension_semantics=("arbitrary", "arbitrary"),
                                             vmem_limit_bytes=VMEM_LIMIT),
        name="mixers",
    )(p.reshape(B, T, IN_COLS), dec, w_lora_up[0], a_lora_up[0], *vecs,
      ret_gn_g[0][None, :], seg, x, w_out[0], final_norm_g[None, :])
```

```python
import functools

import numpy as np
import jax
import jax.numpy as jnp
from jax import lax
from jax.experimental import pallas as pl
from jax.experimental.pallas import tpu as pltpu

D_MODEL = 1024
CHUNK = 64
RET_HEADS = 4
RET_DK = 64
RET_DV = 128
RET_QK = RET_HEADS * RET_DK
RET_WIDTH = RET_HEADS * RET_DV
RWKV_WIDTH = 512
RWKV_HEAD = 64
LORA = 64
RET_COLS = 2 * RET_QK + 2 * RET_WIDTH
RWKV_COLS = 4 * RWKV_WIDTH + 2 * LORA
IN_COLS = RET_COLS + RWKV_COLS
ROPE_BASE = 10000.0
RMS_EPS = 1e-6
RET_GN_EPS = 1e-5
RWKV_GN_EPS = 64e-5
LANES = 128
GROUP = 256
PROJ_TILE = 512
BATCH_BLOCK = 8
RESULT_LAG = 4
INV_BASE = 8
VMEM_LIMIT = 56 * 1024 * 1024

f32 = jnp.float32
bf16 = jnp.bfloat16


def _mm(a, b):
    return jnp.dot(a.astype(bf16), b.astype(bf16), preferred_element_type=f32)


def _mm_nt(a, b):
    return lax.dot_general(a.astype(bf16), b.astype(bf16), (((1,), (1,)), ((), ())),
                           preferred_element_type=f32)


def _split2(x):
    hi = x.astype(bf16)
    lo = (x - hi.astype(f32)).astype(bf16)
    return hi, lo


def _block_diag(x):
    x = x.astype(bf16)
    rows, lanes = x.shape
    assert lanes in (2 * LANES, 4 * LANES)
    zero = jnp.zeros((rows, LANES), bf16)
    cols = []
    if lanes == 4 * LANES:
        for j in range(4):
            cols.append(jnp.concatenate([x[:, j * LANES:(j + 1) * LANES] if i == j else zero for i in range(4)],
                                        axis=0))
    else:
        low = lax.broadcasted_iota(jnp.int32, (rows, LANES), 1) < LANES // 2
        for j in range(2):
            xj = x[:, j * LANES:(j + 1) * LANES]
            pair = [jnp.where(low, xj, zero), jnp.where(low, zero, xj)]
            cols.append(jnp.concatenate([zero] * (2 * j) + pair + [zero] * (2 - 2 * j), axis=0))
    return jnp.concatenate(cols, axis=1)


def _sigmoid(x):
    return 1.0 / (1.0 + jnp.exp(-x))


def _in_proj_kernel(tiles_per_seq, x_ref, g_ref, w_ref, cos_ref, sin_ref, mu_ref, p_ref, wb_ref, carry_ref):
    i = pl.program_id(0)

    @pl.when(i == 0)
    def _():
        wb_ref[...] = w_ref[...].astype(bf16)

    x = x_ref[...]
    u = (x * lax.rsqrt(jnp.mean(x * x, axis=-1, keepdims=True) + RMS_EPS) * g_ref[...]).astype(bf16)
    rows = x.shape[0]
    seq_start = (i % tiles_per_seq) == 0
    W = RWKV_WIDTH

    def rope(pg):
        cos = jnp.concatenate([cos_ref[...]] * (pg.shape[1] // LANES), axis=1)
        sin = jnp.concatenate([sin_ref[...]] * (pg.shape[1] // LANES), axis=1)
        half = (lax.broadcasted_iota(jnp.int32, pg.shape, 1) & (RET_DK - 1)) < RET_DK // 2
        lanes = pg.shape[1]
        swapped = jnp.where(half, pltpu.roll(pg, lanes - RET_DK // 2, 1), pltpu.roll(pg, RET_DK // 2, 1))
        return pg * cos + swapped * sin

    def silu(pg):
        return pg * _sigmoid(pg)

    def shifted(lo, hi):
        def fn(pg):
            first = jnp.where(seq_start, 0.0, carry_ref[:, lo:hi])
            row0 = lax.broadcasted_iota(jnp.int32, pg.shape, 0) == 0
            prev = jnp.where(row0, first, pltpu.roll(pg, 1, 0))
            carry_ref[:, lo:hi] = pg[rows - 1:rows, :]
            return pg + mu_ref[:, lo:hi] * (prev - pg)
        return fn

    groups = [(0, 2 * RET_QK, rope),
              (2 * RET_QK, 2 * RET_QK + RET_WIDTH, lambda pg: pg),
              (2 * RET_QK + RET_WIDTH, RET_COLS, silu),
              (RET_COLS, RET_COLS + W, shifted(0, W)),
              (RET_COLS + W, RET_COLS + 2 * W, shifted(W, 2 * W)),
              (RET_COLS + 2 * W, RET_COLS + 3 * W, shifted(2 * W, 3 * W)),
              (RET_COLS + 3 * W, RET_COLS + 4 * W, lambda pg: silu(shifted(3 * W, 4 * W)(pg))),
              (RET_COLS + 4 * W, IN_COLS, shifted(4 * W, RWKV_COLS))]
    pending = None
    for lo, hi, fn in groups:
        acc = jnp.dot(u, wb_ref[:, lo:hi], preferred_element_type=f32)
        if pending is not None:
            p_ref[:, pending[1]:pending[2]] = pending[3](pending[0])
        pending = (acc, lo, hi, fn)
    p_ref[:, pending[1]:pending[2]] = pending[3](pending[0])


def _staged(n, produce, consume):
    for i in range(n + RESULT_LAG):
        if i < n:
            produce(i)
        if i >= RESULT_LAG:
            consume(i - RESULT_LAG)


def _retention_chunks(qs, ks, vs, states, decay4, qdec, kdec, head_decay):
    n = range(len(qs))
    hs = range(RET_HEADS)
    scores = [_mm_nt(qs[i], _block_diag(ks[i])) * decay4 for i in n]
    qds = [qs[i] * qdec for i in n]
    low = lax.broadcasted_iota(jnp.int32, (CHUNK, LANES), 1) < RET_DK

    def head_lhs(sc, qd, h):
        sl = slice((h // 2) * LANES, (h // 2 + 1) * LANES)
        if h % 2 == 0:
            return jnp.where(low, sc[:, sl], pltpu.roll(qd[:, sl], RET_DK, 1))
        return jnp.where(low, pltpu.roll(sc[:, sl], RET_DK, 1), qd[:, sl])

    out = [jnp.concatenate(
        [_mm(head_lhs(scores[i], qds[i], h),
             jnp.concatenate([vs[i][:, h * RET_DV:(h + 1) * RET_DV].astype(bf16), states[i][h].astype(bf16)],
                             axis=0)) for h in hs], axis=1) for i in n]
    kts = [(ks[i] * kdec).T for i in n]
    kv = [[_mm(kts[i][h * RET_DK:(h + 1) * RET_DK], vs[i][:, h * RET_DV:(h + 1) * RET_DV]) for h in hs]
          for i in n]
    s_new = [jnp.stack([states[i][h] * head_decay[h] + kv[i][h] for h in hs]) for i in n]
    return out, s_new


def _rwkv_chunks(rs, ks, vs, kkns, aas, lws, m_cs):
    C = CHUNK
    n = len(rs)
    heads = GROUP // RWKV_HEAD
    ti = lax.broadcasted_iota(jnp.int32, (C, C), 0)
    si = lax.broadcasted_iota(jnp.int32, (C, C), 1)
    tri_incl = jnp.where(si <= ti, 1.0, 0.0).astype(bf16)
    tri_incl = jnp.concatenate([tri_incl, tri_incl], axis=1)
    ti = lax.broadcasted_iota(jnp.int32, (C, GROUP), 0)
    si = lax.broadcasted_iota(jnp.int32, (C, GROUP), 1) & (C - 1)
    strict = si < ti
    incl = si <= ti
    eye = jnp.where(si == ti, 1.0, 0.0)
    low = lax.broadcasted_iota(jnp.int32, (RWKV_HEAD, LANES), 1) < RWKV_HEAD

    def slots():
        return [None] * n

    def staged(produce, consume):
        _staged(n, produce, consume)

    cum, lhs, rhs, decay_end, key_end = slots(), slots(), slots(), slots(), slots()

    def cum_matmul(i):
        cum[i] = jnp.dot(tri_incl, jnp.concatenate(_split2(lws[i]), axis=0), preferred_element_type=f32)

    def scale_operands(i):
        g_inc = jnp.exp(cum[i])
        g_inv = jnp.exp(-cum[i])
        g_end = jnp.exp(cum[i][C - 1:C, :] - cum[i])
        beta = kkns[i] * aas[i]
        lhs[i] = jnp.concatenate([-kkns[i] * jnp.exp(cum[i] - lws[i]), rs[i]() * g_inc], axis=0).astype(bf16)
        rhs[i] = jnp.concatenate([_block_diag(beta * g_inv), _block_diag(ks[i] * g_inv)], axis=0)
        decay_end[i] = g_inc[C - 1:C, :]
        key_end[i] = jnp.concatenate([beta * g_end, ks[i] * g_end], axis=0).astype(bf16)

    staged(cum_matmul, scale_operands)

    amat, mv, a_ab, a_k, a_rb, t_inv = slots(), slots(), slots(), slots(), slots(), slots()

    def score_matmuls(i):
        amat[i] = _mm_nt(lhs[i], rhs[i])
        mv[i] = _mm_nt(lhs[i], _block_diag(m_cs[i]))

    def same_block(width):
        shift = width.bit_length() - 1
        return (ti >> shift) == (si >> shift)

    a_off = slots()

    def mask_scores(i):
        lower = jnp.where(strict, amat[i][:C, :GROUP], 0.0)
        base = jnp.where(same_block(INV_BASE), lower, 0.0)
        a_ab[i] = base.astype(bf16)
        a_off[i] = lower - base
        a_rb[i] = jnp.where(incl, amat[i][C:, :GROUP], 0.0).astype(bf16)
        a_k[i] = jnp.concatenate([jnp.where(strict, amat[i][:C, GROUP:], 0.0),
                                  jnp.where(incl, amat[i][C:, GROUP:], 0.0)], axis=0).astype(bf16)
        t_inv[i] = eye + base

    staged(score_matmuls, mask_scores)

    av, st, st_lhs, pow_bd = slots(), slots(), slots(), slots()

    def first_square(i):
        av[i] = _mm(a_k[i], _block_diag(vs[i]()))
        st[i] = _mm(a_ab[i], _block_diag(a_ab[i]))

    def after_first_square(i):
        s_b = st[i].astype(bf16)
        st_lhs[i] = jnp.concatenate([s_b, t_inv[i].astype(bf16)], axis=0)
        pow_bd[i] = _block_diag(s_b)

    staged(first_square, after_first_square)

    def square(i):
        st[i] = _mm(st_lhs[i], pow_bd[i])

    def after_square(i):
        s_b = st[i][:C].astype(bf16)
        t_inv[i] = t_inv[i] + st[i][C:]
        st_lhs[i] = jnp.concatenate([s_b, t_inv[i].astype(bf16)], axis=0)
        pow_bd[i] = _block_diag(s_b)

    for _ in range(INV_BASE.bit_length() - 3):
        staged(square, after_square)

    def last_product(i):
        st[i] = _mm(st_lhs[i][C:], pow_bd[i])

    def after_last_product(i):
        t_inv[i] = t_inv[i] + st[i]

    staged(last_product, after_last_product)

    widths = [INV_BASE << j for j in range((C // INV_BASE).bit_length() - 1)]
    brought_in = [same_block(2 * w) & jnp.logical_not(same_block(w)) for w in widths]
    xs = slots()

    def off_products(i):
        stacked = jnp.concatenate([jnp.where(m, a_off[i], 0.0) for m in brought_in], axis=0)
        st[i] = _mm(stacked, _block_diag(t_inv[i]))

    def after_off_products(i):
        xs[i] = [st[i][m * C:(m + 1) * C] for m in range(len(widths))]

    staged(off_products, after_off_products)

    for j in range(len(widths)):
        def merge(i, j=j):
            st[i] = _mm(jnp.concatenate([t_inv[i]] + xs[i][j + 1:], axis=0), _block_diag(xs[i][j]))

        def after_merge(i, j=j):
            t_inv[i] = t_inv[i] + st[i][:C]
            for m in range(j + 1, len(widths)):
                xs[i][m] = xs[i][m] + st[i][(m - j) * C:(m - j + 1) * C]

        staged(merge, after_merge)

    t_b, y_bd = slots(), slots()
    for i in range(n):
        t_b[i] = t_inv[i].astype(bf16)
        y_bd[i] = _block_diag(mv[i][:C] + av[i][:C])

    u, u_bd, uvt = slots(), slots(), slots()

    def solve(i):
        u[i] = _mm(t_b[i], y_bd[i])

    def after_solve(i):
        u_bd[i] = _block_diag(u[i])
        uvt[i] = jnp.concatenate([u[i], vs[i]()], axis=0).T.astype(bf16)

    staged(solve, after_solve)

    o_u, upd, out, m_new = slots(), slots(), slots(), slots()

    def output_and_update(i):
        o_u[i] = _mm(a_rb[i], u_bd[i])
        upd[i] = [_mm(uvt[i][h * RWKV_HEAD:(h + 1) * RWKV_HEAD],
                      key_end[i][:, (h // 2) * LANES:(h // 2 + 1) * LANES]) for h in range(heads)]

    def finish(i):
        out[i] = mv[i][C:] + av[i][C:] + o_u[i]
        m_new[i] = m_cs[i] * decay_end[i] + jnp.concatenate(
            [jnp.where(low, upd[i][2 * j], upd[i][2 * j + 1]) for j in range(heads // 2)], axis=1)

    staged(output_and_update, finish)
    return out, m_new


def _mix_chunk(p_ref, dec_ref, lora_ref, vec_refs, rgn_ref, seg_ref, y_ref, m_ref, s_ref):
    C = CHUNK
    nb = BATCH_BLOCK
    R = nb * C
    W = RWKV_WIDTH
    NG = W // GROUP

    def rows(x, bi):
        return x[bi * C:(bi + 1) * C]

    def rwkv_cols(lo, hi):
        return p_ref[:, :, RET_COLS + lo:RET_COLS + hi].reshape(R, hi - lo)

    kr = rwkv_cols(W, 2 * W)
    xwa = rwkv_cols(4 * W, RWKV_COLS)
    lane = lax.broadcasted_iota(jnp.int32, xwa.shape, 1)
    lora = _mm(jnp.where(lane < LORA, jnp.tanh(xwa), xwa), lora_ref[...])
    w0, a0, k_k, k_a, r_k, gn_g, gn_b = (ref[...] for ref in vec_refs)
    lw = -np.float32(np.exp(-0.5)) * _sigmoid(w0 + lora[:, :W])
    a = _sigmoid(a0 + lora[:, W:])
    seg = seg_ref[...]

    def segsum(x):
        xs = jnp.concatenate([x[:, g * GROUP:(g + 1) * GROUP] for g in range(NG)], axis=0)
        tot = jnp.dot(xs.astype(bf16), seg, preferred_element_type=f32)
        return jnp.concatenate([tot[g * R:(g + 1) * R] for g in range(NG)], axis=1)

    kk = kr * k_k
    kkn = kk * lax.rsqrt(jnp.maximum(segsum(kk * kk), 1e-24))
    kmod = kr * (1.0 + (a - 1.0) * k_a)

    q = p_ref[:, :, 0:RET_QK].reshape(R, RET_QK)
    k = p_ref[:, :, RET_QK:2 * RET_QK].reshape(R, RET_QK)
    v = p_ref[:, :, 2 * RET_QK:2 * RET_QK + RET_WIDTH].reshape(R, RET_WIDTH)
    rets, s_new = _retention_chunks([rows(q, bi) for bi in range(nb)], [rows(k, bi) for bi in range(nb)],
                                    [rows(v, bi) for bi in range(nb)], [s_ref[bi] for bi in range(nb)],
                                    dec_ref[0], dec_ref[1], dec_ref[2], _RET_CHUNK_DECAY)
    for bi in range(nb):
        s_ref[bi] = s_new[bi]
    ret = jnp.concatenate(rets, axis=0)
    rgn = rgn_ref[...]
    for h in range(RET_HEADS):
        sl = slice(h * RET_DV, (h + 1) * RET_DV)
        xh = ret[:, sl]
        d = xh - jnp.mean(xh, axis=-1, keepdims=True)
        var = jnp.mean(d * d, axis=-1, keepdims=True)
        gh = p_ref[:, :, 2 * RET_QK + RET_WIDTH + h * RET_DV:2 * RET_QK + RET_WIDTH + (h + 1) * RET_DV]
        gh = gh.reshape(R, RET_DV)
        yh = gh * (d * lax.rsqrt(var + RET_GN_EPS) * rgn[:, sl])
        y_ref[:, :, sl] = yh.reshape(nb, C, RET_DV).astype(y_ref.dtype)

    chains = [(bi, g) for bi in range(nb) for g in range(NG)]

    def pick(x):
        return [x[bi * C:(bi + 1) * C, g * GROUP:(g + 1) * GROUP] for bi, g in chains]

    def readers(lo):
        def reader(bi, g):
            start = RET_COLS + lo + g * GROUP
            return lambda: p_ref[bi, :, start:start + GROUP]
        return [reader(bi, g) for bi, g in chains]

    outs, m_new = _rwkv_chunks(readers(0), pick(kmod), readers(2 * W), pick(kkn), pick(a), pick(lw),
                               [m_ref[bi, g] for bi, g in chains])
    for i, (bi, g) in enumerate(chains):
        m_ref[bi, g] = m_new[i]
    o = jnp.concatenate([jnp.concatenate(outs[bi * NG:(bi + 1) * NG], axis=1) for bi in range(nb)], axis=0)
    d = o - segsum(o) * (1.0 / RWKV_HEAD)
    var = segsum(d * d) * (1.0 / RWKV_HEAD)
    o = d * lax.rsqrt(var + RWKV_GN_EPS) * gn_g + gn_b
    bonus = segsum(rwkv_cols(0, W) * kmod * r_k) * rwkv_cols(2 * W, 3 * W)
    y_rw = rwkv_cols(3 * W, 4 * W) * (o + bonus)
    y_ref[:, :, RET_WIDTH:] = y_rw.reshape(nb, C, W).astype(y_ref.dtype)


def _project_chunk(x_ref, y_ref, w_ref, g_ref, o_ref):
    nb, C, D = x_ref.shape
    y = y_ref[...].reshape(nb * C, D)
    h = x_ref[...].reshape(nb * C, D) + jnp.dot(y, w_ref[...], preferred_element_type=f32)
    out = h * lax.rsqrt(jnp.mean(h * h, axis=-1, keepdims=True) + RMS_EPS) * g_ref[...]
    o_ref[...] = out.reshape(nb, C, D)


def _mixer_kernel(p_ref, dec_ref, wlora_ref, alora_ref, w0_ref, a0_ref, kk_ref, ka_ref, rk_ref, gng_ref, gnb_ref,
                  rgn_ref, seg_ref, x_ref, wout_ref, fg_ref, o_ref, y_ref, m_ref, s_ref, woutb_ref, lora_ref):
    @pl.when(pl.program_id(1) == 0)
    def _():
        m_ref[...] = jnp.zeros_like(m_ref)
        s_ref[...] = jnp.zeros_like(s_ref)
        woutb_ref[...] = wout_ref[...].astype(bf16)
        zero = jnp.zeros((LORA, RWKV_WIDTH), bf16)
        lora_ref[...] = jnp.concatenate(
            [jnp.concatenate([wlora_ref[...].astype(bf16), zero], axis=1),
             jnp.concatenate([zero, alora_ref[...].astype(bf16)], axis=1)], axis=0)

    vec_refs = (w0_ref, a0_ref, kk_ref, ka_ref, rk_ref, gng_ref, gnb_ref)
    _mix_chunk(p_ref, dec_ref, lora_ref, vec_refs, rgn_ref, seg_ref, y_ref, m_ref, s_ref)
    _project_chunk(x_ref, y_ref, woutb_ref, fg_ref, o_ref)


def _rope_tables(seq):
    half = RET_DK // 2
    expo = -jnp.arange(half, dtype=f32) / f32(half)
    freqs = jnp.exp(expo * f32(np.log(ROPE_BASE)))
    ang = jnp.arange(seq, dtype=jnp.int32).astype(f32)[:, None] * freqs[None, :]
    cos = jnp.cos(ang)
    sin = jnp.sin(ang)
    cos_full = jnp.tile(jnp.concatenate([cos, cos], axis=1), (1, LANES // RET_DK))
    sin_signed = jnp.tile(jnp.concatenate([-sin, sin], axis=1), (1, LANES // RET_DK))
    return cos_full, sin_signed


_RET_LOG_GAMMA = np.log(1.0 - np.exp2(-5.0 - np.arange(RET_HEADS, dtype=np.float64)))
_RET_CHUNK_DECAY = tuple(np.float32(v) for v in np.exp(_RET_LOG_GAMMA * CHUNK))


def _retention_constants():
    C = CHUNK
    lg = _RET_LOG_GAMMA
    lane_lg = np.repeat(lg, RET_DK)[None, :]
    n = np.arange(C, dtype=np.float64)[:, None]
    m = np.tile(np.arange(C, dtype=np.float64), RET_HEADS)[None, :]
    scale = RET_DK ** -0.5
    decay4 = scale * np.exp(lane_lg * np.abs(n - m))
    qdec = np.exp(lane_lg * (n + 1.0)) * np.ones((1, RET_QK))
    kdec = scale * np.exp(lane_lg * (C - 1.0 - n)) * np.ones((1, RET_QK))
    return jnp.asarray(np.stack([decay4, qdec, kdec]).astype(np.float32))


def _const_spec(shape, single_buffer=False):
    mode = pl.Buffered(1) if single_buffer else None
    return pl.BlockSpec(shape, lambda *_: (0,) * len(shape), pipeline_mode=mode)


@jax.jit
def kernel(x, norm_g, w_in, ret_gn_g, rwkv_mu, w_lora_up, w0, a_lora_up, a0, k_k, k_a, r_k,
           rwkv_gn_g, rwkv_gn_b, w_out, final_norm_g):
    B, T, D = x.shape
    assert D == D_MODEL and T % CHUNK == 0 and T % PROJ_TILE == 0 and B % BATCH_BLOCK == 0
    assert norm_g.shape[0] == 1, "single-layer block"
    n_tok = B * T
    xf = x.reshape(n_tok, D)
    params = pltpu.CompilerParams(dimension_semantics=("arbitrary",), vmem_limit_bytes=VMEM_LIMIT)
    tiles_per_seq = T // PROJ_TILE
    cos, sin = _rope_tables(T)

    p = pl.pallas_call(
        functools.partial(_in_proj_kernel, tiles_per_seq),
        grid=(n_tok // PROJ_TILE,),
        in_specs=[pl.BlockSpec((PROJ_TILE, D), lambda i: (i, 0)),
                  _const_spec((1, D)),
                  _const_spec((D, IN_COLS), single_buffer=True),
                  pl.BlockSpec((PROJ_TILE, LANES), lambda i: (i % tiles_per_seq, 0)),
                  pl.BlockSpec((PROJ_TILE, LANES), lambda i: (i % tiles_per_seq, 0)),
                  _const_spec((1, RWKV_COLS))],
        out_specs=pl.BlockSpec((PROJ_TILE, IN_COLS), lambda i: (i, 0)),
        out_shape=jax.ShapeDtypeStruct((n_tok, IN_COLS), f32),
        scratch_shapes=[pltpu.VMEM((D, IN_COLS), bf16),
                        pltpu.VMEM((1, RWKV_COLS), f32)],
        compiler_params=params,
        name="in_proj",
    )(xf, norm_g[0][None, :], w_in[0], cos, sin, rwkv_mu[0][None, :])

    dec = _retention_constants()
    vecs = [v.reshape(1, RWKV_WIDTH) for v in (w0, a0, k_k, k_a, r_k, rwkv_gn_g, rwkv_gn_b)]
    seg_ids = np.arange(GROUP) // RWKV_HEAD
    seg = jnp.asarray(seg_ids[:, None] == seg_ids[None, :], bf16)

    return pl.pallas_call(
        _mixer_kernel,
        grid=(B // BATCH_BLOCK, T // CHUNK),
        in_specs=[pl.BlockSpec((BATCH_BLOCK, CHUNK, IN_COLS), lambda b, c: (b, c, 0)),
                  _const_spec((3, CHUNK, RET_QK)),
                  _const_spec((LORA, RWKV_WIDTH)),
                  _const_spec((LORA, RWKV_WIDTH))]
                 + [_const_spec((1, RWKV_WIDTH))] * len(vecs)
                 + [_const_spec((1, RET_WIDTH)),
                  _const_spec((GROUP, GROUP)),
                  pl.BlockSpec((BATCH_BLOCK, CHUNK, D), lambda b, c: (b, c, 0)),
                  _const_spec((D, D), single_buffer=True),
                  _const_spec((1, D))],
        out_specs=pl.BlockSpec((BATCH_BLOCK, CHUNK, D), lambda b, c: (b, c, 0)),
        out_shape=jax.ShapeDtypeStruct((B, T, D), f32),
        scratch_shapes=[pltpu.VMEM((BATCH_BLOCK, CHUNK, D), bf16),
                        pltpu.VMEM((BATCH_BLOCK, RWKV_WIDTH // GROUP, RWKV_HEAD, GROUP), f32),
                        pltpu.VMEM((BATCH_BLOCK, RET_HEADS, RET_DK, RET_DV), f32),
                        pltpu.VMEM((D, D), bf16),
                        pltpu.VMEM((2 * LORA, 2 * RWKV_WIDTH), bf16)],
        compiler_params=pltpu.CompilerParams(dimension_semantics=("arbitrary", "arbitrary"),
                                             vmem_limit_bytes=VMEM_LIMIT),
        name="mixers",
    )(p.reshape(B, T, IN_COLS), dec, w_lora_up[0], a_lora_up[0], *vecs,
      ret_gn_g[0][None, :], seg, x, w_out[0], final_norm_g[None, :])
```

```python
import functools

import numpy as np
import jax
import jax.numpy as jnp
from jax import lax
from jax.experimental import pallas as pl
from jax.experimental.pallas import tpu as pltpu

D_MODEL = 1024
CHUNK = 64
RET_HEADS = 4
RET_DK = 64
RET_DV = 128
RET_QK = RET_HEADS * RET_DK
RET_WIDTH = RET_HEADS * RET_DV
RWKV_WIDTH = 512
RWKV_HEAD = 64
LORA = 64
RET_COLS = 2 * RET_QK + 2 * RET_WIDTH
RWKV_COLS = 4 * RWKV_WIDTH + 2 * LORA
IN_COLS = RET_COLS + RWKV_COLS
ROPE_BASE = 10000.0
RMS_EPS = 1e-6
RET_GN_EPS = 1e-5
RWKV_GN_EPS = 64e-5
LANES = 128
GROUP = 256
PROJ_TILE = 512
BATCH_BLOCK = 8
RESULT_LAG = 4
INV_BASE = 8
VMEM_LIMIT = 56 * 1024 * 1024

f32 = jnp.float32
bf16 = jnp.bfloat16


def _mm(a, b):
    return jnp.dot(a.astype(bf16), b.astype(bf16), preferred_element_type=f32)


def _mm_nt(a, b):
    return lax.dot_general(a.astype(bf16), b.astype(bf16), (((1,), (1,)), ((), ())),
                           preferred_element_type=f32)


def _split2(x):
    hi = x.astype(bf16)
    lo = (x - hi.astype(f32)).astype(bf16)
    return hi, lo


def _block_diag(x):
    x = x.astype(bf16)
    rows, lanes = x.shape
    assert lanes in (2 * LANES, 4 * LANES)
    zero = jnp.zeros((rows, LANES), bf16)
    cols = []
    if lanes == 4 * LANES:
        for j in range(4):
            cols.append(jnp.concatenate([x[:, j * LANES:(j + 1) * LANES] if i == j else zero for i in range(4)],
                                        axis=0))
    else:
        low = lax.broadcasted_iota(jnp.int32, (rows, LANES), 1) < LANES // 2
        for j in range(2):
            xj = x[:, j * LANES:(j + 1) * LANES]
            pair = [jnp.where(low, xj, zero), jnp.where(low, zero, xj)]
            cols.append(jnp.concatenate([zero] * (2 * j) + pair + [zero] * (2 - 2 * j), axis=0))
    return jnp.concatenate(cols, axis=1)


def _sigmoid(x):
    return 1.0 / (1.0 + jnp.exp(-x))


def _in_proj_kernel(tiles_per_seq, x_ref, g_ref, w_ref, cos_ref, sin_ref, mu_ref, p_ref, wb_ref, carry_ref):
    i = pl.program_id(0)

    @pl.when(i == 0)
    def _():
        wb_ref[...] = w_ref[...].astype(bf16)

    x = x_ref[...]
    u = (x * lax.rsqrt(jnp.mean(x * x, axis=-1, keepdims=True) + RMS_EPS) * g_ref[...]).astype(bf16)
    rows = x.shape[0]
    seq_start = (i % tiles_per_seq) == 0
    W = RWKV_WIDTH

    def rope(pg):
        cos = jnp.concatenate([cos_ref[...]] * (pg.shape[1] // LANES), axis=1)
        sin = jnp.concatenate([sin_ref[...]] * (pg.shape[1] // LANES), axis=1)
        half = (lax.broadcasted_iota(jnp.int32, pg.shape, 1) & (RET_DK - 1)) < RET_DK // 2
        lanes = pg.shape[1]
        swapped = jnp.where(half, pltpu.roll(pg, lanes - RET_DK // 2, 1), pltpu.roll(pg, RET_DK // 2, 1))
        return pg * cos + swapped * sin

    def silu(pg):
        return pg * _sigmoid(pg)

    def shifted(lo, hi):
        def fn(pg):
            first = jnp.where(seq_start, 0.0, carry_ref[:, lo:hi])
            row0 = lax.broadcasted_iota(jnp.int32, pg.shape, 0) == 0
            prev = jnp.where(row0, first, pltpu.roll(pg, 1, 0))
            carry_ref[:, lo:hi] = pg[rows - 1:rows, :]
            return pg + mu_ref[:, lo:hi] * (prev - pg)
        return fn

    groups = [(0, 2 * RET_QK, rope),
              (2 * RET_QK, 2 * RET_QK + RET_WIDTH, lambda pg: pg),
              (2 * RET_QK + RET_WIDTH, RET_COLS, silu),
              (RET_COLS, RET_COLS + W, shifted(0, W)),
              (RET_COLS + W, RET_COLS + 2 * W, shifted(W, 2 * W)),
              (RET_COLS + 2 * W, RET_COLS + 3 * W, shifted(2 * W, 3 * W)),
              (RET_COLS + 3 * W, RET_COLS + 4 * W, lambda pg: silu(shifted(3 * W, 4 * W)(pg))),
              (RET_COLS + 4 * W, IN_COLS, shifted(4 * W, RWKV_COLS))]
    pending = None
    for lo, hi, fn in groups:
        acc = jnp.dot(u, wb_ref[:, lo:hi], preferred_element_type=f32)
        if pending is not None:
            p_ref[:, pending[1]:pending[2]] = pending[3](pending[0])
        pending = (acc, lo, hi, fn)
    p_ref[:, pending[1]:pending[2]] = pending[3](pending[0])


def _staged(n, produce, consume):
    for i in range(n + RESULT_LAG):
        if i < n:
            produce(i)
        if i >= RESULT_LAG:
            consume(i - RESULT_LAG)


def _retention_chunks(qs, ks, vs, states, decay4, qdec, kdec, head_decay):
    n = range(len(qs))
    hs = range(RET_HEADS)
    scores = [_mm_nt(qs[i], _block_diag(ks[i])) * decay4 for i in n]
    qds = [qs[i] * qdec for i in n]
    low = lax.broadcasted_iota(jnp.int32, (CHUNK, LANES), 1) < RET_DK

    def head_lhs(sc, qd, h):
        sl = slice((h // 2) * LANES, (h // 2 + 1) * LANES)
        if h % 2 == 0:
            return jnp.where(low, sc[:, sl], pltpu.roll(qd[:, sl], RET_DK, 1))
        return jnp.where(low, pltpu.roll(sc[:, sl], RET_DK, 1), qd[:, sl])

    out = [jnp.concatenate(
        [_mm(head_lhs(scores[i], qds[i], h),
             jnp.concatenate([vs[i][:, h * RET_DV:(h + 1) * RET_DV].astype(bf16), states[i][h].astype(bf16)],
                             axis=0)) for h in hs], axis=1) for i in n]
    kts = [(ks[i] * kdec).T for i in n]
    kv = [[_mm(kts[i][h * RET_DK:(h + 1) * RET_DK], vs[i][:, h * RET_DV:(h + 1) * RET_DV]) for h in hs]
          for i in n]
    s_new = [jnp.stack([states[i][h] * head_decay[h] + kv[i][h] for h in hs]) for i in n]
    return out, s_new


def _rwkv_chunks(rs, ks, vs, kkns, aas, lws, m_cs):
    C = CHUNK
    n = len(rs)
    heads = GROUP // RWKV_HEAD
    ti = lax.broadcasted_iota(jnp.int32, (C, C), 0)
    si = lax.broadcasted_iota(jnp.int32, (C, C), 1)
    tri_incl = jnp.where(si <= ti, 1.0, 0.0).astype(bf16)
    tri_incl = jnp.concatenate([tri_incl, tri_incl], axis=1)
    ti = lax.broadcasted_iota(jnp.int32, (C, GROUP), 0)
    si = lax.broadcasted_iota(jnp.int32, (C, GROUP), 1) & (C - 1)
    strict = si < ti
    incl = si <= ti
    eye = jnp.where(si == ti, 1.0, 0.0)
    low = lax.broadcasted_iota(jnp.int32, (RWKV_HEAD, LANES), 1) < RWKV_HEAD

    def slots():
        return [None] * n

    def staged(produce, consume):
        _staged(n, produce, consume)

    cum, lhs, rhs, decay_end, key_end = slots(), slots(), slots(), slots(), slots()

    def cum_matmul(i):
        cum[i] = jnp.dot(tri_incl, jnp.concatenate(_split2(lws[i]), axis=0), preferred_element_type=f32)

    def scale_operands(i):
        g_inc = jnp.exp(cum[i])
        g_inv = jnp.exp(-cum[i])
        g_end = jnp.exp(cum[i][C - 1:C, :] - cum[i])
        beta = kkns[i] * aas[i]
        lhs[i] = jnp.concatenate([-kkns[i] * jnp.exp(cum[i] - lws[i]), rs[i]() * g_inc], axis=0).astype(bf16)
        rhs[i] = jnp.concatenate([_block_diag(beta * g_inv), _block_diag(ks[i] * g_inv)], axis=0)
        decay_end[i] = g_inc[C - 1:C, :]
        key_end[i] = jnp.concatenate([beta * g_end, ks[i] * g_end], axis=0).astype(bf16)

    staged(cum_matmul, scale_operands)

    amat, mv, a_ab, a_k, a_rb, t_inv = slots(), slots(), slots(), slots(), slots(), slots()

    def score_matmuls(i):
        amat[i] = _mm_nt(lhs[i], rhs[i])
        mv[i] = _mm_nt(lhs[i], _block_diag(m_cs[i]))

    def same_block(width):
        shift = width.bit_length() - 1
        return (ti >> shift) == (si >> shift)

    a_off = slots()

    def mask_scores(i):
        lower = jnp.where(strict, amat[i][:C, :GROUP], 0.0)
        base = jnp.where(same_block(INV_BASE), lower, 0.0)
        a_ab[i] = base.astype(bf16)
        a_off[i] = lower - base
        a_rb[i] = jnp.where(incl, amat[i][C:, :GROUP], 0.0).astype(bf16)
        a_k[i] = jnp.concatenate([jnp.where(strict, amat[i][:C, GROUP:], 0.0),
                                  jnp.where(incl, amat[i][C:, GROUP:], 0.0)], axis=0).astype(bf16)
        t_inv[i] = eye + base

    staged(score_matmuls, mask_scores)

    av, st, st_lhs, pow_bd = slots(), slots(), slots(), slots()

    def first_square(i):
        av[i] = _mm(a_k[i], _block_diag(vs[i]()))
        st[i] = _mm(a_ab[i], _block_diag(a_ab[i]))

    def after_first_square(i):
        s_b = st[i].astype(bf16)
        st_lhs[i] = jnp.concatenate([s_b, t_inv[i].astype(bf16)], axis=0)
        pow_bd[i] = _block_diag(s_b)

    staged(first_square, after_first_square)

    def square(i):
        st[i] = _mm(st_lhs[i], pow_bd[i])

    def after_square(i):
        s_b = st[i][:C].astype(bf16)
        t_inv[i] = t_inv[i] + st[i][C:]
        st_lhs[i] = jnp.concatenate([s_b, t_inv[i].astype(bf16)], axis=0)
        pow_bd[i] = _block_diag(s_b)

    for _ in range(INV_BASE.bit_length() - 3):
        staged(square, after_square)

    def last_product(i):
        st[i] = _mm(st_lhs[i][C:], pow_bd[i])

    def after_last_product(i):
        t_inv[i] = t_inv[i] + st[i]

    staged(last_product, after_last_product)

    widths = [INV_BASE << j for j in range((C // INV_BASE).bit_length() - 1)]
    brought_in = [same_block(2 * w) & jnp.logical_not(same_block(w)) for w in widths]
    xs = slots()

    def off_products(i):
        stacked = jnp.concatenate([jnp.where(m, a_off[i], 0.0) for m in brought_in], axis=0)
        st[i] = _mm(stacked, _block_diag(t_inv[i]))

    def after_off_products(i):
        xs[i] = [st[i][m * C:(m + 1) * C] for m in range(len(widths))]

    staged(off_products, after_off_products)

    for j in range(len(widths)):
        def merge(i, j=j):
            st[i] = _mm(jnp.concatenate([t_inv[i]] + xs[i][j + 1:], axis=0), _block_diag(xs[i][j]))

        def after_merge(i, j=j):
            t_inv[i] = t_inv[i] + st[i][:C]
            for m in range(j + 1, len(widths)):
                xs[i][m] = xs[i][m] + st[i][(m - j) * C:(m - j + 1) * C]

        staged(merge, after_merge)

    t_b, y_bd = slots(), slots()
    for i in range(n):
        t_b[i] = t_inv[i].astype(bf16)
        y_bd[i] = _block_diag(mv[i][:C] + av[i][:C])

    u, u_bd, uvt = slots(), slots(), slots()

    def solve(i):
        u[i] = _mm(t_b[i], y_bd[i])

    def after_solve(i):
        u_bd[i] = _block_diag(u[i])
        uvt[i] = jnp.concatenate([u[i], vs[i]()], axis=0).T.astype(bf16)

    staged(solve, after_solve)

    o_u, upd, out, m_new = slots(), slots(), slots(), slots()

    def output_and_update(i):
        o_u[i] = _mm(a_rb[i], u_bd[i])
        upd[i] = [_mm(uvt[i][h * RWKV_HEAD:(h + 1) * RWKV_HEAD],
                      key_end[i][:, (h // 2) * LANES:(h // 2 + 1) * LANES]) for h in range(heads)]

    def finish(i):
        out[i] = mv[i][C:] + av[i][C:] + o_u[i]
        m_new[i] = m_cs[i] * decay_end[i] + jnp.concatenate(
            [jnp.where(low, upd[i][2 * j], upd[i][2 * j + 1]) for j in range(heads // 2)], axis=1)

    staged(output_and_update, finish)
    return out, m_new


def _mix_chunk(p_ref, dec_ref, lora_ref, vec_refs, rgn_ref, seg_ref, y_ref, m_ref, s_ref):
    C = CHUNK
    nb = BATCH_BLOCK
    R = nb * C
    W = RWKV_WIDTH
    NG = W // GROUP

    def rows(x, bi):
        return x[bi * C:(bi + 1) * C]

    def rwkv_cols(lo, hi):
        return p_ref[:, :, RET_COLS + lo:RET_COLS + hi].reshape(R, hi - lo)

    kr = rwkv_cols(W, 2 * W)
    xwa = rwkv_cols(4 * W, RWKV_COLS)
    lane = lax.broadcasted_iota(jnp.int32, xwa.shape, 1)
    lora = _mm(jnp.where(lane < LORA, jnp.tanh(xwa), xwa), lora_ref[...])
    w0, a0, k_k, k_a, r_k, gn_g, gn_b = (ref[...] for ref in vec_refs)
    seg = seg_ref[...]

    def segsum(x):
        xs = jnp.concatenate([x[:, g * GROUP:(g + 1) * GROUP] for g in range(NG)], axis=0)
        tot = jnp.dot(xs.astype(bf16), seg, preferred_element_type=f32)
        return jnp.concatenate([tot[g * R:(g + 1) * R] for g in range(NG)], axis=1)

    kk = kr * k_k
    kk_sq = segsum(kk * kk)

    q = p_ref[:, :, 0:RET_QK].reshape(R, RET_QK)
    k = p_ref[:, :, RET_QK:2 * RET_QK].reshape(R, RET_QK)
    v = p_ref[:, :, 2 * RET_QK:2 * RET_QK + RET_WIDTH].reshape(R, RET_WIDTH)
    rets, s_new = _retention_chunks([rows(q, bi) for bi in range(nb)], [rows(k, bi) for bi in range(nb)],
                                    [rows(v, bi) for bi in range(nb)], [s_ref[bi] for bi in range(nb)],
                                    dec_ref[0], dec_ref[1], dec_ref[2], _RET_CHUNK_DECAY)
    for bi in range(nb):
        s_ref[bi] = s_new[bi]
    ret = jnp.concatenate(rets, axis=0)
    rgn = rgn_ref[...]
    for h in range(RET_HEADS):
        sl = slice(h * RET_DV, (h + 1) * RET_DV)
        xh = ret[:, sl]
        d = xh - jnp.mean(xh, axis=-1, keepdims=True)
        var = jnp.mean(d * d, axis=-1, keepdims=True)
        gh = p_ref[:, :, 2 * RET_QK + RET_WIDTH + h * RET_DV:2 * RET_QK + RET_WIDTH + (h + 1) * RET_DV]
        gh = gh.reshape(R, RET_DV)
        yh = gh * (d * lax.rsqrt(var + RET_GN_EPS) * rgn[:, sl])
        y_ref[:, :, sl] = yh.reshape(nb, C, RET_DV).astype(y_ref.dtype)

    kr = rwkv_cols(W, 2 * W)
    lw = -np.float32(np.exp(-0.5)) * _sigmoid(w0 + lora[:, :W])
    a = _sigmoid(a0 + lora[:, W:])
    kkn = (kr * k_k) * lax.rsqrt(jnp.maximum(kk_sq, 1e-24))
    kmod = kr * (1.0 + (a - 1.0) * k_a)
    chains =[(bi, g) for bi in range(nb) for g in range(NG)]

    def pick(x):
        return [x[bi * C:(bi + 1) * C, g * GROUP:(g + 1) * GROUP] for bi, g in chains]

    def readers(lo):
        def reader(bi, g):
            start = RET_COLS + lo + g * GROUP
            return lambda: p_ref[bi, :, start:start + GROUP]
        return [reader(bi, g) for bi, g in chains]

    outs, m_new = _rwkv_chunks(readers(0), pick(kmod), readers(2 * W), pick(kkn), pick(a), pick(lw),
                               [m_ref[bi, g] for bi, g in chains])
    for i, (bi, g) in enumerate(chains):
        m_ref[bi, g] = m_new[i]
    o = jnp.concatenate([jnp.concatenate(outs[bi * NG:(bi + 1) * NG], axis=1) for bi in range(nb)], axis=0)
    d = o - segsum(o) * (1.0 / RWKV_HEAD)
    var = segsum(d * d) * (1.0 / RWKV_HEAD)
    o = d * lax.rsqrt(var + RWKV_GN_EPS) * gn_g + gn_b
    bonus = segsum(rwkv_cols(0, W) * kmod * r_k) * rwkv_cols(2 * W, 3 * W)
    y_rw = rwkv_cols(3 * W, 4 * W) * (o + bonus)
    y_ref[:, :, RET_WIDTH:] = y_rw.reshape(nb, C, W).astype(y_ref.dtype)


def _project_chunk(x_ref, y_ref, w_ref, g_ref, o_ref):
    nb, C, D = x_ref.shape
    y = y_ref[...].reshape(nb * C, D)
    h = x_ref[...].reshape(nb * C, D) + jnp.dot(y, w_ref[...], preferred_element_type=f32)
    out = h * lax.rsqrt(jnp.mean(h * h, axis=-1, keepdims=True) + RMS_EPS) * g_ref[...]
    o_ref[...] = out.reshape(nb, C, D)


def _mixer_kernel(p_ref, dec_ref, wlora_ref, alora_ref, w0_ref, a0_ref, kk_ref, ka_ref, rk_ref, gng_ref, gnb_ref,
                  rgn_ref, seg_ref, x_ref, wout_ref, fg_ref, o_ref, y_ref, m_ref, s_ref, woutb_ref, lora_ref):
    @pl.when(pl.program_id(1) == 0)
    def _():
        m_ref[...] = jnp.zeros_like(m_ref)
        s_ref[...] = jnp.zeros_like(s_ref)
        woutb_ref[...] = wout_ref[...].astype(bf16)
        zero = jnp.zeros((LORA, RWKV_WIDTH), bf16)
        lora_ref[...] = jnp.concatenate(
            [jnp.concatenate([wlora_ref[...].astype(bf16), zero], axis=1),
             jnp.concatenate([zero, alora_ref[...].astype(bf16)], axis=1)], axis=0)

    vec_refs = (w0_ref, a0_ref, kk_ref, ka_ref, rk_ref, gng_ref, gnb_ref)
    _mix_chunk(p_ref, dec_ref, lora_ref, vec_refs, rgn_ref, seg_ref, y_ref, m_ref, s_ref)
    _project_chunk(x_ref, y_ref, woutb_ref, fg_ref, o_ref)


def _rope_tables(seq):
    half = RET_DK // 2
    expo = -jnp.arange(half, dtype=f32) / f32(half)
    freqs = jnp.exp(expo * f32(np.log(ROPE_BASE)))
    ang = jnp.arange(seq, dtype=jnp.int32).astype(f32)[:, None] * freqs[None, :]
    cos = jnp.cos(ang)
    sin = jnp.sin(ang)
    cos_full = jnp.tile(jnp.concatenate([cos, cos], axis=1), (1, LANES // RET_DK))
    sin_signed = jnp.tile(jnp.concatenate([-sin, sin], axis=1), (1, LANES // RET_DK))
    return cos_full, sin_signed


_RET_LOG_GAMMA = np.log(1.0 - np.exp2(-5.0 - np.arange(RET_HEADS, dtype=np.float64)))
_RET_CHUNK_DECAY = tuple(np.float32(v) for v in np.exp(_RET_LOG_GAMMA * CHUNK))


def _retention_constants():
    C = CHUNK
    lg = _RET_LOG_GAMMA
    lane_lg = np.repeat(lg, RET_DK)[None, :]
    n = np.arange(C, dtype=np.float64)[:, None]
    m = np.tile(np.arange(C, dtype=np.float64), RET_HEADS)[None, :]
    scale = RET_DK ** -0.5
    decay4 = scale * np.exp(lane_lg * np.abs(n - m))
    qdec = np.exp(lane_lg * (n + 1.0)) * np.ones((1, RET_QK))
    kdec = scale * np.exp(lane_lg * (C - 1.0 - n)) * np.ones((1, RET_QK))
    return jnp.asarray(np.stack([decay4, qdec, kdec]).astype(np.float32))


def _const_spec(shape, single_buffer=False):
    mode = pl.Buffered(1) if single_buffer else None
    return pl.BlockSpec(shape, lambda *_: (0,) * len(shape), pipeline_mode=mode)


@jax.jit
def kernel(x, norm_g, w_in, ret_gn_g, rwkv_mu, w_lora_up, w0, a_lora_up, a0, k_k, k_a, r_k,
           rwkv_gn_g, rwkv_gn_b, w_out, final_norm_g):
    B, T, D = x.shape
    assert D == D_MODEL and T % CHUNK == 0 and T % PROJ_TILE == 0 and B % BATCH_BLOCK == 0
    assert norm_g.shape[0] == 1, "single-layer block"
    n_tok = B * T
    xf = x.reshape(n_tok, D)
    params = pltpu.CompilerParams(dimension_semantics=("arbitrary",), vmem_limit_bytes=VMEM_LIMIT)
    tiles_per_seq = T // PROJ_TILE
    cos, sin = _rope_tables(T)

    p = pl.pallas_call(
        functools.partial(_in_proj_kernel, tiles_per_seq),
        grid=(n_tok // PROJ_TILE,),
        in_specs=[pl.BlockSpec((PROJ_TILE, D), lambda i: (i, 0)),
                  _const_spec((1, D)),
                  _const_spec((D, IN_COLS), single_buffer=True),
                  pl.BlockSpec((PROJ_TILE, LANES), lambda i: (i % tiles_per_seq, 0)),
                  pl.BlockSpec((PROJ_TILE, LANES), lambda i: (i % tiles_per_seq, 0)),
                  _const_spec((1, RWKV_COLS))],
        out_specs=pl.BlockSpec((PROJ_TILE, IN_COLS), lambda i: (i, 0)),
        out_shape=jax.ShapeDtypeStruct((n_tok, IN_COLS), f32),
        scratch_shapes=[pltpu.VMEM((D, IN_COLS), bf16),
                        pltpu.VMEM((1, RWKV_COLS), f32)],
        compiler_params=params,
        name="in_proj",
    )(xf, norm_g[0][None, :], w_in[0], cos, sin, rwkv_mu[0][None, :])

    dec = _retention_constants()
    vecs = [v.reshape(1, RWKV_WIDTH) for v in (w0, a0, k_k, k_a, r_k, rwkv_gn_g, rwkv_gn_b)]
    seg_ids = np.arange(GROUP) // RWKV_HEAD
    seg = jnp.asarray(seg_ids[:, None] == seg_ids[None, :], bf16)

    return pl.pallas_call(
        _mixer_kernel,
        grid=(B // BATCH_BLOCK, T // CHUNK),
        in_specs=[pl.BlockSpec((BATCH_BLOCK, CHUNK, IN_COLS), lambda b, c: (b, c, 0)),
                  _const_spec((3, CHUNK, RET_QK)),
                  _const_spec((LORA, RWKV_WIDTH)),
                  _const_spec((LORA, RWKV_WIDTH))]
                 + [_const_spec((1, RWKV_WIDTH))] * len(vecs)
                 + [_const_spec((1, RET_WIDTH)),
                  _const_spec((GROUP, GROUP)),
                  pl.BlockSpec((BATCH_BLOCK, CHUNK, D), lambda b, c: (b, c, 0)),
                  _const_spec((D, D), single_buffer=True),
                  _const_spec((1, D))],
        out_specs=pl.BlockSpec((BATCH_BLOCK, CHUNK, D), lambda b, c: (b, c, 0)),
        out_shape=jax.ShapeDtypeStruct((B, T, D), f32),
        scratch_shapes=[pltpu.VMEM((BATCH_BLOCK, CHUNK, D), bf16),
                        pltpu.VMEM((BATCH_BLOCK, RWKV_WIDTH // GROUP, RWKV_HEAD, GROUP), f32),
                        pltpu.VMEM((BATCH_BLOCK, RET_HEADS, RET_DK, RET_DV), f32),
                        pltpu.VMEM((D, D), bf16),
                        pltpu.VMEM((2 * LORA, 2 * RWKV_WIDTH), bf16)],
        compiler_params=pltpu.CompilerParams(dimension_semantics=("arbitrary", "arbitrary"),
                                             vmem_limit_bytes=VMEM_LIMIT),
        name="mixers",
    )(p.reshape(B, T, IN_COLS), dec, w_lora_up[0], a_lora_up[0], *vecs,
      ret_gn_g[0][None, :], seg, x, w_out[0], final_norm_g[None, :])
```

```python
import functools

import numpy as np
import jax
import jax.numpy as jnp
from jax import lax
from jax.experimental import pallas as pl
from jax.experimental.pallas import tpu as pltpu

D_MODEL = 1024
CHUNK = 64
RET_HEADS = 4
RET_DK = 64
RET_DV = 128
RET_QK = RET_HEADS * RET_DK
RET_WIDTH = RET_HEADS * RET_DV
RWKV_WIDTH = 512
RWKV_HEAD = 64
LORA = 64
RET_COLS = 2 * RET_QK + 2 * RET_WIDTH
RWKV_COLS = 4 * RWKV_WIDTH + 2 * LORA
IN_COLS = RET_COLS + RWKV_COLS
ROPE_BASE = 10000.0
RMS_EPS = 1e-6
RET_GN_EPS = 1e-5
RWKV_GN_EPS = 64e-5
LANES = 128
GROUP = 256
PROJ_TILE = 512
BATCH_BLOCK = 8
RESULT_LAG = 4
INV_BASE = 8
VMEM_LIMIT = 56 * 1024 * 1024

f32 = jnp.float32
bf16 = jnp.bfloat16


def _mm(a, b):
    return jnp.dot(a.astype(bf16), b.astype(bf16), preferred_element_type=f32)


def _mm_nt(a, b):
    return lax.dot_general(a.astype(bf16), b.astype(bf16), (((1,), (1,)), ((), ())),
                           preferred_element_type=f32)


def _split2(x):
    hi = x.astype(bf16)
    lo = (x - hi.astype(f32)).astype(bf16)
    return hi, lo


def _block_diag(x):
    x = x.astype(bf16)
    rows, lanes = x.shape
    assert lanes in (2 * LANES, 4 * LANES)
    zero = jnp.zeros((rows, LANES), bf16)
    cols = []
    if lanes == 4 * LANES:
        for j in range(4):
            cols.append(jnp.concatenate([x[:, j * LANES:(j + 1) * LANES] if i == j else zero for i in range(4)],
                                        axis=0))
    else:
        low = lax.broadcasted_iota(jnp.int32, (rows, LANES), 1) < LANES // 2
        for j in range(2):
            xj = x[:, j * LANES:(j + 1) * LANES]
            pair = [jnp.where(low, xj, zero), jnp.where(low, zero, xj)]
            cols.append(jnp.concatenate([zero] * (2 * j) + pair + [zero] * (2 - 2 * j), axis=0))
    return jnp.concatenate(cols, axis=1)


def _sigmoid(x):
    return 1.0 / (1.0 + jnp.exp(-x))


def _in_proj_kernel(tiles_per_seq, x_ref, g_ref, w_ref, cos_ref, sin_ref, mu_ref, p_ref, wb_ref, carry_ref):
    i = pl.program_id(0)

    @pl.when(i == 0)
    def _():
        wb_ref[...] = w_ref[...].astype(bf16)

    x = x_ref[...]
    u = (x * lax.rsqrt(jnp.mean(x * x, axis=-1, keepdims=True) + RMS_EPS) * g_ref[...]).astype(bf16)
    rows = x.shape[0]
    seq_start = (i % tiles_per_seq) == 0
    W = RWKV_WIDTH

    def rope(pg):
        cos = jnp.concatenate([cos_ref[...]] * (pg.shape[1] // LANES), axis=1)
        sin = jnp.concatenate([sin_ref[...]] * (pg.shape[1] // LANES), axis=1)
        half = (lax.broadcasted_iota(jnp.int32, pg.shape, 1) & (RET_DK - 1)) < RET_DK // 2
        lanes = pg.shape[1]
        swapped = jnp.where(half, pltpu.roll(pg, lanes - RET_DK // 2, 1), pltpu.roll(pg, RET_DK // 2, 1))
        return pg * cos + swapped * sin

    def silu(pg):
        return pg * _sigmoid(pg)

    def shifted(lo, hi):
        def fn(pg):
            first = jnp.where(seq_start, 0.0, carry_ref[:, lo:hi])
            row0 = lax.broadcasted_iota(jnp.int32, pg.shape, 0) == 0
            prev = jnp.where(row0, first, pltpu.roll(pg, 1, 0))
            carry_ref[:, lo:hi] = pg[rows - 1:rows, :]
            return pg + mu_ref[:, lo:hi] * (prev - pg)
        return fn

    groups = [(0, 2 * RET_QK, rope),
              (2 * RET_QK, 2 * RET_QK + RET_WIDTH, lambda pg: pg),
              (2 * RET_QK + RET_WIDTH, RET_COLS, silu),
              (RET_COLS, RET_COLS + W, shifted(0, W)),
              (RET_COLS + W, RET_COLS + 2 * W, shifted(W, 2 * W)),
              (RET_COLS + 2 * W, RET_COLS + 3 * W, shifted(2 * W, 3 * W)),
              (RET_COLS + 3 * W, RET_COLS + 4 * W, lambda pg: silu(shifted(3 * W, 4 * W)(pg))),
              (RET_COLS + 4 * W, IN_COLS, shifted(4 * W, RWKV_COLS))]
    pending = None
    for lo, hi, fn in groups:
        acc = jnp.dot(u, wb_ref[:, lo:hi], preferred_element_type=f32)
        if pending is not None:
            p_ref[:, pending[1]:pending[2]] = pending[3](pending[0])
        pending = (acc, lo, hi, fn)
    p_ref[:, pending[1]:pending[2]] = pending[3](pending[0])


def _staged(n, produce, consume):
    for i in range(n + RESULT_LAG):
        if i < n:
            produce(i)
        if i >= RESULT_LAG:
            consume(i - RESULT_LAG)


def _retention_chunks(qs, ks, vs, states, decay4, qdec, kdec, head_decay):
    n = range(len(qs))
    hs = range(RET_HEADS)
    scores = [_mm_nt(qs[i](), _block_diag(ks[i]())) * decay4 for i in n]
    qds = [qs[i]() * qdec for i in n]
    low = lax.broadcasted_iota(jnp.int32, (CHUNK, LANES), 1) < RET_DK

    def head_lhs(sc, qd, h):
        sl = slice((h // 2) * LANES, (h // 2 + 1) * LANES)
        if h % 2 == 0:
            return jnp.where(low, sc[:, sl], pltpu.roll(qd[:, sl], RET_DK, 1))
        return jnp.where(low, pltpu.roll(sc[:, sl], RET_DK, 1), qd[:, sl])

    out = [jnp.concatenate(
        [_mm(head_lhs(scores[i], qds[i], h),
             jnp.concatenate([vs[i][h]().astype(bf16), states[i][h].astype(bf16)],
                             axis=0)) for h in hs], axis=1) for i in n]
    kts = [(ks[i]() * kdec).T for i in n]
    kv = [[_mm(kts[i][h * RET_DK:(h + 1) * RET_DK], vs[i][h]()) for h in hs]
          for i in n]
    s_new = [jnp.stack([states[i][h] * head_decay[h] + kv[i][h] for h in hs]) for i in n]
    return out, s_new


def _rwkv_chunks(rs, ks, vs, kkns, aas, lws, m_cs):
    C = CHUNK
    n = len(rs)
    heads = GROUP // RWKV_HEAD
    ti = lax.broadcasted_iota(jnp.int32, (C, C), 0)
    si = lax.broadcasted_iota(jnp.int32, (C, C), 1)
    tri_incl = jnp.where(si <= ti, 1.0, 0.0).astype(bf16)
    tri_incl = jnp.concatenate([tri_incl, tri_incl], axis=1)
    ti = lax.broadcasted_iota(jnp.int32, (C, GROUP), 0)
    si = lax.broadcasted_iota(jnp.int32, (C, GROUP), 1) & (C - 1)
    strict = si < ti
    incl = si <= ti
    eye = jnp.where(si == ti, 1.0, 0.0)
    low = lax.broadcasted_iota(jnp.int32, (RWKV_HEAD, LANES), 1) < RWKV_HEAD

    def slots():
        return [None] * n

    def staged(produce, consume):
        _staged(n, produce, consume)

    cum, lhs, rhs, decay_end, key_end = slots(), slots(), slots(), slots(), slots()

    def cum_matmul(i):
        cum[i] = jnp.dot(tri_incl, jnp.concatenate(_split2(lws[i]), axis=0), preferred_element_type=f32)

    def scale_operands(i):
        g_inc = jnp.exp(cum[i])
        g_inv = jnp.exp(-cum[i])
        g_end = jnp.exp(cum[i][C - 1:C, :] - cum[i])
        beta = kkns[i] * aas[i]
        lhs[i] = jnp.concatenate([-kkns[i] * jnp.exp(cum[i] - lws[i]), rs[i]() * g_inc], axis=0).astype(bf16)
        rhs[i] = jnp.concatenate([_block_diag(beta * g_inv), _block_diag(ks[i] * g_inv)], axis=0)
        decay_end[i] = g_inc[C - 1:C, :]
        key_end[i] = jnp.concatenate([beta * g_end, ks[i] * g_end], axis=0).astype(bf16)

    staged(cum_matmul, scale_operands)

    amat, mv, a_ab, a_k, a_rb, t_inv = slots(), slots(), slots(), slots(), slots(), slots()

    def score_matmuls(i):
        amat[i] = _mm_nt(lhs[i], rhs[i])
        mv[i] = _mm_nt(lhs[i], _block_diag(m_cs[i]))

    def same_block(width):
        shift = width.bit_length() - 1
        return (ti >> shift) == (si >> shift)

    a_off = slots()

    def mask_scores(i):
        lower = jnp.where(strict, amat[i][:C, :GROUP], 0.0)
        base = jnp.where(same_block(INV_BASE), lower, 0.0)
        a_ab[i] = base.astype(bf16)
        a_off[i] = lower - base
        a_rb[i] = jnp.where(incl, amat[i][C:, :GROUP], 0.0).astype(bf16)
        a_k[i] = jnp.concatenate([jnp.where(strict, amat[i][:C, GROUP:], 0.0),
                                  jnp.where(incl, amat[i][C:, GROUP:], 0.0)], axis=0).astype(bf16)
        t_inv[i] = eye + base

    staged(score_matmuls, mask_scores)

    av, st, st_lhs, pow_bd = slots(), slots(), slots(), slots()

    def first_square(i):
        av[i] = _mm(a_k[i], _block_diag(vs[i]()))
        st[i] = _mm(a_ab[i], _block_diag(a_ab[i]))

    def after_first_square(i):
        s_b = st[i].astype(bf16)
        st_lhs[i] = jnp.concatenate([s_b, t_inv[i].astype(bf16)], axis=0)
        pow_bd[i] = _block_diag(s_b)

    staged(first_square, after_first_square)

    def square(i):
        st[i] = _mm(st_lhs[i], pow_bd[i])

    def after_square(i):
        s_b = st[i][:C].astype(bf16)
        t_inv[i] = t_inv[i] + st[i][C:]
        st_lhs[i] = jnp.concatenate([s_b, t_inv[i].astype(bf16)], axis=0)
        pow_bd[i] = _block_diag(s_b)

    for _ in range(INV_BASE.bit_length() - 3):
        staged(square, after_square)

    def last_product(i):
        st[i] = _mm(st_lhs[i][C:], pow_bd[i])

    def after_last_product(i):
        t_inv[i] = t_inv[i] + st[i]

    staged(last_product, after_last_product)

    widths = [INV_BASE << j for j in range((C // INV_BASE).bit_length() - 1)]
    brought_in = [same_block(2 * w) & jnp.logical_not(same_block(w)) for w in widths]
    xs = slots()

    def off_products(i):
        stacked = jnp.concatenate([jnp.where(m, a_off[i], 0.0) for m in brought_in], axis=0)
        st[i] = _mm(stacked, _block_diag(t_inv[i]))

    def after_off_products(i):
        xs[i] = [st[i][m * C:(m + 1) * C] for m in range(len(widths))]

    staged(off_products, after_off_products)

    for j in range(len(widths)):
        def merge(i, j=j):
            st[i] = _mm(jnp.concatenate([t_inv[i]] + xs[i][j + 1:], axis=0), _block_diag(xs[i][j]))

        def after_merge(i, j=j):
            t_inv[i] = t_inv[i] + st[i][:C]
            for m in range(j + 1, len(widths)):
                xs[i][m] = xs[i][m] + st[i][(m - j) * C:(m - j + 1) * C]

        staged(merge, after_merge)

    t_b, y_bd = slots(), slots()
    for i in range(n):
        t_b[i] = t_inv[i].astype(bf16)
        y_bd[i] = _block_diag(mv[i][:C] + av[i][:C])

    u, u_bd, uvt = slots(), slots(), slots()

    def solve(i):
        u[i] = _mm(t_b[i], y_bd[i])

    def after_solve(i):
        u_bd[i] = _block_diag(u[i])
        uvt[i] = jnp.concatenate([u[i], vs[i]()], axis=0).T.astype(bf16)

    staged(solve, after_solve)

    o_u, upd, out, m_new = slots(), slots(), slots(), slots()

    def output_and_update(i):
        o_u[i] = _mm(a_rb[i], u_bd[i])
        upd[i] = [_mm(uvt[i][h * RWKV_HEAD:(h + 1) * RWKV_HEAD],
                      key_end[i][:, (h // 2) * LANES:(h // 2 + 1) * LANES]) for h in range(heads)]

    def finish(i):
        out[i] = mv[i][C:] + av[i][C:] + o_u[i]
        m_new[i] = m_cs[i] * decay_end[i] + jnp.concatenate(
            [jnp.where(low, upd[i][2 * j], upd[i][2 * j + 1]) for j in range(heads // 2)], axis=1)

    staged(output_and_update, finish)
    return out, m_new


def _mix_chunk(p_ref, dec_ref, lora_ref, vec_refs, rgn_ref, seg_ref, y_ref, m_ref, s_ref):
    C = CHUNK
    nb = BATCH_BLOCK
    R = nb * C
    W = RWKV_WIDTH
    NG = W // GROUP

    def rwkv_cols(lo, hi):
        return p_ref[:, :, RET_COLS + lo:RET_COLS + hi].reshape(R, hi - lo)

    kr = rwkv_cols(W, 2 * W)
    xwa = rwkv_cols(4 * W, RWKV_COLS)
    lane = lax.broadcasted_iota(jnp.int32, xwa.shape, 1)
    lora = _mm(jnp.where(lane < LORA, jnp.tanh(xwa), xwa), lora_ref[...])
    w0, a0, k_k, k_a, r_k, gn_g, gn_b = (ref[...] for ref in vec_refs)
    seg = seg_ref[...]

    def segsum(x):
        xs = jnp.concatenate([x[:, g * GROUP:(g + 1) * GROUP] for g in range(NG)], axis=0)
        tot = jnp.dot(xs.astype(bf16), seg, preferred_element_type=f32)
        return jnp.concatenate([tot[g * R:(g + 1) * R] for g in range(NG)], axis=1)

    kk = kr * k_k
    kk_sq = segsum(kk * kk)

    def ret_reader(bi, lo, width):
        return lambda: p_ref[bi, :, lo:lo + width]

    rets, s_new = _retention_chunks([ret_reader(bi, 0, RET_QK) for bi in range(nb)],
                                    [ret_reader(bi, RET_QK, RET_QK) for bi in range(nb)],
                                    [[ret_reader(bi, 2 * RET_QK + h * RET_DV, RET_DV) for h in range(RET_HEADS)]
                                     for bi in range(nb)],
                                    [s_ref[bi] for bi in range(nb)],
                                    dec_ref[0], dec_ref[1], dec_ref[2], _RET_CHUNK_DECAY)
    for bi in range(nb):
        s_ref[bi] = s_new[bi]
    ret = jnp.concatenate(rets, axis=0)
    rgn = rgn_ref[...]
    for h in range(RET_HEADS):
        sl = slice(h * RET_DV, (h + 1) * RET_DV)
        xh = ret[:, sl]
        d = xh - jnp.mean(xh, axis=-1, keepdims=True)
        var = jnp.mean(d * d, axis=-1, keepdims=True)
        gh = p_ref[:, :, 2 * RET_QK + RET_WIDTH + h * RET_DV:2 * RET_QK + RET_WIDTH + (h + 1) * RET_DV]
        gh = gh.reshape(R, RET_DV)
        yh = gh * (d * lax.rsqrt(var + RET_GN_EPS) * rgn[:, sl])
        y_ref[:, :, sl] = yh.reshape(nb, C, RET_DV).astype(y_ref.dtype)

    kr = rwkv_cols(W, 2 * W)
    lw = -np.float32(np.exp(-0.5)) * _sigmoid(w0 + lora[:, :W])
    a = _sigmoid(a0 + lora[:, W:])
    kkn = (kr * k_k) * lax.rsqrt(jnp.maximum(kk_sq, 1e-24))
    kmod = kr * (1.0 + (a - 1.0) * k_a)
    chains =[(bi, g) for bi in range(nb) for g in range(NG)]

    def pick(x):
        return [x[bi * C:(bi + 1) * C, g * GROUP:(g + 1) * GROUP] for bi, g in chains]

    def readers(lo):
        def reader(bi, g):
            start = RET_COLS + lo + g * GROUP
            return lambda: p_ref[bi, :, start:start + GROUP]
        return [reader(bi, g) for bi, g in chains]

    outs, m_new = _rwkv_chunks(readers(0), pick(kmod), readers(2 * W), pick(kkn), pick(a), pick(lw),
                               [m_ref[bi, g] for bi, g in chains])
    for i, (bi, g) in enumerate(chains):
        m_ref[bi, g] = m_new[i]
    o = jnp.concatenate([jnp.concatenate(outs[bi * NG:(bi + 1) * NG], axis=1) for bi in range(nb)], axis=0)
    d = o - segsum(o) * (1.0 / RWKV_HEAD)
    var = segsum(d * d) * (1.0 / RWKV_HEAD)
    o = d * lax.rsqrt(var + RWKV_GN_EPS) * gn_g + gn_b
    bonus = segsum(rwkv_cols(0, W) * kmod * r_k) * rwkv_cols(2 * W, 3 * W)
    y_rw = rwkv_cols(3 * W, 4 * W) * (o + bonus)
    y_ref[:, :, RET_WIDTH:] = y_rw.reshape(nb, C, W).astype(y_ref.dtype)


def _project_chunk(x_ref, y_ref, w_ref, g_ref, o_ref):
    nb, C, D = x_ref.shape
    y = y_ref[...].reshape(nb * C, D)
    h = x_ref[...].reshape(nb * C, D) + jnp.dot(y, w_ref[...], preferred_element_type=f32)
    out = h * lax.rsqrt(jnp.mean(h * h, axis=-1, keepdims=True) + RMS_EPS) * g_ref[...]
    o_ref[...] = out.reshape(nb, C, D)


def _mixer_kernel(p_ref, dec_ref, wlora_ref, alora_ref, w0_ref, a0_ref, kk_ref, ka_ref, rk_ref, gng_ref, gnb_ref,
                  rgn_ref, seg_ref, x_ref, wout_ref, fg_ref, o_ref, y_ref, m_ref, s_ref, woutb_ref, lora_ref):
    @pl.when(pl.program_id(1) == 0)
    def _():
        m_ref[...] = jnp.zeros_like(m_ref)
        s_ref[...] = jnp.zeros_like(s_ref)
        woutb_ref[...] = wout_ref[...].astype(bf16)
        zero = jnp.zeros((LORA, RWKV_WIDTH), bf16)
        lora_ref[...] = jnp.concatenate(
            [jnp.concatenate([wlora_ref[...].astype(bf16), zero], axis=1),
             jnp.concatenate([zero, alora_ref[...].astype(bf16)], axis=1)], axis=0)

    vec_refs = (w0_ref, a0_ref, kk_ref, ka_ref, rk_ref, gng_ref, gnb_ref)
    _mix_chunk(p_ref, dec_ref, lora_ref, vec_refs, rgn_ref, seg_ref, y_ref, m_ref, s_ref)
    _project_chunk(x_ref, y_ref, woutb_ref, fg_ref, o_ref)


def _rope_tables(seq):
    half = RET_DK // 2
    expo = -jnp.arange(half, dtype=f32) / f32(half)
    freqs = jnp.exp(expo * f32(np.log(ROPE_BASE)))
    ang = jnp.arange(seq, dtype=jnp.int32).astype(f32)[:, None] * freqs[None, :]
    cos = jnp.cos(ang)
    sin = jnp.sin(ang)
    cos_full = jnp.tile(jnp.concatenate([cos, cos], axis=1), (1, LANES // RET_DK))
    sin_signed = jnp.tile(jnp.concatenate([-sin, sin], axis=1), (1, LANES // RET_DK))
    return cos_full, sin_signed


_RET_LOG_GAMMA = np.log(1.0 - np.exp2(-5.0 - np.arange(RET_HEADS, dtype=np.float64)))
_RET_CHUNK_DECAY = tuple(np.float32(v) for v in np.exp(_RET_LOG_GAMMA * CHUNK))


def _retention_constants():
    C = CHUNK
    lg = _RET_LOG_GAMMA
    lane_lg = np.repeat(lg, RET_DK)[None, :]
    n = np.arange(C, dtype=np.float64)[:, None]
    m = np.tile(np.arange(C, dtype=np.float64), RET_HEADS)[None, :]
    scale = RET_DK ** -0.5
    decay4 = scale * np.exp(lane_lg * np.abs(n - m))
    qdec = np.exp(lane_lg * (n + 1.0)) * np.ones((1, RET_QK))
    kdec = scale * np.exp(lane_lg * (C - 1.0 - n)) * np.ones((1, RET_QK))
    return jnp.asarray(np.stack([decay4, qdec, kdec]).astype(np.float32))


def _const_spec(shape, single_buffer=False):
    mode = pl.Buffered(1) if single_buffer else None
    return pl.BlockSpec(shape, lambda *_: (0,) * len(shape), pipeline_mode=mode)


@jax.jit
def kernel(x, norm_g, w_in, ret_gn_g, rwkv_mu, w_lora_up, w0, a_lora_up, a0, k_k, k_a, r_k,
           rwkv_gn_g, rwkv_gn_b, w_out, final_norm_g):
    B, T, D = x.shape
    assert D == D_MODEL and T % CHUNK == 0 and T % PROJ_TILE == 0 and B % BATCH_BLOCK == 0
    assert norm_g.shape[0] == 1, "single-layer block"
    n_tok = B * T
    xf = x.reshape(n_tok, D)
    params = pltpu.CompilerParams(dimension_semantics=("arbitrary",), vmem_limit_bytes=VMEM_LIMIT)
    tiles_per_seq = T // PROJ_TILE
    cos, sin = _rope_tables(T)

    p = pl.pallas_call(
        functools.partial(_in_proj_kernel, tiles_per_seq),
        grid=(n_tok // PROJ_TILE,),
        in_specs=[pl.BlockSpec((PROJ_TILE, D), lambda i: (i, 0)),
                  _const_spec((1, D)),
                  _const_spec((D, IN_COLS), single_buffer=True),
                  pl.BlockSpec((PROJ_TILE, LANES), lambda i: (i % tiles_per_seq, 0)),
                  pl.BlockSpec((PROJ_TILE, LANES), lambda i: (i % tiles_per_seq, 0)),
                  _const_spec((1, RWKV_COLS))],
        out_specs=pl.BlockSpec((PROJ_TILE, IN_COLS), lambda i: (i, 0)),
        out_shape=jax.ShapeDtypeStruct((n_tok, IN_COLS), f32),
        scratch_shapes=[pltpu.VMEM((D, IN_COLS), bf16),
                        pltpu.VMEM((1, RWKV_COLS), f32)],
        compiler_params=params,
        name="in_proj",
    )(xf, norm_g[0][None, :], w_in[0], cos, sin, rwkv_mu[0][None, :])

    dec = _retention_constants()
    vecs = [v.reshape(1, RWKV_WIDTH) for v in (w0, a0, k_k, k_a, r_k, rwkv_gn_g, rwkv_gn_b)]
    seg_ids = np.arange(GROUP) // RWKV_HEAD
    seg = jnp.asarray(seg_ids[:, None] == seg_ids[None, :], bf16)

    return pl.pallas_call(
        _mixer_kernel,
        grid=(B // BATCH_BLOCK, T // CHUNK),
        in_specs=[pl.BlockSpec((BATCH_BLOCK, CHUNK, IN_COLS), lambda b, c: (b, c, 0)),
                  _const_spec((3, CHUNK, RET_QK)),
                  _const_spec((LORA, RWKV_WIDTH)),
                  _const_spec((LORA, RWKV_WIDTH))]
                 + [_const_spec((1, RWKV_WIDTH))] * len(vecs)
                 + [_const_spec((1, RET_WIDTH)),
                  _const_spec((GROUP, GROUP)),
                  pl.BlockSpec((BATCH_BLOCK, CHUNK, D), lambda b, c: (b, c, 0)),
                  _const_spec((D, D), single_buffer=True),
                  _const_spec((1, D))],
        out_specs=pl.BlockSpec((BATCH_BLOCK, CHUNK, D), lambda b, c: (b, c, 0)),
        out_shape=jax.ShapeDtypeStruct((B, T, D), f32),
        scratch_shapes=[pltpu.VMEM((BATCH_BLOCK, CHUNK, D), bf16),
                        pltpu.VMEM((BATCH_BLOCK, RWKV_WIDTH // GROUP, RWKV_HEAD, GROUP), f32),
                        pltpu.VMEM((BATCH_BLOCK, RET_HEADS, RET_DK, RET_DV), f32),
                        pltpu.VMEM((D, D), bf16),
                        pltpu.VMEM((2 * LORA, 2 * RWKV_WIDTH), bf16)],
        compiler_params=pltpu.CompilerParams(dimension_semantics=("arbitrary", "arbitrary"),
                                             vmem_limit_bytes=VMEM_LIMIT),
        name="mixers",
    )(p.reshape(B, T, IN_COLS), dec, w_lora_up[0], a_lora_up[0], *vecs,
      ret_gn_g[0][None, :], seg, x, w_out[0], final_norm_g[None, :])
```

```python
import functools

import numpy as np
import jax
import jax.numpy as jnp
from jax import lax
from jax.experimental import pallas as pl
from jax.experimental.pallas import tpu as pltpu

D_MODEL = 1024
CHUNK = 64
RET_HEADS = 4
RET_DK = 64
RET_DV = 128
RET_QK = RET_HEADS * RET_DK
RET_WIDTH = RET_HEADS * RET_DV
RWKV_WIDTH = 512
RWKV_HEAD = 64
LORA = 64
RET_COLS = 2 * RET_QK + 2 * RET_WIDTH
RWKV_COLS = 4 * RWKV_WIDTH + 2 * LORA
IN_COLS = RET_COLS + RWKV_COLS
ROPE_BASE = 10000.0
RMS_EPS = 1e-6
RET_GN_EPS = 1e-5
RWKV_GN_EPS = 64e-5
LANES = 128
GROUP = 256
PROJ_TILE = 512
BATCH_BLOCK = 8
RESULT_LAG = 4
INV_BASE = 8
VMEM_LIMIT = 56 * 1024 * 1024

f32 = jnp.float32
bf16 = jnp.bfloat16


def _mm(a, b):
    return jnp.dot(a.astype(bf16), b.astype(bf16), preferred_element_type=f32)


def _mm_nt(a, b):
    return lax.dot_general(a.astype(bf16), b.astype(bf16), (((1,), (1,)), ((), ())),
                           preferred_element_type=f32)


def _split2(x):
    hi = x.astype(bf16)
    lo = (x - hi.astype(f32)).astype(bf16)
    return hi, lo


def _block_diag(x):
    x = x.astype(bf16)
    rows, lanes = x.shape
    assert lanes in (2 * LANES, 4 * LANES)
    zero = jnp.zeros((rows, LANES), bf16)
    cols = []
    if lanes == 4 * LANES:
        for j in range(4):
            cols.append(jnp.concatenate([x[:, j * LANES:(j + 1) * LANES] if i == j else zero for i in range(4)],
                                        axis=0))
    else:
        low = lax.broadcasted_iota(jnp.int32, (rows, LANES), 1) < LANES // 2
        for j in range(2):
            xj = x[:, j * LANES:(j + 1) * LANES]
            pair = [jnp.where(low, xj, zero), jnp.where(low, zero, xj)]
            cols.append(jnp.concatenate([zero] * (2 * j) + pair + [zero] * (2 - 2 * j), axis=0))
    return jnp.concatenate(cols, axis=1)


def _sigmoid(x):
    return 1.0 / (1.0 + jnp.exp(-x))


def _in_proj_kernel(tiles_per_seq, x_ref, g_ref, w_ref, cos_ref, sin_ref, mu_ref, p_ref, wb_ref, carry_ref):
    i = pl.program_id(0)

    @pl.when(i == 0)
    def _():
        wb_ref[...] = w_ref[...].astype(bf16)

    x = x_ref[...]
    u = (x * lax.rsqrt(jnp.mean(x * x, axis=-1, keepdims=True) + RMS_EPS) * g_ref[...]).astype(bf16)
    rows = x.shape[0]
    seq_start = (i % tiles_per_seq) == 0
    W = RWKV_WIDTH

    def rope(pg):
        cos = jnp.concatenate([cos_ref[...]] * (pg.shape[1] // LANES), axis=1)
        sin = jnp.concatenate([sin_ref[...]] * (pg.shape[1] // LANES), axis=1)
        half = (lax.broadcasted_iota(jnp.int32, pg.shape, 1) & (RET_DK - 1)) < RET_DK // 2
        lanes = pg.shape[1]
        swapped = jnp.where(half, pltpu.roll(pg, lanes - RET_DK // 2, 1), pltpu.roll(pg, RET_DK // 2, 1))
        return pg * cos + swapped * sin

    def silu(pg):
        return pg * _sigmoid(pg)

    def shifted(lo, hi):
        def fn(pg):
            first = jnp.where(seq_start, 0.0, carry_ref[:, lo:hi])
            row0 = lax.broadcasted_iota(jnp.int32, pg.shape, 0) == 0
            prev = jnp.where(row0, first, pltpu.roll(pg, 1, 0))
            carry_ref[:, lo:hi] = pg[rows - 1:rows, :]
            return pg + mu_ref[:, lo:hi] * (prev - pg)
        return fn

    groups = [(0, 2 * RET_QK, rope),
              (2 * RET_QK, 2 * RET_QK + RET_WIDTH, lambda pg: pg),
              (2 * RET_QK + RET_WIDTH, RET_COLS, silu),
              (RET_COLS, RET_COLS + W, shifted(0, W)),
              (RET_COLS + W, RET_COLS + 2 * W, shifted(W, 2 * W)),
              (RET_COLS + 2 * W, RET_COLS + 3 * W, shifted(2 * W, 3 * W)),
              (RET_COLS + 3 * W, RET_COLS + 4 * W, lambda pg: silu(shifted(3 * W, 4 * W)(pg))),
              (RET_COLS + 4 * W, IN_COLS, shifted(4 * W, RWKV_COLS))]
    pending = None
    for lo, hi, fn in groups:
        acc = jnp.dot(u, wb_ref[:, lo:hi], preferred_element_type=f32)
        if pending is not None:
            p_ref[:, pending[1]:pending[2]] = pending[3](pending[0])
        pending = (acc, lo, hi, fn)
    p_ref[:, pending[1]:pending[2]] = pending[3](pending[0])


def _staged(n, produce, consume):
    for i in range(n + RESULT_LAG):
        if i < n:
            produce(i)
        if i >= RESULT_LAG:
            consume(i - RESULT_LAG)


def _retention_chunks(qs, ks, vs, states, decay4, qdec, kdec, head_decay):
    n = range(len(qs))
    hs = range(RET_HEADS)
    scores = [_mm_nt(qs[i](), _block_diag(ks[i]())) * decay4 for i in n]
    qds = [qs[i]() * qdec for i in n]
    low = lax.broadcasted_iota(jnp.int32, (CHUNK, LANES), 1) < RET_DK

    def head_lhs(sc, qd, h):
        sl = slice((h // 2) * LANES, (h // 2 + 1) * LANES)
        if h % 2 == 0:
            return jnp.where(low, sc[:, sl], pltpu.roll(qd[:, sl], RET_DK, 1))
        return jnp.where(low, pltpu.roll(sc[:, sl], RET_DK, 1), qd[:, sl])

    out = [jnp.concatenate(
        [_mm(head_lhs(scores[i], qds[i], h),
             jnp.concatenate([vs[i][h]().astype(bf16), states[i](h).astype(bf16)],
                             axis=0)) for h in hs], axis=1) for i in n]
    kts = [(ks[i]() * kdec).T for i in n]
    kv = [[_mm(kts[i][h * RET_DK:(h + 1) * RET_DK], vs[i][h]()) for h in hs]
          for i in n]
    s_new = [jnp.stack([states[i](h) * head_decay[h] + kv[i][h] for h in hs]) for i in n]
    return out, s_new


def _rwkv_chunks(rs, ks, vs, kkns, aas, lws, m_cs):
    C = CHUNK
    n = len(rs)
    heads = GROUP // RWKV_HEAD
    ti = lax.broadcasted_iota(jnp.int32, (C, C), 0)
    si = lax.broadcasted_iota(jnp.int32, (C, C), 1)
    tri_incl = jnp.where(si <= ti, 1.0, 0.0).astype(bf16)
    tri_incl = jnp.concatenate([tri_incl, tri_incl], axis=1)
    ti = lax.broadcasted_iota(jnp.int32, (C, GROUP), 0)
    si = lax.broadcasted_iota(jnp.int32, (C, GROUP), 1) & (C - 1)
    strict = si < ti
    incl = si <= ti
    eye = jnp.where(si == ti, 1.0, 0.0)
    low = lax.broadcasted_iota(jnp.int32, (RWKV_HEAD, LANES), 1) < RWKV_HEAD

    def slots():
        return [None] * n

    def staged(produce, consume):
        _staged(n, produce, consume)

    cum, lhs, rhs, decay_end, key_end = slots(), slots(), slots(), slots(), slots()

    def cum_matmul(i):
        cum[i] = jnp.dot(tri_incl, jnp.concatenate(_split2(lws[i]), axis=0), preferred_element_type=f32)

    def scale_operands(i):
        g_inc = jnp.exp(cum[i])
        g_inv = jnp.exp(-cum[i])
        g_end = jnp.exp(cum[i][C - 1:C, :] - cum[i])
        beta = kkns[i] * aas[i]
        lhs[i] = jnp.concatenate([-kkns[i] * jnp.exp(cum[i] - lws[i]), rs[i]() * g_inc], axis=0).astype(bf16)
        rhs[i] = jnp.concatenate([_block_diag(beta * g_inv), _block_diag(ks[i] * g_inv)], axis=0)
        decay_end[i] = g_inc[C - 1:C, :]
        key_end[i] = jnp.concatenate([beta * g_end, ks[i] * g_end], axis=0).astype(bf16)

    staged(cum_matmul, scale_operands)

    amat, mv, a_ab, a_k, a_rb, t_inv = slots(), slots(), slots(), slots(), slots(), slots()

    def score_matmuls(i):
        amat[i] = _mm_nt(lhs[i], rhs[i])
        mv[i] = _mm_nt(lhs[i], _block_diag(m_cs[i]()))

    def same_block(width):
        shift = width.bit_length() - 1
        return (ti >> shift) == (si >> shift)

    a_off = slots()

    def mask_scores(i):
        lower = jnp.where(strict, amat[i][:C, :GROUP], 0.0)
        base = jnp.where(same_block(INV_BASE), lower, 0.0)
        a_ab[i] = base.astype(bf16)
        a_off[i] = lower - base
        a_rb[i] = jnp.where(incl, amat[i][C:, :GROUP], 0.0).astype(bf16)
        a_k[i] = jnp.concatenate([jnp.where(strict, amat[i][:C, GROUP:], 0.0),
                                  jnp.where(incl, amat[i][C:, GROUP:], 0.0)], axis=0).astype(bf16)
        t_inv[i] = eye + base

    staged(score_matmuls, mask_scores)

    av, st, st_lhs, pow_bd = slots(), slots(), slots(), slots()

    def first_square(i):
        av[i] = _mm(a_k[i], _block_diag(vs[i]()))
        st[i] = _mm(a_ab[i], _block_diag(a_ab[i]))

    def after_first_square(i):
        s_b = st[i].astype(bf16)
        st_lhs[i] = jnp.concatenate([s_b, t_inv[i].astype(bf16)], axis=0)
        pow_bd[i] = _block_diag(s_b)

    staged(first_square, after_first_square)

    def square(i):
        st[i] = _mm(st_lhs[i], pow_bd[i])

    def after_square(i):
        s_b = st[i][:C].astype(bf16)
        t_inv[i] = t_inv[i] + st[i][C:]
        st_lhs[i] = jnp.concatenate([s_b, t_inv[i].astype(bf16)], axis=0)
        pow_bd[i] = _block_diag(s_b)

    for _ in range(INV_BASE.bit_length() - 3):
        staged(square, after_square)

    def last_product(i):
        st[i] = _mm(st_lhs[i][C:], pow_bd[i])

    def after_last_product(i):
        t_inv[i] = t_inv[i] + st[i]

    staged(last_product, after_last_product)

    widths = [INV_BASE << j for j in range((C // INV_BASE).bit_length() - 1)]
    brought_in = [same_block(2 * w) & jnp.logical_not(same_block(w)) for w in widths]
    xs = slots()

    def off_products(i):
        stacked = jnp.concatenate([jnp.where(m, a_off[i], 0.0) for m in brought_in], axis=0)
        st[i] = _mm(stacked, _block_diag(t_inv[i]))

    def after_off_products(i):
        xs[i] = [st[i][m * C:(m + 1) * C] for m in range(len(widths))]

    staged(off_products, after_off_products)

    for j in range(len(widths)):
        def merge(i, j=j):
            st[i] = _mm(jnp.concatenate([t_inv[i]] + xs[i][j + 1:], axis=0), _block_diag(xs[i][j]))

        def after_merge(i, j=j):
            t_inv[i] = t_inv[i] + st[i][:C]
            for m in range(j + 1, len(widths)):
                xs[i][m] = xs[i][m] + st[i][(m - j) * C:(m - j + 1) * C]

        staged(merge, after_merge)

    t_b, y_bd = slots(), slots()
    for i in range(n):
        t_b[i] = t_inv[i].astype(bf16)
        y_bd[i] = _block_diag(mv[i][:C] + av[i][:C])

    u, u_bd, uvt = slots(), slots(), slots()

    def solve(i):
        u[i] = _mm(t_b[i], y_bd[i])

    def after_solve(i):
        u_bd[i] = _block_diag(u[i])
        uvt[i] = jnp.concatenate([u[i], vs[i]()], axis=0).T.astype(bf16)

    staged(solve, after_solve)

    o_u, upd, out, m_new = slots(), slots(), slots(), slots()

    def output_and_update(i):
        o_u[i] = _mm(a_rb[i], u_bd[i])
        upd[i] = [_mm(uvt[i][h * RWKV_HEAD:(h + 1) * RWKV_HEAD],
                      key_end[i][:, (h // 2) * LANES:(h // 2 + 1) * LANES]) for h in range(heads)]

    def finish(i):
        out[i] = mv[i][C:] + av[i][C:] + o_u[i]
        m_new[i] = m_cs[i]() * decay_end[i] + jnp.concatenate(
            [jnp.where(low, upd[i][2 * j], upd[i][2 * j + 1]) for j in range(heads // 2)], axis=1)

    staged(output_and_update, finish)
    return out, m_new


def _mix_chunk(p_ref, dec_ref, lora_ref, vec_refs, rgn_ref, seg_ref, y_ref, m_ref, s_ref):
    C = CHUNK
    nb = BATCH_BLOCK
    R = nb * C
    W = RWKV_WIDTH
    NG = W // GROUP

    def rwkv_cols(lo, hi):
        return p_ref[:, :, RET_COLS + lo:RET_COLS + hi].reshape(R, hi - lo)

    kr = rwkv_cols(W, 2 * W)
    xwa = rwkv_cols(4 * W, RWKV_COLS)
    lane = lax.broadcasted_iota(jnp.int32, xwa.shape, 1)
    lora = _mm(jnp.where(lane < LORA, jnp.tanh(xwa), xwa), lora_ref[...])
    w0, a0, k_k, k_a, r_k, gn_g, gn_b = (ref[...] for ref in vec_refs)
    seg = seg_ref[...]

    def segsum(x):
        xs = jnp.concatenate([x[:, g * GROUP:(g + 1) * GROUP] for g in range(NG)], axis=0)
        tot = jnp.dot(xs.astype(bf16), seg, preferred_element_type=f32)
        return jnp.concatenate([tot[g * R:(g + 1) * R] for g in range(NG)], axis=1)

    kk = kr * k_k
    kk_sq = segsum(kk * kk)

    def ret_reader(bi, lo, width):
        return lambda: p_ref[bi, :, lo:lo + width]

    rets, s_new = _retention_chunks([ret_reader(bi, 0, RET_QK) for bi in range(nb)],
                                    [ret_reader(bi, RET_QK, RET_QK) for bi in range(nb)],
                                    [[ret_reader(bi, 2 * RET_QK + h * RET_DV, RET_DV) for h in range(RET_HEADS)]
                                     for bi in range(nb)],
                                    [functools.partial(lambda bi, h: s_ref[bi, h], bi) for bi in range(nb)],
                                    dec_ref[0], dec_ref[1], dec_ref[2], _RET_CHUNK_DECAY)
    for bi in range(nb):
        s_ref[bi] = s_new[bi]
    ret = jnp.concatenate(rets, axis=0)
    rgn = rgn_ref[...]
    for h in range(RET_HEADS):
        sl = slice(h * RET_DV, (h + 1) * RET_DV)
        xh = ret[:, sl]
        d = xh - jnp.mean(xh, axis=-1, keepdims=True)
        var = jnp.mean(d * d, axis=-1, keepdims=True)
        gh = p_ref[:, :, 2 * RET_QK + RET_WIDTH + h * RET_DV:2 * RET_QK + RET_WIDTH + (h + 1) * RET_DV]
        gh = gh.reshape(R, RET_DV)
        yh = gh * (d * lax.rsqrt(var + RET_GN_EPS) * rgn[:, sl])
        y_ref[:, :, sl] = yh.reshape(nb, C, RET_DV).astype(y_ref.dtype)

    kr = rwkv_cols(W, 2 * W)
    lw = -np.float32(np.exp(-0.5)) * _sigmoid(w0 + lora[:, :W])
    a = _sigmoid(a0 + lora[:, W:])
    kkn = (kr * k_k) * lax.rsqrt(jnp.maximum(kk_sq, 1e-24))
    kmod = kr * (1.0 + (a - 1.0) * k_a)
    chains =[(bi, g) for bi in range(nb) for g in range(NG)]

    def pick(x):
        return [x[bi * C:(bi + 1) * C, g * GROUP:(g + 1) * GROUP] for bi, g in chains]

    def readers(lo):
        def reader(bi, g):
            start = RET_COLS + lo + g * GROUP
            return lambda: p_ref[bi, :, start:start + GROUP]
        return [reader(bi, g) for bi, g in chains]

    outs, m_new = _rwkv_chunks(readers(0), pick(kmod), readers(2 * W), pick(kkn), pick(a), pick(lw),
                               [functools.partial(lambda bi, g: m_ref[bi, g], bi, g) for bi, g in chains])
    for i, (bi, g) in enumerate(chains):
        m_ref[bi, g] = m_new[i]
    o = jnp.concatenate([jnp.concatenate(outs[bi * NG:(bi + 1) * NG], axis=1) for bi in range(nb)], axis=0)
    d = o - segsum(o) * (1.0 / RWKV_HEAD)
    var = segsum(d * d) * (1.0 / RWKV_HEAD)
    o = d * lax.rsqrt(var + RWKV_GN_EPS) * gn_g + gn_b
    bonus = segsum(rwkv_cols(0, W) * kmod * r_k) * rwkv_cols(2 * W, 3 * W)
    y_rw = rwkv_cols(3 * W, 4 * W) * (o + bonus)
    y_ref[:, :, RET_WIDTH:] = y_rw.reshape(nb, C, W).astype(y_ref.dtype)


def _project_chunk(x_ref, y_ref, w_ref, g_ref, o_ref):
    nb, C, D = x_ref.shape
    y = y_ref[...].reshape(nb * C, D)
    h = x_ref[...].reshape(nb * C, D) + jnp.dot(y, w_ref[...], preferred_element_type=f32)
    out = h * lax.rsqrt(jnp.mean(h * h, axis=-1, keepdims=True) + RMS_EPS) * g_ref[...]
    o_ref[...] = out.reshape(nb, C, D)


def _mixer_kernel(p_ref, dec_ref, wlora_ref, alora_ref, w0_ref, a0_ref, kk_ref, ka_ref, rk_ref, gng_ref, gnb_ref,
                  rgn_ref, seg_ref, x_ref, wout_ref, fg_ref, o_ref, y_ref, m_ref, s_ref, woutb_ref, lora_ref):
    @pl.when(pl.program_id(1) == 0)
    def _():
        m_ref[...] = jnp.zeros_like(m_ref)
        s_ref[...] = jnp.zeros_like(s_ref)
        woutb_ref[...] = wout_ref[...].astype(bf16)
        zero = jnp.zeros((LORA, RWKV_WIDTH), bf16)
        lora_ref[...] = jnp.concatenate(
            [jnp.concatenate([wlora_ref[...].astype(bf16), zero], axis=1),
             jnp.concatenate([zero, alora_ref[...].astype(bf16)], axis=1)], axis=0)

    vec_refs = (w0_ref, a0_ref, kk_ref, ka_ref, rk_ref, gng_ref, gnb_ref)
    _mix_chunk(p_ref, dec_ref, lora_ref, vec_refs, rgn_ref, seg_ref, y_ref, m_ref, s_ref)
    _project_chunk(x_ref, y_ref, woutb_ref, fg_ref, o_ref)


def _rope_tables(seq):
    half = RET_DK // 2
    expo = -jnp.arange(half, dtype=f32) / f32(half)
    freqs = jnp.exp(expo * f32(np.log(ROPE_BASE)))
    ang = jnp.arange(seq, dtype=jnp.int32).astype(f32)[:, None] * freqs[None, :]
    cos = jnp.cos(ang)
    sin = jnp.sin(ang)
    cos_full = jnp.tile(jnp.concatenate([cos, cos], axis=1), (1, LANES // RET_DK))
    sin_signed = jnp.tile(jnp.concatenate([-sin, sin], axis=1), (1, LANES // RET_DK))
    return cos_full, sin_signed


_RET_LOG_GAMMA = np.log(1.0 - np.exp2(-5.0 - np.arange(RET_HEADS, dtype=np.float64)))
_RET_CHUNK_DECAY = tuple(np.float32(v) for v in np.exp(_RET_LOG_GAMMA * CHUNK))


def _retention_constants():
    C = CHUNK
    lg = _RET_LOG_GAMMA
    lane_lg = np.repeat(lg, RET_DK)[None, :]
    n = np.arange(C, dtype=np.float64)[:, None]
    m = np.tile(np.arange(C, dtype=np.float64), RET_HEADS)[None, :]
    scale = RET_DK ** -0.5
    decay4 = scale * np.exp(lane_lg * np.abs(n - m))
    qdec = np.exp(lane_lg * (n + 1.0)) * np.ones((1, RET_QK))
    kdec = scale * np.exp(lane_lg * (C - 1.0 - n)) * np.ones((1, RET_QK))
    return jnp.asarray(np.stack([decay4, qdec, kdec]).astype(np.float32))


def _const_spec(shape, single_buffer=False):
    mode = pl.Buffered(1) if single_buffer else None
    return pl.BlockSpec(shape, lambda *_: (0,) * len(shape), pipeline_mode=mode)


@jax.jit
def kernel(x, norm_g, w_in, ret_gn_g, rwkv_mu, w_lora_up, w0, a_lora_up, a0, k_k, k_a, r_k,
           rwkv_gn_g, rwkv_gn_b, w_out, final_norm_g):
    B, T, D = x.shape
    assert D == D_MODEL and T % CHUNK == 0 and T % PROJ_TILE == 0 and B % BATCH_BLOCK == 0
    assert norm_g.shape[0] == 1, "single-layer block"
    n_tok = B * T
    xf = x.reshape(n_tok, D)
    params = pltpu.CompilerParams(dimension_semantics=("arbitrary",), vmem_limit_bytes=VMEM_LIMIT)
    tiles_per_seq = T // PROJ_TILE
    cos, sin = _rope_tables(T)

    p = pl.pallas_call(
        functools.partial(_in_proj_kernel, tiles_per_seq),
        grid=(n_tok // PROJ_TILE,),
        in_specs=[pl.BlockSpec((PROJ_TILE, D), lambda i: (i, 0)),
                  _const_spec((1, D)),
                  _const_spec((D, IN_COLS), single_buffer=True),
                  pl.BlockSpec((PROJ_TILE, LANES), lambda i: (i % tiles_per_seq, 0)),
                  pl.BlockSpec((PROJ_TILE, LANES), lambda i: (i % tiles_per_seq, 0)),
                  _const_spec((1, RWKV_COLS))],
        out_specs=pl.BlockSpec((PROJ_TILE, IN_COLS), lambda i: (i, 0)),
        out_shape=jax.ShapeDtypeStruct((n_tok, IN_COLS), f32),
        scratch_shapes=[pltpu.VMEM((D, IN_COLS), bf16),
                        pltpu.VMEM((1, RWKV_COLS), f32)],
        compiler_params=params,
        name="in_proj",
    )(xf, norm_g[0][None, :], w_in[0], cos, sin, rwkv_mu[0][None, :])

    dec = _retention_constants()
    vecs = [v.reshape(1, RWKV_WIDTH) for v in (w0, a0, k_k, k_a, r_k, rwkv_gn_g, rwkv_gn_b)]
    seg_ids = np.arange(GROUP) // RWKV_HEAD
    seg = jnp.asarray(seg_ids[:, None] == seg_ids[None, :], bf16)

    return pl.pallas_call(
        _mixer_kernel,
        grid=(B // BATCH_BLOCK, T // CHUNK),
        in_specs=[pl.BlockSpec((BATCH_BLOCK, CHUNK, IN_COLS), lambda b, c: (b, c, 0)),
                  _const_spec((3, CHUNK, RET_QK)),
                  _const_spec((LORA, RWKV_WIDTH)),
                  _const_spec((LORA, RWKV_WIDTH))]
                 + [_const_spec((1, RWKV_WIDTH))] * len(vecs)
                 + [_const_spec((1, RET_WIDTH)),
                  _const_spec((GROUP, GROUP)),
                  pl.BlockSpec((BATCH_BLOCK, CHUNK, D), lambda b, c: (b, c, 0)),
                  _const_spec((D, D), single_buffer=True),
                  _const_spec((1, D))],
        out_specs=pl.BlockSpec((BATCH_BLOCK, CHUNK, D), lambda b, c: (b, c, 0)),
        out_shape=jax.ShapeDtypeStruct((B, T, D), f32),
        scratch_shapes=[pltpu.VMEM((BATCH_BLOCK, CHUNK, D), bf16),
                        pltpu.VMEM((BATCH_BLOCK, RWKV_WIDTH // GROUP, RWKV_HEAD, GROUP), f32),
                        pltpu.VMEM((BATCH_BLOCK, RET_HEADS, RET_DK, RET_DV), f32),
                        pltpu.VMEM((D, D), bf16),
                        pltpu.VMEM((2 * LORA, 2 * RWKV_WIDTH), bf16)],
        compiler_params=pltpu.CompilerParams(dimension_semantics=("arbitrary", "arbitrary"),
                                             vmem_limit_bytes=VMEM_LIMIT),
        name="mixers",
    )(p.reshape(B, T, IN_COLS), dec, w_lora_up[0], a_lora_up[0], *vecs,
      ret_gn_g[0][None, :], seg, x, w_out[0], final_norm_g[None, :])
```

```python
import functools

import numpy as np
import jax
import jax.numpy as jnp
from jax import lax
from jax.experimental import pallas as pl
from jax.experimental.pallas import tpu as pltpu

D_MODEL = 1024
CHUNK = 64
RET_HEADS = 4
RET_DK = 64
RET_DV = 128
RET_QK = RET_HEADS * RET_DK
RET_WIDTH = RET_HEADS * RET_DV
RWKV_WIDTH = 512
RWKV_HEAD = 64
LORA = 64
RET_COLS = 2 * RET_QK + 2 * RET_WIDTH
RWKV_COLS = 4 * RWKV_WIDTH + 2 * LORA
IN_COLS = RET_COLS + RWKV_COLS
ROPE_BASE = 10000.0
RMS_EPS = 1e-6
RET_GN_EPS = 1e-5
RWKV_GN_EPS = 64e-5
LANES = 128
GROUP = 256
PROJ_TILE = 512
BATCH_BLOCK = 8
RESULT_LAG = 4
INV_BASE = 8
RWKV_CHAIN_GROUPS = 2
VMEM_LIMIT = 56 * 1024 * 1024

f32 = jnp.float32
bf16 = jnp.bfloat16


def _mm(a, b):
    return jnp.dot(a.astype(bf16), b.astype(bf16), preferred_element_type=f32)


def _mm_nt(a, b):
    return lax.dot_general(a.astype(bf16), b.astype(bf16), (((1,), (1,)), ((), ())),
                           preferred_element_type=f32)


def _split2(x):
    hi = x.astype(bf16)
    lo = (x - hi.astype(f32)).astype(bf16)
    return hi, lo


def _block_diag(x):
    x = x.astype(bf16)
    rows, lanes = x.shape
    assert lanes in (2 * LANES, 4 * LANES)
    zero = jnp.zeros((rows, LANES), bf16)
    cols = []
    if lanes == 4 * LANES:
        for j in range(4):
            cols.append(jnp.concatenate([x[:, j * LANES:(j + 1) * LANES] if i == j else zero for i in range(4)],
                                        axis=0))
    else:
        low = lax.broadcasted_iota(jnp.int32, (rows, LANES), 1) < LANES // 2
        for j in range(2):
            xj = x[:, j * LANES:(j + 1) * LANES]
            pair = [jnp.where(low, xj, zero), jnp.where(low, zero, xj)]
            cols.append(jnp.concatenate([zero] * (2 * j) + pair + [zero] * (2 - 2 * j), axis=0))
    return jnp.concatenate(cols, axis=1)


def _sigmoid(x):
    return 1.0 / (1.0 + jnp.exp(-x))


def _in_proj_kernel(tiles_per_seq, x_ref, g_ref, w_ref, cos_ref, sin_ref, mu_ref, p_ref, wb_ref, carry_ref):
    i = pl.program_id(0)

    @pl.when(i == 0)
    def _():
        wb_ref[...] = w_ref[...].astype(bf16)

    x = x_ref[...]
    u = (x * lax.rsqrt(jnp.mean(x * x, axis=-1, keepdims=True) + RMS_EPS) * g_ref[...]).astype(bf16)
    rows = x.shape[0]
    seq_start = (i % tiles_per_seq) == 0
    W = RWKV_WIDTH

    def rope(pg):
        cos = jnp.concatenate([cos_ref[...]] * (pg.shape[1] // LANES), axis=1)
        sin = jnp.concatenate([sin_ref[...]] * (pg.shape[1] // LANES), axis=1)
        half = (lax.broadcasted_iota(jnp.int32, pg.shape, 1) & (RET_DK - 1)) < RET_DK // 2
        lanes = pg.shape[1]
        swapped = jnp.where(half, pltpu.roll(pg, lanes - RET_DK // 2, 1), pltpu.roll(pg, RET_DK // 2, 1))
        return pg * cos + swapped * sin

    def silu(pg):
        return pg * _sigmoid(pg)

    def shifted(lo, hi):
        def fn(pg):
            first = jnp.where(seq_start, 0.0, carry_ref[:, lo:hi])
            row0 = lax.broadcasted_iota(jnp.int32, pg.shape, 0) == 0
            prev = jnp.where(row0, first, pltpu.roll(pg, 1, 0))
            carry_ref[:, lo:hi] = pg[rows - 1:rows, :]
            return pg + mu_ref[:, lo:hi] * (prev - pg)
        return fn

    groups = [(0, 2 * RET_QK, rope),
              (2 * RET_QK, 2 * RET_QK + RET_WIDTH, lambda pg: pg),
              (2 * RET_QK + RET_WIDTH, RET_COLS, silu),
              (RET_COLS, RET_COLS + W, shifted(0, W)),
              (RET_COLS + W, RET_COLS + 2 * W, shifted(W, 2 * W)),
              (RET_COLS + 2 * W, RET_COLS + 3 * W, shifted(2 * W, 3 * W)),
              (RET_COLS + 3 * W, RET_COLS + 4 * W, lambda pg: silu(shifted(3 * W, 4 * W)(pg))),
              (RET_COLS + 4 * W, IN_COLS, shifted(4 * W, RWKV_COLS))]
    pending = None
    for lo, hi, fn in groups:
        acc = jnp.dot(u, wb_ref[:, lo:hi], preferred_element_type=f32)
        if pending is not None:
            p_ref[:, pending[1]:pending[2]] = pending[3](pending[0])
        pending = (acc, lo, hi, fn)
    p_ref[:, pending[1]:pending[2]] = pending[3](pending[0])


def _staged(n, produce, consume):
    for i in range(n + RESULT_LAG):
        if i < n:
            produce(i)
        if i >= RESULT_LAG:
            consume(i - RESULT_LAG)


def _retention_chunks(qs, ks, vs, states, decay4, qdec, kdec, head_decay):
    n = range(len(qs))
    hs = range(RET_HEADS)
    scores = [_mm_nt(qs[i](), _block_diag(ks[i]())) * decay4 for i in n]
    qds = [qs[i]() * qdec for i in n]
    low = lax.broadcasted_iota(jnp.int32, (CHUNK, LANES), 1) < RET_DK

    def head_lhs(sc, qd, h):
        sl = slice((h // 2) * LANES, (h // 2 + 1) * LANES)
        if h % 2 == 0:
            return jnp.where(low, sc[:, sl], pltpu.roll(qd[:, sl], RET_DK, 1))
        return jnp.where(low, pltpu.roll(sc[:, sl], RET_DK, 1), qd[:, sl])

    out = [jnp.concatenate(
        [_mm(head_lhs(scores[i], qds[i], h),
             jnp.concatenate([vs[i][h]().astype(bf16), states[i](h).astype(bf16)],
                             axis=0)) for h in hs], axis=1) for i in n]
    kts = [(ks[i]() * kdec).T for i in n]
    kv = [[_mm(kts[i][h * RET_DK:(h + 1) * RET_DK], vs[i][h]()) for h in hs]
          for i in n]
    s_new = [jnp.stack([states[i](h) * head_decay[h] + kv[i][h] for h in hs]) for i in n]
    return out, s_new


def _rwkv_chunks(rs, ks, vs, kkns, aas, lws, m_cs):
    C = CHUNK
    n = len(rs)
    heads = GROUP // RWKV_HEAD
    ti = lax.broadcasted_iota(jnp.int32, (C, C), 0)
    si = lax.broadcasted_iota(jnp.int32, (C, C), 1)
    tri_incl = jnp.where(si <= ti, 1.0, 0.0).astype(bf16)
    tri_incl = jnp.concatenate([tri_incl, tri_incl], axis=1)
    ti = lax.broadcasted_iota(jnp.int32, (C, GROUP), 0)
    si = lax.broadcasted_iota(jnp.int32, (C, GROUP), 1) & (C - 1)
    strict = si < ti
    incl = si <= ti
    eye = jnp.where(si == ti, 1.0, 0.0)
    low = lax.broadcasted_iota(jnp.int32, (RWKV_HEAD, LANES), 1) < RWKV_HEAD

    def slots():
        return [None] * n

    def staged(produce, consume):
        _staged(n, produce, consume)

    cum, lhs, rhs, decay_end, key_end = slots(), slots(), slots(), slots(), slots()

    def cum_matmul(i):
        cum[i] = jnp.dot(tri_incl, jnp.concatenate(_split2(lws[i]), axis=0), preferred_element_type=f32)

    def scale_operands(i):
        g_inc = jnp.exp(cum[i])
        g_inv = jnp.exp(-cum[i])
        g_end = jnp.exp(cum[i][C - 1:C, :] - cum[i])
        beta = kkns[i] * aas[i]
        lhs[i] = jnp.concatenate([-kkns[i] * jnp.exp(cum[i] - lws[i]), rs[i]() * g_inc], axis=0).astype(bf16)
        rhs[i] = jnp.concatenate([_block_diag(beta * g_inv), _block_diag(ks[i] * g_inv)], axis=0)
        decay_end[i] = g_inc[C - 1:C, :]
        key_end[i] = jnp.concatenate([beta * g_end, ks[i] * g_end], axis=0).astype(bf16)

    staged(cum_matmul, scale_operands)

    amat, mv, a_ab, a_k, a_rb, t_inv = slots(), slots(), slots(), slots(), slots(), slots()

    def score_matmuls(i):
        amat[i] = _mm_nt(lhs[i], rhs[i])
        mv[i] = _mm_nt(lhs[i], _block_diag(m_cs[i]()))

    def same_block(width):
        shift = width.bit_length() - 1
        return (ti >> shift) == (si >> shift)

    a_off = slots()

    def mask_scores(i):
        lower = jnp.where(strict, amat[i][:C, :GROUP], 0.0)
        base = jnp.where(same_block(INV_BASE), lower, 0.0)
        a_ab[i] = base.astype(bf16)
        a_off[i] = lower - base
        a_rb[i] = jnp.where(incl, amat[i][C:, :GROUP], 0.0).astype(bf16)
        a_k[i] = jnp.concatenate([jnp.where(strict, amat[i][:C, GROUP:], 0.0),
                                  jnp.where(incl, amat[i][C:, GROUP:], 0.0)], axis=0).astype(bf16)
        t_inv[i] = eye + base

    staged(score_matmuls, mask_scores)

    av, st, st_lhs, pow_bd = slots(), slots(), slots(), slots()

    def first_square(i):
        av[i] = _mm(a_k[i], _block_diag(vs[i]()))
        st[i] = _mm(a_ab[i], _block_diag(a_ab[i]))

    def after_first_square(i):
        s_b = st[i].astype(bf16)
        st_lhs[i] = jnp.concatenate([s_b, t_inv[i].astype(bf16)], axis=0)
        pow_bd[i] = _block_diag(s_b)

    staged(first_square, after_first_square)

    def square(i):
        st[i] = _mm(st_lhs[i], pow_bd[i])

    def after_square(i):
        s_b = st[i][:C].astype(bf16)
        t_inv[i] = t_inv[i] + st[i][C:]
        st_lhs[i] = jnp.concatenate([s_b, t_inv[i].astype(bf16)], axis=0)
        pow_bd[i] = _block_diag(s_b)

    for _ in range(INV_BASE.bit_length() - 3):
        staged(square, after_square)

    def last_product(i):
        st[i] = _mm(st_lhs[i][C:], pow_bd[i])

    def after_last_product(i):
        t_inv[i] = t_inv[i] + st[i]

    staged(last_product, after_last_product)

    widths = [INV_BASE << j for j in range((C // INV_BASE).bit_length() - 1)]
    brought_in = [same_block(2 * w) & jnp.logical_not(same_block(w)) for w in widths]
    xs = slots()

    def off_products(i):
        stacked = jnp.concatenate([jnp.where(m, a_off[i], 0.0) for m in brought_in], axis=0)
        st[i] = _mm(stacked, _block_diag(t_inv[i]))

    def after_off_products(i):
        xs[i] = [st[i][m * C:(m + 1) * C] for m in range(len(widths))]

    staged(off_products, after_off_products)

    for j in range(len(widths)):
        def merge(i, j=j):
            st[i] = _mm(jnp.concatenate([t_inv[i]] + xs[i][j + 1:], axis=0), _block_diag(xs[i][j]))

        def after_merge(i, j=j):
            t_inv[i] = t_inv[i] + st[i][:C]
            for m in range(j + 1, len(widths)):
                xs[i][m] = xs[i][m] + st[i][(m - j) * C:(m - j + 1) * C]

        staged(merge, after_merge)

    t_b, y_bd = slots(), slots()
    for i in range(n):
        t_b[i] = t_inv[i].astype(bf16)
        y_bd[i] = _block_diag(mv[i][:C] + av[i][:C])

    u, u_bd, uvt = slots(), slots(), slots()

    def solve(i):
        u[i] = _mm(t_b[i], y_bd[i])

    def after_solve(i):
        u_bd[i] = _block_diag(u[i])
        uvt[i] = jnp.concatenate([u[i], vs[i]()], axis=0).T.astype(bf16)

    staged(solve, after_solve)

    o_u, upd, out, m_new = slots(), slots(), slots(), slots()

    def output_and_update(i):
        o_u[i] = _mm(a_rb[i], u_bd[i])
        upd[i] = [_mm(uvt[i][h * RWKV_HEAD:(h + 1) * RWKV_HEAD],
                      key_end[i][:, (h // 2) * LANES:(h // 2 + 1) * LANES]) for h in range(heads)]

    def finish(i):
        out[i] = mv[i][C:] + av[i][C:] + o_u[i]
        m_new[i] = m_cs[i]() * decay_end[i] + jnp.concatenate(
            [jnp.where(low, upd[i][2 * j], upd[i][2 * j + 1]) for j in range(heads // 2)], axis=1)

    staged(output_and_update, finish)
    return out, m_new


def _mix_chunk(p_ref, dec_ref, lora_ref, vec_refs, rgn_ref, seg_ref, y_ref, m_ref, s_ref):
    C = CHUNK
    nb = BATCH_BLOCK
    R = nb * C
    W = RWKV_WIDTH
    NG = W // GROUP

    def rwkv_cols(lo, hi):
        return p_ref[:, :, RET_COLS + lo:RET_COLS + hi].reshape(R, hi - lo)

    kr = rwkv_cols(W, 2 * W)
    xwa = rwkv_cols(4 * W, RWKV_COLS)
    lane = lax.broadcasted_iota(jnp.int32, xwa.shape, 1)
    lora = _mm(jnp.where(lane < LORA, jnp.tanh(xwa), xwa), lora_ref[...])
    w0, a0, k_k, k_a, r_k, gn_g, gn_b = (ref[...] for ref in vec_refs)
    seg = seg_ref[...]

    def segsum(x):
        xs = jnp.concatenate([x[:, g * GROUP:(g + 1) * GROUP] for g in range(NG)], axis=0)
        tot = jnp.dot(xs.astype(bf16), seg, preferred_element_type=f32)
        return jnp.concatenate([tot[g * R:(g + 1) * R] for g in range(NG)], axis=1)

    kk = kr * k_k
    kk_sq = segsum(kk * kk)

    def ret_reader(bi, lo, width):
        return lambda: p_ref[bi, :, lo:lo + width]

    rets, s_new = _retention_chunks([ret_reader(bi, 0, RET_QK) for bi in range(nb)],
                                    [ret_reader(bi, RET_QK, RET_QK) for bi in range(nb)],
                                    [[ret_reader(bi, 2 * RET_QK + h * RET_DV, RET_DV) for h in range(RET_HEADS)]
                                     for bi in range(nb)],
                                    [functools.partial(lambda bi, h: s_ref[bi, h], bi) for bi in range(nb)],
                                    dec_ref[0], dec_ref[1], dec_ref[2], _RET_CHUNK_DECAY)
    for bi in range(nb):
        s_ref[bi] = s_new[bi]
    ret = jnp.concatenate(rets, axis=0)
    rgn = rgn_ref[...]
    for h in range(RET_HEADS):
        sl = slice(h * RET_DV, (h + 1) * RET_DV)
        xh = ret[:, sl]
        d = xh - jnp.mean(xh, axis=-1, keepdims=True)
        var = jnp.mean(d * d, axis=-1, keepdims=True)
        gh = p_ref[:, :, 2 * RET_QK + RET_WIDTH + h * RET_DV:2 * RET_QK + RET_WIDTH + (h + 1) * RET_DV]
        gh = gh.reshape(R, RET_DV)
        yh = gh * (d * lax.rsqrt(var + RET_GN_EPS) * rgn[:, sl])
        y_ref[:, :, sl] = yh.reshape(nb, C, RET_DV).astype(y_ref.dtype)

    kr = rwkv_cols(W, 2 * W)
    lw = -np.float32(np.exp(-0.5)) * _sigmoid(w0 + lora[:, :W])
    a = _sigmoid(a0 + lora[:, W:])
    kkn = (kr * k_k) * lax.rsqrt(jnp.maximum(kk_sq, 1e-24))
    kmod = kr * (1.0 + (a - 1.0) * k_a)
    chains = [(bi, g) for bi in range(nb) for g in range(NG)]

    def pick(x):
        return [x[bi * C:(bi + 1) * C, g * GROUP:(g + 1) * GROUP] for bi, g in chains]

    def readers(lo):
        def reader(bi, g):
            start = RET_COLS + lo + g * GROUP
            return lambda: p_ref[bi, :, start:start + GROUP]
        return [reader(bi, g) for bi, g in chains]

    operands = (readers(0), pick(kmod), readers(2 * W), pick(kkn), pick(a), pick(lw),
                [functools.partial(lambda bi, g: m_ref[bi, g], bi, g) for bi, g in chains])
    outs, m_new = [], []
    per_group = len(chains) // RWKV_CHAIN_GROUPS
    for lo in range(0, len(chains), per_group):
        group_outs, group_states = _rwkv_chunks(*(x[lo:lo + per_group] for x in operands))
        outs += group_outs
        m_new += group_states
    for i, (bi, g) in enumerate(chains):
        m_ref[bi, g] = m_new[i]
    o = jnp.concatenate([jnp.concatenate(outs[bi * NG:(bi + 1) * NG], axis=1) for bi in range(nb)], axis=0)
    d = o - segsum(o) * (1.0 / RWKV_HEAD)
    var = segsum(d * d) * (1.0 / RWKV_HEAD)
    o = d * lax.rsqrt(var + RWKV_GN_EPS) * gn_g + gn_b
    bonus = segsum(rwkv_cols(0, W) * kmod * r_k) * rwkv_cols(2 * W, 3 * W)
    y_rw = rwkv_cols(3 * W, 4 * W) * (o + bonus)
    y_ref[:, :, RET_WIDTH:] = y_rw.reshape(nb, C, W).astype(y_ref.dtype)


def _project_chunk(x_ref, y_ref, w_ref, g_ref, o_ref):
    nb, C, D = x_ref.shape
    y = y_ref[...].reshape(nb * C, D)
    h = x_ref[...].reshape(nb * C, D) + jnp.dot(y, w_ref[...], preferred_element_type=f32)
    out = h * lax.rsqrt(jnp.mean(h * h, axis=-1, keepdims=True) + RMS_EPS) * g_ref[...]
    o_ref[...] = out.reshape(nb, C, D)


def _mixer_kernel(p_ref, dec_ref, wlora_ref, alora_ref, w0_ref, a0_ref, kk_ref, ka_ref, rk_ref, gng_ref, gnb_ref,
                  rgn_ref, seg_ref, x_ref, wout_ref, fg_ref, o_ref, y_ref, m_ref, s_ref, woutb_ref, lora_ref):
    @pl.when(pl.program_id(1) == 0)
    def _():
        m_ref[...] = jnp.zeros_like(m_ref)
        s_ref[...] = jnp.zeros_like(s_ref)
        woutb_ref[...] = wout_ref[...].astype(bf16)
        zero = jnp.zeros((LORA, RWKV_WIDTH), bf16)
        lora_ref[...] = jnp.concatenate(
            [jnp.concatenate([wlora_ref[...].astype(bf16), zero], axis=1),
             jnp.concatenate([zero, alora_ref[...].astype(bf16)], axis=1)], axis=0)

    vec_refs = (w0_ref, a0_ref, kk_ref, ka_ref, rk_ref, gng_ref, gnb_ref)
    _mix_chunk(p_ref, dec_ref, lora_ref, vec_refs, rgn_ref, seg_ref, y_ref, m_ref, s_ref)
    _project_chunk(x_ref, y_ref, woutb_ref, fg_ref, o_ref)


def _rope_tables(seq):
    half = RET_DK // 2
    expo = -jnp.arange(half, dtype=f32) / f32(half)
    freqs = jnp.exp(expo * f32(np.log(ROPE_BASE)))
    ang = jnp.arange(seq, dtype=jnp.int32).astype(f32)[:, None] * freqs[None, :]
    cos = jnp.cos(ang)
    sin = jnp.sin(ang)
    cos_full = jnp.tile(jnp.concatenate([cos, cos], axis=1), (1, LANES // RET_DK))
    sin_signed = jnp.tile(jnp.concatenate([-sin, sin], axis=1), (1, LANES // RET_DK))
    return cos_full, sin_signed


_RET_LOG_GAMMA = np.log(1.0 - np.exp2(-5.0 - np.arange(RET_HEADS, dtype=np.float64)))
_RET_CHUNK_DECAY = tuple(np.float32(v) for v in np.exp(_RET_LOG_GAMMA * CHUNK))


def _retention_constants():
    C = CHUNK
    lg = _RET_LOG_GAMMA
    lane_lg = np.repeat(lg, RET_DK)[None, :]
    n = np.arange(C, dtype=np.float64)[:, None]
    m = np.tile(np.arange(C, dtype=np.float64), RET_HEADS)[None, :]
    scale = RET_DK ** -0.5
    decay4 = scale * np.exp(lane_lg * np.abs(n - m))
    qdec = np.exp(lane_lg * (n + 1.0)) * np.ones((1, RET_QK))
    kdec = scale * np.exp(lane_lg * (C - 1.0 - n)) * np.ones((1, RET_QK))
    return jnp.asarray(np.stack([decay4, qdec, kdec]).astype(np.float32))


def _const_spec(shape, single_buffer=False):
    mode = pl.Buffered(1) if single_buffer else None
    return pl.BlockSpec(shape, lambda *_: (0,) * len(shape), pipeline_mode=mode)


@jax.jit
def kernel(x, norm_g, w_in, ret_gn_g, rwkv_mu, w_lora_up, w0, a_lora_up, a0, k_k, k_a, r_k,
           rwkv_gn_g, rwkv_gn_b, w_out, final_norm_g):
    B, T, D = x.shape
    assert D == D_MODEL and T % CHUNK == 0 and T % PROJ_TILE == 0 and B % BATCH_BLOCK == 0
    assert norm_g.shape[0] == 1, "single-layer block"
    n_tok = B * T
    xf = x.reshape(n_tok, D)
    params = pltpu.CompilerParams(dimension_semantics=("arbitrary",), vmem_limit_bytes=VMEM_LIMIT)
    tiles_per_seq = T // PROJ_TILE
    cos, sin = _rope_tables(T)

    p = pl.pallas_call(
        functools.partial(_in_proj_kernel, tiles_per_seq),
        grid=(n_tok // PROJ_TILE,),
        in_specs=[pl.BlockSpec((PROJ_TILE, D), lambda i: (i, 0)),
                  _const_spec((1, D)),
                  _const_spec((D, IN_COLS), single_buffer=True),
                  pl.BlockSpec((PROJ_TILE, LANES), lambda i: (i % tiles_per_seq, 0)),
                  pl.BlockSpec((PROJ_TILE, LANES), lambda i: (i % tiles_per_seq, 0)),
                  _const_spec((1, RWKV_COLS))],
        out_specs=pl.BlockSpec((PROJ_TILE, IN_COLS), lambda i: (i, 0)),
        out_shape=jax.ShapeDtypeStruct((n_tok, IN_COLS), f32),
        scratch_shapes=[pltpu.VMEM((D, IN_COLS), bf16),
                        pltpu.VMEM((1, RWKV_COLS), f32)],
        compiler_params=params,
        name="in_proj",
    )(xf, norm_g[0][None, :], w_in[0], cos, sin, rwkv_mu[0][None, :])

    dec = _retention_constants()
    vecs = [v.reshape(1, RWKV_WIDTH) for v in (w0, a0, k_k, k_a, r_k, rwkv_gn_g, rwkv_gn_b)]
    seg_ids = np.arange(GROUP) // RWKV_HEAD
    seg = jnp.asarray(seg_ids[:, None] == seg_ids[None, :], bf16)

    return pl.pallas_call(
        _mixer_kernel,
        grid=(B // BATCH_BLOCK, T // CHUNK),
        in_specs=[pl.BlockSpec((BATCH_BLOCK, CHUNK, IN_COLS), lambda b, c: (b, c, 0)),
                  _const_spec((3, CHUNK, RET_QK)),
                  _const_spec((LORA, RWKV_WIDTH)),
                  _const_spec((LORA, RWKV_WIDTH))]
                 + [_const_spec((1, RWKV_WIDTH))] * len(vecs)
                 + [_const_spec((1, RET_WIDTH)),
                  _const_spec((GROUP, GROUP)),
                  pl.BlockSpec((BATCH_BLOCK, CHUNK, D), lambda b, c: (b, c, 0)),
                  _const_spec((D, D), single_buffer=True),
                  _const_spec((1, D))],
        out_specs=pl.BlockSpec((BATCH_BLOCK, CHUNK, D), lambda b, c: (b, c, 0)),
        out_shape=jax.ShapeDtypeStruct((B, T, D), f32),
        scratch_shapes=[pltpu.VMEM((BATCH_BLOCK, CHUNK, D), bf16),
                        pltpu.VMEM((BATCH_BLOCK, RWKV_WIDTH // GROUP, RWKV_HEAD, GROUP), f32),
                        pltpu.VMEM((BATCH_BLOCK, RET_HEADS, RET_DK, RET_DV), f32),
                        pltpu.VMEM((D, D), bf16),
                        pltpu.VMEM((2 * LORA, 2 * RWKV_WIDTH), bf16)],
        compiler_params=pltpu.CompilerParams(dimension_semantics=("arbitrary", "arbitrary"),
                                             vmem_limit_bytes=VMEM_LIMIT),
        name="mixers",
    )(p.reshape(B, T, IN_COLS), dec, w_lora_up[0], a_lora_up[0], *vecs,
      ret_gn_g[0][None, :], seg, x, w_out[0], final_norm_g[None, :])
```
